```python
import jax, jax.numpy as jnp
from jax import lax
import numpy as np

D_MODEL = 2048
BATCH = 4
SEQ = 2048
DEPTH = 1
DEC_BATCH = 128
DEC_SEQ = 1
PAST_LEN = 16384
PAGE_SIZE = 128

D_MIX = D_MODEL
D_CONV = D_MIX // 2
CONV_W = 3
N_HEADS = 8
HEAD_DK = (D_MIX - D_CONV) // N_HEADS
HEAD_DV = (D_MIX - D_CONV) // N_HEADS
GDN_CONV_W = 4
CHUNK = 64
D_QK = N_HEADS * HEAD_DK
D_V = N_HEADS * HEAD_DV
D_QKV = 2 * D_QK + D_V
D_IN = 3 * D_CONV + D_QKV + D_V + 2 * N_HEADS
SPLIT_IDX = (D_CONV, 2 * D_CONV, 3 * D_CONV, 3 * D_CONV + D_QKV,
             3 * D_CONV + D_QKV + D_V, 3 * D_CONV + D_QKV + D_V + N_HEADS)
N_EXPERTS = 32
TOP_K = 4
D_FF = D_MODEL
SWIGLU_LIMIT = 7.0
SWIGLU_ALPHA = 1.702
EPS = 1e-6

kernel_name = 'hymba_conv_gdn_moe_decode_step'


def rmsnorm(x, g):
    xf = x.astype(jnp.float32)
    y = xf * lax.rsqrt(jnp.mean(xf * xf, axis=-1, keepdims=True) + EPS)
    return (y * g.astype(jnp.float32)).astype(x.dtype)


def l2norm(x):
    xf = x.astype(jnp.float32)
    return xf * lax.rsqrt(jnp.sum(xf * xf, axis=-1, keepdims=True) + EPS)


def causal_dwconv(u, hist, w):
    width = w.shape[0]
    t_len = u.shape[1]
    full = jnp.concatenate([hist.astype(u.dtype), u], axis=1)
    out = full[:, 0:t_len] * w[0]
    for j in range(1, width):
        out = out + full[:, j:j + t_len] * w[j]
    return out, full[:, t_len:]


def gdn_chunked(q, k, v, g, beta, s0):
    bsz, t_len = q.shape[0], q.shape[1]
    n_chunks = -(-t_len // CHUNK)
    pad = n_chunks * CHUNK - t_len

    def chunked(a):
        a = jnp.pad(a, [(0, 0), (0, pad)] + [(0, 0)] * (a.ndim - 2))
        a = a.reshape((bsz, n_chunks, CHUNK) + a.shape[2:])
        return jnp.moveaxis(a, (1, 3), (0, 2))

    q, k, v, g, beta = [chunked(a) for a in (q, k, v, g, beta)]
    gc = jnp.cumsum(g, axis=-1)
    idx = jnp.arange(CHUNK)
    causal = idx[:, None] >= idx[None, :]
    strict = idx[:, None] > idx[None, :]
    decay = jnp.exp(jnp.where(causal, gc[..., :, None] - gc[..., None, :], -jnp.inf))
    kb = k * beta[..., None]
    a_mat = jnp.where(strict, jnp.einsum('nbhid,nbhjd->nbhij', kb, k) * decay, 0.0)
    rhs = jnp.concatenate([v * beta[..., None], kb * jnp.exp(gc)[..., None]], axis=-1)
    uw = lax.linalg.triangular_solve(a_mat, rhs, left_side=True, lower=True, unit_diagonal=True)
    u, w = uw[..., :HEAD_DV], uw[..., HEAD_DV:]
    qk = jnp.einsum('nbhid,nbhjd->nbhij', q, k) * decay
    qg = q * jnp.exp(gc)[..., None]
    kg = k * jnp.exp(gc[..., -1:] - gc)[..., None]
    g_last = jnp.exp(gc[..., -1])

    def step(s, inp):
        qg_c, kg_c, u_c, w_c, qk_c, gl_c = inp
        v_new = u_c - jnp.einsum('bhck,bhkv->bhcv', w_c, s)
        o = jnp.einsum('bhck,bhkv->bhcv', qg_c, s) + jnp.einsum('bhij,bhjv->bhiv', qk_c, v_new)
        s = s * gl_c[..., None, None] + jnp.einsum('bhck,bhcv->bhkv', kg_c, v_new)
        return s, o

    s_fin, o = lax.scan(step, s0.astype(jnp.float32), (qg, kg, u, w, qk, g_last))
    o = jnp.moveaxis(o, (0, 2), (1, 3)).reshape(bsz, n_chunks * CHUNK, N_HEADS, HEAD_DV)[:, :t_len]
    return o, s_fin


def token_mixers(h, hist_a, hist_qkv, s0, w_in, conv_a_w, gdn_conv_w, gdn_a_log, gdn_dt_bias,
                 g_conv_out, gdn_norm_g, w_out):
    bsz, t_len, _ = h.shape
    proj = h @ w_in
    b_gate, c_gate, h_conv, qkv, z, b_raw, a_raw = jnp.split(proj, SPLIT_IDX, axis=-1)
    ya, new_hist_a = causal_dwconv(c_gate * h_conv, hist_a, conv_a_w)
    ya = b_gate * ya
    qkv_c, new_hist_qkv = causal_dwconv(qkv, hist_qkv, gdn_conv_w)
    qkv_c = jax.nn.silu(qkv_c)
    q, k, v = jnp.split(qkv_c, (D_QK, 2 * D_QK), axis=-1)
    q = l2norm(q.reshape(bsz, t_len, N_HEADS, HEAD_DK)) * (HEAD_DK ** -0.5)
    k = l2norm(k.reshape(bsz, t_len, N_HEADS, HEAD_DK))
    v = v.reshape(bsz, t_len, N_HEADS, HEAD_DV).astype(jnp.float32)
    beta = jax.nn.sigmoid(b_raw.astype(jnp.float32))
    g = -jnp.exp(gdn_a_log.astype(jnp.float32)) * jax.nn.softplus(
        a_raw.astype(jnp.float32) + gdn_dt_bias.astype(jnp.float32))
    o, s_new = gdn_chunked(q, k, v, g, beta, s0)
    o = rmsnorm(o, gdn_norm_g) * jax.nn.silu(z.reshape(bsz, t_len, N_HEADS, HEAD_DV).astype(jnp.float32))
    yb = o.reshape(bsz, t_len, D_V).astype(h.dtype)
    mix = jnp.concatenate([rmsnorm(ya, g_conv_out), yb], axis=-1) @ w_out
    return mix, new_hist_a, new_hist_qkv, s_new


def moe(h, router_w, router_b, w_gu, b_gu, w_down, b_down):
    bsz, t_len, d = h.shape
    t = h.reshape(-1, d)
    logits = (t @ router_w + router_b).astype(jnp.float32)
    top_v, top_i = lax.top_k(logits, TOP_K)
    probs = jax.nn.softmax(top_v, axis=-1)
    combine = jnp.einsum('nk,nke->en', probs, jax.nn.one_hot(top_i, N_EXPERTS, dtype=jnp.float32))
    combine = combine.astype(h.dtype)

    def expert(acc, p):
        wgu, bgu, wd, bd, cw = p
        gu = t @ wgu + bgu
        gate, up = gu[:, :D_FF], gu[:, D_FF:]
        gate = jnp.minimum(gate, SWIGLU_LIMIT)
        up = jnp.clip(up, -SWIGLU_LIMIT, SWIGLU_LIMIT)
        act = (up + 1.0) * gate * jax.nn.sigmoid(SWIGLU_ALPHA * gate)
        return acc + cw[:, None] * (act @ wd + bd), None

    acc, _ = lax.scan(expert, jnp.zeros_like(t), (w_gu, b_gu, w_down, b_down, combine))
    return acc.reshape(bsz, t_len, d)


def layer(x, c, hist_a, hist_qkv, s0, w_mod, b_mod, g_pre_mix, g_post_mix, g_pre_ffn, g_post_ffn,
          w_in, conv_a_w, gdn_conv_w, gdn_a_log, gdn_dt_bias, g_conv_out, gdn_norm_g, w_out,
          router_w, router_b, exp_w_gate_up, exp_b_gate_up, exp_w_down, exp_b_down):
    mod = (jax.nn.silu(c) @ w_mod + b_mod)[:, None, :]
    shift1, scale1, gate1, shift2, scale2, gate2 = jnp.split(mod, 6, axis=-1)
    h = rmsnorm(x, g_pre_mix) * (1.0 + scale1) + shift1
    mix, new_hist_a, new_hist_qkv, s_new = token_mixers(
        h, hist_a, hist_qkv, s0, w_in, conv_a_w, gdn_conv_w, gdn_a_log, gdn_dt_bias,
        g_conv_out, gdn_norm_g, w_out)
    x = x + gate1 * rmsnorm(mix, g_post_mix)
    h = rmsnorm(x, g_pre_ffn) * (1.0 + scale2) + shift2
    f = moe(h, router_w, router_b, exp_w_gate_up, exp_b_gate_up, exp_w_down, exp_b_down)
    x = x + gate2 * rmsnorm(f, g_post_ffn)
    return x, new_hist_a, new_hist_qkv, s_new


def setup_inputs(seed: int = 0) -> dict:
    key = jax.random.key(seed)
    ks = jax.random.split(key, 27)
    f32 = jnp.float32
    L = DEPTH

    def nrm(k, shape, scale):
        return jax.random.normal(k, shape, f32) * scale

    return {
        'x_prompt': nrm(ks[0], (BATCH, SEQ, D_MODEL), 1.0),
        'x_sample': nrm(ks[1], (DEC_BATCH, DEC_SEQ, D_MODEL), 1.0),
        'state_conv_a': nrm(ks[2], (L, DEC_BATCH, CONV_W - 1, D_CONV), 1.0),
        'state_gdn_conv': nrm(ks[3], (L, DEC_BATCH, GDN_CONV_W - 1, D_QKV), 1.0),
        'state_gdn_S': nrm(ks[4], (L, DEC_BATCH, N_HEADS, HEAD_DK, HEAD_DV), 0.1),
        'c_prompt': nrm(ks[5], (BATCH, D_MODEL), 1.0),
        'c_sample': nrm(ks[6], (DEC_BATCH, D_MODEL), 1.0),
        'w_mod': nrm(ks[7], (L, D_MODEL, 6 * D_MODEL), 0.5 * D_MODEL ** -0.5),
        'b_mod': nrm(ks[8], (L, 6 * D_MODEL), 0.02),
        'g_pre_mix': 1.0 + nrm(ks[9], (L, D_MODEL), 0.05),
        'g_post_mix': 1.0 + nrm(ks[10], (L, D_MODEL), 0.05),
        'g_pre_ffn': 1.0 + nrm(ks[11], (L, D_MODEL), 0.05),
        'g_post_ffn': 1.0 + nrm(ks[12], (L, D_MODEL), 0.05),
        'w_in': nrm(ks[13], (L, D_MODEL, D_IN), D_MODEL ** -0.5),
        'conv_a_w': nrm(ks[14], (L, CONV_W, D_CONV), CONV_W ** -0.5),
        'gdn_conv_w': nrm(ks[15], (L, GDN_CONV_W, D_QKV), GDN_CONV_W ** -0.5),
        'gdn_a_log': jnp.log(jax.random.uniform(ks[16], (L, N_HEADS), f32, 1.0, 16.0)),
        'gdn_dt_bias': nrm(ks[17], (L, N_HEADS), 0.1),
        'g_conv_out': 1.0 + nrm(ks[18], (L, D_CONV), 0.05),
        'gdn_norm_g': 1.0 + nrm(ks[19], (L, HEAD_DV), 0.05),
        'w_out': nrm(ks[20], (L, D_MIX, D_MODEL), D_MIX ** -0.5),
        'router_w': nrm(ks[21], (L, D_MODEL, N_EXPERTS), D_MODEL ** -0.5),
        'router_b': nrm(ks[22], (L, N_EXPERTS), 0.01),
        'exp_w_gate_up': nrm(ks[23], (L, N_EXPERTS, D_MODEL, 2 * D_FF), D_MODEL ** -0.5),
        'exp_b_gate_up': nrm(ks[24], (L, N_EXPERTS, 2 * D_FF), 0.02),
        'exp_w_down': nrm(ks[25], (L, N_EXPERTS, D_FF, D_MODEL), D_FF ** -0.5),
        'exp_b_down': nrm(ks[26], (L, N_EXPERTS, D_MODEL), 0.02),
    }


def reference(x_prompt, x_sample, state_conv_a, state_gdn_conv, state_gdn_S, c_prompt, c_sample,
              w_mod, b_mod, g_pre_mix, g_post_mix, g_pre_ffn, g_post_ffn, w_in, conv_a_w, gdn_conv_w,
              gdn_a_log, gdn_dt_bias, g_conv_out, gdn_norm_g, w_out, router_w, router_b,
              exp_w_gate_up, exp_b_gate_up, exp_w_down, exp_b_down):
    xp, xs = x_prompt, x_sample
    bp = xp.shape[0]
    ca_p, cq_p, s_p, ca_s, cq_s, s_s = [], [], [], [], [], []
    for l in range(DEPTH):
        w_l = (w_mod[l], b_mod[l], g_pre_mix[l], g_post_mix[l], g_pre_ffn[l], g_post_ffn[l],
               w_in[l], conv_a_w[l], gdn_conv_w[l], gdn_a_log[l], gdn_dt_bias[l], g_conv_out[l],
               gdn_norm_g[l], w_out[l], router_w[l], router_b[l], exp_w_gate_up[l], exp_b_gate_up[l],
               exp_w_down[l], exp_b_down[l])
        hist_a0 = jnp.zeros((bp, CONV_W - 1, D_CONV), xp.dtype)
        hist_q0 = jnp.zeros((bp, GDN_CONV_W - 1, D_QKV), xp.dtype)
        s00 = jnp.zeros((bp, N_HEADS, HEAD_DK, HEAD_DV), jnp.float32)
        xp, ha, hq, sn = layer(xp, c_prompt, hist_a0, hist_q0, s00, *w_l)
        ca_p.append(ha); cq_p.append(hq); s_p.append(sn)
        xs, ha, hq, sn = layer(xs, c_sample, state_conv_a[l], state_gdn_conv[l], state_gdn_S[l], *w_l)
        ca_s.append(ha); cq_s.append(hq); s_s.append(sn)
    return (xp, xs, jnp.stack(ca_p), jnp.stack(cq_p), jnp.stack(s_p),
            jnp.stack(ca_s), jnp.stack(cq_s), jnp.stack(s_s))
```

```python
import functools

import jax
import jax.numpy as jnp
from jax import lax
from jax.experimental import pallas as pl
from jax.experimental.pallas import tpu as pltpu

F32 = jnp.float32
BF16 = jnp.bfloat16
I32 = jnp.int32
HIGHEST = lax.Precision.HIGHEST

EPS = 1e-6
N_HEADS = 8
HEAD_D = 128
TOP_K = 4
TOP_K_SHIFT = 2
SWIGLU_LIMIT = 7.0
SWIGLU_ALPHA = 1.702
CHUNK = 64
LANES = 128
EXPERT_ROW_TILE = 256
NEG_BIG = -1e30
VMEM_LIMIT = 56 * 1024 * 1024


def _params(semantics, vmem=VMEM_LIMIT):
    return pltpu.CompilerParams(dimension_semantics=semantics, vmem_limit_bytes=vmem)


def _mm(a, b):
    return jnp.dot(a.astype(BF16), b.astype(BF16), preferred_element_type=F32)


def _mm_nt(a, b):
    return lax.dot_general(a.astype(BF16), b.astype(BF16), (((1,), (1,)), ((), ())),
                           preferred_element_type=F32)


def _rms(x):
    return x * lax.rsqrt(jnp.mean(x * x, axis=-1, keepdims=True) + EPS)


def _silu(x):
    return x * jax.nn.sigmoid(x)


def _softplus(x):
    return jnp.maximum(x, 0.0) + jnp.log1p(jnp.exp(-jnp.abs(x)))


def _mod_kernel(c_ref, w_ref, b_ref, o_ref):
    s = _silu(c_ref[...])
    o_ref[...] = _mm(s, w_ref[...]) + b_ref[...]


def _modulation(c_all, w_mod, b_mod):
    n, d = c_all.shape
    m = w_mod.shape[1]
    tn = 1024
    return pl.pallas_call(
        _mod_kernel,
        out_shape=jax.ShapeDtypeStruct((n, m), F32),
        grid=(m // tn,),
        in_specs=[pl.BlockSpec((n, d), lambda j: (0, 0)),
                  pl.BlockSpec((d, tn), lambda j: (0, j)),
                  pl.BlockSpec((1, tn), lambda j: (0, j))],
        out_specs=pl.BlockSpec((n, tn), lambda j: (0, j)),
        compiler_params=_params(("arbitrary",)),
        name="modulation",
    )(c_all, w_mod, b_mod.reshape(1, m))


def _mod_spec(arr, tm, rows_per_group):
    if arr.ndim == 3:
        tiles = rows_per_group // tm
        return pl.BlockSpec((None, 1, arr.shape[-1]), lambda i, *_: (i // tiles, 0, 0))
    return pl.BlockSpec((tm, arr.shape[-1]), lambda i, *_: (i, 0))


def _proj_kernel(x_ref, sc_ref, sh_ref, g_ref, w_ref, wba_ref, o_ref, ba_ref, h_scr):
    @pl.when(pl.program_id(1) == 0)
    def _():
        h = (_rms(x_ref[...]) * g_ref[...]) * (1.0 + sc_ref[...]) + sh_ref[...]
        hb = h.astype(BF16)
        h_scr[...] = hb
        ba_ref[...] = jnp.dot(hb, wba_ref[...], preferred_element_type=F32)

    o_ref[...] = jnp.dot(h_scr[...], w_ref[...], preferred_element_type=F32)


def _in_proj(x, scale, shift, g, w_main, w_ba, rows_per_group):
    m, d = x.shape
    n = w_main.shape[1]
    tm = min(m, 1024, rows_per_group if scale.ndim == 3 else m)
    tn = 1024
    return pl.pallas_call(
        _proj_kernel,
        out_shape=(jax.ShapeDtypeStruct((m, n), F32), jax.ShapeDtypeStruct((m, LANES), F32)),
        grid=(m // tm, n // tn),
        in_specs=[pl.BlockSpec((tm, d), lambda i, j: (i, 0)),
                  _mod_spec(scale, tm, rows_per_group),
                  _mod_spec(shift, tm, rows_per_group),
                  pl.BlockSpec((1, d), lambda i, j: (0, 0)),
                  pl.BlockSpec((d, tn), lambda i, j: (0, j)),
                  pl.BlockSpec((d, LANES), lambda i, j: (0, 0))],
        out_specs=(pl.BlockSpec((tm, tn), lambda i, j: (i, j)),
                   pl.BlockSpec((tm, LANES), lambda i, j: (i, 0))),
        scratch_shapes=[pltpu.VMEM((tm, d), BF16)],
        compiler_params=_params(("arbitrary", "arbitrary")),
        name="in_proj",
    )(x, scale, shift, g.reshape(1, d), w_main, w_ba)


def _mixer_prompt_kernel(proj_ref, ba_ref, caw_ref, gcw_ref, alog_ref, dtb_ref, gco_ref, gng_ref,
                         mix_ref, ha_ref, hq_ref, s_ref, extu, extq, qc_scr, s_scr):
    c = CHUNK
    dc = N_HEADS * HEAD_D
    dq = 3 * dc
    t = pl.program_id(1)

    @pl.when(t == 0)
    def _():
        extu[0:8, :] = jnp.zeros((8, dc), F32)
        extq[0:8, :] = jnp.zeros((8, dq), F32)
        s_scr[...] = jnp.zeros_like(s_scr)

    u = proj_ref[:, dc:2 * dc] * proj_ref[:, 2 * dc:3 * dc]
    extu[8:8 + c, :] = u
    caw = caw_ref[...]
    ya = caw[0:1] * extu[6:6 + c, :] + caw[1:2] * extu[7:7 + c, :] + caw[2:3] * u
    ya = proj_ref[:, 0:dc] * ya
    mix_ref[:, 0:dc] = (_rms(ya) * gco_ref[...]).astype(BF16)
    last_u = extu[6 + c:8 + c, :]
    extu[6:8, :] = last_u

    qkv = proj_ref[:, 3 * dc:3 * dc + dq]
    extq[8:8 + c, :] = qkv
    gcw = gcw_ref[...]
    qc = (gcw[0:1] * extq[5:5 + c, :] + gcw[1:2] * extq[6:6 + c, :]
          + gcw[2:3] * extq[7:7 + c, :] + gcw[3:4] * qkv)
    qc_scr[...] = _silu(qc)
    last_q = extq[5 + c:8 + c, :]
    extq[5:8, :] = last_q

    @pl.when(t == pl.num_programs(1) - 1)
    def _():
        ha_ref[...] = last_u
        hq_ref[...] = last_q

    ba = ba_ref[...]
    beta_all = jax.nn.sigmoid(ba)
    g_all = -jnp.exp(alog_ref[...]) * _softplus(ba + dtb_ref[...])
    row = lax.broadcasted_iota(I32, (c, c), 0)
    col = lax.broadcasted_iota(I32, (c, c), 1)
    causal = row >= col
    strict = row > col
    lower = jnp.where(causal, 1.0, 0.0).astype(F32)
    upper = jnp.where(row <= col, 1.0, 0.0).astype(F32)
    gc_all = jnp.dot(lower, g_all, precision=HIGHEST, preferred_element_type=F32)
    gc_t = lax.dot_general(g_all, upper, (((0,), (0,)), ((), ())), precision=HIGHEST,
                           preferred_element_type=F32)

    for h in range(N_HEADS):
        lo = h * HEAD_D
        q = qc_scr[:, lo:lo + HEAD_D]
        k = qc_scr[:, dc + lo:dc + lo + HEAD_D]
        v = qc_scr[:, 2 * dc + lo:2 * dc + lo + HEAD_D]
        qn = q * lax.rsqrt(jnp.sum(q * q, axis=-1, keepdims=True) + EPS) * (HEAD_D ** -0.5)
        kn = k * lax.rsqrt(jnp.sum(k * k, axis=-1, keepdims=True) + EPS)
        beta = beta_all[:, h:h + 1]
        gcc = gc_all[:, N_HEADS + h:N_HEADS + h + 1]
        gcr = gc_t[N_HEADS + h:N_HEADS + h + 1, :]
        gl = gc_all[c - 1:c, N_HEADS + h:N_HEADS + h + 1]
        decay = jnp.where(causal, jnp.exp(jnp.minimum(gcc - gcr, 0.0)), 0.0)
        eg = jnp.exp(gcc)
        kb = kn * beta
        a_mat = jnp.where(strict, _mm_nt(kb, kn) * decay, 0.0)
        qk = _mm_nt(qn, kn) * decay
        n_mat = -a_mat
        p = a_mat
        size = 2
        while size < c:
            p = _mm(p, p)
            n_mat = n_mat + p + _mm(n_mat, p)
            size *= 2
        rhs = jnp.concatenate([v * beta, kb * eg], axis=-1)
        uw = rhs + _mm(n_mat, rhs)
        s_old = s_scr[h]
        s_b = s_old.astype(BF16)
        v_new = uw[:, :HEAD_D] - _mm(uw[:, HEAD_D:], s_b)
        o = _mm(qn * eg, s_b) + _mm(qk, v_new)
        kg = kn * jnp.exp(gl - gcc)
        s_scr[h] = s_old * jnp.exp(gl) + _mm(kg.T, v_new)
        z = proj_ref[:, 3 * dc + dq + lo:3 * dc + dq + lo + HEAD_D]
        yb = _rms(o) * gng_ref[...] * _silu(z)
        mix_ref[:, dc + lo:dc + lo + HEAD_D] = yb.astype(BF16)

    @pl.when(t == pl.num_programs(1) - 1)
    def _():
        s_ref[...] = s_scr[...]


def _lane_row(vec, offset):
    return jnp.zeros((1, LANES), F32).at[0, offset:offset + vec.shape[0]].set(vec.astype(F32))


def _mixer_prompt(proj, ba, bsz, seq, conv_a_w, gdn_conv_w, a_log, dt_bias, g_conv_out, gdn_norm_g):
    c = CHUNK
    dc = N_HEADS * HEAD_D
    dq = 3 * dc
    dproj = proj.shape[1]
    nt = seq // c
    const = lambda shape: pl.BlockSpec(shape, lambda b, t: (0,) * len(shape))
    return pl.pallas_call(
        _mixer_prompt_kernel,
        out_shape=(jax.ShapeDtypeStruct((bsz * seq, 2 * dc), BF16),
                   jax.ShapeDtypeStruct((bsz, 2, dc), F32),
                   jax.ShapeDtypeStruct((bsz, 3, dq), F32),
                   jax.ShapeDtypeStruct((bsz, N_HEADS, HEAD_D, HEAD_D), F32)),
        grid=(bsz, nt),
        in_specs=[pl.BlockSpec((c, dproj), lambda b, t: (b * nt + t, 0)),
                  pl.BlockSpec((c, LANES), lambda b, t: (b * nt + t, 0)),
                  const((3, dc)), const((4, dq)), const((1, LANES)), const((1, LANES)),
                  const((1, dc)), const((1, HEAD_D))],
        out_specs=(pl.BlockSpec((c, 2 * dc), lambda b, t: (b * nt + t, 0)),
                   pl.BlockSpec((None, 2, dc), lambda b, t: (b, 0, 0)),
                   pl.BlockSpec((None, 3, dq), lambda b, t: (b, 0, 0)),
                   pl.BlockSpec((None, N_HEADS, HEAD_D, HEAD_D), lambda b, t: (b, 0, 0, 0))),
        scratch_shapes=[pltpu.VMEM((8 + c, dc), F32), pltpu.VMEM((8 + c, dq), F32),
                        pltpu.VMEM((c, dq), F32), pltpu.VMEM((N_HEADS, HEAD_D, HEAD_D), F32)],
        compiler_params=_params(("arbitrary", "arbitrary")),
        name="mixer_prompt",
    )(proj, ba, conv_a_w, gdn_conv_w, _lane_row(a_log, N_HEADS), _lane_row(dt_bias, N_HEADS),
      g_conv_out.reshape(1, dc), gdn_norm_g.reshape(1, HEAD_D))


SAMPLE_GROUP = 16


def _mixer_sample_kernel(proj_ref, ba_ref, hista_ref, histq_ref, s_in_ref, caw_ref, gcw_ref, alog_ref,
                         dtb_ref, gco_ref, gng_ref, mix_ref, ha_ref, hq_ref, s_out_ref, qc_scr, o_scr):
    tb = SAMPLE_GROUP
    dc = N_HEADS * HEAD_D
    dq = 3 * dc

    u = proj_ref[:, dc:2 * dc] * proj_ref[:, 2 * dc:3 * dc]
    caw = caw_ref[...]
    ya = caw[0:1] * hista_ref[:, 0:dc] + caw[1:2] * hista_ref[:, dc:2 * dc] + caw[2:3] * u
    ya = proj_ref[:, 0:dc] * ya
    mix_ref[:, 0:dc] = _rms(ya) * gco_ref[...]
    ha_ref[:, 0:dc] = hista_ref[:, dc:2 * dc]
    ha_ref[:, dc:2 * dc] = u

    qkv = proj_ref[:, 3 * dc:3 * dc + dq]
    gcw = gcw_ref[...]
    qc = (gcw[0:1] * histq_ref[:, 0:dq] + gcw[1:2] * histq_ref[:, dq:2 * dq]
          + gcw[2:3] * histq_ref[:, 2 * dq:3 * dq] + gcw[3:4] * qkv)
    qc_scr[...] = _silu(qc)
    hq_ref[:, 0:dq] = histq_ref[:, dq:2 * dq]
    hq_ref[:, dq:2 * dq] = histq_ref[:, 2 * dq:3 * dq]
    hq_ref[:, 2 * dq:3 * dq] = qkv

    ba = ba_ref[...]
    beta_all = jax.nn.sigmoid(ba)
    eg_all = jnp.exp(-jnp.exp(alog_ref[...]) * _softplus(ba + dtb_ref[...]))

    for h in range(N_HEADS):
        lo = h * HEAD_D
        q = qc_scr[:, lo:lo + HEAD_D]
        k = qc_scr[:, dc + lo:dc + lo + HEAD_D]
        v = qc_scr[:, 2 * dc + lo:2 * dc + lo + HEAD_D]
        qn = q * lax.rsqrt(jnp.sum(q * q, axis=-1, keepdims=True) + EPS) * (HEAD_D ** -0.5)
        kn = k * lax.rsqrt(jnp.sum(k * k, axis=-1, keepdims=True) + EPS)
        qk = jnp.sum(qn * kn, axis=-1, keepdims=True)
        kn_t = kn.T
        qn_t = qn.T
        for b in range(tb):
            s_old = s_in_ref[b, h]
            kc = kn_t[:, b:b + 1]
            e = eg_all[b:b + 1, N_HEADS + h:N_HEADS + h + 1]
            ks = jnp.sum(s_old * kc, axis=0, keepdims=True)
            qs = jnp.sum(s_old * qn_t[:, b:b + 1], axis=0, keepdims=True)
            v_new = beta_all[b:b + 1, h:h + 1] * (v[b:b + 1, :] - e * ks)
            o_scr[b:b + 1, lo:lo + HEAD_D] = e * qs + qk[b:b + 1, :] * v_new
            s_out_ref[b, h] = s_old * e + kc * v_new
        z = proj_ref[:, 3 * dc + dq + lo:3 * dc + dq + lo + HEAD_D]
        o = o_scr[:, lo:lo + HEAD_D]
        mix_ref[:, dc + lo:dc + lo + HEAD_D] = _rms(o) * gng_ref[...] * _silu(z)


def _mixer_sample(proj, ba, hist_a, hist_q, s_in, conv_a_w, gdn_conv_w, a_log, dt_bias, g_conv_out,
                  gdn_norm_g):
    n = proj.shape[0]
    tb = SAMPLE_GROUP
    dc = N_HEADS * HEAD_D
    dq = 3 * dc
    dproj = proj.shape[1]
    const = lambda shape: pl.BlockSpec(shape, lambda i: (0,) * len(shape))
    rows = lambda width: pl.BlockSpec((tb, width), lambda i: (i, 0))
    state = pl.BlockSpec((tb, N_HEADS, HEAD_D, HEAD_D), lambda i: (i, 0, 0, 0))
    return pl.pallas_call(
        _mixer_sample_kernel,
        out_shape=(jax.ShapeDtypeStruct((n, 2 * dc), F32),
                   jax.ShapeDtypeStruct((n, 2 * dc), F32),
                   jax.ShapeDtypeStruct((n, 3 * dq), F32),
                   jax.ShapeDtypeStruct((n, N_HEADS, HEAD_D, HEAD_D), F32)),
        grid=(n // tb,),
        in_specs=[rows(dproj), rows(LANES), rows(2 * dc), rows(3 * dq), state,
                  const((3, dc)), const((4, dq)), const((1, LANES)), const((1, LANES)),
                  const((1, dc)), const((1, HEAD_D))],
        out_specs=(rows(2 * dc), rows(2 * dc), rows(3 * dq), state),
        scratch_shapes=[pltpu.VMEM((tb, dq), F32), pltpu.VMEM((tb, dc), F32)],
        compiler_params=_params(("arbitrary",)),
        name="mixer_sample",
    )(proj, ba, hist_a, hist_q, s_in, conv_a_w, gdn_conv_w, _lane_row(a_log, N_HEADS),
      _lane_row(dt_bias, N_HEADS), g_conv_out.reshape(1, dc), gdn_norm_g.reshape(1, HEAD_D))


def _post_mix_kernel(mix_ref, x_ref, gate_ref, sc_ref, sh_ref, gpost_ref, gpre_ref, wout_ref,
                     rwh_ref, rwl_ref, rb_ref, x1_ref, h2_ref, lg_ref):
    mix =jnp.dot(mix_ref[...].astype(BF16), wout_ref[...], preferred_element_type=F32)
    x1 = x_ref[...] + gate_ref[...] * (_rms(mix) * gpost_ref[...])
    x1_ref[...] = x1
    h2 = (_rms(x1) * gpre_ref[...]) * (1.0 + sc_ref[...]) + sh_ref[...]
    h2_ref[...] = h2
    hi = h2.astype(BF16)
    lo = (h2 - hi.astype(F32)).astype(BF16)
    rwh = rwh_ref[...]
    lg_ref[...] = (jnp.dot(hi, rwh, preferred_element_type=F32)
                   + jnp.dot(lo, rwh, preferred_element_type=F32)
                   + jnp.dot(hi, rwl_ref[...], preferred_element_type=F32) + rb_ref[...])


def _post_mix(mix_in, x, gate, scale, shift, g_post, g_pre, w_out, rw_hi, rw_lo, rb, rows_per_group):
    m, d = x.shape
    tm = min(m, 512, rows_per_group if gate.ndim == 3 else m)
    const = lambda shape: pl.BlockSpec(shape, lambda i: (0,) * len(shape))
    rows = lambda width: pl.BlockSpec((tm, width), lambda i: (i, 0))
    return pl.pallas_call(
        _post_mix_kernel,
        out_shape=(jax.ShapeDtypeStruct((m, d), F32), jax.ShapeDtypeStruct((m, d), F32),
                   jax.ShapeDtypeStruct((m, LANES), F32)),
        grid=(m // tm,),
        in_specs=[rows(d), rows(d),
                  _mod_spec(gate, tm, rows_per_group),
                  _mod_spec(scale, tm, rows_per_group),
                  _mod_spec(shift, tm, rows_per_group),
                  const((1, d)), const((1, d)), const((d, d)),
                  const((d, LANES)), const((d, LANES)), const((1, LANES))],
        out_specs=(rows(d), rows(d), rows(LANES)),
        compiler_params=_params(("arbitrary",)),
        name="post_mix",
    )(mix_in, x, gate, scale, shift, g_post.reshape(1, d), g_pre.reshape(1, d), w_out, rw_hi, rw_lo, rb)


ROUTE_TOKENS = 128


def _route_kernel(lgp_ref, lgs_ref, idx_ref, p_ref, rank_ref, cnt_ref, carry, *, prompt_tiles):
    tm = lgp_ref.shape[0]

    @pl.when(pl.program_id(0) == 0)
    def _():
        carry[...] = jnp.zeros_like(carry)

    l = jnp.where(pl.program_id(0) < prompt_tiles, lgp_ref[...], lgs_ref[...])
    lane = lax.broadcasted_iota(I32, l.shape, 1)
    lane_f = lane.astype(F32)
    vals, hots = [], []
    idx_out = jnp.zeros(l.shape, F32)
    for k in range(TOP_K):
        m = jnp.max(l, axis=-1, keepdims=True)
        idx = jnp.min(jnp.where(l == m, lane_f, float(LANES)), axis=-1, keepdims=True)
        hot = lane_f == idx
        vals.append(m)
        hots.append(hot)
        idx_out = jnp.where(lane == k, idx, idx_out)
        l = jnp.where(hot, -jnp.inf, l)
    exps = [jnp.exp(v - vals[0]) for v in vals]
    denom = exps[0] + exps[1] + exps[2] + exps[3]
    p_out = jnp.zeros(l.shape, F32)
    for k in range(TOP_K):
        p_out = jnp.where(lane == k, exps[k] / denom, p_out)
    member = jnp.where(hots[0] | hots[1] | hots[2] | hots[3], 1.0, 0.0).astype(F32)
    row = lax.broadcasted_iota(I32, (tm, tm), 0)
    col = lax.broadcasted_iota(I32, (tm, tm), 1)
    before = jnp.where(row > col, 1.0, 0.0).astype(BF16)
    prefix = jnp.dot(before, member.astype(BF16), preferred_element_type=F32) + carry[...]
    rank_out = jnp.zeros(l.shape, F32)
    for k in range(TOP_K):
        r = jnp.sum(jnp.where(hots[k], prefix, 0.0), axis=-1, keepdims=True)
        rank_out = jnp.where(lane == k, r, rank_out)
    carry[...] = carry[...] + jnp.sum(member, axis=0, keepdims=True)
    idx_ref[...] = idx_out.astype(I32)
    p_ref[...] = p_out
    rank_ref[...] = rank_out.astype(I32)
    cnt_ref[...] = carry[...]


def _route(logits_p, logits_s):
    tm = ROUTE_TOKENS
    pt = logits_p.shape[0] // tm
    n = logits_p.shape[0] + logits_s.shape[0]
    tile = pl.BlockSpec((tm, LANES), lambda i: (i, 0))
    return pl.pallas_call(
        functools.partial(_route_kernel, prompt_tiles=pt),
        out_shape=(jax.ShapeDtypeStruct((n, LANES), I32), jax.ShapeDtypeStruct((n, LANES), F32),
                   jax.ShapeDtypeStruct((n, LANES), I32), jax.ShapeDtypeStruct((1, LANES), F32)),
        grid=(n // tm,),
        in_specs=[pl.BlockSpec((tm, LANES), lambda i: (jnp.minimum(i, pt - 1), 0)),
                  pl.BlockSpec((tm, LANES), lambda i: (jnp.maximum(i - pt, 0), 0))],
        out_specs=(tile, tile, tile, pl.BlockSpec((1, LANES), lambda i: (0, 0))),
        scratch_shapes=[pltpu.VMEM((1, LANES), F32)],
        compiler_params=_params(("arbitrary",)),
        name="route",
    )(logits_p, logits_s)


DISPATCH_TOKENS = 128


def _dispatch_kernel(meta_ref, slot_ref, h2p_ref, h2s_ref, xs_ref, zbuf, zsem, sem, *, n_exp,
                     prompt_tiles):
    i = pl.program_id(0)
    pairs = DISPATCH_TOKENS * TOP_K
    tile = EXPERT_ROW_TILE
    n_tiles = xs_ref.shape[0] // tile

    def zero_copy(start):
        return pltpu.make_async_copy(zbuf, xs_ref.at[pl.ds(pl.multiple_of(start, tile), tile)], zsem)

    def pad_start(e):
        return meta_ref[e] + meta_ref[n_exp + e] - tile

    def needs_pad(e):
        return meta_ref[n_exp + e] > meta_ref[2 * n_exp + e]

    @pl.when(i == 0)
    def _():
        zbuf[...] = jnp.zeros_like(zbuf)
        n_active = meta_ref[3 * n_exp]

        def start_pad(e, carry):
            @pl.when(needs_pad(e))
            def _():
                zero_copy(pad_start(e)).start()
            return carry

        def wait_pad(e, carry):
            @pl.when(needs_pad(e))
            def _():
                zero_copy(pad_start(e)).wait()
            return carry

        def start_tail(t, carry):
            zero_copy(t * tile).start()
            return carry

        def wait_tail(t, carry):
            zero_copy(t * tile).wait()
            return carry

        lax.fori_loop(0, n_exp, start_pad, 0)
        lax.fori_loop(n_active, n_tiles, start_tail, 0)
        lax.fori_loop(0, n_exp, wait_pad, 0)
        lax.fori_loop(n_active, n_tiles, wait_tail, 0)

    def scatter_rows(src_ref, first_row):
        def row_copy(p):
            src = first_row + (p >> TOP_K_SHIFT)
            return pltpu.make_async_copy(src_ref.at[pl.ds(src, 1)], xs_ref.at[pl.ds(slot_ref[0, p], 1)],
                                         sem)

        def start(p, carry):
            row_copy(p).start()
            return carry

        def wait(p, carry):
            row_copy(p).wait()
            return carry

        lax.fori_loop(0, pairs, start, 0)
        lax.fori_loop(0, pairs, wait, 0)

    @pl.when(i < prompt_tiles)
    def _():
        scatter_rows(h2p_ref, i * DISPATCH_TOKENS)

    @pl.when(i >= prompt_tiles)
    def _():
        scatter_rows(h2s_ref, (i - prompt_tiles) * DISPATCH_TOKENS)


def _dispatch(h2_p, h2_s, slots, meta, n_rows, n_exp):
    d = h2_p.shape[1]
    pt = h2_p.shape[0] // DISPATCH_TOKENS
    nt = pt + h2_s.shape[0] // DISPATCH_TOKENS
    pairs = DISPATCH_TOKENS * TOP_K
    return pl.pallas_call(
        functools.partial(_dispatch_kernel, n_exp=n_exp, prompt_tiles=pt),
        out_shape=jax.ShapeDtypeStruct((n_rows, d), F32),
        grid_spec=pltpu.PrefetchScalarGridSpec(
            num_scalar_prefetch=1,
            grid=(nt,),
            in_specs=[pl.BlockSpec((None, 1, pairs), lambda i, meta: (i, 0, 0), memory_space=pltpu.SMEM),
                      pl.BlockSpec(memory_space=pl.ANY), pl.BlockSpec(memory_space=pl.ANY)],
            out_specs=pl.BlockSpec(memory_space=pl.ANY),
            scratch_shapes=[pltpu.VMEM((EXPERT_ROW_TILE, d), F32), pltpu.SemaphoreType.DMA(()),
                            pltpu.SemaphoreType.DMA(())]),
        compiler_params=_params(("arbitrary",)),
        name="dispatch",
    )(meta, slots.reshape(nt, 1, pairs), h2_p, h2_s)


def _gate_up_kernel(te_ref, na_ref, x_ref, wg_ref, wu_ref, bg_ref, bu_ref, o_ref, wg_b, wu_b):
    i = pl.program_id(1)
    active = i < na_ref[0]
    changed = jnp.logical_or(i == 0, te_ref[i] != te_ref[jnp.maximum(i - 1, 0)])

    @pl.when(jnp.logical_and(active, changed))
    def _():
        wg_b[...] = wg_ref[...].astype(BF16)
        wu_b[...] = wu_ref[...].astype(BF16)

    @pl.when(active)
    def _():
        x = x_ref[...].astype(BF16)
        gate = jnp.dot(x, wg_b[...], preferred_element_type=F32) + bg_ref[...]
        up = jnp.dot(x, wu_b[...], preferred_element_type=F32) + bu_ref[...]
        gate = jnp.minimum(gate, SWIGLU_LIMIT)
        up = jnp.clip(up, -SWIGLU_LIMIT, SWIGLU_LIMIT)
        o_ref[...] = ((up + 1.0) * gate * jax.nn.sigmoid(SWIGLU_ALPHA * gate)).astype(BF16)

    @pl.when(jnp.logical_not(active))
    def _():
        o_ref[...] = jnp.zeros_like(o_ref)


def _expert_gate_up(xs, w_gu, b_gu, tile_expert, n_active):
    n_rows, d = xs.shape
    n_exp, _, f2 = w_gu.shape
    f = f2 // 2
    tm = EXPERT_ROW_TILE
    tn = 1024
    nj = f // tn
    row = lambda i, na: jnp.minimum(i, na[0] - 1)
    return pl.pallas_call(
        _gate_up_kernel,
        out_shape=jax.ShapeDtypeStruct((n_rows, f), BF16),
        grid_spec=pltpu.PrefetchScalarGridSpec(
            num_scalar_prefetch=2,
            grid=(nj, n_rows // tm),
            in_specs=[pl.BlockSpec((tm, d), lambda j, i, te, na: (row(i, na), 0)),
                      pl.BlockSpec((None, d, tn), lambda j, i, te, na: (te[row(i, na)], 0, j)),
                      pl.BlockSpec((None, d, tn), lambda j, i, te, na: (te[row(i, na)], 0, nj + j)),
                      pl.BlockSpec((None, 1, tn), lambda j, i, te, na: (te[row(i, na)], 0, j)),
                      pl.BlockSpec((None, 1, tn), lambda j, i, te, na: (te[row(i, na)], 0, nj + j))],
            out_specs=pl.BlockSpec((tm, tn), lambda j, i, te, na: (i, j)),
            scratch_shapes=[pltpu.VMEM((d, tn), BF16), pltpu.VMEM((d, tn), BF16)]),
        compiler_params=_params(("arbitrary", "arbitrary")),
        name="expert_gate_up",
    )(tile_expert, n_active, xs, w_gu, w_gu, b_gu.reshape(n_exp, 1, f2), b_gu.reshape(n_exp, 1, f2))


def _down_kernel(te_ref, na_ref, a_ref, w_ref, b_ref, o_ref, w_b):
    i = pl.program_id(1)
    active = i < na_ref[0]
    changed = jnp.logical_or(i == 0, te_ref[i] != te_ref[jnp.maximum(i - 1, 0)])

    @pl.when(jnp.logical_and(active, changed))
    def _():
        w_b[...] = w_ref[...].astype(BF16)

    @pl.when(active)
    def _():
        o_ref[...] = jnp.dot(a_ref[...], w_b[...], preferred_element_type=F32) + b_ref[...]

    @pl.when(jnp.logical_not(active))
    def _():
        o_ref[...] = jnp.zeros_like(o_ref)


def _expert_down(act, w_d, b_d, tile_expert, n_active):
    n_rows, f = act.shape
    n_exp, _, d = w_d.shape
    tm = EXPERT_ROW_TILE
    tn = 1024
    row = lambda i, na: jnp.minimum(i, na[0] - 1)
    return pl.pallas_call(
        _down_kernel,
        out_shape=jax.ShapeDtypeStruct((n_rows, d), F32),
        grid_spec=pltpu.PrefetchScalarGridSpec(
            num_scalar_prefetch=2,
            grid=(d // tn, n_rows // tm),
            in_specs=[pl.BlockSpec((tm, f), lambda j, i, te, na: (row(i, na), 0)),
                      pl.BlockSpec((None, f, tn), lambda j, i, te, na: (te[row(i, na)], 0, j)),
                      pl.BlockSpec((None, 1, tn), lambda j, i, te, na: (te[row(i, na)], 0, j))],
            out_specs=pl.BlockSpec((tm, tn), lambda j, i, te, na: (i, j)),
            scratch_shapes=[pltpu.VMEM((f, tn), BF16)]),
        compiler_params=_params(("arbitrary", "arbitrary")),
        name="expert_down",
    )(tile_expert, n_active, act, w_d, b_d.reshape(n_exp, 1, d))


COMBINE_TOKENS = 128


def _combine_kernel(slot_ref, p_ref, x1_ref, gate_ref, g_ref, y_ref, o_ref, buf, sem):
    tm = COMBINE_TOKENS
    pairs = tm * TOP_K

    def row_copy(p):
        return pltpu.make_async_copy(y_ref.at[pl.ds(slot_ref[0, p], 1)],
                                     buf.at[p & (TOP_K - 1), pl.ds(p >> TOP_K_SHIFT, 1)], sem)

    def start(p, carry):
        row_copy(p).start()
        return carry

    def wait(p, carry):
        row_copy(p).wait()
        return carry

    lax.fori_loop(0, pairs, start, 0)
    lax.fori_loop(0, pairs, wait, 0)
    probs = p_ref[...]
    f = probs[:, 0:1] * buf[0]
    for k in range(1, TOP_K):
        f = f + probs[:, k:k + 1] * buf[k]
    o_ref[...] = x1_ref[...] + gate_ref[...] * (_rms(f) * g_ref[...])


def _combine(y, slots, probs, x1, gate, g_post, rows_per_group):
    m, d = x1.shape
    tm = COMBINE_TOKENS
    nt = m // tm
    pairs = tm * TOP_K
    return pl.pallas_call(
        _combine_kernel,
        out_shape=jax.ShapeDtypeStruct((m, d), F32),
        grid=(nt,),
        in_specs=[pl.BlockSpec((None, 1, pairs), lambda i: (i, 0, 0), memory_space=pltpu.SMEM),
                  pl.BlockSpec((tm, LANES), lambda i: (i, 0)),
                  pl.BlockSpec((tm, d), lambda i: (i, 0)),
                  _mod_spec(gate, tm, rows_per_group),
                  pl.BlockSpec((1, d), lambda i: (0, 0)),
                  pl.BlockSpec(memory_space=pl.ANY)],
        out_specs=pl.BlockSpec((tm, d), lambda i: (i, 0)),
        scratch_shapes=[pltpu.VMEM((TOP_K, tm, d), F32), pltpu.SemaphoreType.DMA(())],
        compiler_params=_params(("arbitrary",)),
        name="combine",
    )(slots.reshape(nt, 1, pairs), probs, x1, gate, g_post.reshape(1, d), y)


def _moe(h2_p, h2_s, logits_p, logits_s, n_exp, w_gu, b_gu, w_d, b_d):
    n = h2_p.shape[0] + h2_s.shape[0]
    tile = EXPERT_ROW_TILE
    top_i, probs, rank, counts = _route(logits_p, logits_s)
    counts = counts[0, :n_exp].astype(I32)
    cap = (counts + tile - 1) // tile * tile
    ends = jnp.cumsum(cap)
    offs = ends - cap
    n_tiles = (n * TOP_K) // tile + n_exp
    n_active = (ends[-1] // tile).astype(I32).reshape(1)
    tile_expert = jnp.minimum(
        jnp.searchsorted(ends, jnp.arange(n_tiles, dtype=I32) * tile, side="right"), n_exp - 1).astype(I32)
    slots = offs[top_i[:, :TOP_K]] + rank[:, :TOP_K]
    meta = jnp.concatenate([offs, cap, counts, n_active]).astype(I32)
    xs = _dispatch(h2_p, h2_s, slots, meta, n_tiles * tile, n_exp)
    act = _expert_gate_up(xs, w_gu, b_gu, tile_expert, n_active)
    y = _expert_down(act, w_d, b_d, tile_expert, n_active)
    return y, slots, probs


def _layer(xp, xs_, cp_mod, cs_mod, hist_a, hist_q, s0, g_pre_mix, g_post_mix, g_pre_ffn, g_post_ffn,
           w_in, conv_a_w, gdn_conv_w, a_log, dt_bias, g_conv_out, gdn_norm_g, w_out, router_w, router_b,
           w_gu, b_gu, w_d, b_d):
    bsz, seq, d = xp.shape
    ns = xs_.shape[0]
    n_exp = router_w.shape[1]
    dc = N_HEADS * HEAD_D
    d_main = 3 * dc + 3 * dc + dc
    xp2 = xp.reshape(bsz * seq, d)
    xs2 = xs_.reshape(ns, d)

    w_main = w_in[:, :d_main].astype(BF16)
    w_ba = jnp.zeros((d, LANES), BF16).at[:, :2 * N_HEADS].set(w_in[:, d_main:].astype(BF16))
    w_out_b = w_out.astype(BF16)
    rw = jnp.zeros((d, LANES), F32).at[:, :n_exp].set(router_w)
    rw_hi = rw.astype(BF16)
    rw_lo = (rw - rw_hi.astype(F32)).astype(BF16)
    rb = jnp.full((1, LANES), NEG_BIG, F32).at[0, :n_exp].set(router_b)

    mp = [cp_mod[:, i * d:(i + 1) * d].reshape(bsz, 1, d) for i in range(6)]
    ms = [cs_mod[:, i * d:(i + 1) * d] for i in range(6)]

    proj_p, ba_p = _in_proj(xp2, mp[1], mp[0], g_pre_mix, w_main, w_ba, seq)
    proj_s, ba_s = _in_proj(xs2, ms[1], ms[0], g_pre_mix, w_main, w_ba, 1)

    mix_p, ha_p, hq_p, s_p = _mixer_prompt(proj_p, ba_p, bsz, seq, conv_a_w, gdn_conv_w, a_log, dt_bias,
                                           g_conv_out, gdn_norm_g)
    mix_s, ha_s, hq_s, s_s = _mixer_sample(proj_s, ba_s, hist_a.reshape(ns, 2 * dc),
                                           hist_q.reshape(ns, 9 * dc), s0, conv_a_w, gdn_conv_w, a_log,
                                           dt_bias, g_conv_out, gdn_norm_g)

    x1_p, h2_p, lg_p = _post_mix(mix_p, xp2, mp[2], mp[4], mp[3], g_post_mix, g_pre_ffn, w_out_b, rw_hi,
                                 rw_lo, rb, seq)
    x1_s, h2_s, lg_s = _post_mix(mix_s, xs2, ms[2], ms[4], ms[3], g_post_mix, g_pre_ffn, w_out_b, rw_hi,
                                 rw_lo, rb, 1)

    y, slots, probs = _moe(h2_p, h2_s, lg_p, lg_s, n_exp, w_gu, b_gu, w_d, b_d)
    np_ = bsz * seq
    out_p = _combine(y, slots[:np_], probs[:np_], x1_p, mp[5], g_post_ffn, seq)
    out_s = _combine(y, slots[np_:], probs[np_:], x1_s, ms[5], g_post_ffn, 1)
    return (out_p.reshape(bsz, seq, d), out_s.reshape(ns, 1, d), ha_p, hq_p, s_p,
            ha_s.reshape(ns, 2, dc), hq_s.reshape(ns, 3, 3 * dc), s_s)


def kernel(x_prompt, x_sample, state_conv_a, state_gdn_conv, state_gdn_S, c_prompt, c_sample, w_mod, b_mod, g_pre_mix, g_post_mix, g_pre_ffn, g_post_ffn, w_in, conv_a_w, gdn_conv_w, gdn_a_log, gdn_dt_bias, g_conv_out, gdn_norm_g, w_out, router_w, router_b, exp_w_gate_up, exp_b_gate_up, exp_w_down, exp_b_down):
    depth = w_mod.shape[0]
    bp = x_prompt.shape[0]
    xp, xs_ = x_prompt, x_sample
    outs = [[] for _ in range(6)]
    for l in range(depth):
        mod = _modulation(jnp.concatenate([c_prompt, c_sample], axis=0), w_mod[l], b_mod[l])
        res = _layer(xp, xs_, mod[:bp], mod[bp:], state_conv_a[l], state_gdn_conv[l], state_gdn_S[l],
                     g_pre_mix[l], g_post_mix[l], g_pre_ffn[l], g_post_ffn[l], w_in[l], conv_a_w[l],
                     gdn_conv_w[l], gdn_a_log[l], gdn_dt_bias[l], g_conv_out[l], gdn_norm_g[l], w_out[l],
                     router_w[l], router_b[l], exp_w_gate_up[l], exp_b_gate_up[l], exp_w_down[l],
                     exp_b_down[l])
        xp, xs_ = res[0], res[1]
        for acc, r in zip(outs, res[2:]):
            acc.append(r)
    return (xp, xs_) + tuple(jnp.stack(o) for o in outs)
```

```python
import functools

import jax
import jax.numpy as jnp
from jax import lax
from jax.experimental import pallas as pl
from jax.experimental.pallas import tpu as pltpu

F32 = jnp.float32
BF16 = jnp.bfloat16
I32 = jnp.int32
HIGHEST = lax.Precision.HIGHEST

EPS = 1e-6
N_HEADS = 8
HEAD_D = 128
TOP_K = 4
TOP_K_SHIFT = 2
SWIGLU_LIMIT = 7.0
SWIGLU_ALPHA = 1.702
CHUNK = 64
LANES = 128
EXPERT_ROW_TILE = 256
NEG_BIG = -1e30
VMEM_LIMIT = 56 * 1024 * 1024


def _params(semantics, vmem=VMEM_LIMIT):
    return pltpu.CompilerParams(dimension_semantics=semantics, vmem_limit_bytes=vmem)


def _mm(a, b):
    return jnp.dot(a.astype(BF16), b.astype(BF16), preferred_element_type=F32)


def _mm_nt(a, b):
    return lax.dot_general(a.astype(BF16), b.astype(BF16), (((1,), (1,)), ((), ())),
                           preferred_element_type=F32)


def _rms(x):
    return x * lax.rsqrt(jnp.mean(x * x, axis=-1, keepdims=True) + EPS)


def _silu(x):
    return x * jax.nn.sigmoid(x)


def _softplus(x):
    return jnp.maximum(x, 0.0) + jnp.log1p(jnp.exp(-jnp.abs(x)))


def _mod_kernel(c_ref, w_ref, b_ref, o_ref):
    s = _silu(c_ref[...])
    o_ref[...] = _mm(s, w_ref[...]) + b_ref[...]


def _modulation(c_all, w_mod, b_mod):
    n, d = c_all.shape
    m = w_mod.shape[1]
    tn = 1024
    return pl.pallas_call(
        _mod_kernel,
        out_shape=jax.ShapeDtypeStruct((n, m), F32),
        grid=(m // tn,),
        in_specs=[pl.BlockSpec((n, d), lambda j: (0, 0)),
                  pl.BlockSpec((d, tn), lambda j: (0, j)),
                  pl.BlockSpec((1, tn), lambda j: (0, j))],
        out_specs=pl.BlockSpec((n, tn), lambda j: (0, j)),
        compiler_params=_params(("arbitrary",)),
        name="modulation",
    )(c_all, w_mod, b_mod.reshape(1, m))


def _mod_spec(arr, tm, rows_per_group):
    if arr.ndim == 3:
        tiles = rows_per_group // tm
        return pl.BlockSpec((None, 1, arr.shape[-1]), lambda i, *_: (i // tiles, 0, 0))
    return pl.BlockSpec((tm, arr.shape[-1]), lambda i, *_: (i, 0))


def _proj_kernel(x_ref, sc_ref, sh_ref, g_ref, w_ref, wba_ref, o_ref, ba_ref, h_scr):
    @pl.when(pl.program_id(1) == 0)
    def _():
        h = (_rms(x_ref[...]) * g_ref[...]) * (1.0 + sc_ref[...]) + sh_ref[...]
        hb = h.astype(BF16)
        h_scr[...] = hb
        ba_ref[...] = jnp.dot(hb, wba_ref[...], preferred_element_type=F32)

    o_ref[...] = jnp.dot(h_scr[...], w_ref[...], preferred_element_type=F32)


def _in_proj(x, scale, shift, g, w_main, w_ba, rows_per_group):
    m, d = x.shape
    n = w_main.shape[1]
    tm = min(m, 1024, rows_per_group if scale.ndim == 3 else m)
    tn = 1024
    return pl.pallas_call(
        _proj_kernel,
        out_shape=(jax.ShapeDtypeStruct((m, n), F32), jax.ShapeDtypeStruct((m, LANES), F32)),
        grid=(m // tm, n // tn),
        in_specs=[pl.BlockSpec((tm, d), lambda i, j: (i, 0)),
                  _mod_spec(scale, tm, rows_per_group),
                  _mod_spec(shift, tm, rows_per_group),
                  pl.BlockSpec((1, d), lambda i, j: (0, 0)),
                  pl.BlockSpec((d, tn), lambda i, j: (0, j)),
                  pl.BlockSpec((d, LANES), lambda i, j: (0, 0))],
        out_specs=(pl.BlockSpec((tm, tn), lambda i, j: (i, j)),
                   pl.BlockSpec((tm, LANES), lambda i, j: (i, 0))),
        scratch_shapes=[pltpu.VMEM((tm, d), BF16)],
        compiler_params=_params(("arbitrary", "arbitrary")),
        name="in_proj",
    )(x, scale, shift, g.reshape(1, d), w_main, w_ba)


def _mixer_prompt_kernel(proj_ref, ba_ref, caw_ref, gcw_ref, alog_ref, dtb_ref, gco_ref, gng_ref,
                         mix_ref, ha_ref, hq_ref, s_ref, extu, extq, qc_scr, s_scr):
    c = CHUNK
    dc = N_HEADS * HEAD_D
    dq = 3 * dc
    t = pl.program_id(1)

    @pl.when(t == 0)
    def _():
        extu[0:8, :] = jnp.zeros((8, dc), F32)
        extq[0:8, :] = jnp.zeros((8, dq), F32)
        s_scr[...] = jnp.zeros_like(s_scr)

    u = proj_ref[:, dc:2 * dc] * proj_ref[:, 2 * dc:3 * dc]
    extu[8:8 + c, :] = u
    caw = caw_ref[...]
    ya = caw[0:1] * extu[6:6 + c, :] + caw[1:2] * extu[7:7 + c, :] + caw[2:3] * u
    ya = proj_ref[:, 0:dc] * ya
    mix_ref[:, 0:dc] = (_rms(ya) * gco_ref[...]).astype(BF16)
    last_u = extu[6 + c:8 + c, :]
    extu[6:8, :] = last_u

    qkv = proj_ref[:, 3 * dc:3 * dc + dq]
    extq[8:8 + c, :] = qkv
    gcw = gcw_ref[...]
    qc = (gcw[0:1] * extq[5:5 + c, :] + gcw[1:2] * extq[6:6 + c, :]
          + gcw[2:3] * extq[7:7 + c, :] + gcw[3:4] * qkv)
    qc_scr[...] = _silu(qc)
    last_q = extq[5 + c:8 + c, :]
    extq[5:8, :] = last_q

    @pl.when(t == pl.num_programs(1) - 1)
    def _():
        ha_ref[...] = last_u
        hq_ref[...] = last_q

    ba = ba_ref[...]
    beta_all = jax.nn.sigmoid(ba)
    g_all = -jnp.exp(alog_ref[...]) * _softplus(ba + dtb_ref[...])
    row = lax.broadcasted_iota(I32, (c, c), 0)
    col = lax.broadcasted_iota(I32, (c, c), 1)
    causal = row >= col
    strict = row > col
    lower = jnp.where(causal, 1.0, 0.0).astype(F32)
    upper = jnp.where(row <= col, 1.0, 0.0).astype(F32)
    gc_all = jnp.dot(lower, g_all, precision=HIGHEST, preferred_element_type=F32)
    gc_t = lax.dot_general(g_all, upper, (((0,), (0,)), ((), ())), precision=HIGHEST,
                           preferred_element_type=F32)

    for h in range(N_HEADS):
        lo = h * HEAD_D
        q = qc_scr[:, lo:lo + HEAD_D]
        k = qc_scr[:, dc + lo:dc + lo + HEAD_D]
        v = qc_scr[:, 2 * dc + lo:2 * dc + lo + HEAD_D]
        qn = q * lax.rsqrt(jnp.sum(q * q, axis=-1, keepdims=True) + EPS) * (HEAD_D ** -0.5)
        kn = k * lax.rsqrt(jnp.sum(k * k, axis=-1, keepdims=True) + EPS)
        beta = beta_all[:, h:h + 1]
        gcc = gc_all[:, N_HEADS + h:N_HEADS + h + 1]
        gcr = gc_t[N_HEADS + h:N_HEADS + h + 1, :]
        gl = gc_all[c - 1:c, N_HEADS + h:N_HEADS + h + 1]
        decay = jnp.where(causal, jnp.exp(jnp.minimum(gcc - gcr, 0.0)), 0.0)
        eg = jnp.exp(gcc)
        kb = kn * beta
        a_mat = jnp.where(strict, _mm_nt(kb, kn) * decay, 0.0)
        qk = _mm_nt(qn, kn) * decay
        n_mat = -a_mat
        p = a_mat
        size = 2
        while size < c:
            p = _mm(p, p)
            n_mat = n_mat + p + _mm(n_mat, p)
            size *= 2
        rhs = jnp.concatenate([v * beta, kb * eg], axis=-1)
        uw = rhs + _mm(n_mat, rhs)
        s_old = s_scr[h]
        s_b = s_old.astype(BF16)
        v_new = uw[:, :HEAD_D] - _mm(uw[:, HEAD_D:], s_b)
        o = _mm(qn * eg, s_b) + _mm(qk, v_new)
        kg = kn * jnp.exp(gl - gcc)
        s_scr[h] = s_old * jnp.exp(gl) + _mm(kg.T, v_new)
        z = proj_ref[:, 3 * dc + dq + lo:3 * dc + dq + lo + HEAD_D]
        yb = _rms(o) * gng_ref[...] * _silu(z)
        mix_ref[:, dc + lo:dc + lo + HEAD_D] = yb.astype(BF16)

    @pl.when(t == pl.num_programs(1) - 1)
    def _():
        s_ref[...] = s_scr[...]


def _lane_row(vec, offset):
    return jnp.zeros((1, LANES), F32).at[0, offset:offset + vec.shape[0]].set(vec.astype(F32))


def _mixer_prompt(proj, ba, bsz, seq, conv_a_w, gdn_conv_w, a_log, dt_bias, g_conv_out, gdn_norm_g):
    c = CHUNK
    dc = N_HEADS * HEAD_D
    dq = 3 * dc
    dproj = proj.shape[1]
    nt = seq // c
    const = lambda shape: pl.BlockSpec(shape, lambda b, t: (0,) * len(shape))
    return pl.pallas_call(
        _mixer_prompt_kernel,
        out_shape=(jax.ShapeDtypeStruct((bsz * seq, 2 * dc), BF16),
                   jax.ShapeDtypeStruct((bsz, 2, dc), F32),
                   jax.ShapeDtypeStruct((bsz, 3, dq), F32),
                   jax.ShapeDtypeStruct((bsz, N_HEADS, HEAD_D, HEAD_D), F32)),
        grid=(bsz, nt),
        in_specs=[pl.BlockSpec((c, dproj), lambda b, t: (b * nt + t, 0)),
                  pl.BlockSpec((c, LANES), lambda b, t: (b * nt + t, 0)),
                  const((3, dc)), const((4, dq)), const((1, LANES)), const((1, LANES)),
                  const((1, dc)), const((1, HEAD_D))],
        out_specs=(pl.BlockSpec((c, 2 * dc), lambda b, t: (b * nt + t, 0)),
                   pl.BlockSpec((None, 2, dc), lambda b, t: (b, 0, 0)),
                   pl.BlockSpec((None, 3, dq), lambda b, t: (b, 0, 0)),
                   pl.BlockSpec((None, N_HEADS, HEAD_D, HEAD_D), lambda b, t: (b, 0, 0, 0))),
        scratch_shapes=[pltpu.VMEM((8 + c, dc), F32), pltpu.VMEM((8 + c, dq), F32),
                        pltpu.VMEM((c, dq), F32), pltpu.VMEM((N_HEADS, HEAD_D, HEAD_D), F32)],
        compiler_params=_params(("arbitrary", "arbitrary")),
        name="mixer_prompt",
    )(proj, ba, conv_a_w, gdn_conv_w, _lane_row(a_log, N_HEADS), _lane_row(dt_bias, N_HEADS),
      g_conv_out.reshape(1, dc), gdn_norm_g.reshape(1, HEAD_D))


SAMPLE_GROUP = 16


def _mixer_sample_kernel(proj_ref, ba_ref, hista_ref, histq_ref, s_in_ref, caw_ref, gcw_ref, alog_ref,
                         dtb_ref, gco_ref, gng_ref, mix_ref, ha_ref, hq_ref, s_out_ref, qc_scr, o_scr):
    tb = SAMPLE_GROUP
    dc = N_HEADS * HEAD_D
    dq = 3 * dc

    u = proj_ref[:, dc:2 * dc] * proj_ref[:, 2 * dc:3 * dc]
    caw = caw_ref[...]
    ya = caw[0:1] * hista_ref[:, 0:dc] + caw[1:2] * hista_ref[:, dc:2 * dc] + caw[2:3] * u
    ya = proj_ref[:, 0:dc] * ya
    mix_ref[:, 0:dc] = _rms(ya) * gco_ref[...]
    ha_ref[:, 0:dc] = hista_ref[:, dc:2 * dc]
    ha_ref[:, dc:2 * dc] = u

    qkv = proj_ref[:, 3 * dc:3 * dc + dq]
    gcw = gcw_ref[...]
    qc = (gcw[0:1] * histq_ref[:, 0:dq] + gcw[1:2] * histq_ref[:, dq:2 * dq]
          + gcw[2:3] * histq_ref[:, 2 * dq:3 * dq] + gcw[3:4] * qkv)
    qc_scr[...] = _silu(qc)
    hq_ref[:, 0:dq] = histq_ref[:, dq:2 * dq]
    hq_ref[:, dq:2 * dq] = histq_ref[:, 2 * dq:3 * dq]
    hq_ref[:, 2 * dq:3 * dq] = qkv

    ba = ba_ref[...]
    beta_all = jax.nn.sigmoid(ba)
    eg_all = jnp.exp(-jnp.exp(alog_ref[...]) * _softplus(ba + dtb_ref[...]))

    for h in range(N_HEADS):
        lo = h * HEAD_D
        q = qc_scr[:, lo:lo + HEAD_D]
        k = qc_scr[:, dc + lo:dc + lo + HEAD_D]
        v = qc_scr[:, 2 * dc + lo:2 * dc + lo + HEAD_D]
        qn = q * lax.rsqrt(jnp.sum(q * q, axis=-1, keepdims=True) + EPS) * (HEAD_D ** -0.5)
        kn = k * lax.rsqrt(jnp.sum(k * k, axis=-1, keepdims=True) + EPS)
        qk = jnp.sum(qn * kn, axis=-1, keepdims=True)
        kn_t = kn.T
        qn_t = qn.T
        for b in range(tb):
            s_old = s_in_ref[b, h]
            kc = kn_t[:, b:b + 1]
            e = eg_all[b:b + 1, N_HEADS + h:N_HEADS + h + 1]
            ks = jnp.sum(s_old * kc, axis=0, keepdims=True)
            qs = jnp.sum(s_old * qn_t[:, b:b + 1], axis=0, keepdims=True)
            v_new = beta_all[b:b + 1, h:h + 1] * (v[b:b + 1, :] - e * ks)
            o_scr[b:b + 1, lo:lo + HEAD_D] = e * qs + qk[b:b + 1, :] * v_new
            s_out_ref[b, h] = s_old * e + kc * v_new
        z = proj_ref[:, 3 * dc + dq + lo:3 * dc + dq + lo + HEAD_D]
        o = o_scr[:, lo:lo + HEAD_D]
        mix_ref[:, dc + lo:dc + lo + HEAD_D] = _rms(o) * gng_ref[...] * _silu(z)


def _mixer_sample(proj, ba, hist_a, hist_q, s_in, conv_a_w, gdn_conv_w, a_log, dt_bias, g_conv_out,
                  gdn_norm_g):
    n = proj.shape[0]
    tb = SAMPLE_GROUP
    dc = N_HEADS * HEAD_D
    dq = 3 * dc
    dproj = proj.shape[1]
    const = lambda shape: pl.BlockSpec(shape, lambda i: (0,) * len(shape))
    rows = lambda width: pl.BlockSpec((tb, width), lambda i: (i, 0))
    state = pl.BlockSpec((tb, N_HEADS, HEAD_D, HEAD_D), lambda i: (i, 0, 0, 0))
    return pl.pallas_call(
        _mixer_sample_kernel,
        out_shape=(jax.ShapeDtypeStruct((n, 2 * dc), F32),
                   jax.ShapeDtypeStruct((n, 2 * dc), F32),
                   jax.ShapeDtypeStruct((n, 3 * dq), F32),
                   jax.ShapeDtypeStruct((n, N_HEADS, HEAD_D, HEAD_D), F32)),
        grid=(n // tb,),
        in_specs=[rows(dproj), rows(LANES), rows(2 * dc), rows(3 * dq), state,
                  const((3, dc)), const((4, dq)), const((1, LANES)), const((1, LANES)),
                  const((1, dc)), const((1, HEAD_D))],
        out_specs=(rows(2 * dc), rows(2 * dc), rows(3 * dq), state),
        scratch_shapes=[pltpu.VMEM((tb, dq), F32), pltpu.VMEM((tb, dc), F32)],
        compiler_params=_params(("arbitrary",)),
        name="mixer_sample",
    )(proj, ba, hist_a, hist_q, s_in, conv_a_w, gdn_conv_w, _lane_row(a_log, N_HEADS),
      _lane_row(dt_bias, N_HEADS), g_conv_out.reshape(1, dc), gdn_norm_g.reshape(1, HEAD_D))


def _post_mix_kernel(mix_ref, x_ref, gate_ref, sc_ref, sh_ref, gpost_ref, gpre_ref, wout_ref,
                     rwh_ref, rwl_ref, rb_ref, x1_ref, h2_ref, lg_ref):
    mix =jnp.dot(mix_ref[...].astype(BF16), wout_ref[...], preferred_element_type=F32)
    x1 = x_ref[...] + gate_ref[...] * (_rms(mix) * gpost_ref[...])
    x1_ref[...] = x1
    h2 = (_rms(x1) * gpre_ref[...]) * (1.0 + sc_ref[...]) + sh_ref[...]
    h2_ref[...] = h2
    hi = h2.astype(BF16)
    lo = (h2 - hi.astype(F32)).astype(BF16)
    rwh = rwh_ref[...]
    lg_ref[...] = (jnp.dot(hi, rwh, preferred_element_type=F32)
                   + jnp.dot(lo, rwh, preferred_element_type=F32)
                   + jnp.dot(hi, rwl_ref[...], preferred_element_type=F32) + rb_ref[...])


def _post_mix(mix_in, x, gate, scale, shift, g_post, g_pre, w_out, rw_hi, rw_lo, rb, rows_per_group):
    m, d = x.shape
    tm = min(m, 512, rows_per_group if gate.ndim == 3 else m)
    const = lambda shape: pl.BlockSpec(shape, lambda i: (0,) * len(shape))
    rows = lambda width: pl.BlockSpec((tm, width), lambda i: (i, 0))
    return pl.pallas_call(
        _post_mix_kernel,
        out_shape=(jax.ShapeDtypeStruct((m, d), F32), jax.ShapeDtypeStruct((m, d), F32),
                   jax.ShapeDtypeStruct((m, LANES), F32)),
        grid=(m // tm,),
        in_specs=[rows(d), rows(d),
                  _mod_spec(gate, tm, rows_per_group),
                  _mod_spec(scale, tm, rows_per_group),
                  _mod_spec(shift, tm, rows_per_group),
                  const((1, d)), const((1, d)), const((d, d)),
                  const((d, LANES)), const((d, LANES)), const((1, LANES))],
        out_specs=(rows(d), rows(d), rows(LANES)),
        compiler_params=_params(("arbitrary",)),
        name="post_mix",
    )(mix_in, x, gate, scale, shift, g_post.reshape(1, d), g_pre.reshape(1, d), w_out, rw_hi, rw_lo, rb)


ROUTE_TOKENS = 128


def _route_kernel(lgp_ref, lgs_ref, idx_ref, p_ref, rank_ref, cnt_ref, carry, *, prompt_tiles):
    tm = lgp_ref.shape[0]

    @pl.when(pl.program_id(0) == 0)
    def _():
        carry[...] = jnp.zeros_like(carry)

    l = jnp.where(pl.program_id(0) < prompt_tiles, lgp_ref[...], lgs_ref[...])
    lane = lax.broadcasted_iota(I32, l.shape, 1)
    lane_f = lane.astype(F32)
    vals, hots = [], []
    idx_out = jnp.zeros(l.shape, F32)
    for k in range(TOP_K):
        m = jnp.max(l, axis=-1, keepdims=True)
        idx = jnp.min(jnp.where(l == m, lane_f, float(LANES)), axis=-1, keepdims=True)
        hot = lane_f == idx
        vals.append(m)
        hots.append(hot)
        idx_out = jnp.where(lane == k, idx, idx_out)
        l = jnp.where(hot, -jnp.inf, l)
    exps = [jnp.exp(v - vals[0]) for v in vals]
    denom = exps[0] + exps[1] + exps[2] + exps[3]
    p_out = jnp.zeros(l.shape, F32)
    for k in range(TOP_K):
        p_out = jnp.where(lane == k, exps[k] / denom, p_out)
    member = jnp.where(hots[0] | hots[1] | hots[2] | hots[3], 1.0, 0.0).astype(F32)
    row = lax.broadcasted_iota(I32, (tm, tm), 0)
    col = lax.broadcasted_iota(I32, (tm, tm), 1)
    before = jnp.where(row > col, 1.0, 0.0).astype(BF16)
    prefix = jnp.dot(before, member.astype(BF16), preferred_element_type=F32) + carry[...]
    rank_out = jnp.zeros(l.shape, F32)
    for k in range(TOP_K):
        r = jnp.sum(jnp.where(hots[k], prefix, 0.0), axis=-1, keepdims=True)
        rank_out = jnp.where(lane == k, r, rank_out)
    carry[...] = carry[...] + jnp.sum(member, axis=0, keepdims=True)
    idx_ref[...] = idx_out.astype(I32)
    p_ref[...] = p_out
    rank_ref[...] = rank_out.astype(I32)
    cnt_ref[...] = carry[...]


def _route(logits_p, logits_s):
    tm = ROUTE_TOKENS
    pt = logits_p.shape[0] // tm
    n = logits_p.shape[0] + logits_s.shape[0]
    tile = pl.BlockSpec((tm, LANES), lambda i: (i, 0))
    return pl.pallas_call(
        functools.partial(_route_kernel, prompt_tiles=pt),
        out_shape=(jax.ShapeDtypeStruct((n, LANES), I32), jax.ShapeDtypeStruct((n, LANES), F32),
                   jax.ShapeDtypeStruct((n, LANES), I32), jax.ShapeDtypeStruct((1, LANES), F32)),
        grid=(n // tm,),
        in_specs=[pl.BlockSpec((tm, LANES), lambda i: (jnp.minimum(i, pt - 1), 0)),
                  pl.BlockSpec((tm, LANES), lambda i: (jnp.maximum(i - pt, 0), 0))],
        out_specs=(tile, tile, tile, pl.BlockSpec((1, LANES), lambda i: (0, 0))),
        scratch_shapes=[pltpu.VMEM((1, LANES), F32)],
        compiler_params=_params(("arbitrary",)),
        name="route",
    )(logits_p, logits_s)


INVERT_TOKENS = 128


def _invert_kernel(slot_ref, table_ref, *, pad_token):
    i = pl.program_id(0)
    pairs = INVERT_TOKENS * TOP_K

    @pl.when(i == 0)
    def _():
        def fill(r, carry):
            table_ref[r] = pad_token
            return carry

        lax.fori_loop(0, table_ref.shape[0], fill, 0, unroll=8)

    def put(p, carry):
        table_ref[slot_ref[0, p]] = i * INVERT_TOKENS + (p >> TOP_K_SHIFT)
        return carry

    lax.fori_loop(0, pairs, put, 0, unroll=8)


def _invert(slots, n_rows, pad_token):
    n = slots.shape[0]
    nt = n // INVERT_TOKENS
    pairs = INVERT_TOKENS * TOP_K
    return pl.pallas_call(
        functools.partial(_invert_kernel, pad_token=pad_token),
        out_shape=jax.ShapeDtypeStruct((n_rows,), I32),
        grid=(nt,),
        in_specs=[pl.BlockSpec((None, 1, pairs), lambda i: (i, 0, 0), memory_space=pltpu.SMEM)],
        out_specs=pl.BlockSpec(memory_space=pltpu.SMEM),
        compiler_params=_params(("arbitrary",)),
        name="invert",
    )(slots.reshape(nt, 1, pairs))


def _dispatch_kernel(na_ref, split_ref, tok_ref, h2p_ref, h2s_ref, xs_ref, buf, sem, *, n_prompt):
    i = pl.program_id(0)
    tile = EXPERT_ROW_TILE

    @pl.when(i < na_ref[0])
    def _():
        split = split_ref[i]

        def prompt_copy(r):
            return pltpu.make_async_copy(h2p_ref.at[pl.ds(tok_ref[0, r], 1)], buf.at[pl.ds(r, 1)], sem)

        def sample_copy(r):
            return pltpu.make_async_copy(h2s_ref.at[pl.ds(tok_ref[0, r] - n_prompt, 1)],
                                         buf.at[pl.ds(r, 1)], sem)

        def loop(lo, hi, fn):
            def body(r, carry):
                fn(r)
                return carry
            lax.fori_loop(lo, hi, body, 0)

        loop(0, split, lambda r: prompt_copy(r).start())
        loop(split, tile, lambda r: sample_copy(r).start())
        loop(0, split, lambda r: prompt_copy(r).wait())
        loop(split, tile, lambda r: sample_copy(r).wait())
        xs_ref[...] = buf[...].astype(BF16)

    @pl.when(i >= na_ref[0])
    def _():
        xs_ref[...] = jnp.zeros_like(xs_ref)


def _dispatch(h2_p, h2_s, table, split, n_active):
    d = h2_p.shape[1]
    tile = EXPERT_ROW_TILE
    n_rows = table.shape[0]
    nt = n_rows // tile
    return pl.pallas_call(
        functools.partial(_dispatch_kernel, n_prompt=h2_p.shape[0]),
        out_shape=jax.ShapeDtypeStruct((n_rows, d), BF16),
        grid_spec=pltpu.PrefetchScalarGridSpec(
            num_scalar_prefetch=2,
            grid=(nt,),
            in_specs=[pl.BlockSpec((None, 1, tile), lambda i, na, sp: (i, 0, 0), memory_space=pltpu.SMEM),
                      pl.BlockSpec(memory_space=pl.ANY), pl.BlockSpec(memory_space=pl.ANY)],
            out_specs=pl.BlockSpec((tile, d), lambda i, na, sp: (i, 0)),
            scratch_shapes=[pltpu.VMEM((tile, d), F32), pltpu.SemaphoreType.DMA(())]),
        compiler_params=_params(("arbitrary",)),
        name="dispatch",
    )(n_active, split, table.reshape(nt, 1, tile), h2_p, h2_s)


def _gate_up_kernel(te_ref, na_ref, x_ref, wg_ref, wu_ref, bg_ref, bu_ref, o_ref, wg_b, wu_b):
    i = pl.program_id(1)
    active = i < na_ref[0]
    changed = jnp.logical_or(i == 0, te_ref[i] != te_ref[jnp.maximum(i - 1, 0)])

    @pl.when(jnp.logical_and(active, changed))
    def _():
        wg_b[...] = wg_ref[...].astype(BF16)
        wu_b[...] = wu_ref[...].astype(BF16)

    @pl.when(active)
    def _():
        x = x_ref[...]
        gate = jnp.dot(x, wg_b[...], preferred_element_type=F32) + bg_ref[...]
        up = jnp.dot(x, wu_b[...], preferred_element_type=F32) + bu_ref[...]
        gate = jnp.minimum(gate, SWIGLU_LIMIT)
        up = jnp.clip(up, -SWIGLU_LIMIT, SWIGLU_LIMIT)
        o_ref[...] = ((up + 1.0) * gate * jax.nn.sigmoid(SWIGLU_ALPHA * gate)).astype(BF16)

    @pl.when(jnp.logical_not(active))
    def _():
        o_ref[...] = jnp.zeros_like(o_ref)


def _expert_gate_up(xs, w_gu, b_gu, tile_expert, n_active):
    n_rows, d = xs.shape
    n_exp, _, f2 = w_gu.shape
    f = f2 // 2
    tm = EXPERT_ROW_TILE
    tn = 1024
    nj = f // tn
    row = lambda i, na: jnp.minimum(i, na[0] - 1)
    return pl.pallas_call(
        _gate_up_kernel,
        out_shape=jax.ShapeDtypeStruct((n_rows, f), BF16),
        grid_spec=pltpu.PrefetchScalarGridSpec(
            num_scalar_prefetch=2,
            grid=(nj, n_rows // tm),
            in_specs=[pl.BlockSpec((tm, d), lambda j, i, te, na: (row(i, na), 0)),
                      pl.BlockSpec((None, d, tn), lambda j, i, te, na: (te[row(i, na)], 0, j)),
                      pl.BlockSpec((None, d, tn), lambda j, i, te, na: (te[row(i, na)], 0, nj + j)),
                      pl.BlockSpec((None, 1, tn), lambda j, i, te, na: (te[row(i, na)], 0, j)),
                      pl.BlockSpec((None, 1, tn), lambda j, i, te, na: (te[row(i, na)], 0, nj + j))],
            out_specs=pl.BlockSpec((tm, tn), lambda j, i, te, na: (i, j)),
            scratch_shapes=[pltpu.VMEM((d, tn), BF16), pltpu.VMEM((d, tn), BF16)]),
        compiler_params=_params(("arbitrary", "arbitrary")),
        name="expert_gate_up",
    )(tile_expert, n_active, xs, w_gu, w_gu, b_gu.reshape(n_exp, 1, f2), b_gu.reshape(n_exp, 1, f2))


def _down_kernel(te_ref, na_ref, a_ref, w_ref, b_ref, o_ref, w_b):
    i = pl.program_id(1)
    active = i < na_ref[0]
    changed = jnp.logical_or(i == 0, te_ref[i] != te_ref[jnp.maximum(i - 1, 0)])

    @pl.when(jnp.logical_and(active, changed))
    def _():
        w_b[...] = w_ref[...].astype(BF16)

    @pl.when(active)
    def _():
        o_ref[...] = jnp.dot(a_ref[...], w_b[...], preferred_element_type=F32) + b_ref[...]

    @pl.when(jnp.logical_not(active))
    def _():
        o_ref[...] = jnp.zeros_like(o_ref)


def _expert_down(act, w_d, b_d, tile_expert, n_active):
    n_rows, f = act.shape
    n_exp, _, d = w_d.shape
    tm = EXPERT_ROW_TILE
    tn = 1024
    row = lambda i, na: jnp.minimum(i, na[0] - 1)
    return pl.pallas_call(
        _down_kernel,
        out_shape=jax.ShapeDtypeStruct((n_rows, d), F32),
        grid_spec=pltpu.PrefetchScalarGridSpec(
            num_scalar_prefetch=2,
            grid=(d // tn, n_rows // tm),
            in_specs=[pl.BlockSpec((tm, f), lambda j, i, te, na: (row(i, na), 0)),
                      pl.BlockSpec((None, f, tn), lambda j, i, te, na: (te[row(i, na)], 0, j)),
                      pl.BlockSpec((None, 1, tn), lambda j, i, te, na: (te[row(i, na)], 0, j))],
            out_specs=pl.BlockSpec((tm, tn), lambda j, i, te, na: (i, j)),
            scratch_shapes=[pltpu.VMEM((f, tn), BF16)]),
        compiler_params=_params(("arbitrary", "arbitrary")),
        name="expert_down",
    )(tile_expert, n_active, act, w_d, b_d.reshape(n_exp, 1, d))


COMBINE_TOKENS = 128


def _combine_kernel(slot_ref, p_ref, x1_ref, gate_ref, g_ref, y_ref, o_ref, buf, sem):
    tm = COMBINE_TOKENS
    pairs = tm * TOP_K

    def row_copy(p):
        return pltpu.make_async_copy(y_ref.at[pl.ds(slot_ref[0, p], 1)],
                                     buf.at[p & (TOP_K - 1), pl.ds(p >> TOP_K_SHIFT, 1)], sem)

    def start(p, carry):
        row_copy(p).start()
        return carry

    def wait(p, carry):
        row_copy(p).wait()
        return carry

    lax.fori_loop(0, pairs, start, 0)
    lax.fori_loop(0, pairs, wait, 0)
    probs = p_ref[...]
    f = probs[:, 0:1] * buf[0]
    for k in range(1, TOP_K):
        f = f + probs[:, k:k + 1] * buf[k]
    o_ref[...] = x1_ref[...] + gate_ref[...] * (_rms(f) * g_ref[...])


def _combine(y, slots, probs, x1, gate, g_post, rows_per_group):
    m, d = x1.shape
    tm = COMBINE_TOKENS
    nt = m // tm
    pairs = tm * TOP_K
    return pl.pallas_call(
        _combine_kernel,
        out_shape=jax.ShapeDtypeStruct((m, d), F32),
        grid=(nt,),
        in_specs=[pl.BlockSpec((None, 1, pairs), lambda i: (i, 0, 0), memory_space=pltpu.SMEM),
                  pl.BlockSpec((tm, LANES), lambda i: (i, 0)),
                  pl.BlockSpec((tm, d), lambda i: (i, 0)),
                  _mod_spec(gate, tm, rows_per_group),
                  pl.BlockSpec((1, d), lambda i: (0, 0)),
                  pl.BlockSpec(memory_space=pl.ANY)],
        out_specs=pl.BlockSpec((tm, d), lambda i: (i, 0)),
        scratch_shapes=[pltpu.VMEM((TOP_K, tm, d), F32), pltpu.SemaphoreType.DMA(())],
        compiler_params=_params(("arbitrary",)),
        name="combine",
    )(slots.reshape(nt, 1, pairs), probs, x1, gate, g_post.reshape(1, d), y)


def _moe(h2_p, h2_s, logits_p, logits_s, n_exp, w_gu, b_gu, w_d, b_d):
    n = h2_p.shape[0] + h2_s.shape[0]
    tile = EXPERT_ROW_TILE
    top_i, probs, rank, counts = _route(logits_p, logits_s)
    counts = counts[0, :n_exp].astype(I32)
    cap = (counts + tile - 1) // tile * tile
    ends = jnp.cumsum(cap)
    offs = ends - cap
    n_tiles = (n * TOP_K) // tile + n_exp
    n_active = (ends[-1] // tile).astype(I32).reshape(1)
    tile_start = jnp.arange(n_tiles, dtype=I32) * tile
    tile_expert = jnp.minimum(jnp.sum(ends[None, :] <= tile_start[:, None], axis=1), n_exp - 1).astype(I32)
    slots = offs[top_i[:, :TOP_K]] + rank[:, :TOP_K]
    n_prompt = h2_p.shape[0]
    table = _invert(slots, n_tiles * tile, n_prompt)
    split = jnp.sum(table.reshape(n_tiles, tile) < n_prompt, axis=1).astype(I32)
    xs = _dispatch(h2_p, h2_s, table, split, n_active)
    act = _expert_gate_up(xs, w_gu, b_gu, tile_expert, n_active)
    y = _expert_down(act, w_d, b_d, tile_expert, n_active)
    return y, slots, probs


def _layer(xp, xs_, cp_mod, cs_mod, hist_a, hist_q, s0, g_pre_mix, g_post_mix, g_pre_ffn, g_post_ffn,
           w_in, conv_a_w, gdn_conv_w, a_log, dt_bias, g_conv_out, gdn_norm_g, w_out, router_w, router_b,
           w_gu, b_gu, w_d, b_d):
    bsz, seq, d = xp.shape
    ns = xs_.shape[0]
    n_exp = router_w.shape[1]
    dc = N_HEADS * HEAD_D
    d_main = 3 * dc + 3 * dc + dc
    xp2 = xp.reshape(bsz * seq, d)
    xs2 = xs_.reshape(ns, d)

    w_main = w_in[:, :d_main].astype(BF16)
    w_ba = jnp.zeros((d, LANES), BF16).at[:, :2 * N_HEADS].set(w_in[:, d_main:].astype(BF16))
    w_out_b = w_out.astype(BF16)
    rw = jnp.zeros((d, LANES), F32).at[:, :n_exp].set(router_w)
    rw_hi = rw.astype(BF16)
    rw_lo = (rw - rw_hi.astype(F32)).astype(BF16)
    rb = jnp.full((1, LANES), NEG_BIG, F32).at[0, :n_exp].set(router_b)

    mp = [cp_mod[:, i * d:(i + 1) * d].reshape(bsz, 1, d) for i in range(6)]
    ms = [cs_mod[:, i * d:(i + 1) * d] for i in range(6)]

    proj_p, ba_p = _in_proj(xp2, mp[1], mp[0], g_pre_mix, w_main, w_ba, seq)
    proj_s, ba_s = _in_proj(xs2, ms[1], ms[0], g_pre_mix, w_main, w_ba, 1)

    mix_p, ha_p, hq_p, s_p = _mixer_prompt(proj_p, ba_p, bsz, seq, conv_a_w, gdn_conv_w, a_log, dt_bias,
                                           g_conv_out, gdn_norm_g)
    mix_s, ha_s, hq_s, s_s = _mixer_sample(proj_s, ba_s, hist_a.reshape(ns, 2 * dc),
                                           hist_q.reshape(ns, 9 * dc), s0, conv_a_w, gdn_conv_w, a_log,
                                           dt_bias, g_conv_out, gdn_norm_g)

    x1_p, h2_p, lg_p = _post_mix(mix_p, xp2, mp[2], mp[4], mp[3], g_post_mix, g_pre_ffn, w_out_b, rw_hi,
                                 rw_lo, rb, seq)
    x1_s, h2_s, lg_s = _post_mix(mix_s, xs2, ms[2], ms[4], ms[3], g_post_mix, g_pre_ffn, w_out_b, rw_hi,
                                 rw_lo, rb, 1)

    y, slots, probs = _moe(h2_p, h2_s, lg_p, lg_s, n_exp, w_gu, b_gu, w_d, b_d)
    np_ = bsz * seq
    out_p = _combine(y, slots[:np_], probs[:np_], x1_p, mp[5], g_post_ffn, seq)
    out_s = _combine(y, slots[np_:], probs[np_:], x1_s, ms[5], g_post_ffn, 1)
    return (out_p.reshape(bsz, seq, d), out_s.reshape(ns, 1, d), ha_p, hq_p, s_p,
            ha_s.reshape(ns, 2, dc), hq_s.reshape(ns, 3, 3 * dc), s_s)


def kernel(x_prompt, x_sample, state_conv_a, state_gdn_conv, state_gdn_S, c_prompt, c_sample, w_mod, b_mod, g_pre_mix, g_post_mix, g_pre_ffn, g_post_ffn, w_in, conv_a_w, gdn_conv_w, gdn_a_log, gdn_dt_bias, g_conv_out, gdn_norm_g, w_out, router_w, router_b, exp_w_gate_up, exp_b_gate_up, exp_w_down, exp_b_down):
    depth = w_mod.shape[0]
    bp = x_prompt.shape[0]
    xp, xs_ = x_prompt, x_sample
    outs = [[] for _ in range(6)]
    for l in range(depth):
        mod = _modulation(jnp.concatenate([c_prompt, c_sample], axis=0), w_mod[l], b_mod[l])
        res = _layer(xp, xs_, mod[:bp], mod[bp:], state_conv_a[l], state_gdn_conv[l], state_gdn_S[l],
                     g_pre_mix[l], g_post_mix[l], g_pre_ffn[l], g_post_ffn[l], w_in[l], conv_a_w[l],
                     gdn_conv_w[l], gdn_a_log[l], gdn_dt_bias[l], g_conv_out[l], gdn_norm_g[l], w_out[l],
                     router_w[l], router_b[l], exp_w_gate_up[l], exp_b_gate_up[l], exp_w_down[l],
                     exp_b_down[l])
        xp, xs_ = res[0], res[1]
        for acc, r in zip(outs, res[2:]):
            acc.append(r)
    return (xp, xs_) + tuple(jnp.stack(o) for o in outs)
```

```python
import functools

import jax
import jax.numpy as jnp
from jax import lax
from jax.experimental import pallas as pl
from jax.experimental.pallas import tpu as pltpu

F32 = jnp.float32
BF16 = jnp.bfloat16
I32 = jnp.int32
HIGHEST = lax.Precision.HIGHEST

EPS = 1e-6
N_HEADS = 8
HEAD_D = 128
TOP_K = 4
TOP_K_SHIFT = 2
SWIGLU_LIMIT = 7.0
SWIGLU_ALPHA = 1.702
CHUNK = 64
LANES = 128
EXPERT_ROW_TILE = 256
NEG_BIG = -1e30
VMEM_LIMIT = 56 * 1024 * 1024


def _params(semantics, vmem=VMEM_LIMIT):
    return pltpu.CompilerParams(dimension_semantics=semantics, vmem_limit_bytes=vmem)


def _mm(a, b):
    return jnp.dot(a.astype(BF16), b.astype(BF16), preferred_element_type=F32)


def _mm_nt(a, b):
    return lax.dot_general(a.astype(BF16), b.astype(BF16), (((1,), (1,)), ((), ())),
                           preferred_element_type=F32)


def _rms(x):
    return x * lax.rsqrt(jnp.mean(x * x, axis=-1, keepdims=True) + EPS)


def _silu(x):
    return x * jax.nn.sigmoid(x)


def _softplus(x):
    return jnp.maximum(x, 0.0) + jnp.log1p(jnp.exp(-jnp.abs(x)))


def _mod_kernel(c_ref, w_ref, b_ref, o_ref):
    s = _silu(c_ref[...])
    o_ref[...] = _mm(s, w_ref[...]) + b_ref[...]


def _modulation(c_all, w_mod, b_mod):
    n, d = c_all.shape
    m = w_mod.shape[1]
    tn = 1024
    return pl.pallas_call(
        _mod_kernel,
        out_shape=jax.ShapeDtypeStruct((n, m), F32),
        grid=(m // tn,),
        in_specs=[pl.BlockSpec((n, d), lambda j: (0, 0)),
                  pl.BlockSpec((d, tn), lambda j: (0, j)),
                  pl.BlockSpec((1, tn), lambda j: (0, j))],
        out_specs=pl.BlockSpec((n, tn), lambda j: (0, j)),
        compiler_params=_params(("arbitrary",)),
        name="modulation",
    )(c_all, w_mod, b_mod.reshape(1, m))


def _mod_spec(arr, tm, rows_per_group):
    if arr.ndim == 3:
        tiles = rows_per_group // tm
        return pl.BlockSpec((None, 1, arr.shape[-1]), lambda i, *_: (i // tiles, 0, 0))
    return pl.BlockSpec((tm, arr.shape[-1]), lambda i, *_: (i, 0))


def _proj_kernel(x_ref, sc_ref, sh_ref, g_ref, w_ref, wba_ref, o_ref, ba_ref, h_scr):
    @pl.when(pl.program_id(1) == 0)
    def _():
        h = (_rms(x_ref[...]) * g_ref[...]) * (1.0 + sc_ref[...]) + sh_ref[...]
        hb = h.astype(BF16)
        h_scr[...] = hb
        ba_ref[...] = jnp.dot(hb, wba_ref[...], preferred_element_type=F32)

    o_ref[...] = jnp.dot(h_scr[...], w_ref[...], preferred_element_type=F32)


def _in_proj(x, scale, shift, g, w_main, w_ba, rows_per_group):
    m, d = x.shape
    n = w_main.shape[1]
    tm = min(m, 1024, rows_per_group if scale.ndim == 3 else m)
    tn = 1024
    return pl.pallas_call(
        _proj_kernel,
        out_shape=(jax.ShapeDtypeStruct((m, n), F32), jax.ShapeDtypeStruct((m, LANES), F32)),
        grid=(m // tm, n // tn),
        in_specs=[pl.BlockSpec((tm, d), lambda i, j: (i, 0)),
                  _mod_spec(scale, tm, rows_per_group),
                  _mod_spec(shift, tm, rows_per_group),
                  pl.BlockSpec((1, d), lambda i, j: (0, 0)),
                  pl.BlockSpec((d, tn), lambda i, j: (0, j)),
                  pl.BlockSpec((d, LANES), lambda i, j: (0, 0))],
        out_specs=(pl.BlockSpec((tm, tn), lambda i, j: (i, j)),
                   pl.BlockSpec((tm, LANES), lambda i, j: (i, 0))),
        scratch_shapes=[pltpu.VMEM((tm, d), BF16)],
        compiler_params=_params(("arbitrary", "arbitrary")),
        name="in_proj",
    )(x, scale, shift, g.reshape(1, d), w_main, w_ba)


def _mixer_prompt_kernel(proj_ref, ba_ref, caw_ref, gcw_ref, alog_ref, dtb_ref, gco_ref, gng_ref,
                         mix_ref, ha_ref, hq_ref, s_ref, extu, extq, qc_scr, s_scr):
    c = CHUNK
    dc = N_HEADS * HEAD_D
    dq = 3 * dc
    t = pl.program_id(1)

    @pl.when(t == 0)
    def _():
        extu[0:8, :] = jnp.zeros((8, dc), F32)
        extq[0:8, :] = jnp.zeros((8, dq), F32)
        s_scr[...] = jnp.zeros_like(s_scr)

    u = proj_ref[:, dc:2 * dc] * proj_ref[:, 2 * dc:3 * dc]
    extu[8:8 + c, :] = u
    caw = caw_ref[...]
    ya = caw[0:1] * extu[6:6 + c, :] + caw[1:2] * extu[7:7 + c, :] + caw[2:3] * u
    ya = proj_ref[:, 0:dc] * ya
    mix_ref[:, 0:dc] = (_rms(ya) * gco_ref[...]).astype(BF16)
    last_u = extu[6 + c:8 + c, :]
    extu[6:8, :] = last_u

    qkv = proj_ref[:, 3 * dc:3 * dc + dq]
    extq[8:8 + c, :] = qkv
    gcw = gcw_ref[...]
    qc = (gcw[0:1] * extq[5:5 + c, :] + gcw[1:2] * extq[6:6 + c, :]
          + gcw[2:3] * extq[7:7 + c, :] + gcw[3:4] * qkv)
    qc_scr[...] = _silu(qc)
    last_q = extq[5 + c:8 + c, :]
    extq[5:8, :] = last_q

    @pl.when(t == pl.num_programs(1) - 1)
    def _():
        ha_ref[...] = last_u
        hq_ref[...] = last_q

    ba = ba_ref[...]
    beta_all = jax.nn.sigmoid(ba)
    g_all = -jnp.exp(alog_ref[...]) * _softplus(ba + dtb_ref[...])
    row = lax.broadcasted_iota(I32, (c, c), 0)
    col = lax.broadcasted_iota(I32, (c, c), 1)
    causal = row >= col
    strict = row > col
    lower = jnp.where(causal, 1.0, 0.0).astype(F32)
    upper = jnp.where(row <= col, 1.0, 0.0).astype(F32)
    gc_all = jnp.dot(lower, g_all, precision=HIGHEST, preferred_element_type=F32)
    gc_t = lax.dot_general(g_all, upper, (((0,), (0,)), ((), ())), precision=HIGHEST,
                           preferred_element_type=F32)

    heads = range(N_HEADS)
    qn, kn, vb, kb, kbg, qg, kg, decay, s_decay = ([] for _ in range(9))
    for h in heads:
        lo = h * HEAD_D
        q = qc_scr[:, lo:lo + HEAD_D]
        k = qc_scr[:, dc + lo:dc + lo + HEAD_D]
        v = qc_scr[:, 2 * dc + lo:2 * dc + lo + HEAD_D]
        qn_h = q * lax.rsqrt(jnp.sum(q * q, axis=-1, keepdims=True) + EPS) * (HEAD_D ** -0.5)
        kn_h = k * lax.rsqrt(jnp.sum(k * k, axis=-1, keepdims=True) + EPS)
        beta = beta_all[:, h:h + 1]
        gcc = gc_all[:, N_HEADS + h:N_HEADS + h + 1]
        gcr = gc_t[N_HEADS + h:N_HEADS + h + 1, :]
        gl = gc_all[c - 1:c, N_HEADS + h:N_HEADS + h + 1]
        eg = jnp.exp(gcc)
        kb_h = kn_h * beta
        qn.append(qn_h)
        kn.append(kn_h)
        vb.append(v * beta)
        kb.append(kb_h)
        kbg.append(kb_h * eg)
        qg.append(qn_h * eg)
        kg.append(kn_h * jnp.exp(gl - gcc))
        decay.append(jnp.where(causal, jnp.exp(jnp.minimum(gcc - gcr, 0.0)), 0.0))
        s_decay.append(jnp.exp(gl))

    kq = [_mm_nt(jnp.concatenate([kb[h], qn[h]], axis=0), kn[h]) for h in heads]
    a_mat = [jnp.where(strict, kq[h][:c] * decay[h], 0.0) for h in heads]
    qk = [kq[h][c:] * decay[h] for h in heads]
    n_mat = [-a_mat[h] for h in heads]
    p = a_mat
    size = 2
    while size < c:
        p = [_mm(p[h], p[h]) for h in heads]
        n_p = [_mm(n_mat[h], p[h]) for h in heads]
        n_mat = [n_mat[h] + p[h] + n_p[h] for h in heads]
        size *= 2
    rhs = [jnp.concatenate([vb[h], kbg[h]], axis=-1) for h in heads]
    uw = [rhs[h] + _mm(n_mat[h], rhs[h]) for h in heads]
    s_old = [s_scr[h] for h in heads]
    ws = [_mm(jnp.concatenate([uw[h][:, HEAD_D:], qg[h]], axis=0), s_old[h]) for h in heads]
    v_new = [uw[h][:, :HEAD_D] - ws[h][:c] for h in heads]
    fin = [_mm(jnp.concatenate([qk[h], kg[h].T], axis=0), v_new[h]) for h in heads]
    for h in heads:
        lo = h * HEAD_D
        s_scr[h] = s_old[h] * s_decay[h] + fin[h][c:]
        o = ws[h][c:] + fin[h][:c]
        z = proj_ref[:, 3 * dc + dq + lo:3 * dc + dq + lo + HEAD_D]
        yb = _rms(o) * gng_ref[...] * _silu(z)
        mix_ref[:, dc + lo:dc + lo + HEAD_D] = yb.astype(BF16)

    @pl.when(t == pl.num_programs(1) - 1)
    def _():
        s_ref[...] = s_scr[...]


def _lane_row(vec, offset):
    return jnp.zeros((1, LANES), F32).at[0, offset:offset + vec.shape[0]].set(vec.astype(F32))


def _mixer_prompt(proj, ba, bsz, seq, conv_a_w, gdn_conv_w, a_log, dt_bias, g_conv_out, gdn_norm_g):
    c = CHUNK
    dc = N_HEADS * HEAD_D
    dq = 3 * dc
    dproj = proj.shape[1]
    nt = seq // c
    const = lambda shape: pl.BlockSpec(shape, lambda b, t: (0,) * len(shape))
    return pl.pallas_call(
        _mixer_prompt_kernel,
        out_shape=(jax.ShapeDtypeStruct((bsz * seq, 2 * dc), BF16),
                   jax.ShapeDtypeStruct((bsz, 2, dc), F32),
                   jax.ShapeDtypeStruct((bsz, 3, dq), F32),
                   jax.ShapeDtypeStruct((bsz, N_HEADS, HEAD_D, HEAD_D), F32)),
        grid=(bsz, nt),
        in_specs=[pl.BlockSpec((c, dproj), lambda b, t: (b * nt + t, 0)),
                  pl.BlockSpec((c, LANES), lambda b, t: (b * nt + t, 0)),
                  const((3, dc)), const((4, dq)), const((1, LANES)), const((1, LANES)),
                  const((1, dc)), const((1, HEAD_D))],
        out_specs=(pl.BlockSpec((c, 2 * dc), lambda b, t: (b * nt + t, 0)),
                   pl.BlockSpec((None, 2, dc), lambda b, t: (b, 0, 0)),
                   pl.BlockSpec((None, 3, dq), lambda b, t: (b, 0, 0)),
                   pl.BlockSpec((None, N_HEADS, HEAD_D, HEAD_D), lambda b, t: (b, 0, 0, 0))),
        scratch_shapes=[pltpu.VMEM((8 + c, dc), F32), pltpu.VMEM((8 + c, dq), F32),
                        pltpu.VMEM((c, dq), F32), pltpu.VMEM((N_HEADS, HEAD_D, HEAD_D), F32)],
        compiler_params=_params(("arbitrary", "arbitrary")),
        name="mixer_prompt",
    )(proj, ba, conv_a_w, gdn_conv_w, _lane_row(a_log, N_HEADS), _lane_row(dt_bias, N_HEADS),
      g_conv_out.reshape(1, dc), gdn_norm_g.reshape(1, HEAD_D))


SAMPLE_GROUP = 16


def _mixer_sample_kernel(proj_ref, ba_ref, hista_ref, histq_ref, s_in_ref, caw_ref, gcw_ref, alog_ref,
                         dtb_ref, gco_ref, gng_ref, mix_ref, ha_ref, hq_ref, s_out_ref, qc_scr, o_scr):
    tb = SAMPLE_GROUP
    dc = N_HEADS * HEAD_D
    dq = 3 * dc

    u = proj_ref[:, dc:2 * dc] * proj_ref[:, 2 * dc:3 * dc]
    caw = caw_ref[...]
    ya = caw[0:1] * hista_ref[:, 0:dc] + caw[1:2] * hista_ref[:, dc:2 * dc] + caw[2:3] * u
    ya = proj_ref[:, 0:dc] * ya
    mix_ref[:, 0:dc] = _rms(ya) * gco_ref[...]
    ha_ref[:, 0:dc] = hista_ref[:, dc:2 * dc]
    ha_ref[:, dc:2 * dc] = u

    qkv = proj_ref[:, 3 * dc:3 * dc + dq]
    gcw = gcw_ref[...]
    qc = (gcw[0:1] * histq_ref[:, 0:dq] + gcw[1:2] * histq_ref[:, dq:2 * dq]
          + gcw[2:3] * histq_ref[:, 2 * dq:3 * dq] + gcw[3:4] * qkv)
    qc_scr[...] = _silu(qc)
    hq_ref[:, 0:dq] = histq_ref[:, dq:2 * dq]
    hq_ref[:, dq:2 * dq] = histq_ref[:, 2 * dq:3 * dq]
    hq_ref[:, 2 * dq:3 * dq] = qkv

    ba = ba_ref[...]
    beta_all = jax.nn.sigmoid(ba)
    eg_all = jnp.exp(-jnp.exp(alog_ref[...]) * _softplus(ba + dtb_ref[...]))

    for h in range(N_HEADS):
        lo = h * HEAD_D
        q = qc_scr[:, lo:lo + HEAD_D]
        k = qc_scr[:, dc + lo:dc + lo + HEAD_D]
        v = qc_scr[:, 2 * dc + lo:2 * dc + lo + HEAD_D]
        qn = q * lax.rsqrt(jnp.sum(q * q, axis=-1, keepdims=True) + EPS) * (HEAD_D ** -0.5)
        kn = k * lax.rsqrt(jnp.sum(k * k, axis=-1, keepdims=True) + EPS)
        qk = jnp.sum(qn * kn, axis=-1, keepdims=True)
        kn_t = kn.T
        qn_t = qn.T
        for b in range(tb):
            s_old = s_in_ref[b, h]
            kc = kn_t[:, b:b + 1]
            e = eg_all[b:b + 1, N_HEADS + h:N_HEADS + h + 1]
            ks = jnp.sum(s_old * kc, axis=0, keepdims=True)
            qs = jnp.sum(s_old * qn_t[:, b:b + 1], axis=0, keepdims=True)
            v_new = beta_all[b:b + 1, h:h + 1] * (v[b:b + 1, :] - e * ks)
            o_scr[b:b + 1, lo:lo + HEAD_D] = e * qs + qk[b:b + 1, :] * v_new
            s_out_ref[b, h] = s_old * e + kc * v_new
        z = proj_ref[:, 3 * dc + dq + lo:3 * dc + dq + lo + HEAD_D]
        o = o_scr[:, lo:lo + HEAD_D]
        mix_ref[:, dc + lo:dc + lo + HEAD_D] = _rms(o) * gng_ref[...] * _silu(z)


def _mixer_sample(proj, ba, hist_a, hist_q, s_in, conv_a_w, gdn_conv_w, a_log, dt_bias, g_conv_out,
                  gdn_norm_g):
    n = proj.shape[0]
    tb = SAMPLE_GROUP
    dc = N_HEADS * HEAD_D
    dq = 3 * dc
    dproj = proj.shape[1]
    const = lambda shape: pl.BlockSpec(shape, lambda i: (0,) * len(shape))
    rows = lambda width: pl.BlockSpec((tb, width), lambda i: (i, 0))
    state = pl.BlockSpec((tb, N_HEADS, HEAD_D, HEAD_D), lambda i: (i, 0, 0, 0))
    return pl.pallas_call(
        _mixer_sample_kernel,
        out_shape=(jax.ShapeDtypeStruct((n, 2 * dc), F32),
                   jax.ShapeDtypeStruct((n, 2 * dc), F32),
                   jax.ShapeDtypeStruct((n, 3 * dq), F32),
                   jax.ShapeDtypeStruct((n, N_HEADS, HEAD_D, HEAD_D), F32)),
        grid=(n // tb,),
        in_specs=[rows(dproj), rows(LANES), rows(2 * dc), rows(3 * dq), state,
                  const((3, dc)), const((4, dq)), const((1, LANES)), const((1, LANES)),
                  const((1, dc)), const((1, HEAD_D))],
        out_specs=(rows(2 * dc), rows(2 * dc), rows(3 * dq), state),
        scratch_shapes=[pltpu.VMEM((tb, dq), F32), pltpu.VMEM((tb, dc), F32)],
        compiler_params=_params(("arbitrary",)),
        name="mixer_sample",
    )(proj, ba, hist_a, hist_q, s_in, conv_a_w, gdn_conv_w, _lane_row(a_log, N_HEADS),
      _lane_row(dt_bias, N_HEADS), g_conv_out.reshape(1, dc), gdn_norm_g.reshape(1, HEAD_D))


def _post_mix_kernel(mix_ref, x_ref, gate_ref, sc_ref, sh_ref, gpost_ref, gpre_ref, wout_ref,
                     rwh_ref, rwl_ref, rb_ref, x1_ref, h2_ref, lg_ref):
    mix =jnp.dot(mix_ref[...].astype(BF16), wout_ref[...], preferred_element_type=F32)
    x1 = x_ref[...] + gate_ref[...] * (_rms(mix) * gpost_ref[...])
    x1_ref[...] = x1
    h2 = (_rms(x1) * gpre_ref[...]) * (1.0 + sc_ref[...]) + sh_ref[...]
    h2_ref[...] = h2
    hi = h2.astype(BF16)
    lo = (h2 - hi.astype(F32)).astype(BF16)
    rwh = rwh_ref[...]
    lg_ref[...] = (jnp.dot(hi, rwh, preferred_element_type=F32)
                   + jnp.dot(lo, rwh, preferred_element_type=F32)
                   + jnp.dot(hi, rwl_ref[...], preferred_element_type=F32) + rb_ref[...])


def _post_mix(mix_in, x, gate, scale, shift, g_post, g_pre, w_out, rw_hi, rw_lo, rb, rows_per_group):
    m, d = x.shape
    tm = min(m, 512, rows_per_group if gate.ndim == 3 else m)
    const = lambda shape: pl.BlockSpec(shape, lambda i: (0,) * len(shape))
    rows = lambda width: pl.BlockSpec((tm, width), lambda i: (i, 0))
    return pl.pallas_call(
        _post_mix_kernel,
        out_shape=(jax.ShapeDtypeStruct((m, d), F32), jax.ShapeDtypeStruct((m, d), F32),
                   jax.ShapeDtypeStruct((m, LANES), F32)),
        grid=(m // tm,),
        in_specs=[rows(d), rows(d),
                  _mod_spec(gate, tm, rows_per_group),
                  _mod_spec(scale, tm, rows_per_group),
                  _mod_spec(shift, tm, rows_per_group),
                  const((1, d)), const((1, d)), const((d, d)),
                  const((d, LANES)), const((d, LANES)), const((1, LANES))],
        out_specs=(rows(d), rows(d), rows(LANES)),
        compiler_params=_params(("arbitrary",)),
        name="post_mix",
    )(mix_in, x, gate, scale, shift, g_post.reshape(1, d), g_pre.reshape(1, d), w_out, rw_hi, rw_lo, rb)


ROUTE_TOKENS = 128


def _route_kernel(lgp_ref, lgs_ref, idx_ref, p_ref, rank_ref, cnt_ref, carry, *, prompt_tiles):
    tm = lgp_ref.shape[0]

    @pl.when(pl.program_id(0) == 0)
    def _():
        carry[...] = jnp.zeros_like(carry)

    l = jnp.where(pl.program_id(0) < prompt_tiles, lgp_ref[...], lgs_ref[...])
    lane = lax.broadcasted_iota(I32, l.shape, 1)
    lane_f = lane.astype(F32)
    vals, hots = [], []
    idx_out = jnp.zeros(l.shape, F32)
    for k in range(TOP_K):
        m = jnp.max(l, axis=-1, keepdims=True)
        idx = jnp.min(jnp.where(l == m, lane_f, float(LANES)), axis=-1, keepdims=True)
        hot = lane_f == idx
        vals.append(m)
        hots.append(hot)
        idx_out = jnp.where(lane == k, idx, idx_out)
        l = jnp.where(hot, -jnp.inf, l)
    exps = [jnp.exp(v - vals[0]) for v in vals]
    denom = exps[0] + exps[1] + exps[2] + exps[3]
    p_out = jnp.zeros(l.shape, F32)
    for k in range(TOP_K):
        p_out = jnp.where(lane == k, exps[k] / denom, p_out)
    member = jnp.where(hots[0] | hots[1] | hots[2] | hots[3], 1.0, 0.0).astype(F32)
    row = lax.broadcasted_iota(I32, (tm, tm), 0)
    col = lax.broadcasted_iota(I32, (tm, tm), 1)
    before = jnp.where(row > col, 1.0, 0.0).astype(BF16)
    prefix = jnp.dot(before, member.astype(BF16), preferred_element_type=F32) + carry[...]
    rank_out = jnp.zeros(l.shape, F32)
    for k in range(TOP_K):
        r = jnp.sum(jnp.where(hots[k], prefix, 0.0), axis=-1, keepdims=True)
        rank_out = jnp.where(lane == k, r, rank_out)
    carry[...] = carry[...] + jnp.sum(member, axis=0, keepdims=True)
    idx_ref[...] = idx_out.astype(I32)
    p_ref[...] = p_out
    rank_ref[...] = rank_out.astype(I32)
    cnt_ref[...] = carry[...]


def _route(logits_p, logits_s):
    tm = ROUTE_TOKENS
    pt = logits_p.shape[0] // tm
    n = logits_p.shape[0] + logits_s.shape[0]
    tile = pl.BlockSpec((tm, LANES), lambda i: (i, 0))
    return pl.pallas_call(
        functools.partial(_route_kernel, prompt_tiles=pt),
        out_shape=(jax.ShapeDtypeStruct((n, LANES), I32), jax.ShapeDtypeStruct((n, LANES), F32),
                   jax.ShapeDtypeStruct((n, LANES), I32), jax.ShapeDtypeStruct((1, LANES), F32)),
        grid=(n // tm,),
        in_specs=[pl.BlockSpec((tm, LANES), lambda i: (jnp.minimum(i, pt - 1), 0)),
                  pl.BlockSpec((tm, LANES), lambda i: (jnp.maximum(i - pt, 0), 0))],
        out_specs=(tile, tile, tile, pl.BlockSpec((1, LANES), lambda i: (0, 0))),
        scratch_shapes=[pltpu.VMEM((1, LANES), F32)],
        compiler_params=_params(("arbitrary",)),
        name="route",
    )(logits_p, logits_s)


INVERT_TOKENS = 128


def _invert_kernel(slot_ref, table_ref, *, pad_token):
    i = pl.program_id(0)
    pairs = INVERT_TOKENS * TOP_K

    @pl.when(i == 0)
    def _():
        def fill(r, carry):
            table_ref[r] = pad_token
            return carry

        lax.fori_loop(0, table_ref.shape[0], fill, 0, unroll=8)

    def put(p, carry):
        table_ref[slot_ref[0, p]] = i * INVERT_TOKENS + (p >> TOP_K_SHIFT)
        return carry

    lax.fori_loop(0, pairs, put, 0, unroll=8)


def _invert(slots, n_rows, pad_token):
    n = slots.shape[0]
    nt = n // INVERT_TOKENS
    pairs = INVERT_TOKENS * TOP_K
    return pl.pallas_call(
        functools.partial(_invert_kernel, pad_token=pad_token),
        out_shape=jax.ShapeDtypeStruct((n_rows,), I32),
        grid=(nt,),
        in_specs=[pl.BlockSpec((None, 1, pairs), lambda i: (i, 0, 0), memory_space=pltpu.SMEM)],
        out_specs=pl.BlockSpec(memory_space=pltpu.SMEM),
        compiler_params=_params(("arbitrary",)),
        name="invert",
    )(slots.reshape(nt, 1, pairs))


ROW_UNROLL = 8
ROW_UNROLL_SHIFT = 3


def _dispatch_kernel(na_ref, split_ref, tok_ref, tok_next_ref, h2p_ref, h2s_ref, xs_ref, buf, sem, *,
                     n_prompt):
    i = pl.program_id(0)
    tile = EXPERT_ROW_TILE
    n_active = na_ref[0]

    def issue(tok, split, slot):
        def prompt_copy(r):
            pltpu.make_async_copy(h2p_ref.at[pl.ds(tok[0, r], 1)], buf.at[slot, pl.ds(r, 1)],
                                  sem.at[slot]).start()

        def sample_copy(r):
            pltpu.make_async_copy(h2s_ref.at[pl.ds(tok[0, r] - n_prompt, 1)], buf.at[slot, pl.ds(r, 1)],
                                  sem.at[slot]).start()

        def rows(lo, hi, fn):
            def body(r, carry):
                fn(r)
                return carry
            lax.fori_loop(lo, hi, body, 0)

        def groups(lo, hi, fn):
            def body(g, carry):
                for u in range(ROW_UNROLL):
                    fn(g * ROW_UNROLL + u)
                return carry
            lax.fori_loop(lo, hi, body, 0)

        whole = split >> ROW_UNROLL_SHIFT
        first = (split + ROW_UNROLL - 1) >> ROW_UNROLL_SHIFT
        groups(0, whole, prompt_copy)
        rows(whole * ROW_UNROLL, split, prompt_copy)
        rows(split, first * ROW_UNROLL, sample_copy)
        groups(first, tile // ROW_UNROLL, sample_copy)

    @pl.when(i == 0)
    def _():
        issue(tok_ref, split_ref[0], 0)

    @pl.when(i + 1 < n_active)
    def _():
        issue(tok_next_ref, split_ref[i + 1], (i + 1) & 1)

    @pl.when(i < n_active)
    def _():
        slot = i & 1
        pltpu.make_async_copy(h2p_ref.at[pl.ds(0, tile)], buf.at[slot], sem.at[slot]).wait()
        xs_ref[...] = buf[slot].astype(BF16)

    @pl.when(i >= n_active)
    def _():
        xs_ref[...] = jnp.zeros_like(xs_ref)


def _dispatch(h2_p, h2_s, table, split, n_active):
    d = h2_p.shape[1]
    tile = EXPERT_ROW_TILE
    n_rows = table.shape[0]
    nt = n_rows // tile
    return pl.pallas_call(
        functools.partial(_dispatch_kernel, n_prompt=h2_p.shape[0]),
        out_shape=jax.ShapeDtypeStruct((n_rows, d), BF16),
        grid_spec=pltpu.PrefetchScalarGridSpec(
            num_scalar_prefetch=2,
            grid=(nt,),
            in_specs=[pl.BlockSpec((None, 1, tile), lambda i, na, sp: (i, 0, 0), memory_space=pltpu.SMEM),
                      pl.BlockSpec((None, 1, tile), lambda i, na, sp: (jnp.minimum(i + 1, nt - 1), 0, 0),
                                   memory_space=pltpu.SMEM),
                      pl.BlockSpec(memory_space=pl.ANY), pl.BlockSpec(memory_space=pl.ANY)],
            out_specs=pl.BlockSpec((tile, d), lambda i, na, sp: (i, 0)),
            scratch_shapes=[pltpu.VMEM((2, tile, d), F32), pltpu.SemaphoreType.DMA((2,))]),
        compiler_params=_params(("arbitrary",)),
        name="dispatch",
    )(n_active, split, table.reshape(nt, 1, tile), table.reshape(nt, 1, tile), h2_p, h2_s)


def _gate_up_kernel(te_ref, na_ref, x_ref, wg_ref, wu_ref, bg_ref, bu_ref, o_ref, wg_b, wu_b):
    i = pl.program_id(1)
    active = i < na_ref[0]
    changed = jnp.logical_or(i == 0, te_ref[i] != te_ref[jnp.maximum(i - 1, 0)])

    @pl.when(jnp.logical_and(active, changed))
    def _():
        wg_b[...] = wg_ref[...].astype(BF16)
        wu_b[...] = wu_ref[...].astype(BF16)

    @pl.when(active)
    def _():
        x = x_ref[...]
        gate = jnp.dot(x, wg_b[...], preferred_element_type=F32) + bg_ref[...]
        up = jnp.dot(x, wu_b[...], preferred_element_type=F32) + bu_ref[...]
        gate = jnp.minimum(gate, SWIGLU_LIMIT)
        up = jnp.clip(up, -SWIGLU_LIMIT, SWIGLU_LIMIT)
        o_ref[...] = ((up + 1.0) * gate * jax.nn.sigmoid(SWIGLU_ALPHA * gate)).astype(BF16)

    @pl.when(jnp.logical_not(active))
    def _():
        o_ref[...] = jnp.zeros_like(o_ref)


def _expert_gate_up(xs, w_gu, b_gu, tile_expert, n_active):
    n_rows, d = xs.shape
    n_exp, _, f2 = w_gu.shape
    f = f2 // 2
    tm = EXPERT_ROW_TILE
    tn = 1024
    nj = f // tn
    row = lambda i, na: jnp.minimum(i, na[0] - 1)
    return pl.pallas_call(
        _gate_up_kernel,
        out_shape=jax.ShapeDtypeStruct((n_rows, f), BF16),
        grid_spec=pltpu.PrefetchScalarGridSpec(
            num_scalar_prefetch=2,
            grid=(nj, n_rows // tm),
            in_specs=[pl.BlockSpec((tm, d), lambda j, i, te, na: (row(i, na), 0)),
                      pl.BlockSpec((None, d, tn), lambda j, i, te, na: (te[row(i, na)], 0, j)),
                      pl.BlockSpec((None, d, tn), lambda j, i, te, na: (te[row(i, na)], 0, nj + j)),
                      pl.BlockSpec((None, 1, tn), lambda j, i, te, na: (te[row(i, na)], 0, j)),
                      pl.BlockSpec((None, 1, tn), lambda j, i, te, na: (te[row(i, na)], 0, nj + j))],
            out_specs=pl.BlockSpec((tm, tn), lambda j, i, te, na: (i, j)),
            scratch_shapes=[pltpu.VMEM((d, tn), BF16), pltpu.VMEM((d, tn), BF16)]),
        compiler_params=_params(("arbitrary", "arbitrary")),
        name="expert_gate_up",
    )(tile_expert, n_active, xs, w_gu, w_gu, b_gu.reshape(n_exp, 1, f2), b_gu.reshape(n_exp, 1, f2))


def _down_kernel(te_ref, na_ref, a_ref, w_ref, b_ref, o_ref, w_b):
    i = pl.program_id(1)
    active = i < na_ref[0]
    changed = jnp.logical_or(i == 0, te_ref[i] != te_ref[jnp.maximum(i - 1, 0)])

    @pl.when(jnp.logical_and(active, changed))
    def _():
        w_b[...] = w_ref[...].astype(BF16)

    @pl.when(active)
    def _():
        o_ref[...] = jnp.dot(a_ref[...], w_b[...], preferred_element_type=F32) + b_ref[...]

    @pl.when(jnp.logical_not(active))
    def _():
        o_ref[...] = jnp.zeros_like(o_ref)


def _expert_down(act, w_d, b_d, tile_expert, n_active):
    n_rows, f = act.shape
    n_exp, _, d = w_d.shape
    tm = EXPERT_ROW_TILE
    tn = 1024
    row = lambda i, na: jnp.minimum(i, na[0] - 1)
    return pl.pallas_call(
        _down_kernel,
        out_shape=jax.ShapeDtypeStruct((n_rows, d), F32),
        grid_spec=pltpu.PrefetchScalarGridSpec(
            num_scalar_prefetch=2,
            grid=(d // tn, n_rows // tm),
            in_specs=[pl.BlockSpec((tm, f), lambda j, i, te, na: (row(i, na), 0)),
                      pl.BlockSpec((None, f, tn), lambda j, i, te, na: (te[row(i, na)], 0, j)),
                      pl.BlockSpec((None, 1, tn), lambda j, i, te, na: (te[row(i, na)], 0, j))],
            out_specs=pl.BlockSpec((tm, tn), lambda j, i, te, na: (i, j)),
            scratch_shapes=[pltpu.VMEM((f, tn), BF16)]),
        compiler_params=_params(("arbitrary", "arbitrary")),
        name="expert_down",
    )(tile_expert, n_active, act, w_d, b_d.reshape(n_exp, 1, d))


COMBINE_TOKENS = 128


def _combine_kernel(slot_ref, slot_next_ref, p_ref, x1_ref, gate_ref, g_ref, y_ref, o_ref, buf, sem):
    i = pl.program_id(0)
    tm = COMBINE_TOKENS
    pairs = tm * TOP_K

    def issue(slots, b):
        def body(g, carry):
            for u in range(ROW_UNROLL):
                p = g * ROW_UNROLL + u
                tok = g * (ROW_UNROLL // TOP_K) + u // TOP_K
                pltpu.make_async_copy(y_ref.at[pl.ds(slots[0, p], 1)],
                                      buf.at[b, u % TOP_K, pl.ds(tok, 1)], sem.at[b]).start()
            return carry
        lax.fori_loop(0, pairs // ROW_UNROLL, body, 0)

    @pl.when(i == 0)
    def _():
        issue(slot_ref, 0)

    @pl.when(i + 1 < pl.num_programs(0))
    def _():
        issue(slot_next_ref, (i + 1) & 1)

    b = i & 1
    for k in range(TOP_K):
        pltpu.make_async_copy(y_ref.at[pl.ds(0, tm)], buf.at[b, k], sem.at[b]).wait()
    probs = p_ref[...]
    f = probs[:, 0:1] * buf[b, 0]
    for k in range(1, TOP_K):
        f = f + probs[:, k:k + 1] * buf[b, k]
    o_ref[...] = x1_ref[...] + gate_ref[...] * (_rms(f) * g_ref[...])


def _combine(y, slots, probs, x1, gate, g_post, rows_per_group):
    m, d = x1.shape
    tm = COMBINE_TOKENS
    nt = m // tm
    pairs = tm * TOP_K
    return pl.pallas_call(
        _combine_kernel,
        out_shape=jax.ShapeDtypeStruct((m, d), F32),
        grid=(nt,),
        in_specs=[pl.BlockSpec((None, 1, pairs), lambda i: (i, 0, 0), memory_space=pltpu.SMEM),
                  pl.BlockSpec((None, 1, pairs), lambda i: (jnp.minimum(i + 1, nt - 1), 0, 0),
                               memory_space=pltpu.SMEM),
                  pl.BlockSpec((tm, LANES), lambda i: (i, 0)),
                  pl.BlockSpec((tm, d), lambda i: (i, 0)),
                  _mod_spec(gate, tm, rows_per_group),
                  pl.BlockSpec((1, d), lambda i: (0, 0)),
                  pl.BlockSpec(memory_space=pl.ANY)],
        out_specs=pl.BlockSpec((tm, d), lambda i: (i, 0)),
        scratch_shapes=[pltpu.VMEM((2, TOP_K, tm, d), F32), pltpu.SemaphoreType.DMA((2,))],
        compiler_params=_params(("arbitrary",)),
        name="combine",
    )(slots.reshape(nt, 1, pairs), slots.reshape(nt, 1, pairs), probs, x1, gate, g_post.reshape(1, d), y)


def _moe(h2_p, h2_s, logits_p, logits_s, n_exp, w_gu, b_gu, w_d, b_d):
    n = h2_p.shape[0] + h2_s.shape[0]
    tile = EXPERT_ROW_TILE
    top_i, probs, rank, counts = _route(logits_p, logits_s)
    counts = counts[0, :n_exp].astype(I32)
    cap = (counts + tile - 1) // tile * tile
    ends = jnp.cumsum(cap)
    offs = ends - cap
    n_tiles = (n * TOP_K) // tile + n_exp
    n_active = (ends[-1] // tile).astype(I32).reshape(1)
    tile_start = jnp.arange(n_tiles, dtype=I32) * tile
    tile_expert = jnp.minimum(jnp.sum(ends[None, :] <= tile_start[:, None], axis=1), n_exp - 1).astype(I32)
    slots = offs[top_i[:, :TOP_K]] + rank[:, :TOP_K]
    n_prompt = h2_p.shape[0]
    table = _invert(slots, n_tiles * tile, n_prompt)
    split = jnp.sum(table.reshape(n_tiles, tile) < n_prompt, axis=1).astype(I32)
    xs = _dispatch(h2_p, h2_s, table, split, n_active)
    act = _expert_gate_up(xs, w_gu, b_gu, tile_expert, n_active)
    y = _expert_down(act, w_d, b_d, tile_expert, n_active)
    return y, slots, probs


def _layer(xp, xs_, cp_mod, cs_mod, hist_a, hist_q, s0, g_pre_mix, g_post_mix, g_pre_ffn, g_post_ffn,
           w_in, conv_a_w, gdn_conv_w, a_log, dt_bias, g_conv_out, gdn_norm_g, w_out, router_w, router_b,
           w_gu, b_gu, w_d, b_d):
    bsz, seq, d = xp.shape
    ns = xs_.shape[0]
    n_exp = router_w.shape[1]
    dc = N_HEADS * HEAD_D
    d_main = 3 * dc + 3 * dc + dc
    xp2 = xp.reshape(bsz * seq, d)
    xs2 = xs_.reshape(ns, d)

    w_main = w_in[:, :d_main].astype(BF16)
    w_ba = jnp.zeros((d, LANES), BF16).at[:, :2 * N_HEADS].set(w_in[:, d_main:].astype(BF16))
    w_out_b = w_out.astype(BF16)
    rw = jnp.zeros((d, LANES), F32).at[:, :n_exp].set(router_w)
    rw_hi = rw.astype(BF16)
    rw_lo = (rw - rw_hi.astype(F32)).astype(BF16)
    rb = jnp.full((1, LANES), NEG_BIG, F32).at[0, :n_exp].set(router_b)

    mp = [cp_mod[:, i * d:(i + 1) * d].reshape(bsz, 1, d) for i in range(6)]
    ms = [cs_mod[:, i * d:(i + 1) * d] for i in range(6)]

    proj_p, ba_p = _in_proj(xp2, mp[1], mp[0], g_pre_mix, w_main, w_ba, seq)
    proj_s, ba_s = _in_proj(xs2, ms[1], ms[0], g_pre_mix, w_main, w_ba, 1)

    mix_p, ha_p, hq_p, s_p = _mixer_prompt(proj_p, ba_p, bsz, seq, conv_a_w, gdn_conv_w, a_log, dt_bias,
                                           g_conv_out, gdn_norm_g)
    mix_s, ha_s, hq_s, s_s = _mixer_sample(proj_s, ba_s, hist_a.reshape(ns, 2 * dc),
                                           hist_q.reshape(ns, 9 * dc), s0, conv_a_w, gdn_conv_w, a_log,
                                           dt_bias, g_conv_out, gdn_norm_g)

    x1_p, h2_p, lg_p = _post_mix(mix_p, xp2, mp[2], mp[4], mp[3], g_post_mix, g_pre_ffn, w_out_b, rw_hi,
                                 rw_lo, rb, seq)
    x1_s, h2_s, lg_s = _post_mix(mix_s, xs2, ms[2], ms[4], ms[3], g_post_mix, g_pre_ffn, w_out_b, rw_hi,
                                 rw_lo, rb, 1)

    y, slots, probs = _moe(h2_p, h2_s, lg_p, lg_s, n_exp, w_gu, b_gu, w_d, b_d)
    np_ = bsz * seq
    out_p = _combine(y, slots[:np_], probs[:np_], x1_p, mp[5], g_post_ffn, seq)
    out_s = _combine(y, slots[np_:], probs[np_:], x1_s, ms[5], g_post_ffn, 1)
    return (out_p.reshape(bsz, seq, d), out_s.reshape(ns, 1, d), ha_p, hq_p, s_p,
            ha_s.reshape(ns, 2, dc), hq_s.reshape(ns, 3, 3 * dc), s_s)


def kernel(x_prompt, x_sample, state_conv_a, state_gdn_conv, state_gdn_S, c_prompt, c_sample, w_mod, b_mod, g_pre_mix, g_post_mix, g_pre_ffn, g_post_ffn, w_in, conv_a_w, gdn_conv_w, gdn_a_log, gdn_dt_bias, g_conv_out, gdn_norm_g, w_out, router_w, router_b, exp_w_gate_up, exp_b_gate_up, exp_w_down, exp_b_down):
    depth = w_mod.shape[0]
    bp = x_prompt.shape[0]
    xp, xs_ = x_prompt, x_sample
    outs = [[] for _ in range(6)]
    for l in range(depth):
        mod = _modulation(jnp.concatenate([c_prompt, c_sample], axis=0), w_mod[l], b_mod[l])
        res = _layer(xp, xs_, mod[:bp], mod[bp:], state_conv_a[l], state_gdn_conv[l], state_gdn_S[l],
                     g_pre_mix[l], g_post_mix[l], g_pre_ffn[l], g_post_ffn[l], w_in[l], conv_a_w[l],
                     gdn_conv_w[l], gdn_a_log[l], gdn_dt_bias[l], g_conv_out[l], gdn_norm_g[l], w_out[l],
                     router_w[l], router_b[l], exp_w_gate_up[l], exp_b_gate_up[l], exp_w_down[l],
                     exp_b_down[l])
        xp, xs_ = res[0], res[1]
        for acc, r in zip(outs, res[2:]):
            acc.append(r)
    return (xp, xs_) + tuple(jnp.stack(o) for o in outs)
```

```python
import functools

import jax
import jax.numpy as jnp
from jax import lax
from jax.experimental import pallas as pl
from jax.experimental.pallas import tpu as pltpu

F32 = jnp.float32
BF16 = jnp.bfloat16
I32 = jnp.int32
HIGHEST = lax.Precision.HIGHEST

EPS = 1e-6
N_HEADS = 8
HEAD_D = 128
TOP_K = 4
TOP_K_SHIFT = 2
SWIGLU_LIMIT = 7.0
SWIGLU_ALPHA = 1.702
CHUNK = 64
LANES = 128
EXPERT_ROW_TILE = 256
NEG_BIG = -1e30
VMEM_LIMIT = 56 * 1024 * 1024


def _params(semantics, vmem=VMEM_LIMIT):
    return pltpu.CompilerParams(dimension_semantics=semantics, vmem_limit_bytes=vmem)


def _mm(a, b):
    return jnp.dot(a.astype(BF16), b.astype(BF16), preferred_element_type=F32)


def _mm_nt(a, b):
    return lax.dot_general(a.astype(BF16), b.astype(BF16), (((1,), (1,)), ((), ())),
                           preferred_element_type=F32)


def _rms(x):
    return x * lax.rsqrt(jnp.mean(x * x, axis=-1, keepdims=True) + EPS)


def _silu(x):
    return x * jax.nn.sigmoid(x)


def _softplus(x):
    return jnp.maximum(x, 0.0) + jnp.log1p(jnp.exp(-jnp.abs(x)))


def _mod_kernel(c_ref, w_ref, b_ref, o_ref):
    s = _silu(c_ref[...])
    o_ref[...] = _mm(s, w_ref[...]) + b_ref[...]


def _modulation(c_all, w_mod, b_mod):
    n, d = c_all.shape
    m = w_mod.shape[1]
    tn = 1024
    return pl.pallas_call(
        _mod_kernel,
        out_shape=jax.ShapeDtypeStruct((n, m), F32),
        grid=(m // tn,),
        in_specs=[pl.BlockSpec((n, d), lambda j: (0, 0)),
                  pl.BlockSpec((d, tn), lambda j: (0, j)),
                  pl.BlockSpec((1, tn), lambda j: (0, j))],
        out_specs=pl.BlockSpec((n, tn), lambda j: (0, j)),
        compiler_params=_params(("arbitrary",)),
        name="modulation",
    )(c_all, w_mod, b_mod.reshape(1, m))


def _mod_spec(arr, tm, rows_per_group):
    if arr.ndim == 3:
        tiles = rows_per_group // tm
        return pl.BlockSpec((None, 1, arr.shape[-1]), lambda i, *_: (i // tiles, 0, 0))
    return pl.BlockSpec((tm, arr.shape[-1]), lambda i, *_: (i, 0))


def _proj_kernel(x_ref, sc_ref, sh_ref, g_ref, w_ref, wba_ref, o_ref, ba_ref, h_scr):
    @pl.when(pl.program_id(1) == 0)
    def _():
        h = (_rms(x_ref[...]) * g_ref[...]) * (1.0 + sc_ref[...]) + sh_ref[...]
        hb = h.astype(BF16)
        h_scr[...] = hb
        ba_ref[...] = jnp.dot(hb, wba_ref[...], preferred_element_type=F32)

    o_ref[...] = jnp.dot(h_scr[...], w_ref[...], preferred_element_type=F32)


def _in_proj(x, scale, shift, g, w_main, w_ba, rows_per_group):
    m, d = x.shape
    n = w_main.shape[1]
    tm = min(m, 1024, rows_per_group if scale.ndim == 3 else m)
    tn = 1024
    return pl.pallas_call(
        _proj_kernel,
        out_shape=(jax.ShapeDtypeStruct((m, n), F32), jax.ShapeDtypeStruct((m, LANES), F32)),
        grid=(m // tm, n // tn),
        in_specs=[pl.BlockSpec((tm, d), lambda i, j: (i, 0)),
                  _mod_spec(scale, tm, rows_per_group),
                  _mod_spec(shift, tm, rows_per_group),
                  pl.BlockSpec((1, d), lambda i, j: (0, 0)),
                  pl.BlockSpec((d, tn), lambda i, j: (0, j)),
                  pl.BlockSpec((d, LANES), lambda i, j: (0, 0))],
        out_specs=(pl.BlockSpec((tm, tn), lambda i, j: (i, j)),
                   pl.BlockSpec((tm, LANES), lambda i, j: (i, 0))),
        scratch_shapes=[pltpu.VMEM((tm, d), BF16)],
        compiler_params=_params(("arbitrary", "arbitrary")),
        name="in_proj",
    )(x, scale, shift, g.reshape(1, d), w_main, w_ba)


def _mixer_prompt_kernel(proj_ref, ba_ref, caw_ref, gcw_ref, alog_ref, dtb_ref, gco_ref, gng_ref,
                         mix_ref, ha_ref, hq_ref, s_ref, extu, extq, qc_scr, s_scr):
    c = CHUNK
    dc = N_HEADS * HEAD_D
    dq = 3 * dc
    t = pl.program_id(1)

    @pl.when(t == 0)
    def _():
        extu[0:8, :] = jnp.zeros((8, dc), F32)
        extq[0:8, :] = jnp.zeros((8, dq), F32)
        s_scr[...] = jnp.zeros_like(s_scr)

    u = proj_ref[:, dc:2 * dc] * proj_ref[:, 2 * dc:3 * dc]
    extu[8:8 + c, :] = u
    caw = caw_ref[...]
    ya = caw[0:1] * extu[6:6 + c, :] + caw[1:2] * extu[7:7 + c, :] + caw[2:3] * u
    ya = proj_ref[:, 0:dc] * ya
    mix_ref[:, 0:dc] = (_rms(ya) * gco_ref[...]).astype(BF16)
    last_u = extu[6 + c:8 + c, :]
    extu[6:8, :] = last_u

    qkv = proj_ref[:, 3 * dc:3 * dc + dq]
    extq[8:8 + c, :] = qkv
    gcw = gcw_ref[...]
    qc = (gcw[0:1] * extq[5:5 + c, :] + gcw[1:2] * extq[6:6 + c, :]
          + gcw[2:3] * extq[7:7 + c, :] + gcw[3:4] * qkv)
    qc_scr[...] = _silu(qc)
    last_q = extq[5 + c:8 + c, :]
    extq[5:8, :] = last_q

    @pl.when(t == pl.num_programs(1) - 1)
    def _():
        ha_ref[...] = last_u
        hq_ref[...] = last_q

    ba = ba_ref[...]
    beta_all = jax.nn.sigmoid(ba)
    g_all = -jnp.exp(alog_ref[...]) * _softplus(ba + dtb_ref[...])
    row = lax.broadcasted_iota(I32, (c, c), 0)
    col = lax.broadcasted_iota(I32, (c, c), 1)
    causal = row >= col
    strict = row > col
    lower = jnp.where(causal, 1.0, 0.0).astype(F32)
    upper = jnp.where(row <= col, 1.0, 0.0).astype(F32)
    gc_all = jnp.dot(lower, g_all, precision=HIGHEST, preferred_element_type=F32)
    gc_t = lax.dot_general(g_all, upper, (((0,), (0,)), ((), ())), precision=HIGHEST,
                           preferred_element_type=F32)

    heads = range(N_HEADS)
    qn, kn, vb, kb, kbg, qg, kg, decay, s_decay = ([] for _ in range(9))
    for h in heads:
        lo = h * HEAD_D
        q = qc_scr[:, lo:lo + HEAD_D]
        k = qc_scr[:, dc + lo:dc + lo + HEAD_D]
        v = qc_scr[:, 2 * dc + lo:2 * dc + lo + HEAD_D]
        qn_h = q * lax.rsqrt(jnp.sum(q * q, axis=-1, keepdims=True) + EPS) * (HEAD_D ** -0.5)
        kn_h = k * lax.rsqrt(jnp.sum(k * k, axis=-1, keepdims=True) + EPS)
        beta = beta_all[:, h:h + 1]
        gcc = gc_all[:, N_HEADS + h:N_HEADS + h + 1]
        gcr = gc_t[N_HEADS + h:N_HEADS + h + 1, :]
        gl = gc_all[c - 1:c, N_HEADS + h:N_HEADS + h + 1]
        eg = jnp.exp(gcc)
        kb_h = kn_h * beta
        qn.append(qn_h)
        kn.append(kn_h)
        vb.append(v * beta)
        kb.append(kb_h)
        kbg.append(kb_h * eg)
        qg.append(qn_h * eg)
        kg.append(kn_h * jnp.exp(gl - gcc))
        decay.append(jnp.where(causal, jnp.exp(jnp.minimum(gcc - gcr, 0.0)), 0.0))
        s_decay.append(jnp.exp(gl))

    kq = [_mm_nt(jnp.concatenate([kb[h], qn[h]], axis=0), kn[h]) for h in heads]
    a_mat = [jnp.where(strict, kq[h][:c] * decay[h], 0.0) for h in heads]
    qk = [kq[h][c:] * decay[h] for h in heads]
    n_mat = [-a_mat[h] for h in heads]
    p = a_mat
    size = 2
    while size < c:
        p = [_mm(p[h], p[h]) for h in heads]
        n_p = [_mm(n_mat[h], p[h]) for h in heads]
        n_mat = [n_mat[h] + p[h] + n_p[h] for h in heads]
        size *= 2
    rhs = [jnp.concatenate([vb[h], kbg[h]], axis=-1) for h in heads]
    uw = [rhs[h] + _mm(n_mat[h], rhs[h]) for h in heads]
    s_old = [s_scr[h] for h in heads]
    ws = [_mm(jnp.concatenate([uw[h][:, HEAD_D:], qg[h]], axis=0), s_old[h]) for h in heads]
    v_new = [uw[h][:, :HEAD_D] - ws[h][:c] for h in heads]
    fin = [_mm(jnp.concatenate([qk[h], kg[h].T], axis=0), v_new[h]) for h in heads]
    for h in heads:
        lo = h * HEAD_D
        s_scr[h] = s_old[h] * s_decay[h] + fin[h][c:]
        o = ws[h][c:] + fin[h][:c]
        z = proj_ref[:, 3 * dc + dq + lo:3 * dc + dq + lo + HEAD_D]
        yb = _rms(o) * gng_ref[...] * _silu(z)
        mix_ref[:, dc + lo:dc + lo + HEAD_D] = yb.astype(BF16)

    @pl.when(t == pl.num_programs(1) - 1)
    def _():
        s_ref[...] = s_scr[...]


def _lane_row(vec, offset):
    return jnp.zeros((1, LANES), F32).at[0, offset:offset + vec.shape[0]].set(vec.astype(F32))


def _mixer_prompt(proj, ba, bsz, seq, conv_a_w, gdn_conv_w, a_log, dt_bias, g_conv_out, gdn_norm_g):
    c = CHUNK
    dc = N_HEADS * HEAD_D
    dq = 3 * dc
    dproj = proj.shape[1]
    nt = seq // c
    const = lambda shape: pl.BlockSpec(shape, lambda b, t: (0,) * len(shape))
    return pl.pallas_call(
        _mixer_prompt_kernel,
        out_shape=(jax.ShapeDtypeStruct((bsz * seq, 2 * dc), BF16),
                   jax.ShapeDtypeStruct((bsz, 2, dc), F32),
                   jax.ShapeDtypeStruct((bsz, 3, dq), F32),
                   jax.ShapeDtypeStruct((bsz, N_HEADS, HEAD_D, HEAD_D), F32)),
        grid=(bsz, nt),
        in_specs=[pl.BlockSpec((c, dproj), lambda b, t: (b * nt + t, 0)),
                  pl.BlockSpec((c, LANES), lambda b, t: (b * nt + t, 0)),
                  const((3, dc)), const((4, dq)), const((1, LANES)), const((1, LANES)),
                  const((1, dc)), const((1, HEAD_D))],
        out_specs=(pl.BlockSpec((c, 2 * dc), lambda b, t: (b * nt + t, 0)),
                   pl.BlockSpec((None, 2, dc), lambda b, t: (b, 0, 0)),
                   pl.BlockSpec((None, 3, dq), lambda b, t: (b, 0, 0)),
                   pl.BlockSpec((None, N_HEADS, HEAD_D, HEAD_D), lambda b, t: (b, 0, 0, 0))),
        scratch_shapes=[pltpu.VMEM((8 + c, dc), F32), pltpu.VMEM((8 + c, dq), F32),
                        pltpu.VMEM((c, dq), F32), pltpu.VMEM((N_HEADS, HEAD_D, HEAD_D), F32)],
        compiler_params=_params(("arbitrary", "arbitrary")),
        name="mixer_prompt",
    )(proj, ba, conv_a_w, gdn_conv_w, _lane_row(a_log, N_HEADS), _lane_row(dt_bias, N_HEADS),
      g_conv_out.reshape(1, dc), gdn_norm_g.reshape(1, HEAD_D))


SAMPLE_GROUP = 16


def _mixer_sample_kernel(proj_ref, ba_ref, hista_ref, histq_ref, s_in_ref, caw_ref, gcw_ref, alog_ref,
                         dtb_ref, gco_ref, gng_ref, mix_ref, ha_ref, hq_ref, s_out_ref, qc_scr, o_scr):
    tb = SAMPLE_GROUP
    dc = N_HEADS * HEAD_D
    dq = 3 * dc

    u = proj_ref[:, dc:2 * dc] * proj_ref[:, 2 * dc:3 * dc]
    caw = caw_ref[...]
    ya = caw[0:1] * hista_ref[:, 0:dc] + caw[1:2] * hista_ref[:, dc:2 * dc] + caw[2:3] * u
    ya = proj_ref[:, 0:dc] * ya
    mix_ref[:, 0:dc] = _rms(ya) * gco_ref[...]
    ha_ref[:, 0:dc] = hista_ref[:, dc:2 * dc]
    ha_ref[:, dc:2 * dc] = u

    qkv = proj_ref[:, 3 * dc:3 * dc + dq]
    gcw = gcw_ref[...]
    qc = (gcw[0:1] * histq_ref[:, 0:dq] + gcw[1:2] * histq_ref[:, dq:2 * dq]
          + gcw[2:3] * histq_ref[:, 2 * dq:3 * dq] + gcw[3:4] * qkv)
    qc_scr[...] = _silu(qc)
    hq_ref[:, 0:dq] = histq_ref[:, dq:2 * dq]
    hq_ref[:, dq:2 * dq] = histq_ref[:, 2 * dq:3 * dq]
    hq_ref[:, 2 * dq:3 * dq] = qkv

    ba = ba_ref[...]
    beta_all = jax.nn.sigmoid(ba)
    eg_all = jnp.exp(-jnp.exp(alog_ref[...]) * _softplus(ba + dtb_ref[...]))

    for h in range(N_HEADS):
        lo = h * HEAD_D
        q = qc_scr[:, lo:lo + HEAD_D]
        k = qc_scr[:, dc + lo:dc + lo + HEAD_D]
        v = qc_scr[:, 2 * dc + lo:2 * dc + lo + HEAD_D]
        qn = q * lax.rsqrt(jnp.sum(q * q, axis=-1, keepdims=True) + EPS) * (HEAD_D ** -0.5)
        kn = k * lax.rsqrt(jnp.sum(k * k, axis=-1, keepdims=True) + EPS)
        qk = jnp.sum(qn * kn, axis=-1, keepdims=True)
        kn_t = kn.T
        qn_t = qn.T
        for b in range(tb):
            s_old = s_in_ref[b, h]
            kc = kn_t[:, b:b + 1]
            e = eg_all[b:b + 1, N_HEADS + h:N_HEADS + h + 1]
            ks = jnp.sum(s_old * kc, axis=0, keepdims=True)
            qs = jnp.sum(s_old * qn_t[:, b:b + 1], axis=0, keepdims=True)
            v_new = beta_all[b:b + 1, h:h + 1] * (v[b:b + 1, :] - e * ks)
            o_scr[b:b + 1, lo:lo + HEAD_D] = e * qs + qk[b:b + 1, :] * v_new
            s_out_ref[b, h] = s_old * e + kc * v_new
        z = proj_ref[:, 3 * dc + dq + lo:3 * dc + dq + lo + HEAD_D]
        o = o_scr[:, lo:lo + HEAD_D]
        mix_ref[:, dc + lo:dc + lo + HEAD_D] = _rms(o) * gng_ref[...] * _silu(z)


def _mixer_sample(proj, ba, hist_a, hist_q, s_in, conv_a_w, gdn_conv_w, a_log, dt_bias, g_conv_out,
                  gdn_norm_g):
    n = proj.shape[0]
    tb = SAMPLE_GROUP
    dc = N_HEADS * HEAD_D
    dq = 3 * dc
    dproj = proj.shape[1]
    const = lambda shape: pl.BlockSpec(shape, lambda i: (0,) * len(shape))
    rows = lambda width: pl.BlockSpec((tb, width), lambda i: (i, 0))
    state = pl.BlockSpec((tb, N_HEADS, HEAD_D, HEAD_D), lambda i: (i, 0, 0, 0))
    return pl.pallas_call(
        _mixer_sample_kernel,
        out_shape=(jax.ShapeDtypeStruct((n, 2 * dc), F32),
                   jax.ShapeDtypeStruct((n, 2 * dc), F32),
                   jax.ShapeDtypeStruct((n, 3 * dq), F32),
                   jax.ShapeDtypeStruct((n, N_HEADS, HEAD_D, HEAD_D), F32)),
        grid=(n // tb,),
        in_specs=[rows(dproj), rows(LANES), rows(2 * dc), rows(3 * dq), state,
                  const((3, dc)), const((4, dq)), const((1, LANES)), const((1, LANES)),
                  const((1, dc)), const((1, HEAD_D))],
        out_specs=(rows(2 * dc), rows(2 * dc), rows(3 * dq), state),
        scratch_shapes=[pltpu.VMEM((tb, dq), F32), pltpu.VMEM((tb, dc), F32)],
        compiler_params=_params(("arbitrary",)),
        name="mixer_sample",
    )(proj, ba, hist_a, hist_q, s_in, conv_a_w, gdn_conv_w, _lane_row(a_log, N_HEADS),
      _lane_row(dt_bias, N_HEADS), g_conv_out.reshape(1, dc), gdn_norm_g.reshape(1, HEAD_D))


def _post_mix_kernel(mix_ref, x_ref, gate_ref, sc_ref, sh_ref, gpost_ref, gpre_ref, wout_ref,
                     rwh_ref, rwl_ref, rb_ref, x1_ref, h2_ref, lg_ref):
    mix =jnp.dot(mix_ref[...].astype(BF16), wout_ref[...], preferred_element_type=F32)
    x1 = x_ref[...] + gate_ref[...] * (_rms(mix) * gpost_ref[...])
    x1_ref[...] = x1
    h2 = (_rms(x1) * gpre_ref[...]) * (1.0 + sc_ref[...]) + sh_ref[...]
    h2_ref[...] = h2
    hi = h2.astype(BF16)
    lo = (h2 - hi.astype(F32)).astype(BF16)
    rwh = rwh_ref[...]
    lg_ref[...] = (jnp.dot(hi, rwh, preferred_element_type=F32)
                   + jnp.dot(lo, rwh, preferred_element_type=F32)
                   + jnp.dot(hi, rwl_ref[...], preferred_element_type=F32) + rb_ref[...])


def _post_mix(mix_in, x, gate, scale, shift, g_post, g_pre, w_out, rw_hi, rw_lo, rb, rows_per_group):
    m, d = x.shape
    tm = min(m, 512, rows_per_group if gate.ndim == 3 else m)
    const = lambda shape: pl.BlockSpec(shape, lambda i: (0,) * len(shape))
    rows = lambda width: pl.BlockSpec((tm, width), lambda i: (i, 0))
    return pl.pallas_call(
        _post_mix_kernel,
        out_shape=(jax.ShapeDtypeStruct((m, d), F32), jax.ShapeDtypeStruct((m, d), F32),
                   jax.ShapeDtypeStruct((m, LANES), F32)),
        grid=(m // tm,),
        in_specs=[rows(d), rows(d),
                  _mod_spec(gate, tm, rows_per_group),
                  _mod_spec(scale, tm, rows_per_group),
                  _mod_spec(shift, tm, rows_per_group),
                  const((1, d)), const((1, d)), const((d, d)),
                  const((d, LANES)), const((d, LANES)), const((1, LANES))],
        out_specs=(rows(d), rows(d), rows(LANES)),
        compiler_params=_params(("arbitrary",)),
        name="post_mix",
    )(mix_in, x, gate, scale, shift, g_post.reshape(1, d), g_pre.reshape(1, d), w_out, rw_hi, rw_lo, rb)


ROUTE_TOKENS = 128


def _route_kernel(lgp_ref, lgs_ref, idx_ref, p_ref, rank_ref, cnt_ref, carry, *, prompt_tiles):
    tm = lgp_ref.shape[0]

    @pl.when(pl.program_id(0) == 0)
    def _():
        carry[...] = jnp.zeros_like(carry)

    l = jnp.where(pl.program_id(0) < prompt_tiles, lgp_ref[...], lgs_ref[...])
    lane = lax.broadcasted_iota(I32, l.shape, 1)
    lane_f = lane.astype(F32)
    vals, hots = [], []
    idx_out = jnp.zeros(l.shape, F32)
    for k in range(TOP_K):
        m = jnp.max(l, axis=-1, keepdims=True)
        idx = jnp.min(jnp.where(l == m, lane_f, float(LANES)), axis=-1, keepdims=True)
        hot = lane_f == idx
        vals.append(m)
        hots.append(hot)
        idx_out = jnp.where(lane == k, idx, idx_out)
        l = jnp.where(hot, -jnp.inf, l)
    exps = [jnp.exp(v - vals[0]) for v in vals]
    denom = exps[0] + exps[1] + exps[2] + exps[3]
    p_out = jnp.zeros(l.shape, F32)
    for k in range(TOP_K):
        p_out = jnp.where(lane == k, exps[k] / denom, p_out)
    member = jnp.where(hots[0] | hots[1] | hots[2] | hots[3], 1.0, 0.0).astype(F32)
    row = lax.broadcasted_iota(I32, (tm, tm), 0)
    col = lax.broadcasted_iota(I32, (tm, tm), 1)
    before = jnp.where(row > col, 1.0, 0.0).astype(BF16)
    prefix = jnp.dot(before, member.astype(BF16), preferred_element_type=F32) + carry[...]
    rank_out = jnp.zeros(l.shape, F32)
    for k in range(TOP_K):
        r = jnp.sum(jnp.where(hots[k], prefix, 0.0), axis=-1, keepdims=True)
        rank_out = jnp.where(lane == k, r, rank_out)
    carry[...] = carry[...] + jnp.sum(member, axis=0, keepdims=True)
    idx_ref[...] = idx_out.astype(I32)
    p_ref[...] = p_out
    rank_ref[...] = rank_out.astype(I32)
    cnt_ref[...] = carry[...]


def _route(logits_p, logits_s):
    tm = ROUTE_TOKENS
    pt = logits_p.shape[0] // tm
    n = logits_p.shape[0] + logits_s.shape[0]
    tile = pl.BlockSpec((tm, LANES), lambda i: (i, 0))
    return pl.pallas_call(
        functools.partial(_route_kernel, prompt_tiles=pt),
        out_shape=(jax.ShapeDtypeStruct((n, LANES), I32), jax.ShapeDtypeStruct((n, LANES), F32),
                   jax.ShapeDtypeStruct((n, LANES), I32), jax.ShapeDtypeStruct((1, LANES), F32)),
        grid=(n // tm,),
        in_specs=[pl.BlockSpec((tm, LANES), lambda i: (jnp.minimum(i, pt - 1), 0)),
                  pl.BlockSpec((tm, LANES), lambda i: (jnp.maximum(i - pt, 0), 0))],
        out_specs=(tile, tile, tile, pl.BlockSpec((1, LANES), lambda i: (0, 0))),
        scratch_shapes=[pltpu.VMEM((1, LANES), F32)],
        compiler_params=_params(("arbitrary",)),
        name="route",
    )(logits_p, logits_s)


INVERT_TOKENS = 128


def _invert_kernel(slot_ref, table_ref, *, pad_token):
    i = pl.program_id(0)
    pairs = INVERT_TOKENS * TOP_K

    @pl.when(i == 0)
    def _():
        def fill(r, carry):
            table_ref[r] = pad_token
            return carry

        lax.fori_loop(0, table_ref.shape[0], fill, 0, unroll=8)

    def put(p, carry):
        table_ref[slot_ref[0, p]] = i * INVERT_TOKENS + (p >> TOP_K_SHIFT)
        return carry

    lax.fori_loop(0, pairs, put, 0, unroll=8)


def _invert(slots, n_rows, pad_token):
    n = slots.shape[0]
    nt = n // INVERT_TOKENS
    pairs = INVERT_TOKENS * TOP_K
    return pl.pallas_call(
        functools.partial(_invert_kernel, pad_token=pad_token),
        out_shape=jax.ShapeDtypeStruct((n_rows,), I32),
        grid=(nt,),
        in_specs=[pl.BlockSpec((None, 1, pairs), lambda i: (i, 0, 0), memory_space=pltpu.SMEM)],
        out_specs=pl.BlockSpec(memory_space=pltpu.SMEM),
        compiler_params=_params(("arbitrary",)),
        name="invert",
    )(slots.reshape(nt, 1, pairs))


ROW_UNROLL = 8
ROW_UNROLL_SHIFT = 3


def _dispatch_kernel(na_ref, split_ref, tok_ref, tok_next_ref, h2p_ref, h2s_ref, xs_ref, buf, sem, *,
                     n_prompt):
    i = pl.program_id(0)
    tile = EXPERT_ROW_TILE
    n_active = na_ref[0]

    def issue(tok, split, slot):
        def prompt_copy(r):
            pltpu.make_async_copy(h2p_ref.at[pl.ds(tok[0, r], 1)], buf.at[slot, pl.ds(r, 1)],
                                  sem.at[slot]).start()

        def sample_copy(r):
            pltpu.make_async_copy(h2s_ref.at[pl.ds(tok[0, r] - n_prompt, 1)], buf.at[slot, pl.ds(r, 1)],
                                  sem.at[slot]).start()

        def rows(lo, hi, fn):
            def body(r, carry):
                fn(r)
                return carry
            lax.fori_loop(lo, hi, body, 0)

        def groups(lo, hi, fn):
            def body(g, carry):
                for u in range(ROW_UNROLL):
                    fn(g * ROW_UNROLL + u)
                return carry
            lax.fori_loop(lo, hi, body, 0)

        whole = split >> ROW_UNROLL_SHIFT
        first = (split + ROW_UNROLL - 1) >> ROW_UNROLL_SHIFT
        groups(0, whole, prompt_copy)
        rows(whole * ROW_UNROLL, split, prompt_copy)
        rows(split, first * ROW_UNROLL, sample_copy)
        groups(first, tile // ROW_UNROLL, sample_copy)

    @pl.when(i == 0)
    def _():
        issue(tok_ref, split_ref[0], 0)

    @pl.when(i + 1 < n_active)
    def _():
        issue(tok_next_ref, split_ref[i + 1], (i + 1) & 1)

    @pl.when(i < n_active)
    def _():
        slot = i & 1
        pltpu.make_async_copy(h2p_ref.at[pl.ds(0, tile)], buf.at[slot], sem.at[slot]).wait()
        xs_ref[...] = buf[slot].astype(BF16)

    @pl.when(i >= n_active)
    def _():
        xs_ref[...] = jnp.zeros_like(xs_ref)


def _dispatch(h2_p, h2_s, table, split, n_active):
    d = h2_p.shape[1]
    tile = EXPERT_ROW_TILE
    n_rows = table.shape[0]
    nt = n_rows // tile
    return pl.pallas_call(
        functools.partial(_dispatch_kernel, n_prompt=h2_p.shape[0]),
        out_shape=jax.ShapeDtypeStruct((n_rows, d), BF16),
        grid_spec=pltpu.PrefetchScalarGridSpec(
            num_scalar_prefetch=2,
            grid=(nt,),
            in_specs=[pl.BlockSpec((None, 1, tile), lambda i, na, sp: (i, 0, 0), memory_space=pltpu.SMEM),
                      pl.BlockSpec((None, 1, tile), lambda i, na, sp: (jnp.minimum(i + 1, nt - 1), 0, 0),
                                   memory_space=pltpu.SMEM),
                      pl.BlockSpec(memory_space=pl.ANY), pl.BlockSpec(memory_space=pl.ANY)],
            out_specs=pl.BlockSpec((tile, d), lambda i, na, sp: (i, 0)),
            scratch_shapes=[pltpu.VMEM((2, tile, d), F32), pltpu.SemaphoreType.DMA((2,))]),
        compiler_params=_params(("arbitrary",)),
        name="dispatch",
    )(n_active, split, table.reshape(nt, 1, tile), table.reshape(nt, 1, tile), h2_p, h2_s)


def _stream_expert_weights(te_ref, first_ref, run_ref, next_ref, meta_ref, n_col_blocks, copies, consume):
    j = pl.program_id(0)
    i = pl.program_id(1)
    n_active = meta_ref[0]
    n_runs = meta_ref[1]

    @pl.when(jnp.logical_and(j == 0, i == 0))
    def _():
        for cp in copies(te_ref[0], 0, 0):
            cp.start()

    @pl.when(jnp.logical_and(i < n_active, first_ref[i] == 1))
    def _():
        slot = (j * n_runs + run_ref[i]) & 1
        for cp in copies(te_ref[i], j, slot):
            cp.wait()
        last = run_ref[i] == n_runs - 1
        e_next = jnp.where(last, te_ref[0], next_ref[i])
        j_next = jnp.where(last, j + 1, j)

        @pl.when(j_next < n_col_blocks)
        def _():
            for cp in copies(e_next, j_next, 1 - slot):
                cp.start()

        consume(slot)


def _gate_up_kernel(te_ref, first_ref, run_ref, next_ref, meta_ref, x_ref, bg_ref, bu_ref, w_hbm, o_ref,
                    wbuf, wg_b, wu_b, sem, *, nj, tn):
    i = pl.program_id(1)
    active = i < meta_ref[0]

    def copies(e, jj, slot):
        return [pltpu.make_async_copy(w_hbm.at[e, :, pl.ds(pl.multiple_of((c * nj + jj) * tn, tn), tn)],
                                      wbuf.at[slot, c], sem.at[slot, c]) for c in range(2)]

    def consume(slot):
        wg_b[...] = wbuf[slot, 0].astype(BF16)
        wu_b[...] = wbuf[slot, 1].astype(BF16)

    _stream_expert_weights(te_ref, first_ref, run_ref, next_ref, meta_ref, nj, copies, consume)

    @pl.when(active)
    def _():
        x = x_ref[...]
        gate = jnp.dot(x, wg_b[...], preferred_element_type=F32) + bg_ref[...]
        up = jnp.dot(x, wu_b[...], preferred_element_type=F32) + bu_ref[...]
        gate = jnp.minimum(gate, SWIGLU_LIMIT)
        up = jnp.clip(up, -SWIGLU_LIMIT, SWIGLU_LIMIT)
        o_ref[...] = ((up + 1.0) * gate * jax.nn.sigmoid(SWIGLU_ALPHA * gate)).astype(BF16)

    @pl.when(jnp.logical_not(active))
    def _():
        o_ref[...] = jnp.zeros_like(o_ref)


def _expert_gate_up(xs, w_gu, b_gu, sched):
    n_rows, d = xs.shape
    n_exp, _, f2 = w_gu.shape
    f = f2 // 2
    tm = EXPERT_ROW_TILE
    tn = 1024
    nj = f // tn
    row = lambda i, meta: jnp.minimum(i, meta[0] - 1)
    return pl.pallas_call(
        functools.partial(_gate_up_kernel, nj=nj, tn=tn),
        out_shape=jax.ShapeDtypeStruct((n_rows, f), BF16),
        grid_spec=pltpu.PrefetchScalarGridSpec(
            num_scalar_prefetch=5,
            grid=(nj, n_rows // tm),
            in_specs=[pl.BlockSpec((tm, d), lambda j, i, te, fi, ru, nx, meta: (row(i, meta), 0)),
                      pl.BlockSpec((None, 1, tn), lambda j, i, te, fi, ru, nx, meta: (te[row(i, meta)], 0, j)),
                      pl.BlockSpec((None, 1, tn),
                                   lambda j, i, te, fi, ru, nx, meta: (te[row(i, meta)], 0, nj + j)),
                      pl.BlockSpec(memory_space=pl.ANY)],
            out_specs=pl.BlockSpec((tm, tn), lambda j, i, te, fi, ru, nx, meta: (i, j)),
            scratch_shapes=[pltpu.VMEM((2, 2, d, tn), F32), pltpu.VMEM((d, tn), BF16),
                            pltpu.VMEM((d, tn), BF16), pltpu.SemaphoreType.DMA((2, 2))]),
        compiler_params=_params(("arbitrary", "arbitrary")),
        name="expert_gate_up",
    )(*sched, xs, b_gu.reshape(n_exp, 1, f2), b_gu.reshape(n_exp, 1, f2), w_gu)


def _down_kernel(te_ref, first_ref, run_ref, next_ref, meta_ref, a_ref, b_ref, w_hbm, o_ref, wbuf, w_b, sem):
    i = pl.program_id(1)
    active = i < meta_ref[0]

    def copies(e, jj, slot):
        return [pltpu.make_async_copy(w_hbm.at[e], wbuf.at[slot], sem.at[slot])]

    def consume(slot):
        w_b[...] = wbuf[slot].astype(BF16)

    _stream_expert_weights(te_ref, first_ref, run_ref, next_ref, meta_ref, 1, copies, consume)

    @pl.when(active)
    def _():
        o_ref[...] = jnp.dot(a_ref[...], w_b[...], preferred_element_type=F32) + b_ref[...]

    @pl.when(jnp.logical_not(active))
    def _():
        o_ref[...] = jnp.zeros_like(o_ref)


def _expert_down(act, w_d, b_d, sched):
    n_rows, f = act.shape
    n_exp, _, d = w_d.shape
    tm = EXPERT_ROW_TILE
    row = lambda i, meta: jnp.minimum(i, meta[0] - 1)
    return pl.pallas_call(
        _down_kernel,
        out_shape=jax.ShapeDtypeStruct((n_rows, d), F32),
        grid_spec=pltpu.PrefetchScalarGridSpec(
            num_scalar_prefetch=5,
            grid=(1, n_rows // tm),
            in_specs=[pl.BlockSpec((tm, f), lambda j, i, te, fi, ru, nx, meta: (row(i, meta), 0)),
                      pl.BlockSpec((None, 1, d), lambda j, i, te, fi, ru, nx, meta: (te[row(i, meta)], 0, 0)),
                      pl.BlockSpec(memory_space=pl.ANY)],
            out_specs=pl.BlockSpec((tm, d), lambda j, i, te, fi, ru, nx, meta: (i, 0)),
            scratch_shapes=[pltpu.VMEM((2, f, d), F32), pltpu.VMEM((f, d), BF16),
                            pltpu.SemaphoreType.DMA((2,))]),
        compiler_params=_params(("arbitrary", "arbitrary")),
        name="expert_down",
    )(*sched, act, b_d.reshape(n_exp, 1, d), w_d)


COMBINE_TOKENS = 128


def _combine_kernel(slot_ref, slot_next_ref, p_ref, x1_ref, gate_ref, g_ref, y_ref, o_ref, buf, sem):
    i = pl.program_id(0)
    tm = COMBINE_TOKENS
    pairs = tm * TOP_K

    def issue(slots, b):
        def body(g, carry):
            for u in range(ROW_UNROLL):
                p = g * ROW_UNROLL + u
                tok = g * (ROW_UNROLL // TOP_K) + u // TOP_K
                pltpu.make_async_copy(y_ref.at[pl.ds(slots[0, p], 1)],
                                      buf.at[b, u % TOP_K, pl.ds(tok, 1)], sem.at[b]).start()
            return carry
        lax.fori_loop(0, pairs // ROW_UNROLL, body, 0)

    @pl.when(i == 0)
    def _():
        issue(slot_ref, 0)

    @pl.when(i + 1 < pl.num_programs(0))
    def _():
        issue(slot_next_ref, (i + 1) & 1)

    b = i & 1
    for k in range(TOP_K):
        pltpu.make_async_copy(y_ref.at[pl.ds(0, tm)], buf.at[b, k], sem.at[b]).wait()
    probs = p_ref[...]
    f = probs[:, 0:1] * buf[b, 0]
    for k in range(1, TOP_K):
        f = f + probs[:, k:k + 1] * buf[b, k]
    o_ref[...] = x1_ref[...] + gate_ref[...] * (_rms(f) * g_ref[...])


def _combine(y, slots, probs, x1, gate, g_post, rows_per_group):
    m, d = x1.shape
    tm = COMBINE_TOKENS
    nt = m // tm
    pairs = tm * TOP_K
    return pl.pallas_call(
        _combine_kernel,
        out_shape=jax.ShapeDtypeStruct((m, d), F32),
        grid=(nt,),
        in_specs=[pl.BlockSpec((None, 1, pairs), lambda i: (i, 0, 0), memory_space=pltpu.SMEM),
                  pl.BlockSpec((None, 1, pairs), lambda i: (jnp.minimum(i + 1, nt - 1), 0, 0),
                               memory_space=pltpu.SMEM),
                  pl.BlockSpec((tm, LANES), lambda i: (i, 0)),
                  pl.BlockSpec((tm, d), lambda i: (i, 0)),
                  _mod_spec(gate, tm, rows_per_group),
                  pl.BlockSpec((1, d), lambda i: (0, 0)),
                  pl.BlockSpec(memory_space=pl.ANY)],
        out_specs=pl.BlockSpec((tm, d), lambda i: (i, 0)),
        scratch_shapes=[pltpu.VMEM((2, TOP_K, tm, d), F32), pltpu.SemaphoreType.DMA((2,))],
        compiler_params=_params(("arbitrary",)),
        name="combine",
    )(slots.reshape(nt, 1, pairs), slots.reshape(nt, 1, pairs), probs, x1, gate, g_post.reshape(1, d), y)


def _moe(h2_p, h2_s, logits_p, logits_s, n_exp, w_gu, b_gu, w_d, b_d):
    n = h2_p.shape[0] + h2_s.shape[0]
    tile = EXPERT_ROW_TILE
    top_i, probs, rank, counts = _route(logits_p, logits_s)
    counts = counts[0, :n_exp].astype(I32)
    cap = (counts + tile - 1) // tile * tile
    ends = jnp.cumsum(cap)
    offs = ends - cap
    n_tiles = (n * TOP_K) // tile + n_exp
    n_active = (ends[-1] // tile).astype(I32).reshape(1)
    tile_id = jnp.arange(n_tiles, dtype=I32)
    tile_expert = jnp.minimum(jnp.sum(ends[None, :] <= tile_id[:, None] * tile, axis=1), n_exp - 1).astype(I32)
    first = ((tile_id == 0) | (tile_expert != jnp.roll(tile_expert, 1))) & (tile_id < n_active[0])
    run = jnp.cumsum(first.astype(I32)) - 1
    next_expert = tile_expert[jnp.minimum(ends[tile_expert] // tile, n_tiles - 1)]
    sched = (tile_expert, first.astype(I32), run.astype(I32), next_expert.astype(I32),
             jnp.stack([n_active[0], jnp.sum(first.astype(I32))]).astype(I32))
    expert_ids = jnp.arange(n_exp, dtype=I32)
    pair_offs = jnp.sum(jnp.where(top_i[:, :TOP_K, None] == expert_ids, offs, 0), axis=-1)
    slots = pair_offs + rank[:, :TOP_K]
    n_prompt = h2_p.shape[0]
    table = _invert(slots, n_tiles * tile, n_prompt)
    split = jnp.sum(table.reshape(n_tiles, tile) < n_prompt, axis=1).astype(I32)
    xs = _dispatch(h2_p, h2_s, table, split, n_active)
    act = _expert_gate_up(xs, w_gu, b_gu, sched)
    y = _expert_down(act, w_d, b_d, sched)
    return y, slots, probs


def _layer(xp, xs_, cp_mod, cs_mod, hist_a, hist_q, s0, g_pre_mix, g_post_mix, g_pre_ffn, g_post_ffn,
           w_in, conv_a_w, gdn_conv_w, a_log, dt_bias, g_conv_out, gdn_norm_g, w_out, router_w, router_b,
           w_gu, b_gu, w_d, b_d):
    bsz, seq, d = xp.shape
    ns = xs_.shape[0]
    n_exp = router_w.shape[1]
    dc = N_HEADS * HEAD_D
    d_main = 3 * dc + 3 * dc + dc
    xp2 = xp.reshape(bsz * seq, d)
    xs2 = xs_.reshape(ns, d)

    w_main = w_in[:, :d_main].astype(BF16)
    w_ba = jnp.zeros((d, LANES), BF16).at[:, :2 * N_HEADS].set(w_in[:, d_main:].astype(BF16))
    w_out_b = w_out.astype(BF16)
    rw = jnp.zeros((d, LANES), F32).at[:, :n_exp].set(router_w)
    rw_hi = rw.astype(BF16)
    rw_lo = (rw - rw_hi.astype(F32)).astype(BF16)
    rb = jnp.full((1, LANES), NEG_BIG, F32).at[0, :n_exp].set(router_b)

    mp = [cp_mod[:, i * d:(i + 1) * d].reshape(bsz, 1, d) for i in range(6)]
    ms = [cs_mod[:, i * d:(i + 1) * d] for i in range(6)]

    proj_p, ba_p = _in_proj(xp2, mp[1], mp[0], g_pre_mix, w_main, w_ba, seq)
    proj_s, ba_s = _in_proj(xs2, ms[1], ms[0], g_pre_mix, w_main, w_ba, 1)

    mix_p, ha_p, hq_p, s_p = _mixer_prompt(proj_p, ba_p, bsz, seq, conv_a_w, gdn_conv_w, a_log, dt_bias,
                                           g_conv_out, gdn_norm_g)
    mix_s, ha_s, hq_s, s_s = _mixer_sample(proj_s, ba_s, hist_a.reshape(ns, 2 * dc),
                                           hist_q.reshape(ns, 9 * dc), s0, conv_a_w, gdn_conv_w, a_log,
                                           dt_bias, g_conv_out, gdn_norm_g)

    x1_p, h2_p, lg_p = _post_mix(mix_p, xp2, mp[2], mp[4], mp[3], g_post_mix, g_pre_ffn, w_out_b, rw_hi,
                                 rw_lo, rb, seq)
    x1_s, h2_s, lg_s = _post_mix(mix_s, xs2, ms[2], ms[4], ms[3], g_post_mix, g_pre_ffn, w_out_b, rw_hi,
                                 rw_lo, rb, 1)

    y, slots, probs = _moe(h2_p, h2_s, lg_p, lg_s, n_exp, w_gu, b_gu, w_d, b_d)
    np_ = bsz * seq
    out_p = _combine(y, slots[:np_], probs[:np_], x1_p, mp[5], g_post_ffn, seq)
    out_s = _combine(y, slots[np_:], probs[np_:], x1_s, ms[5], g_post_ffn, 1)
    return (out_p.reshape(bsz, seq, d), out_s.reshape(ns, 1, d), ha_p, hq_p, s_p,
            ha_s.reshape(ns, 2, dc), hq_s.reshape(ns, 3, 3 * dc), s_s)


def kernel(x_prompt, x_sample, state_conv_a, state_gdn_conv, state_gdn_S, c_prompt, c_sample, w_mod, b_mod, g_pre_mix, g_post_mix, g_pre_ffn, g_post_ffn, w_in, conv_a_w, gdn_conv_w, gdn_a_log, gdn_dt_bias, g_conv_out, gdn_norm_g, w_out, router_w, router_b, exp_w_gate_up, exp_b_gate_up, exp_w_down, exp_b_down):
    depth = w_mod.shape[0]
    bp = x_prompt.shape[0]
    xp, xs_ = x_prompt, x_sample
    outs = [[] for _ in range(6)]
    for l in range(depth):
        mod = _modulation(jnp.concatenate([c_prompt, c_sample], axis=0), w_mod[l], b_mod[l])
        res = _layer(xp, xs_, mod[:bp], mod[bp:], state_conv_a[l], state_gdn_conv[l], state_gdn_S[l],
                     g_pre_mix[l], g_post_mix[l], g_pre_ffn[l], g_post_ffn[l], w_in[l], conv_a_w[l],
                     gdn_conv_w[l], gdn_a_log[l], gdn_dt_bias[l], g_conv_out[l], gdn_norm_g[l], w_out[l],
                     router_w[l], router_b[l], exp_w_gate_up[l], exp_b_gate_up[l], exp_w_down[l],
                     exp_b_down[l])
        xp, xs_ = res[0], res[1]
        for acc, r in zip(outs, res[2:]):
            acc.append(r)
    return (xp, xs_) + tuple(o[0][None] if depth == 1 else jnp.stack(o) for o in outs)
```

```python
import functools

import jax
import jax.numpy as jnp
from jax import lax
from jax.experimental import pallas as pl
from jax.experimental.pallas import tpu as pltpu

F32 = jnp.float32
BF16 = jnp.bfloat16
I32 = jnp.int32
HIGHEST = lax.Precision.HIGHEST

EPS = 1e-6
N_HEADS = 8
HEAD_D = 128
TOP_K = 4
TOP_K_SHIFT = 2
SWIGLU_LIMIT = 7.0
SWIGLU_ALPHA = 1.702
CHUNK = 64
LANES = 128
EXPERT_ROW_TILE = 256
NEG_BIG = -1e30
VMEM_LIMIT = 56 * 1024 * 1024


def _params(semantics, vmem=VMEM_LIMIT):
    return pltpu.CompilerParams(dimension_semantics=semantics, vmem_limit_bytes=vmem)


def _mm(a, b):
    return jnp.dot(a.astype(BF16), b.astype(BF16), preferred_element_type=F32)


def _mm_nt(a, b):
    return lax.dot_general(a.astype(BF16), b.astype(BF16), (((1,), (1,)), ((), ())),
                           preferred_element_type=F32)


def _rms(x):
    return x * lax.rsqrt(jnp.mean(x * x, axis=-1, keepdims=True) + EPS)


def _silu(x):
    return x * jax.nn.sigmoid(x)


def _softplus(x):
    return jnp.maximum(x, 0.0) + jnp.log1p(jnp.exp(-jnp.abs(x)))


def _mod_kernel(c_ref, w_ref, b_ref, o_ref):
    s = _silu(c_ref[...])
    o_ref[...] = _mm(s, w_ref[...]) + b_ref[...]


def _modulation(c_all, w_mod, b_mod):
    n, d = c_all.shape
    m = w_mod.shape[1]
    tn = 1024
    return pl.pallas_call(
        _mod_kernel,
        out_shape=jax.ShapeDtypeStruct((n, m), F32),
        grid=(m // tn,),
        in_specs=[pl.BlockSpec((n, d), lambda j: (0, 0)),
                  pl.BlockSpec((d, tn), lambda j: (0, j)),
                  pl.BlockSpec((1, tn), lambda j: (0, j))],
        out_specs=pl.BlockSpec((n, tn), lambda j: (0, j)),
        compiler_params=_params(("arbitrary",)),
        name="modulation",
    )(c_all, w_mod, b_mod.reshape(1, m))


def _mod_spec(arr, tm, rows_per_group):
    if arr.ndim == 3:
        tiles = rows_per_group // tm
        return pl.BlockSpec((None, 1, arr.shape[-1]), lambda i, *_: (i // tiles, 0, 0))
    return pl.BlockSpec((tm, arr.shape[-1]), lambda i, *_: (i, 0))


def _proj_kernel(x_ref, sc_ref, sh_ref, g_ref, w_ref, wba_ref, o_ref, ba_ref, h_scr):
    @pl.when(pl.program_id(1) == 0)
    def _():
        h = (_rms(x_ref[...]) * g_ref[...]) * (1.0 + sc_ref[...]) + sh_ref[...]
        hb = h.astype(BF16)
        h_scr[...] = hb
        ba_ref[...] = jnp.dot(hb, wba_ref[...], preferred_element_type=F32)

    o_ref[...] = jnp.dot(h_scr[...], w_ref[...], preferred_element_type=F32)


def _in_proj(x, scale, shift, g, w_main, w_ba, rows_per_group, n):
    m, d = x.shape
    tm = min(m, 1024, rows_per_group if scale.ndim == 3 else m)
    tn = 1024
    return pl.pallas_call(
        _proj_kernel,
        out_shape=(jax.ShapeDtypeStruct((m, n), F32), jax.ShapeDtypeStruct((m, LANES), F32)),
        grid=(m // tm, n // tn),
        in_specs=[pl.BlockSpec((tm, d), lambda i, j: (i, 0)),
                  _mod_spec(scale, tm, rows_per_group),
                  _mod_spec(shift, tm, rows_per_group),
                  pl.BlockSpec((1, d), lambda i, j: (0, 0)),
                  pl.BlockSpec((d, tn), lambda i, j: (0, j)),
                  pl.BlockSpec((d, LANES), lambda i, j: (0, 0))],
        out_specs=(pl.BlockSpec((tm, tn), lambda i, j: (i, j)),
                   pl.BlockSpec((tm, LANES), lambda i, j: (i, 0))),
        scratch_shapes=[pltpu.VMEM((tm, d), BF16)],
        compiler_params=_params(("arbitrary", "arbitrary")),
        name="in_proj",
    )(x, scale, shift, g.reshape(1, d), w_main, w_ba)


PROMPT_SEQS_PER_STEP = 2

def _mixer_prompt_kernel(proj_ref, ba_ref, caw_ref, gcw_ref, alog_ref, dtb_ref, gco_ref, gng_ref,
                         mix_ref, ha_ref, hq_ref, s_ref, extu, extq, qc_scr, s_scr, *, nseq):
    c = CHUNK
    dc = N_HEADS * HEAD_D
    dq = 3 * dc
    t = pl.program_id(1)
    is_last = t == pl.num_programs(1) - 1

    @pl.when(t == 0)
    def _():
        extu[:, 0:8, :] = jnp.zeros((nseq, 8, dc), F32)
        extq[:, 0:8, :] = jnp.zeros((nseq, 8, dq), F32)
        s_scr[...] = jnp.zeros_like(s_scr)

    row = lax.broadcasted_iota(I32, (c, c), 0)
    col = lax.broadcasted_iota(I32, (c, c), 1)
    causal = row >= col
    strict = row > col
    lower = jnp.where(causal, 1.0, 0.0).astype(F32)
    upper = jnp.where(row <= col, 1.0, 0.0).astype(F32)
    caw = caw_ref[...]
    gcw = gcw_ref[...]

    chains = [(sq, h) for sq in range(nseq) for h in range(N_HEADS)]
    heads = range(len(chains))
    qn, kn, vb, kb, kbg, qg, kg, decay, s_decay = ([] for _ in range(9))
    for sq in range(nseq):
        u = proj_ref[sq, :, dc:2 * dc] * proj_ref[sq, :, 2 * dc:3 * dc]
        extu[sq, 8:8 + c, :] = u
        ya = caw[0:1] * extu[sq, 6:6 + c, :] + caw[1:2] * extu[sq, 7:7 + c, :] + caw[2:3] * u
        ya = proj_ref[sq, :, 0:dc] * ya
        mix_ref[sq, :, 0:dc] = (_rms(ya) * gco_ref[...]).astype(BF16)
        last_u = extu[sq, 6 + c:8 + c, :]
        extu[sq, 6:8, :] = last_u

        qkv = proj_ref[sq, :, 3 * dc:3 * dc + dq]
        extq[sq, 8:8 + c, :] = qkv
        qc = (gcw[0:1] * extq[sq, 5:5 + c, :] + gcw[1:2] * extq[sq, 6:6 + c, :]
              + gcw[2:3] * extq[sq, 7:7 + c, :] + gcw[3:4] * qkv)
        qc_scr[sq] = _silu(qc)
        last_q = extq[sq, 5 + c:8 + c, :]
        extq[sq, 5:8, :] = last_q

        @pl.when(is_last)
        def _(sq=sq, last_u=last_u, last_q=last_q):
            ha_ref[sq] = last_u
            hq_ref[sq] = last_q

        ba = ba_ref[sq]
        beta_all = jax.nn.sigmoid(ba)
        g_all = -jnp.exp(alog_ref[...]) * _softplus(ba + dtb_ref[...])
        gc_all = jnp.dot(lower, g_all, precision=HIGHEST, preferred_element_type=F32)
        gc_t = lax.dot_general(g_all, upper, (((0,), (0,)), ((), ())), precision=HIGHEST,
                               preferred_element_type=F32)
        for h in range(N_HEADS):
            lo = h * HEAD_D
            q = qc_scr[sq, :, lo:lo + HEAD_D]
            k = qc_scr[sq, :, dc + lo:dc + lo + HEAD_D]
            v = qc_scr[sq, :, 2 * dc + lo:2 * dc + lo + HEAD_D]
            qn_h = q * lax.rsqrt(jnp.sum(q * q, axis=-1, keepdims=True) + EPS) * (HEAD_D ** -0.5)
            kn_h = k * lax.rsqrt(jnp.sum(k * k, axis=-1, keepdims=True) + EPS)
            beta = beta_all[:, h:h + 1]
            gcc = gc_all[:, N_HEADS + h:N_HEADS + h + 1]
            gcr = gc_t[N_HEADS + h:N_HEADS + h + 1, :]
            gl = gc_all[c - 1:c, N_HEADS + h:N_HEADS + h + 1]
            eg = jnp.exp(gcc)
            kb_h = kn_h * beta
            qn.append(qn_h)
            kn.append(kn_h)
            vb.append(v * beta)
            kb.append(kb_h)
            kbg.append(kb_h * eg)
            qg.append(qn_h * eg)
            kg.append(kn_h * jnp.exp(gl - gcc))
            decay.append(jnp.where(causal, jnp.exp(jnp.minimum(gcc - gcr, 0.0)), 0.0))
            s_decay.append(jnp.exp(gl))

    kq = [_mm_nt(jnp.concatenate([kb[h], qn[h]], axis=0), kn[h]) for h in heads]
    a_mat = [jnp.where(strict, kq[h][:c] * decay[h], 0.0) for h in heads]
    qk = [kq[h][c:] * decay[h] for h in heads]
    n_mat = [-a_mat[h] for h in heads]
    p = a_mat
    size = 2
    while size < c:
        p = [_mm(p[h], p[h]) for h in heads]
        n_p = [_mm(n_mat[h], p[h]) for h in heads]
        n_mat = [n_mat[h] + p[h] + n_p[h] for h in heads]
        size *= 2
    rhs = [jnp.concatenate([vb[h], kbg[h]], axis=-1) for h in heads]
    uw = [rhs[h] + _mm(n_mat[h], rhs[h]) for h in heads]
    s_old = [s_scr[sq, h] for sq, h in chains]
    ws = [_mm(jnp.concatenate([uw[h][:, HEAD_D:], qg[h]], axis=0), s_old[h]) for h in heads]
    v_new = [uw[h][:, :HEAD_D] - ws[h][:c] for h in heads]
    fin = [_mm(jnp.concatenate([qk[h], kg[h].T], axis=0), v_new[h]) for h in heads]
    for i, (sq, h) in enumerate(chains):
        lo = h * HEAD_D
        s_scr[sq, h] = s_old[i] * s_decay[i] + fin[i][c:]
        o = ws[i][c:] + fin[i][:c]
        z = proj_ref[sq, :, 3 * dc + dq + lo:3 * dc + dq + lo + HEAD_D]
        yb = _rms(o) * gng_ref[...] * _silu(z)
        mix_ref[sq, :, dc + lo:dc + lo + HEAD_D] = yb.astype(BF16)

    @pl.when(is_last)
    def _():
        s_ref[...] = s_scr[...]


def _lane_row(vec, offset):
    return jnp.zeros((1, LANES), F32).at[0, offset:offset + vec.shape[0]].set(vec.astype(F32))


def _mixer_prompt(proj, ba, bsz, seq, conv_a_w, gdn_conv_w, a_log, dt_bias, g_conv_out, gdn_norm_g):
    c = CHUNK
    dc = N_HEADS * HEAD_D
    dq = 3 * dc
    dproj = proj.shape[1]
    nt = seq // c
    const = lambda shape: pl.BlockSpec(shape, lambda b, t: (0,) * len(shape))
    nseq = PROMPT_SEQS_PER_STEP if bsz % PROMPT_SEQS_PER_STEP == 0 else 1
    seq_block = lambda *tail: pl.BlockSpec((nseq,) + tail, lambda b, t: (b, t) + (0,) * (len(tail) - 1))
    whole = lambda *tail: pl.BlockSpec((nseq,) + tail, lambda b, t: (b,) + (0,) * len(tail))
    mix, ha, hq, s_fin = pl.pallas_call(
        functools.partial(_mixer_prompt_kernel, nseq=nseq),
        out_shape=(jax.ShapeDtypeStruct((bsz, seq, 2 * dc), BF16),
                   jax.ShapeDtypeStruct((bsz, 2, dc), F32),
                   jax.ShapeDtypeStruct((bsz, 3, dq), F32),
                   jax.ShapeDtypeStruct((bsz, N_HEADS, HEAD_D, HEAD_D), F32)),
        grid=(bsz // nseq, nt),
        in_specs=[seq_block(c, dproj), seq_block(c, LANES),
                  const((3, dc)), const((4, dq)), const((1, LANES)), const((1, LANES)),
                  const((1, dc)), const((1, HEAD_D))],
        out_specs=(seq_block(c, 2 * dc), whole(2, dc), whole(3, dq), whole(N_HEADS, HEAD_D, HEAD_D)),
        scratch_shapes=[pltpu.VMEM((nseq, 8 + c, dc), F32), pltpu.VMEM((nseq, 8 + c, dq), F32),
                        pltpu.VMEM((nseq, c, dq), F32), pltpu.VMEM((nseq, N_HEADS, HEAD_D, HEAD_D), F32)],
        compiler_params=_params(("arbitrary", "arbitrary")),
        name="mixer_prompt",
    )(proj.reshape(bsz, seq, dproj), ba.reshape(bsz, seq, LANES), conv_a_w, gdn_conv_w,
      _lane_row(a_log, N_HEADS), _lane_row(dt_bias, N_HEADS), g_conv_out.reshape(1, dc),
      gdn_norm_g.reshape(1, HEAD_D))
    return mix.reshape(bsz * seq, 2 * dc), ha, hq, s_fin


SAMPLE_GROUP = 16


def _mixer_sample_kernel(proj_ref, ba_ref, hista_ref, histq_ref, s_in_ref, caw_ref, gcw_ref, alog_ref,
                         dtb_ref, gco_ref, gng_ref, mix_ref, ha_ref, hq_ref, s_out_ref, qc_scr, o_scr):
    tb = SAMPLE_GROUP
    dc = N_HEADS * HEAD_D
    dq = 3 * dc

    u = proj_ref[:, dc:2 * dc] * proj_ref[:, 2 * dc:3 * dc]
    caw = caw_ref[...]
    ya = caw[0:1] * hista_ref[:, 0:dc] + caw[1:2] * hista_ref[:, dc:2 * dc] + caw[2:3] * u
    ya = proj_ref[:, 0:dc] * ya
    mix_ref[:, 0:dc] = _rms(ya) * gco_ref[...]
    ha_ref[:, 0:dc] = hista_ref[:, dc:2 * dc]
    ha_ref[:, dc:2 * dc] = u

    qkv = proj_ref[:, 3 * dc:3 * dc + dq]
    gcw = gcw_ref[...]
    qc = (gcw[0:1] * histq_ref[:, 0:dq] + gcw[1:2] * histq_ref[:, dq:2 * dq]
          + gcw[2:3] * histq_ref[:, 2 * dq:3 * dq] + gcw[3:4] * qkv)
    qc_scr[...] = _silu(qc)
    hq_ref[:, 0:dq] = histq_ref[:, dq:2 * dq]
    hq_ref[:, dq:2 * dq] = histq_ref[:, 2 * dq:3 * dq]
    hq_ref[:, 2 * dq:3 * dq] = qkv

    ba = ba_ref[...]
    beta_all = jax.nn.sigmoid(ba)
    eg_all = jnp.exp(-jnp.exp(alog_ref[...]) * _softplus(ba + dtb_ref[...]))

    for h in range(N_HEADS):
        lo = h * HEAD_D
        q = qc_scr[:, lo:lo + HEAD_D]
        k = qc_scr[:, dc + lo:dc + lo + HEAD_D]
        v = qc_scr[:, 2 * dc + lo:2 * dc + lo + HEAD_D]
        qn = q * lax.rsqrt(jnp.sum(q * q, axis=-1, keepdims=True) + EPS) * (HEAD_D ** -0.5)
        kn = k * lax.rsqrt(jnp.sum(k * k, axis=-1, keepdims=True) + EPS)
        qk = jnp.sum(qn * kn, axis=-1, keepdims=True)
        kn_t = kn.T
        qn_t = qn.T
        for b in range(tb):
            s_old = s_in_ref[b, h]
            kc = kn_t[:, b:b + 1]
            e = eg_all[b:b + 1, N_HEADS + h:N_HEADS + h + 1]
            ks = jnp.sum(s_old * kc, axis=0, keepdims=True)
            qs = jnp.sum(s_old * qn_t[:, b:b + 1], axis=0, keepdims=True)
            v_new = beta_all[b:b + 1, h:h + 1] * (v[b:b + 1, :] - e * ks)
            o_scr[b:b + 1, lo:lo + HEAD_D] = e * qs + qk[b:b + 1, :] * v_new
            s_out_ref[b, h] = s_old * e + kc * v_new
        z = proj_ref[:, 3 * dc + dq + lo:3 * dc + dq + lo + HEAD_D]
        o = o_scr[:, lo:lo + HEAD_D]
        mix_ref[:, dc + lo:dc + lo + HEAD_D] = _rms(o) * gng_ref[...] * _silu(z)


def _mixer_sample(proj, ba, hist_a, hist_q, s_in, conv_a_w, gdn_conv_w, a_log, dt_bias, g_conv_out,
                  gdn_norm_g):
    n = proj.shape[0]
    tb = SAMPLE_GROUP
    dc = N_HEADS * HEAD_D
    dq = 3 * dc
    dproj = proj.shape[1]
    const = lambda shape: pl.BlockSpec(shape, lambda i: (0,) * len(shape))
    rows = lambda width: pl.BlockSpec((tb, width), lambda i: (i, 0))
    state = pl.BlockSpec((tb, N_HEADS, HEAD_D, HEAD_D), lambda i: (i, 0, 0, 0))
    return pl.pallas_call(
        _mixer_sample_kernel,
        out_shape=(jax.ShapeDtypeStruct((n, 2 * dc), F32),
                   jax.ShapeDtypeStruct((n, 2 * dc), F32),
                   jax.ShapeDtypeStruct((n, 3 * dq), F32),
                   jax.ShapeDtypeStruct((n, N_HEADS, HEAD_D, HEAD_D), F32)),
        grid=(n // tb,),
        in_specs=[rows(dproj), rows(LANES), rows(2 * dc), rows(3 * dq), state,
                  const((3, dc)), const((4, dq)), const((1, LANES)), const((1, LANES)),
                  const((1, dc)), const((1, HEAD_D))],
        out_specs=(rows(2 * dc), rows(2 * dc), rows(3 * dq), state),
        scratch_shapes=[pltpu.VMEM((tb, dq), F32), pltpu.VMEM((tb, dc), F32)],
        compiler_params=_params(("arbitrary",)),
        name="mixer_sample",
    )(proj, ba, hist_a, hist_q, s_in, conv_a_w, gdn_conv_w, _lane_row(a_log, N_HEADS),
      _lane_row(dt_bias, N_HEADS), g_conv_out.reshape(1, dc), gdn_norm_g.reshape(1, HEAD_D))


def _post_mix_kernel(mix_ref, x_ref, gate_ref, sc_ref, sh_ref, gpost_ref, gpre_ref, wout_ref,
                     rwh_ref, rwl_ref, rb_ref, x1_ref, h2_ref, lg_ref):
    mix =jnp.dot(mix_ref[...].astype(BF16), wout_ref[...], preferred_element_type=F32)
    x1 = x_ref[...] + gate_ref[...] * (_rms(mix) * gpost_ref[...])
    x1_ref[...] = x1
    h2 = (_rms(x1) * gpre_ref[...]) * (1.0 + sc_ref[...]) + sh_ref[...]
    h2_ref[...] = h2
    hi = h2.astype(BF16)
    lo = (h2 - hi.astype(F32)).astype(BF16)
    rwh = rwh_ref[...]
    lg_ref[...] = (jnp.dot(hi, rwh, preferred_element_type=F32)
                   + jnp.dot(lo, rwh, preferred_element_type=F32)
                   + jnp.dot(hi, rwl_ref[...], preferred_element_type=F32) + rb_ref[...])


def _post_mix(mix_in, x, gate, scale, shift, g_post, g_pre, w_out, rw_hi, rw_lo, rb, rows_per_group):
    m, d = x.shape
    tm = min(m, 512, rows_per_group if gate.ndim == 3 else m)
    const = lambda shape: pl.BlockSpec(shape, lambda i: (0,) * len(shape))
    rows = lambda width: pl.BlockSpec((tm, width), lambda i: (i, 0))
    return pl.pallas_call(
        _post_mix_kernel,
        out_shape=(jax.ShapeDtypeStruct((m, d), F32), jax.ShapeDtypeStruct((m, d), F32),
                   jax.ShapeDtypeStruct((m, LANES), F32)),
        grid=(m // tm,),
        in_specs=[rows(d), rows(d),
                  _mod_spec(gate, tm, rows_per_group),
                  _mod_spec(scale, tm, rows_per_group),
                  _mod_spec(shift, tm, rows_per_group),
                  const((1, d)), const((1, d)), const((d, d)),
                  const((d, LANES)), const((d, LANES)), const((1, LANES))],
        out_specs=(rows(d), rows(d), rows(LANES)),
        compiler_params=_params(("arbitrary",)),
        name="post_mix",
    )(mix_in, x, gate, scale, shift, g_post.reshape(1, d), g_pre.reshape(1, d), w_out, rw_hi, rw_lo, rb)


ROUTE_TOKENS = 128


def _route_kernel(lgp_ref, lgs_ref, idx_ref, p_ref, rank_ref, cnt_ref, carry, *, prompt_tiles):
    tm = lgp_ref.shape[0]

    @pl.when(pl.program_id(0) == 0)
    def _():
        carry[...] = jnp.zeros_like(carry)

    l = jnp.where(pl.program_id(0) < prompt_tiles, lgp_ref[...], lgs_ref[...])
    lane = lax.broadcasted_iota(I32, l.shape, 1)
    lane_f = lane.astype(F32)
    vals, hots = [], []
    idx_out = jnp.zeros(l.shape, F32)
    for k in range(TOP_K):
        m = jnp.max(l, axis=-1, keepdims=True)
        idx = jnp.min(jnp.where(l == m, lane_f, float(LANES)), axis=-1, keepdims=True)
        hot = lane_f == idx
        vals.append(m)
        hots.append(hot)
        idx_out = jnp.where(lane == k, idx, idx_out)
        l = jnp.where(hot, -jnp.inf, l)
    exps = [jnp.exp(v - vals[0]) for v in vals]
    denom = exps[0] + exps[1] + exps[2] + exps[3]
    p_out = jnp.zeros(l.shape, F32)
    for k in range(TOP_K):
        p_out = jnp.where(lane == k, exps[k] / denom, p_out)
    member = jnp.where(hots[0] | hots[1] | hots[2] | hots[3], 1.0, 0.0).astype(F32)
    row = lax.broadcasted_iota(I32, (tm, tm), 0)
    col = lax.broadcasted_iota(I32, (tm, tm), 1)
    before = jnp.where(row > col, 1.0, 0.0).astype(BF16)
    prefix = jnp.dot(before, member.astype(BF16), preferred_element_type=F32) + carry[...]
    rank_out = jnp.zeros(l.shape, F32)
    for k in range(TOP_K):
        r = jnp.sum(jnp.where(hots[k], prefix, 0.0), axis=-1, keepdims=True)
        rank_out = jnp.where(lane == k, r, rank_out)
    carry[...] = carry[...] + jnp.sum(member, axis=0, keepdims=True)
    idx_ref[...] = idx_out.astype(I32)
    p_ref[...] = p_out
    rank_ref[...] = rank_out.astype(I32)
    cnt_ref[...] = carry[...]


def _route(logits_p, logits_s):
    tm = ROUTE_TOKENS
    pt = logits_p.shape[0] // tm
    n = logits_p.shape[0] + logits_s.shape[0]
    tile = pl.BlockSpec((tm, LANES), lambda i: (i, 0))
    return pl.pallas_call(
        functools.partial(_route_kernel, prompt_tiles=pt),
        out_shape=(jax.ShapeDtypeStruct((n, LANES), I32), jax.ShapeDtypeStruct((n, LANES), F32),
                   jax.ShapeDtypeStruct((n, LANES), I32), jax.ShapeDtypeStruct((1, LANES), F32)),
        grid=(n // tm,),
        in_specs=[pl.BlockSpec((tm, LANES), lambda i: (jnp.minimum(i, pt - 1), 0)),
                  pl.BlockSpec((tm, LANES), lambda i: (jnp.maximum(i - pt, 0), 0))],
        out_specs=(tile, tile, tile, pl.BlockSpec((1, LANES), lambda i: (0, 0))),
        scratch_shapes=[pltpu.VMEM((1, LANES), F32)],
        compiler_params=_params(("arbitrary",)),
        name="route",
    )(logits_p, logits_s)


INVERT_TOKENS = 128


def _invert_kernel(meta_ref, slot_ref, table_ref, *, pad_token, n_exp):
    i = pl.program_id(0)
    pairs = INVERT_TOKENS * TOP_K

    @pl.when(i == 0)
    def _():
        def fill(r, carry):
            table_ref[r] = pad_token
            return carry

        def fill_group_tail(e, carry):
            lax.fori_loop(meta_ref[e] + meta_ref[2 * n_exp + e], meta_ref[e] + meta_ref[n_exp + e], fill, 0)
            return carry

        lax.fori_loop(0, n_exp, fill_group_tail, 0)
        lax.fori_loop(meta_ref[3 * n_exp] * EXPERT_ROW_TILE, table_ref.shape[0], fill, 0)

    def put(p, carry):
        table_ref[slot_ref[0, p]] = i * INVERT_TOKENS + (p >> TOP_K_SHIFT)
        return carry

    lax.fori_loop(0, pairs, put, 0, unroll=16)


def _invert(slots, meta, n_rows, pad_token, n_exp):
    n = slots.shape[0]
    nt = n // INVERT_TOKENS
    pairs = INVERT_TOKENS * TOP_K
    return pl.pallas_call(
        functools.partial(_invert_kernel, pad_token=pad_token, n_exp=n_exp),
        out_shape=jax.ShapeDtypeStruct((n_rows,), I32),
        grid_spec=pltpu.PrefetchScalarGridSpec(
            num_scalar_prefetch=1,
            grid=(nt,),
            in_specs=[pl.BlockSpec((None, 1, pairs), lambda i, meta: (i, 0, 0), memory_space=pltpu.SMEM)],
            out_specs=pl.BlockSpec(memory_space=pltpu.SMEM)),
        compiler_params=_params(("arbitrary",)),
        name="invert",
    )(meta, slots.reshape(nt, 1, pairs))


ROW_UNROLL = 8
ROW_UNROLL_SHIFT = 3


def _dispatch_kernel(na_ref, split_ref, tok_ref, tok_next_ref, h2p_ref, h2s_ref, xs_ref, buf, sem, *,
                     n_prompt):
    i = pl.program_id(0)
    tile = EXPERT_ROW_TILE
    n_active = na_ref[0]

    def issue(tok, split, slot):
        def prompt_copy(r):
            pltpu.make_async_copy(h2p_ref.at[pl.ds(tok[0, r], 1)], buf.at[slot, pl.ds(r, 1)],
                                  sem.at[slot]).start()

        def sample_copy(r):
            pltpu.make_async_copy(h2s_ref.at[pl.ds(tok[0, r] - n_prompt, 1)], buf.at[slot, pl.ds(r, 1)],
                                  sem.at[slot]).start()

        def rows(lo, hi, fn):
            def body(r, carry):
                fn(r)
                return carry
            lax.fori_loop(lo, hi, body, 0)

        def groups(lo, hi, fn):
            def body(g, carry):
                for u in range(ROW_UNROLL):
                    fn(g * ROW_UNROLL + u)
                return carry
            lax.fori_loop(lo, hi, body, 0)

        whole = split >> ROW_UNROLL_SHIFT
        first = (split + ROW_UNROLL - 1) >> ROW_UNROLL_SHIFT
        groups(0, whole, prompt_copy)
        rows(whole * ROW_UNROLL, split, prompt_copy)
        rows(split, first * ROW_UNROLL, sample_copy)
        groups(first, tile // ROW_UNROLL, sample_copy)

    @pl.when(i == 0)
    def _():
        issue(tok_ref, split_ref[0], 0)

    @pl.when(i + 1 < n_active)
    def _():
        issue(tok_next_ref, split_ref[i + 1], (i + 1) & 1)

    @pl.when(i < n_active)
    def _():
        slot = i & 1
        pltpu.make_async_copy(h2p_ref.at[pl.ds(0, tile)], buf.at[slot], sem.at[slot]).wait()
        xs_ref[...] = buf[slot].astype(BF16)

    @pl.when(i >= n_active)
    def _():
        xs_ref[...] = jnp.zeros_like(xs_ref)


def _dispatch(h2_p, h2_s, table, split, n_active):
    d = h2_p.shape[1]
    tile = EXPERT_ROW_TILE
    n_rows = table.shape[0]
    nt = n_rows // tile
    return pl.pallas_call(
        functools.partial(_dispatch_kernel, n_prompt=h2_p.shape[0]),
        out_shape=jax.ShapeDtypeStruct((n_rows, d), BF16),
        grid_spec=pltpu.PrefetchScalarGridSpec(
            num_scalar_prefetch=2,
            grid=(nt,),
            in_specs=[pl.BlockSpec((None, 1, tile), lambda i, na, sp: (i, 0, 0), memory_space=pltpu.SMEM),
                      pl.BlockSpec((None, 1, tile), lambda i, na, sp: (jnp.minimum(i + 1, nt - 1), 0, 0),
                                   memory_space=pltpu.SMEM),
                      pl.BlockSpec(memory_space=pl.ANY), pl.BlockSpec(memory_space=pl.ANY)],
            out_specs=pl.BlockSpec((tile, d), lambda i, na, sp: (i, 0)),
            scratch_shapes=[pltpu.VMEM((2, tile, d), F32), pltpu.SemaphoreType.DMA((2,))]),
        compiler_params=_params(("arbitrary",)),
        name="dispatch",
    )(n_active, split, table.reshape(nt, 1, tile), table.reshape(nt, 1, tile), h2_p, h2_s)


def _stream_expert_weights(te_ref, first_ref, run_ref, next_ref, meta_ref, n_col_blocks, copies, consume):
    j = pl.program_id(0)
    i = pl.program_id(1)
    n_active = meta_ref[0]
    n_runs = meta_ref[1]

    @pl.when(jnp.logical_and(j == 0, i == 0))
    def _():
        for cp in copies(te_ref[0], 0, 0):
            cp.start()

    @pl.when(jnp.logical_and(i < n_active, first_ref[i] == 1))
    def _():
        slot = (j * n_runs + run_ref[i]) & 1
        for cp in copies(te_ref[i], j, slot):
            cp.wait()
        last = run_ref[i] == n_runs - 1
        e_next = jnp.where(last, te_ref[0], next_ref[i])
        j_next = jnp.where(last, j + 1, j)

        @pl.when(j_next < n_col_blocks)
        def _():
            for cp in copies(e_next, j_next, 1 - slot):
                cp.start()

        consume(slot)


def _gate_up_kernel(te_ref, first_ref, run_ref, next_ref, meta_ref, x_ref, bg_ref, bu_ref, w_hbm, o_ref,
                    wbuf, wg_b, wu_b, sem, *, nj, tn):
    i = pl.program_id(1)
    active = i < meta_ref[0]

    def copies(e, jj, slot):
        return [pltpu.make_async_copy(w_hbm.at[e, :, pl.ds(pl.multiple_of((c * nj + jj) * tn, tn), tn)],
                                      wbuf.at[slot, c], sem.at[slot, c]) for c in range(2)]

    def consume(slot):
        wg_b[...] = wbuf[slot, 0].astype(BF16)
        wu_b[...] = wbuf[slot, 1].astype(BF16)

    _stream_expert_weights(te_ref, first_ref, run_ref, next_ref, meta_ref, nj, copies, consume)

    @pl.when(active)
    def _():
        x = x_ref[...]
        gate = jnp.dot(x, wg_b[...], preferred_element_type=F32) + bg_ref[...]
        up = jnp.dot(x, wu_b[...], preferred_element_type=F32) + bu_ref[...]
        gate = jnp.minimum(gate, SWIGLU_LIMIT)
        up = jnp.clip(up, -SWIGLU_LIMIT, SWIGLU_LIMIT)
        o_ref[...] = ((up + 1.0) * gate * jax.nn.sigmoid(SWIGLU_ALPHA * gate)).astype(BF16)

    @pl.when(jnp.logical_not(active))
    def _():
        o_ref[...] = jnp.zeros_like(o_ref)


def _expert_gate_up(xs, w_gu, b_gu, sched):
    n_rows, d = xs.shape
    n_exp, _, f2 = w_gu.shape
    f = f2 // 2
    tm = EXPERT_ROW_TILE
    tn = 1024
    nj = f // tn
    row = lambda i, meta: jnp.minimum(i, meta[0] - 1)
    return pl.pallas_call(
        functools.partial(_gate_up_kernel, nj=nj, tn=tn),
        out_shape=jax.ShapeDtypeStruct((n_rows, f), BF16),
        grid_spec=pltpu.PrefetchScalarGridSpec(
            num_scalar_prefetch=5,
            grid=(nj, n_rows // tm),
            in_specs=[pl.BlockSpec((tm, d), lambda j, i, te, fi, ru, nx, meta: (row(i, meta), 0)),
                      pl.BlockSpec((None, 1, tn), lambda j, i, te, fi, ru, nx, meta: (te[row(i, meta)], 0, j)),
                      pl.BlockSpec((None, 1, tn),
                                   lambda j, i, te, fi, ru, nx, meta: (te[row(i, meta)], 0, nj + j)),
                      pl.BlockSpec(memory_space=pl.ANY)],
            out_specs=pl.BlockSpec((tm, tn), lambda j, i, te, fi, ru, nx, meta: (i, j)),
            scratch_shapes=[pltpu.VMEM((2, 2, d, tn), F32), pltpu.VMEM((d, tn), BF16),
                            pltpu.VMEM((d, tn), BF16), pltpu.SemaphoreType.DMA((2, 2))]),
        compiler_params=_params(("arbitrary", "arbitrary")),
        name="expert_gate_up",
    )(*sched, xs, b_gu.reshape(n_exp, 1, f2), b_gu.reshape(n_exp, 1, f2), w_gu)


def _down_kernel(te_ref, first_ref, run_ref, next_ref, meta_ref, a_ref, b_ref, w_hbm, o_ref, wbuf, w_b, sem):
    i = pl.program_id(1)
    active = i < meta_ref[0]

    def copies(e, jj, slot):
        return [pltpu.make_async_copy(w_hbm.at[e], wbuf.at[slot], sem.at[slot])]

    def consume(slot):
        w_b[...] = wbuf[slot].astype(BF16)

    _stream_expert_weights(te_ref, first_ref, run_ref, next_ref, meta_ref, 1, copies, consume)

    @pl.when(active)
    def _():
        o_ref[...] = jnp.dot(a_ref[...], w_b[...], preferred_element_type=F32) + b_ref[...]

    @pl.when(jnp.logical_not(active))
    def _():
        o_ref[...] = jnp.zeros_like(o_ref)


def _expert_down(act, w_d, b_d, sched):
    n_rows, f = act.shape
    n_exp, _, d = w_d.shape
    tm = EXPERT_ROW_TILE
    row = lambda i, meta: jnp.minimum(i, meta[0] - 1)
    return pl.pallas_call(
        _down_kernel,
        out_shape=jax.ShapeDtypeStruct((n_rows, d), F32),
        grid_spec=pltpu.PrefetchScalarGridSpec(
            num_scalar_prefetch=5,
            grid=(1, n_rows // tm),
            in_specs=[pl.BlockSpec((tm, f), lambda j, i, te, fi, ru, nx, meta: (row(i, meta), 0)),
                      pl.BlockSpec((None, 1, d), lambda j, i, te, fi, ru, nx, meta: (te[row(i, meta)], 0, 0)),
                      pl.BlockSpec(memory_space=pl.ANY)],
            out_specs=pl.BlockSpec((tm, d), lambda j, i, te, fi, ru, nx, meta: (i, 0)),
            scratch_shapes=[pltpu.VMEM((2, f, d), F32), pltpu.VMEM((f, d), BF16),
                            pltpu.SemaphoreType.DMA((2,))]),
        compiler_params=_params(("arbitrary", "arbitrary")),
        name="expert_down",
    )(*sched, act, b_d.reshape(n_exp, 1, d), w_d)


COMBINE_TOKENS = 128


def _combine_kernel(slot_ref, slot_next_ref, p_ref, x1_ref, gate_ref, g_ref, y_ref, o_ref, buf, sem):
    i = pl.program_id(0)
    tm = COMBINE_TOKENS
    pairs = tm * TOP_K

    def issue(slots, b):
        def body(g, carry):
            for u in range(ROW_UNROLL):
                p = g * ROW_UNROLL + u
                tok = g * (ROW_UNROLL // TOP_K) + u // TOP_K
                pltpu.make_async_copy(y_ref.at[pl.ds(slots[0, p], 1)],
                                      buf.at[b, u % TOP_K, pl.ds(tok, 1)], sem.at[b]).start()
            return carry
        lax.fori_loop(0, pairs // ROW_UNROLL, body, 0)

    @pl.when(i == 0)
    def _():
        issue(slot_ref, 0)

    @pl.when(i + 1 < pl.num_programs(0))
    def _():
        issue(slot_next_ref, (i + 1) & 1)

    b = i & 1
    for k in range(TOP_K):
        pltpu.make_async_copy(y_ref.at[pl.ds(0, tm)], buf.at[b, k], sem.at[b]).wait()
    probs = p_ref[...]
    f = probs[:, 0:1] * buf[b, 0]
    for k in range(1, TOP_K):
        f = f + probs[:, k:k + 1] * buf[b, k]
    o_ref[...] = x1_ref[...] + gate_ref[...] * (_rms(f) * g_ref[...])


def _combine(y, slots, probs, x1, gate, g_post, rows_per_group):
    m, d = x1.shape
    tm = COMBINE_TOKENS
    nt = m // tm
    pairs = tm * TOP_K
    return pl.pallas_call(
        _combine_kernel,
        out_shape=jax.ShapeDtypeStruct((m, d), F32),
        grid=(nt,),
        in_specs=[pl.BlockSpec((None, 1, pairs), lambda i: (i, 0, 0), memory_space=pltpu.SMEM),
                  pl.BlockSpec((None, 1, pairs), lambda i: (jnp.minimum(i + 1, nt - 1), 0, 0),
                               memory_space=pltpu.SMEM),
                  pl.BlockSpec((tm, LANES), lambda i: (i, 0)),
                  pl.BlockSpec((tm, d), lambda i: (i, 0)),
                  _mod_spec(gate, tm, rows_per_group),
                  pl.BlockSpec((1, d), lambda i: (0, 0)),
                  pl.BlockSpec(memory_space=pl.ANY)],
        out_specs=pl.BlockSpec((tm, d), lambda i: (i, 0)),
        scratch_shapes=[pltpu.VMEM((2, TOP_K, tm, d), F32), pltpu.SemaphoreType.DMA((2,))],
        compiler_params=_params(("arbitrary",)),
        name="combine",
    )(slots.reshape(nt, 1, pairs), slots.reshape(nt, 1, pairs), probs, x1, gate, g_post.reshape(1, d), y)


def _moe(h2_p, h2_s, logits_p, logits_s, n_exp, w_gu, b_gu, w_d, b_d):
    n = h2_p.shape[0] + h2_s.shape[0]
    tile = EXPERT_ROW_TILE
    top_i, probs, rank, counts = _route(logits_p, logits_s)
    counts = counts[0, :n_exp].astype(I32)
    cap = (counts + tile - 1) // tile * tile
    ends = jnp.cumsum(cap)
    offs = ends - cap
    n_tiles = (n * TOP_K) // tile + n_exp
    n_active = (ends[-1] // tile).astype(I32).reshape(1)
    tile_id = jnp.arange(n_tiles, dtype=I32)
    tile_expert = jnp.minimum(jnp.sum(ends[None, :] <= tile_id[:, None] * tile, axis=1), n_exp - 1).astype(I32)
    first = ((tile_id == 0) | (tile_expert != jnp.roll(tile_expert, 1))) & (tile_id < n_active[0])
    run = jnp.cumsum(first.astype(I32)) - 1
    next_expert = tile_expert[jnp.minimum(ends[tile_expert] // tile, n_tiles - 1)]
    sched = (tile_expert, first.astype(I32), run.astype(I32), next_expert.astype(I32),
             jnp.stack([n_active[0], jnp.sum(first.astype(I32))]).astype(I32))
    expert_ids = jnp.arange(n_exp, dtype=I32)
    pair_offs = jnp.sum(jnp.where(top_i[:, :TOP_K, None] == expert_ids, offs, 0), axis=-1)
    slots = pair_offs + rank[:, :TOP_K]
    n_prompt = h2_p.shape[0]
    meta = jnp.concatenate([offs, cap, counts, n_active]).astype(I32)
    table = _invert(slots, meta, n_tiles * tile, n_prompt, n_exp)
    split = jnp.sum(table.reshape(n_tiles, tile) < n_prompt, axis=1).astype(I32)
    xs = _dispatch(h2_p, h2_s, table, split, n_active)
    act = _expert_gate_up(xs, w_gu, b_gu, sched)
    y = _expert_down(act, w_d, b_d, sched)
    return y, slots, probs


def _layer(xp, xs_, cp_mod, cs_mod, hist_a, hist_q, s0, g_pre_mix, g_post_mix, g_pre_ffn, g_post_ffn,
           w_in, conv_a_w, gdn_conv_w, a_log, dt_bias, g_conv_out, gdn_norm_g, w_out, router_w, router_b,
           w_gu, b_gu, w_d, b_d):
    bsz, seq, d = xp.shape
    ns = xs_.shape[0]
    n_exp = router_w.shape[1]
    dc = N_HEADS * HEAD_D
    d_main = 3 * dc + 3 * dc + dc
    xp2 = xp.reshape(bsz * seq, d)
    xs2 = xs_.reshape(ns, d)

    w_main = w_in.astype(BF16)
    w_ba = jnp.zeros((d, LANES), BF16).at[:, :2 * N_HEADS].set(w_main[:, d_main:])
    w_out_b = w_out.astype(BF16)
    rw = jnp.zeros((d, LANES), F32).at[:, :n_exp].set(router_w)
    rw_hi = rw.astype(BF16)
    rw_lo = (rw - rw_hi.astype(F32)).astype(BF16)
    rb = jnp.full((1, LANES), NEG_BIG, F32).at[0, :n_exp].set(router_b)

    mp = [cp_mod[:, i * d:(i + 1) * d].reshape(bsz, 1, d) for i in range(6)]
    ms = [cs_mod[:, i * d:(i + 1) * d] for i in range(6)]

    proj_p, ba_p = _in_proj(xp2, mp[1], mp[0], g_pre_mix, w_main, w_ba, seq, d_main)
    proj_s, ba_s = _in_proj(xs2, ms[1], ms[0], g_pre_mix, w_main, w_ba, 1, d_main)

    mix_p, ha_p, hq_p, s_p = _mixer_prompt(proj_p, ba_p, bsz, seq, conv_a_w, gdn_conv_w, a_log, dt_bias,
                                           g_conv_out, gdn_norm_g)
    mix_s, ha_s, hq_s, s_s = _mixer_sample(proj_s, ba_s, hist_a.reshape(ns, 2 * dc),
                                           hist_q.reshape(ns, 9 * dc), s0, conv_a_w, gdn_conv_w, a_log,
                                           dt_bias, g_conv_out, gdn_norm_g)

    x1_p, h2_p, lg_p = _post_mix(mix_p, xp2, mp[2], mp[4], mp[3], g_post_mix, g_pre_ffn, w_out_b, rw_hi,
                                 rw_lo, rb, seq)
    x1_s, h2_s, lg_s = _post_mix(mix_s, xs2, ms[2], ms[4], ms[3], g_post_mix, g_pre_ffn, w_out_b, rw_hi,
                                 rw_lo, rb, 1)

    y, slots, probs = _moe(h2_p, h2_s, lg_p, lg_s, n_exp, w_gu, b_gu, w_d, b_d)
    np_ = bsz * seq
    out_p = _combine(y, slots[:np_], probs[:np_], x1_p, mp[5], g_post_ffn, seq)
    out_s = _combine(y, slots[np_:], probs[np_:], x1_s, ms[5], g_post_ffn, 1)
    return (out_p.reshape(bsz, seq, d), out_s.reshape(ns, 1, d), ha_p, hq_p, s_p,
            ha_s.reshape(ns, 2, dc), hq_s.reshape(ns, 3, 3 * dc), s_s)


def kernel(x_prompt, x_sample, state_conv_a, state_gdn_conv, state_gdn_S, c_prompt, c_sample, w_mod, b_mod, g_pre_mix, g_post_mix, g_pre_ffn, g_post_ffn, w_in, conv_a_w, gdn_conv_w, gdn_a_log, gdn_dt_bias, g_conv_out, gdn_norm_g, w_out, router_w, router_b, exp_w_gate_up, exp_b_gate_up, exp_w_down, exp_b_down):
    depth = w_mod.shape[0]
    bp = x_prompt.shape[0]
    xp, xs_ = x_prompt, x_sample
    outs = [[] for _ in range(6)]
    for l in range(depth):
        mod = _modulation(jnp.concatenate([c_prompt, c_sample], axis=0), w_mod[l], b_mod[l])
        res = _layer(xp, xs_, mod[:bp], mod[bp:], state_conv_a[l], state_gdn_conv[l], state_gdn_S[l],
                     g_pre_mix[l], g_post_mix[l], g_pre_ffn[l], g_post_ffn[l], w_in[l], conv_a_w[l],
                     gdn_conv_w[l], gdn_a_log[l], gdn_dt_bias[l], g_conv_out[l], gdn_norm_g[l], w_out[l],
                     router_w[l], router_b[l], exp_w_gate_up[l], exp_b_gate_up[l], exp_w_down[l],
                     exp_b_down[l])
        xp, xs_ = res[0], res[1]
        for acc, r in zip(outs, res[2:]):
            acc.append(r)
    return (xp, xs_) + tuple(o[0][None] if depth == 1 else jnp.stack(o) for o in outs)
```

```python
import functools

import jax
import jax.numpy as jnp
from jax import lax
from jax.experimental import pallas as pl
from jax.experimental.pallas import tpu as pltpu

F32 = jnp.float32
BF16 = jnp.bfloat16
I32 = jnp.int32
HIGHEST = lax.Precision.HIGHEST

EPS = 1e-6
N_HEADS = 8
HEAD_D = 128
TOP_K = 4
TOP_K_SHIFT = 2
SWIGLU_LIMIT = 7.0
SWIGLU_ALPHA = 1.702
CHUNK = 64
LANES = 128
EXPERT_ROW_TILE = 256
NEG_BIG = -1e30
VMEM_LIMIT = 56 * 1024 * 1024


def _params(semantics, vmem=VMEM_LIMIT):
    return pltpu.CompilerParams(dimension_semantics=semantics, vmem_limit_bytes=vmem)


def _mm(a, b):
    return jnp.dot(a.astype(BF16), b.astype(BF16), preferred_element_type=F32)


def _mm_nt(a, b):
    return lax.dot_general(a.astype(BF16), b.astype(BF16), (((1,), (1,)), ((), ())),
                           preferred_element_type=F32)


def _rms(x):
    return x * lax.rsqrt(jnp.mean(x * x, axis=-1, keepdims=True) + EPS)


def _silu(x):
    return x * jax.nn.sigmoid(x)


def _softplus(x):
    return jnp.maximum(x, 0.0) + jnp.log1p(jnp.exp(-jnp.abs(x)))


def _mod_kernel(c_ref, w_ref, b_ref, o_ref):
    s = _silu(c_ref[...])
    o_ref[...] = _mm(s, w_ref[...]) + b_ref[...]


def _modulation(c_all, w_mod, b_mod):
    n, d = c_all.shape
    m = w_mod.shape[1]
    tn = 1024
    return pl.pallas_call(
        _mod_kernel,
        out_shape=jax.ShapeDtypeStruct((n, m), F32),
        grid=(m // tn,),
        in_specs=[pl.BlockSpec((n, d), lambda j: (0, 0)),
                  pl.BlockSpec((d, tn), lambda j: (0, j)),
                  pl.BlockSpec((1, tn), lambda j: (0, j))],
        out_specs=pl.BlockSpec((n, tn), lambda j: (0, j)),
        compiler_params=_params(("arbitrary",)),
        name="modulation",
    )(c_all, w_mod, b_mod.reshape(1, m))


def _mod_spec(arr, tm, rows_per_group):
    if arr.ndim == 3:
        tiles = rows_per_group // tm
        return pl.BlockSpec((None, 1, arr.shape[-1]), lambda i, *_: (i // tiles, 0, 0))
    return pl.BlockSpec((tm, arr.shape[-1]), lambda i, *_: (i, 0))


def _proj_kernel(x_ref, sc_ref, sh_ref, g_ref, w_ref, wba_ref, o_ref, ba_ref, h_scr):
    @pl.when(pl.program_id(1) == 0)
    def _():
        h = (_rms(x_ref[...]) * g_ref[...]) * (1.0 + sc_ref[...]) + sh_ref[...]
        hb = h.astype(BF16)
        h_scr[...] = hb
        ba_ref[...] = jnp.dot(hb, wba_ref[...], preferred_element_type=F32)

    o_ref[...] = jnp.dot(h_scr[...], w_ref[...], preferred_element_type=F32)


def _in_proj(x, scale, shift, g, w_main, w_ba, rows_per_group, n):
    m, d = x.shape
    tm = min(m, 1024, rows_per_group if scale.ndim == 3 else m)
    tn = 1024
    return pl.pallas_call(
        _proj_kernel,
        out_shape=(jax.ShapeDtypeStruct((m, n), F32), jax.ShapeDtypeStruct((m, LANES), F32)),
        grid=(m // tm, n // tn),
        in_specs=[pl.BlockSpec((tm, d), lambda i, j: (i, 0)),
                  _mod_spec(scale, tm, rows_per_group),
                  _mod_spec(shift, tm, rows_per_group),
                  pl.BlockSpec((1, d), lambda i, j: (0, 0)),
                  pl.BlockSpec((d, tn), lambda i, j: (0, j)),
                  pl.BlockSpec((d, LANES), lambda i, j: (0, 0))],
        out_specs=(pl.BlockSpec((tm, tn), lambda i, j: (i, j)),
                   pl.BlockSpec((tm, LANES), lambda i, j: (i, 0))),
        scratch_shapes=[pltpu.VMEM((tm, d), BF16)],
        compiler_params=_params(("arbitrary", "arbitrary")),
        name="in_proj",
    )(x, scale, shift, g.reshape(1, d), w_main, w_ba)


PROMPT_SEQS_PER_STEP = 2

def _mixer_prompt_kernel(proj_ref, ba_ref, caw_ref, gcw_ref, alog_ref, dtb_ref, gco_ref, gng_ref,
                         mix_ref, ha_ref, hq_ref, s_ref, extu, extq, qc_scr, s_scr, *, nseq):
    c = CHUNK
    dc = N_HEADS * HEAD_D
    dq = 3 * dc
    t = pl.program_id(1)
    is_last = t == pl.num_programs(1) - 1

    @pl.when(t == 0)
    def _():
        extu[:, 0:8, :] = jnp.zeros((nseq, 8, dc), F32)
        extq[:, 0:8, :] = jnp.zeros((nseq, 8, dq), F32)
        s_scr[...] = jnp.zeros_like(s_scr)

    row = lax.broadcasted_iota(I32, (c, c), 0)
    col = lax.broadcasted_iota(I32, (c, c), 1)
    causal = row >= col
    strict = row > col
    lower = jnp.where(causal, 1.0, 0.0).astype(F32)
    upper = jnp.where(row <= col, 1.0, 0.0).astype(F32)
    caw = caw_ref[...]
    gcw = gcw_ref[...]

    chains = [(sq, h) for sq in range(nseq) for h in range(N_HEADS)]
    heads = range(len(chains))
    qn, kn, vb, kb, kbg, qg, kg, decay, s_decay = ([] for _ in range(9))
    for sq in range(nseq):
        u = proj_ref[sq, :, dc:2 * dc] * proj_ref[sq, :, 2 * dc:3 * dc]
        extu[sq, 8:8 + c, :] = u
        ya = caw[0:1] * extu[sq, 6:6 + c, :] + caw[1:2] * extu[sq, 7:7 + c, :] + caw[2:3] * u
        ya = proj_ref[sq, :, 0:dc] * ya
        mix_ref[sq, :, 0:dc] = (_rms(ya) * gco_ref[...]).astype(BF16)
        last_u = extu[sq, 6 + c:8 + c, :]
        extu[sq, 6:8, :] = last_u

        qkv = proj_ref[sq, :, 3 * dc:3 * dc + dq]
        extq[sq, 8:8 + c, :] = qkv
        qc = (gcw[0:1] * extq[sq, 5:5 + c, :] + gcw[1:2] * extq[sq, 6:6 + c, :]
              + gcw[2:3] * extq[sq, 7:7 + c, :] + gcw[3:4] * qkv)
        qc_scr[sq] = _silu(qc)
        last_q = extq[sq, 5 + c:8 + c, :]
        extq[sq, 5:8, :] = last_q

        @pl.when(is_last)
        def _(sq=sq, last_u=last_u, last_q=last_q):
            ha_ref[sq] = last_u
            hq_ref[sq] = last_q

        ba = ba_ref[sq]
        beta_all = jax.nn.sigmoid(ba)
        g_all = -jnp.exp(alog_ref[...]) * _softplus(ba + dtb_ref[...])
        gc_all = jnp.dot(lower, g_all, precision=HIGHEST, preferred_element_type=F32)
        gc_t = lax.dot_general(g_all, upper, (((0,), (0,)), ((), ())), precision=HIGHEST,
                               preferred_element_type=F32)
        for h in range(N_HEADS):
            lo = h * HEAD_D
            q = qc_scr[sq, :, lo:lo + HEAD_D]
            k = qc_scr[sq, :, dc + lo:dc + lo + HEAD_D]
            v = qc_scr[sq, :, 2 * dc + lo:2 * dc + lo + HEAD_D]
            qn_h = q * lax.rsqrt(jnp.sum(q * q, axis=-1, keepdims=True) + EPS) * (HEAD_D ** -0.5)
            kn_h = k * lax.rsqrt(jnp.sum(k * k, axis=-1, keepdims=True) + EPS)
            beta = beta_all[:, h:h + 1]
            gcc = gc_all[:, N_HEADS + h:N_HEADS + h + 1]
            gcr = gc_t[N_HEADS + h:N_HEADS + h + 1, :]
            gl = gc_all[c - 1:c, N_HEADS + h:N_HEADS + h + 1]
            eg = jnp.exp(gcc)
            kb_h = kn_h * beta
            qn.append(qn_h)
            kn.append(kn_h)
            vb.append(v * beta)
            kb.append(kb_h)
            kbg.append(kb_h * eg)
            qg.append(qn_h * eg)
            kg.append(kn_h * jnp.exp(gl - gcc))
            decay.append(jnp.where(causal, jnp.exp(jnp.minimum(gcc - gcr, 0.0)), 0.0))
            s_decay.append(jnp.exp(gl))

    kq = [_mm_nt(jnp.concatenate([kb[h], qn[h]], axis=0), kn[h]) for h in heads]
    a_mat = [jnp.where(strict, kq[h][:c] * decay[h], 0.0) for h in heads]
    qk = [kq[h][c:] * decay[h] for h in heads]
    n_mat = [-a_mat[h] for h in heads]
    p = a_mat
    size = 2
    while size < c:
        p = [_mm(p[h], p[h]) for h in heads]
        n_p = [_mm(n_mat[h], p[h]) for h in heads]
        n_mat = [n_mat[h] + p[h] + n_p[h] for h in heads]
        size *= 2
    rhs = [jnp.concatenate([vb[h], kbg[h]], axis=-1) for h in heads]
    uw = [rhs[h] + _mm(n_mat[h], rhs[h]) for h in heads]
    s_old = [s_scr[sq, h] for sq, h in chains]
    ws = [_mm(jnp.concatenate([uw[h][:, HEAD_D:], qg[h]], axis=0), s_old[h]) for h in heads]
    v_new = [uw[h][:, :HEAD_D] - ws[h][:c] for h in heads]
    fin = [_mm(jnp.concatenate([qk[h], kg[h].T], axis=0), v_new[h]) for h in heads]
    for i, (sq, h) in enumerate(chains):
        lo = h * HEAD_D
        s_scr[sq, h] = s_old[i] * s_decay[i] + fin[i][c:]
        o = ws[i][c:] + fin[i][:c]
        z = proj_ref[sq, :, 3 * dc + dq + lo:3 * dc + dq + lo + HEAD_D]
        yb = _rms(o) * gng_ref[...] * _silu(z)
        mix_ref[sq, :, dc + lo:dc + lo + HEAD_D] = yb.astype(BF16)

    @pl.when(is_last)
    def _():
        s_ref[...] = s_scr[...]


def _lane_row(vec, offset):
    return jnp.zeros((1, LANES), F32).at[0, offset:offset + vec.shape[0]].set(vec.astype(F32))


def _mixer_prompt(proj, ba, bsz, seq, conv_a_w, gdn_conv_w, a_log, dt_bias, g_conv_out, gdn_norm_g):
    c = CHUNK
    dc = N_HEADS * HEAD_D
    dq = 3 * dc
    dproj = proj.shape[1]
    nt = seq // c
    const = lambda shape: pl.BlockSpec(shape, lambda b, t: (0,) * len(shape))
    nseq = PROMPT_SEQS_PER_STEP if bsz % PROMPT_SEQS_PER_STEP == 0 else 1
    seq_block = lambda *tail: pl.BlockSpec((nseq,) + tail, lambda b, t: (b, t) + (0,) * (len(tail) - 1))
    whole = lambda *tail: pl.BlockSpec((nseq,) + tail, lambda b, t: (b,) + (0,) * len(tail))
    mix, ha, hq, s_fin = pl.pallas_call(
        functools.partial(_mixer_prompt_kernel, nseq=nseq),
        out_shape=(jax.ShapeDtypeStruct((bsz, seq, 2 * dc), BF16),
                   jax.ShapeDtypeStruct((bsz, 2, dc), F32),
                   jax.ShapeDtypeStruct((bsz, 3, dq), F32),
                   jax.ShapeDtypeStruct((bsz, N_HEADS, HEAD_D, HEAD_D), F32)),
        grid=(bsz // nseq, nt),
        in_specs=[seq_block(c, dproj), seq_block(c, LANES),
                  const((3, dc)), const((4, dq)), const((1, LANES)), const((1, LANES)),
                  const((1, dc)), const((1, HEAD_D))],
        out_specs=(seq_block(c, 2 * dc), whole(2, dc), whole(3, dq), whole(N_HEADS, HEAD_D, HEAD_D)),
        scratch_shapes=[pltpu.VMEM((nseq, 8 + c, dc), F32), pltpu.VMEM((nseq, 8 + c, dq), F32),
                        pltpu.VMEM((nseq, c, dq), F32), pltpu.VMEM((nseq, N_HEADS, HEAD_D, HEAD_D), F32)],
        compiler_params=_params(("arbitrary", "arbitrary")),
        name="mixer_prompt",
    )(proj.reshape(bsz, seq, dproj), ba.reshape(bsz, seq, LANES), conv_a_w, gdn_conv_w,
      _lane_row(a_log, N_HEADS), _lane_row(dt_bias, N_HEADS), g_conv_out.reshape(1, dc),
      gdn_norm_g.reshape(1, HEAD_D))
    return mix.reshape(bsz * seq, 2 * dc), ha, hq, s_fin


SAMPLE_GROUP = 16


def _mixer_sample_kernel(proj_ref, ba_ref, hista_ref, histq_ref, s_in_ref, caw_ref, gcw_ref, alog_ref,
                         dtb_ref, gco_ref, gng_ref, mix_ref, ha_ref, hq_ref, s_out_ref, qc_scr, o_scr):
    tb = SAMPLE_GROUP
    dc = N_HEADS * HEAD_D
    dq = 3 * dc

    u = proj_ref[:, dc:2 * dc] * proj_ref[:, 2 * dc:3 * dc]
    caw = caw_ref[...]
    ya = caw[0:1] * hista_ref[:, 0:dc] + caw[1:2] * hista_ref[:, dc:2 * dc] + caw[2:3] * u
    ya = proj_ref[:, 0:dc] * ya
    mix_ref[:, 0:dc] = _rms(ya) * gco_ref[...]
    ha_ref[:, 0:dc] = hista_ref[:, dc:2 * dc]
    ha_ref[:, dc:2 * dc] = u

    qkv = proj_ref[:, 3 * dc:3 * dc + dq]
    gcw = gcw_ref[...]
    qc = (gcw[0:1] * histq_ref[:, 0:dq] + gcw[1:2] * histq_ref[:, dq:2 * dq]
          + gcw[2:3] * histq_ref[:, 2 * dq:3 * dq] + gcw[3:4] * qkv)
    qc_scr[...] = _silu(qc)
    hq_ref[:, 0:dq] = histq_ref[:, dq:2 * dq]
    hq_ref[:, dq:2 * dq] = histq_ref[:, 2 * dq:3 * dq]
    hq_ref[:, 2 * dq:3 * dq] = qkv

    ba = ba_ref[...]
    beta_all = jax.nn.sigmoid(ba)
    eg_all = jnp.exp(-jnp.exp(alog_ref[...]) * _softplus(ba + dtb_ref[...]))

    for h in range(N_HEADS):
        lo = h * HEAD_D
        q = qc_scr[:, lo:lo + HEAD_D]
        k = qc_scr[:, dc + lo:dc + lo + HEAD_D]
        v = qc_scr[:, 2 * dc + lo:2 * dc + lo + HEAD_D]
        qn = q * lax.rsqrt(jnp.sum(q * q, axis=-1, keepdims=True) + EPS) * (HEAD_D ** -0.5)
        kn = k * lax.rsqrt(jnp.sum(k * k, axis=-1, keepdims=True) + EPS)
        qk = jnp.sum(qn * kn, axis=-1, keepdims=True)
        kn_t = kn.T
        qn_t = qn.T
        for b in range(tb):
            s_old = s_in_ref[b, h]
            kc = kn_t[:, b:b + 1]
            e = eg_all[b:b + 1, N_HEADS + h:N_HEADS + h + 1]
            ks = jnp.sum(s_old * kc, axis=0, keepdims=True)
            qs = jnp.sum(s_old * qn_t[:, b:b + 1], axis=0, keepdims=True)
            v_new = beta_all[b:b + 1, h:h + 1] * (v[b:b + 1, :] - e * ks)
            o_scr[b:b + 1, lo:lo + HEAD_D] = e * qs + qk[b:b + 1, :] * v_new
            s_out_ref[b, h] = s_old * e + kc * v_new
        z = proj_ref[:, 3 * dc + dq + lo:3 * dc + dq + lo + HEAD_D]
        o = o_scr[:, lo:lo + HEAD_D]
        mix_ref[:, dc + lo:dc + lo + HEAD_D] = _rms(o) * gng_ref[...] * _silu(z)


def _mixer_sample(proj, ba, hist_a, hist_q, s_in, conv_a_w, gdn_conv_w, a_log, dt_bias, g_conv_out,
                  gdn_norm_g):
    n = proj.shape[0]
    tb = SAMPLE_GROUP
    dc = N_HEADS * HEAD_D
    dq = 3 * dc
    dproj = proj.shape[1]
    const = lambda shape: pl.BlockSpec(shape, lambda i: (0,) * len(shape))
    rows = lambda width: pl.BlockSpec((tb, width), lambda i: (i, 0))
    state = pl.BlockSpec((tb, N_HEADS, HEAD_D, HEAD_D), lambda i: (i, 0, 0, 0))
    return pl.pallas_call(
        _mixer_sample_kernel,
        out_shape=(jax.ShapeDtypeStruct((n, 2 * dc), F32),
                   jax.ShapeDtypeStruct((n, 2 * dc), F32),
                   jax.ShapeDtypeStruct((n, 3 * dq), F32),
                   jax.ShapeDtypeStruct((n, N_HEADS, HEAD_D, HEAD_D), F32)),
        grid=(n // tb,),
        in_specs=[rows(dproj), rows(LANES), rows(2 * dc), rows(3 * dq), state,
                  const((3, dc)), const((4, dq)), const((1, LANES)), const((1, LANES)),
                  const((1, dc)), const((1, HEAD_D))],
        out_specs=(rows(2 * dc), rows(2 * dc), rows(3 * dq), state),
        scratch_shapes=[pltpu.VMEM((tb, dq), F32), pltpu.VMEM((tb, dc), F32)],
        compiler_params=_params(("arbitrary",)),
        name="mixer_sample",
    )(proj, ba, hist_a, hist_q, s_in, conv_a_w, gdn_conv_w, _lane_row(a_log, N_HEADS),
      _lane_row(dt_bias, N_HEADS), g_conv_out.reshape(1, dc), gdn_norm_g.reshape(1, HEAD_D))


POST_MIX_SUB_ROWS = 256

def _post_mix_kernel(mix_ref, x_ref, gate_ref, sc_ref, sh_ref, gpost_ref, gpre_ref, wout_ref,
                     rwh_ref, rwl_ref, rb_ref, x1_ref, h2_ref, lg_ref):
    tm = x_ref.shape[0]
    sub = min(tm, POST_MIX_SUB_ROWS)
    tiles = [slice(r0, r0 + sub) for r0 in range(0, tm, sub)]
    mixes = [jnp.dot(mix_ref[rows].astype(BF16), wout_ref[...], preferred_element_type=F32)
             for rows in tiles]
    for rows, mix in zip(tiles, mixes):
        per_row = lambda ref: ref[rows] if ref.shape[0] == tm else ref[...]
        x1 = x_ref[rows] + per_row(gate_ref) * (_rms(mix) * gpost_ref[...])
        x1_ref[rows] = x1
        h2 = (_rms(x1) * gpre_ref[...]) * (1.0 + per_row(sc_ref)) + per_row(sh_ref)
        h2_ref[rows] = h2
        hi = h2.astype(BF16)
        lo = (h2 - hi.astype(F32)).astype(BF16)
        rwh = rwh_ref[...]
        lg_ref[rows] = (jnp.dot(hi, rwh, preferred_element_type=F32)
                        + jnp.dot(lo, rwh, preferred_element_type=F32)
                        + jnp.dot(hi, rwl_ref[...], preferred_element_type=F32) + rb_ref[...])


def _post_mix(mix_in, x, gate, scale, shift, g_post, g_pre, w_out, rw_hi, rw_lo, rb, rows_per_group):
    m, d = x.shape
    tm = min(m, 512, rows_per_group if gate.ndim == 3 else m)
    const = lambda shape: pl.BlockSpec(shape, lambda i: (0,) * len(shape))
    rows = lambda width: pl.BlockSpec((tm, width), lambda i: (i, 0))
    return pl.pallas_call(
        _post_mix_kernel,
        out_shape=(jax.ShapeDtypeStruct((m, d), F32), jax.ShapeDtypeStruct((m, d), F32),
                   jax.ShapeDtypeStruct((m, LANES), F32)),
        grid=(m // tm,),
        in_specs=[rows(d), rows(d),
                  _mod_spec(gate, tm, rows_per_group),
                  _mod_spec(scale, tm, rows_per_group),
                  _mod_spec(shift, tm, rows_per_group),
                  const((1, d)), const((1, d)), const((d, d)),
                  const((d, LANES)), const((d, LANES)), const((1, LANES))],
        out_specs=(rows(d), rows(d), rows(LANES)),
        compiler_params=_params(("arbitrary",)),
        name="post_mix",
    )(mix_in, x, gate, scale, shift, g_post.reshape(1, d), g_pre.reshape(1, d), w_out, rw_hi, rw_lo, rb)


ROUTE_TOKENS = 128


def _route_kernel(lgp_ref, lgs_ref, idx_ref, p_ref, rank_ref, cnt_ref, carry, *, prompt_tiles):
    tm = lgp_ref.shape[0]

    @pl.when(pl.program_id(0) == 0)
    def _():
        carry[...] = jnp.zeros_like(carry)

    l = jnp.where(pl.program_id(0) < prompt_tiles, lgp_ref[...], lgs_ref[...])
    lane = lax.broadcasted_iota(I32, l.shape, 1)
    lane_f = lane.astype(F32)
    vals, hots = [], []
    idx_out = jnp.zeros(l.shape, F32)
    for k in range(TOP_K):
        m = jnp.max(l, axis=-1, keepdims=True)
        idx = jnp.min(jnp.where(l == m, lane_f, float(LANES)), axis=-1, keepdims=True)
        hot = lane_f == idx
        vals.append(m)
        hots.append(hot)
        idx_out = jnp.where(lane == k, idx, idx_out)
        l = jnp.where(hot, -jnp.inf, l)
    exps = [jnp.exp(v - vals[0]) for v in vals]
    denom = exps[0] + exps[1] + exps[2] + exps[3]
    p_out = jnp.zeros(l.shape, F32)
    for k in range(TOP_K):
        p_out = jnp.where(lane == k, exps[k] / denom, p_out)
    member = jnp.where(hots[0] | hots[1] | hots[2] | hots[3], 1.0, 0.0).astype(F32)
    row = lax.broadcasted_iota(I32, (tm, tm), 0)
    col = lax.broadcasted_iota(I32, (tm, tm), 1)
    before = jnp.where(row > col, 1.0, 0.0).astype(BF16)
    prefix = jnp.dot(before, member.astype(BF16), preferred_element_type=F32) + carry[...]
    rank_out = jnp.zeros(l.shape, F32)
    for k in range(TOP_K):
        r = jnp.sum(jnp.where(hots[k], prefix, 0.0), axis=-1, keepdims=True)
        rank_out = jnp.where(lane == k, r, rank_out)
    carry[...] = carry[...] + jnp.sum(member, axis=0, keepdims=True)
    idx_ref[...] = idx_out.astype(I32)
    p_ref[...] = p_out
    rank_ref[...] = rank_out.astype(I32)
    cnt_ref[...] = carry[...]


def _route(logits_p, logits_s):
    tm = ROUTE_TOKENS
    pt = logits_p.shape[0] // tm
    n = logits_p.shape[0] + logits_s.shape[0]
    tile = pl.BlockSpec((tm, LANES), lambda i: (i, 0))
    return pl.pallas_call(
        functools.partial(_route_kernel, prompt_tiles=pt),
        out_shape=(jax.ShapeDtypeStruct((n, LANES), I32), jax.ShapeDtypeStruct((n, LANES), F32),
                   jax.ShapeDtypeStruct((n, LANES), I32), jax.ShapeDtypeStruct((1, LANES), F32)),
        grid=(n // tm,),
        in_specs=[pl.BlockSpec((tm, LANES), lambda i: (jnp.minimum(i, pt - 1), 0)),
                  pl.BlockSpec((tm, LANES), lambda i: (jnp.maximum(i - pt, 0), 0))],
        out_specs=(tile, tile, tile, pl.BlockSpec((1, LANES), lambda i: (0, 0))),
        scratch_shapes=[pltpu.VMEM((1, LANES), F32)],
        compiler_params=_params(("arbitrary",)),
        name="route",
    )(logits_p, logits_s)


INVERT_TOKENS_MAX = 1024


def _largest_tile(n, unit, cap):
    return max(m for m in range(unit, cap + 1, unit) if n % m == 0)


def _invert_kernel(meta_ref, slot_ref, table_ref, *, pad_token, n_exp, tokens):
    i = pl.program_id(0)
    pairs = tokens * TOP_K

    @pl.when(i == 0)
    def _():
        def fill(r, carry):
            table_ref[r] = pad_token
            return carry

        def fill_group_tail(e, carry):
            lax.fori_loop(meta_ref[e] + meta_ref[2 * n_exp + e], meta_ref[e] + meta_ref[n_exp + e], fill, 0)
            return carry

        lax.fori_loop(0, n_exp, fill_group_tail, 0)
        lax.fori_loop(meta_ref[3 * n_exp] * EXPERT_ROW_TILE, table_ref.shape[0], fill, 0)

    def put(p, carry):
        table_ref[slot_ref[0, p]] = i * tokens + (p >> TOP_K_SHIFT)
        return carry

    lax.fori_loop(0, pairs, put, 0, unroll=16)


def _invert(slots, meta, n_rows, pad_token, n_exp):
    n = slots.shape[0]
    tokens = _largest_tile(n, LANES, INVERT_TOKENS_MAX)
    nt = n // tokens
    pairs = tokens * TOP_K
    return pl.pallas_call(
        functools.partial(_invert_kernel, pad_token=pad_token, n_exp=n_exp, tokens=tokens),
        out_shape=jax.ShapeDtypeStruct((n_rows,), I32),
        grid_spec=pltpu.PrefetchScalarGridSpec(
            num_scalar_prefetch=1,
            grid=(nt,),
            in_specs=[pl.BlockSpec((None, 1, pairs), lambda i, meta: (i, 0, 0), memory_space=pltpu.SMEM)],
            out_specs=pl.BlockSpec(memory_space=pltpu.SMEM)),
        compiler_params=_params(("arbitrary",)),
        name="invert",
    )(meta, slots.reshape(nt, 1, pairs))


ROW_UNROLL = 8
ROW_UNROLL_SHIFT = 3


def _dispatch_kernel(na_ref, split_ref, tok_ref, tok_next_ref, h2p_ref, h2s_ref, xs_ref, buf, sem, *,
                     n_prompt):
    i = pl.program_id(0)
    tile = EXPERT_ROW_TILE
    n_active = na_ref[0]

    def issue(tok, split, slot):
        def prompt_copy(r, priority):
            pltpu.make_async_copy(h2p_ref.at[pl.ds(tok[0, r], 1)], buf.at[slot, pl.ds(r, 1)],
                                  sem.at[slot]).start(priority=priority)

        def sample_copy(r, priority):
            pltpu.make_async_copy(h2s_ref.at[pl.ds(tok[0, r] - n_prompt, 1)], buf.at[slot, pl.ds(r, 1)],
                                  sem.at[slot]).start(priority=priority)

        def rows(lo, hi, fn):
            def body(r, carry):
                fn(r, 0)
                return carry
            lax.fori_loop(lo, hi, body, 0)

        def groups(lo, hi, fn):
            def body(g, carry):
                for u in range(ROW_UNROLL):
                    fn(g * ROW_UNROLL + u, u % 2)
                return carry
            lax.fori_loop(lo, hi, body, 0)

        whole = split >> ROW_UNROLL_SHIFT
        first = (split + ROW_UNROLL - 1) >> ROW_UNROLL_SHIFT
        groups(0, whole, prompt_copy)
        rows(whole * ROW_UNROLL, split, prompt_copy)
        rows(split, first * ROW_UNROLL, sample_copy)
        groups(first, tile // ROW_UNROLL, sample_copy)

    @pl.when(i == 0)
    def _():
        issue(tok_ref, split_ref[0], 0)

    @pl.when(i + 1 < n_active)
    def _():
        issue(tok_next_ref, split_ref[i + 1], (i + 1) & 1)

    @pl.when(i < n_active)
    def _():
        slot = i & 1
        pltpu.make_async_copy(h2p_ref.at[pl.ds(0, tile)], buf.at[slot], sem.at[slot]).wait()
        xs_ref[...] = buf[slot].astype(BF16)

    @pl.when(i >= n_active)
    def _():
        xs_ref[...] = jnp.zeros_like(xs_ref)


def _dispatch(h2_p, h2_s, table, split, n_active):
    d = h2_p.shape[1]
    tile = EXPERT_ROW_TILE
    n_rows = table.shape[0]
    nt = n_rows // tile
    return pl.pallas_call(
        functools.partial(_dispatch_kernel, n_prompt=h2_p.shape[0]),
        out_shape=jax.ShapeDtypeStruct((n_rows, d), BF16),
        grid_spec=pltpu.PrefetchScalarGridSpec(
            num_scalar_prefetch=2,
            grid=(nt,),
            in_specs=[pl.BlockSpec((None, 1, tile), lambda i, na, sp: (i, 0, 0), memory_space=pltpu.SMEM),
                      pl.BlockSpec((None, 1, tile), lambda i, na, sp: (jnp.minimum(i + 1, nt - 1), 0, 0),
                                   memory_space=pltpu.SMEM),
                      pl.BlockSpec(memory_space=pl.ANY), pl.BlockSpec(memory_space=pl.ANY)],
            out_specs=pl.BlockSpec((tile, d), lambda i, na, sp: (i, 0)),
            scratch_shapes=[pltpu.VMEM((2, tile, d), F32), pltpu.SemaphoreType.DMA((2,))]),
        compiler_params=_params(("arbitrary",)),
        name="dispatch",
    )(n_active, split, table.reshape(nt, 1, tile), table.reshape(nt, 1, tile), h2_p, h2_s)


def _stream_expert_weights(te_ref, first_ref, run_ref, next_ref, meta_ref, n_col_blocks, copies, consume):
    j = pl.program_id(0)
    i = pl.program_id(1)
    n_active = meta_ref[0]
    n_runs = meta_ref[1]

    @pl.when(jnp.logical_and(j == 0, i == 0))
    def _():
        for cp in copies(te_ref[0], 0, 0):
            cp.start()

    @pl.when(jnp.logical_and(i < n_active, first_ref[i] == 1))
    def _():
        slot = (j * n_runs + run_ref[i]) & 1
        for cp in copies(te_ref[i], j, slot):
            cp.wait()
        last = run_ref[i] == n_runs - 1
        e_next = jnp.where(last, te_ref[0], next_ref[i])
        j_next = jnp.where(last, j + 1, j)

        @pl.when(j_next < n_col_blocks)
        def _():
            for cp in copies(e_next, j_next, 1 - slot):
                cp.start()

        consume(slot)


def _gate_up_kernel(te_ref, first_ref, run_ref, next_ref, meta_ref, x_ref, bg_ref, bu_ref, w_hbm, o_ref,
                    wbuf, wg_b, wu_b, sem, *, nj, tn):
    i = pl.program_id(1)
    active = i < meta_ref[0]

    def copies(e, jj, slot):
        return [pltpu.make_async_copy(w_hbm.at[e, :, pl.ds(pl.multiple_of((c * nj + jj) * tn, tn), tn)],
                                      wbuf.at[slot, c], sem.at[slot, c]) for c in range(2)]

    def consume(slot):
        wg_b[...] = wbuf[slot, 0].astype(BF16)
        wu_b[...] = wbuf[slot, 1].astype(BF16)

    _stream_expert_weights(te_ref, first_ref, run_ref, next_ref, meta_ref, nj, copies, consume)

    @pl.when(active)
    def _():
        x = x_ref[...]
        gate = jnp.dot(x, wg_b[...], preferred_element_type=F32) + bg_ref[...]
        up = jnp.dot(x, wu_b[...], preferred_element_type=F32) + bu_ref[...]
        gate = jnp.minimum(gate, SWIGLU_LIMIT)
        up = jnp.clip(up, -SWIGLU_LIMIT, SWIGLU_LIMIT)
        o_ref[...] = ((up + 1.0) * gate * jax.nn.sigmoid(SWIGLU_ALPHA * gate)).astype(BF16)

    @pl.when(jnp.logical_not(active))
    def _():
        o_ref[...] = jnp.zeros_like(o_ref)


def _expert_gate_up(xs, w_gu, b_gu, sched):
    n_rows, d = xs.shape
    n_exp, _, f2 = w_gu.shape
    f = f2 // 2
    tm = EXPERT_ROW_TILE
    tn = 1024
    nj = f // tn
    row = lambda i, meta: jnp.minimum(i, meta[0] - 1)
    return pl.pallas_call(
        functools.partial(_gate_up_kernel, nj=nj, tn=tn),
        out_shape=jax.ShapeDtypeStruct((n_rows, f), BF16),
        grid_spec=pltpu.PrefetchScalarGridSpec(
            num_scalar_prefetch=5,
            grid=(nj, n_rows // tm),
            in_specs=[pl.BlockSpec((tm, d), lambda j, i, te, fi, ru, nx, meta: (row(i, meta), 0)),
                      pl.BlockSpec((None, 1, tn), lambda j, i, te, fi, ru, nx, meta: (te[row(i, meta)], 0, j)),
                      pl.BlockSpec((None, 1, tn),
                                   lambda j, i, te, fi, ru, nx, meta: (te[row(i, meta)], 0, nj + j)),
                      pl.BlockSpec(memory_space=pl.ANY)],
            out_specs=pl.BlockSpec((tm, tn), lambda j, i, te, fi, ru, nx, meta: (i, j)),
            scratch_shapes=[pltpu.VMEM((2, 2, d, tn), F32), pltpu.VMEM((d, tn), BF16),
                            pltpu.VMEM((d, tn), BF16), pltpu.SemaphoreType.DMA((2, 2))]),
        compiler_params=_params(("arbitrary", "arbitrary")),
        name="expert_gate_up",
    )(*sched, xs, b_gu.reshape(n_exp, 1, f2), b_gu.reshape(n_exp, 1, f2), w_gu)


def _down_kernel(te_ref, first_ref, run_ref, next_ref, meta_ref, a_ref, b_ref, w_hbm, o_ref, wbuf, w_b, sem):
    i = pl.program_id(1)
    active = i < meta_ref[0]

    def copies(e, jj, slot):
        return [pltpu.make_async_copy(w_hbm.at[e], wbuf.at[slot], sem.at[slot])]

    def consume(slot):
        w_b[...] = wbuf[slot].astype(BF16)

    _stream_expert_weights(te_ref, first_ref, run_ref, next_ref, meta_ref, 1, copies, consume)

    @pl.when(active)
    def _():
        o_ref[...] = jnp.dot(a_ref[...], w_b[...], preferred_element_type=F32) + b_ref[...]

    @pl.when(jnp.logical_not(active))
    def _():
        o_ref[...] = jnp.zeros_like(o_ref)


def _expert_down(act, w_d, b_d, sched):
    n_rows, f = act.shape
    n_exp, _, d = w_d.shape
    tm = EXPERT_ROW_TILE
    row = lambda i, meta: jnp.minimum(i, meta[0] - 1)
    return pl.pallas_call(
        _down_kernel,
        out_shape=jax.ShapeDtypeStruct((n_rows, d), F32),
        grid_spec=pltpu.PrefetchScalarGridSpec(
            num_scalar_prefetch=5,
            grid=(1, n_rows // tm),
            in_specs=[pl.BlockSpec((tm, f), lambda j, i, te, fi, ru, nx, meta: (row(i, meta), 0)),
                      pl.BlockSpec((None, 1, d), lambda j, i, te, fi, ru, nx, meta: (te[row(i, meta)], 0, 0)),
                      pl.BlockSpec(memory_space=pl.ANY)],
            out_specs=pl.BlockSpec((tm, d), lambda j, i, te, fi, ru, nx, meta: (i, 0)),
            scratch_shapes=[pltpu.VMEM((2, f, d), F32), pltpu.VMEM((f, d), BF16),
                            pltpu.SemaphoreType.DMA((2,))]),
        compiler_params=_params(("arbitrary", "arbitrary")),
        name="expert_down",
    )(*sched, act, b_d.reshape(n_exp, 1, d), w_d)


COMBINE_TOKENS = 128


def _combine_kernel(slot_ref, slot_next_ref, p_ref, x1_ref, gate_ref, g_ref, y_ref, o_ref, buf, sem):
    i = pl.program_id(0)
    tm = COMBINE_TOKENS
    pairs = tm * TOP_K

    def issue(slots, b):
        def body(g, carry):
            for u in range(ROW_UNROLL):
                p = g * ROW_UNROLL + u
                tok = g * (ROW_UNROLL // TOP_K) + u // TOP_K
                pltpu.make_async_copy(y_ref.at[pl.ds(slots[0, p], 1)], buf.at[b, u % TOP_K, pl.ds(tok, 1)],
                                      sem.at[b]).start(priority=u % 2)
            return carry
        lax.fori_loop(0, pairs // ROW_UNROLL, body, 0)

    @pl.when(i == 0)
    def _():
        issue(slot_ref, 0)

    @pl.when(i + 1 < pl.num_programs(0))
    def _():
        issue(slot_next_ref, (i + 1) & 1)

    b = i & 1
    for k in range(TOP_K):
        pltpu.make_async_copy(y_ref.at[pl.ds(0, tm)], buf.at[b, k], sem.at[b]).wait()
    probs = p_ref[...]
    f = probs[:, 0:1] * buf[b, 0]
    for k in range(1, TOP_K):
        f = f + probs[:, k:k + 1] * buf[b, k]
    o_ref[...] = x1_ref[...] + gate_ref[...] * (_rms(f) * g_ref[...])


def _combine(y, slots, probs, x1, gate, g_post, rows_per_group):
    m, d = x1.shape
    tm = COMBINE_TOKENS
    nt = m // tm
    pairs = tm * TOP_K
    return pl.pallas_call(
        _combine_kernel,
        out_shape=jax.ShapeDtypeStruct((m, d), F32),
        grid=(nt,),
        in_specs=[pl.BlockSpec((None, 1, pairs), lambda i: (i, 0, 0), memory_space=pltpu.SMEM),
                  pl.BlockSpec((None, 1, pairs), lambda i: (jnp.minimum(i + 1, nt - 1), 0, 0),
                               memory_space=pltpu.SMEM),
                  pl.BlockSpec((tm, LANES), lambda i: (i, 0)),
                  pl.BlockSpec((tm, d), lambda i: (i, 0)),
                  _mod_spec(gate, tm, rows_per_group),
                  pl.BlockSpec((1, d), lambda i: (0, 0)),
                  pl.BlockSpec(memory_space=pl.ANY)],
        out_specs=pl.BlockSpec((tm, d), lambda i: (i, 0)),
        scratch_shapes=[pltpu.VMEM((2, TOP_K, tm, d), F32), pltpu.SemaphoreType.DMA((2,))],
        compiler_params=_params(("arbitrary",)),
        name="combine",
    )(slots.reshape(nt, 1, pairs), slots.reshape(nt, 1, pairs), probs, x1, gate, g_post.reshape(1, d), y)


def _moe(h2_p, h2_s, logits_p, logits_s, n_exp, w_gu, b_gu, w_d, b_d):
    n = h2_p.shape[0] + h2_s.shape[0]
    tile = EXPERT_ROW_TILE
    top_i, probs, rank, counts = _route(logits_p, logits_s)
    counts = counts[0, :n_exp].astype(I32)
    cap = (counts + tile - 1) // tile * tile
    ends = jnp.cumsum(cap)
    offs = ends - cap
    n_tiles = (n * TOP_K) // tile + n_exp
    n_active = (ends[-1] // tile).astype(I32).reshape(1)
    tile_id = jnp.arange(n_tiles, dtype=I32)
    tile_expert = jnp.minimum(jnp.sum(ends[None, :] <= tile_id[:, None] * tile, axis=1), n_exp - 1).astype(I32)
    first = ((tile_id == 0) | (tile_expert != jnp.roll(tile_expert, 1))) & (tile_id < n_active[0])
    run = jnp.cumsum(first.astype(I32)) - 1
    next_expert = tile_expert[jnp.minimum(ends[tile_expert] // tile, n_tiles - 1)]
    sched = (tile_expert, first.astype(I32), run.astype(I32), next_expert.astype(I32),
             jnp.stack([n_active[0], jnp.sum(first.astype(I32))]).astype(I32))
    expert_ids = jnp.arange(n_exp, dtype=I32)
    pair_offs = jnp.sum(jnp.where(top_i[:, :TOP_K, None] == expert_ids, offs, 0), axis=-1)
    slots = pair_offs + rank[:, :TOP_K]
    n_prompt = h2_p.shape[0]
    meta = jnp.concatenate([offs, cap, counts, n_active]).astype(I32)
    table = _invert(slots, meta, n_tiles * tile, n_prompt, n_exp)
    split = jnp.sum(table.reshape(n_tiles, tile) < n_prompt, axis=1).astype(I32)
    xs = _dispatch(h2_p, h2_s, table, split, n_active)
    act = _expert_gate_up(xs, w_gu, b_gu, sched)
    y = _expert_down(act, w_d, b_d, sched)
    return y, slots, probs


def _layer(xp, xs_, cp_mod, cs_mod, hist_a, hist_q, s0, g_pre_mix, g_post_mix, g_pre_ffn, g_post_ffn,
           w_in, conv_a_w, gdn_conv_w, a_log, dt_bias, g_conv_out, gdn_norm_g, w_out, router_w, router_b,
           w_gu, b_gu, w_d, b_d):
    bsz, seq, d = xp.shape
    ns = xs_.shape[0]
    n_exp = router_w.shape[1]
    dc = N_HEADS * HEAD_D
    d_main = 3 * dc + 3 * dc + dc
    xp2 = xp.reshape(bsz * seq, d)
    xs2 = xs_.reshape(ns, d)

    w_main = w_in.astype(BF16)
    w_ba = jnp.zeros((d, LANES), BF16).at[:, :2 * N_HEADS].set(w_main[:, d_main:])
    w_out_b = w_out.astype(BF16)
    rw = jnp.zeros((d, LANES), F32).at[:, :n_exp].set(router_w)
    rw_hi = rw.astype(BF16)
    rw_lo = (rw - rw_hi.astype(F32)).astype(BF16)
    rb = jnp.full((1, LANES), NEG_BIG, F32).at[0, :n_exp].set(router_b)

    mp = [cp_mod[:, i * d:(i + 1) * d].reshape(bsz, 1, d) for i in range(6)]
    ms = [cs_mod[:, i * d:(i + 1) * d] for i in range(6)]

    proj_p, ba_p = _in_proj(xp2, mp[1], mp[0], g_pre_mix, w_main, w_ba, seq, d_main)
    proj_s, ba_s = _in_proj(xs2, ms[1], ms[0], g_pre_mix, w_main, w_ba, 1, d_main)

    mix_p, ha_p, hq_p, s_p = _mixer_prompt(proj_p, ba_p, bsz, seq, conv_a_w, gdn_conv_w, a_log, dt_bias,
                                           g_conv_out, gdn_norm_g)
    mix_s, ha_s, hq_s, s_s = _mixer_sample(proj_s, ba_s, hist_a.reshape(ns, 2 * dc),
                                           hist_q.reshape(ns, 9 * dc), s0, conv_a_w, gdn_conv_w, a_log,
                                           dt_bias, g_conv_out, gdn_norm_g)

    x1_p, h2_p, lg_p = _post_mix(mix_p, xp2, mp[2], mp[4], mp[3], g_post_mix, g_pre_ffn, w_out_b, rw_hi,
                                 rw_lo, rb, seq)
    x1_s, h2_s, lg_s = _post_mix(mix_s, xs2, ms[2], ms[4], ms[3], g_post_mix, g_pre_ffn, w_out_b, rw_hi,
                                 rw_lo, rb, 1)

    y, slots, probs = _moe(h2_p, h2_s, lg_p, lg_s, n_exp, w_gu, b_gu, w_d, b_d)
    np_ = bsz * seq
    out_p = _combine(y, slots[:np_], probs[:np_], x1_p, mp[5], g_post_ffn, seq)
    out_s = _combine(y, slots[np_:], probs[np_:], x1_s, ms[5], g_post_ffn, 1)
    return (out_p.reshape(bsz, seq, d), out_s.reshape(ns, 1, d), ha_p, hq_p, s_p,
            ha_s.reshape(ns, 2, dc), hq_s.reshape(ns, 3, 3 * dc), s_s)


def kernel(x_prompt, x_sample, state_conv_a, state_gdn_conv, state_gdn_S, c_prompt, c_sample, w_mod, b_mod, g_pre_mix, g_post_mix, g_pre_ffn, g_post_ffn, w_in, conv_a_w, gdn_conv_w, gdn_a_log, gdn_dt_bias, g_conv_out, gdn_norm_g, w_out, router_w, router_b, exp_w_gate_up, exp_b_gate_up, exp_w_down, exp_b_down):
    depth = w_mod.shape[0]
    bp = x_prompt.shape[0]
    xp, xs_ = x_prompt, x_sample
    outs = [[] for _ in range(6)]
    for l in range(depth):
        mod = _modulation(jnp.concatenate([c_prompt, c_sample], axis=0), w_mod[l], b_mod[l])
        res = _layer(xp, xs_, mod[:bp], mod[bp:], state_conv_a[l], state_gdn_conv[l], state_gdn_S[l],
                     g_pre_mix[l], g_post_mix[l], g_pre_ffn[l], g_post_ffn[l], w_in[l], conv_a_w[l],
                     gdn_conv_w[l], gdn_a_log[l], gdn_dt_bias[l], g_conv_out[l], gdn_norm_g[l], w_out[l],
                     router_w[l], router_b[l], exp_w_gate_up[l], exp_b_gate_up[l], exp_w_down[l],
                     exp_b_down[l])
        xp, xs_ = res[0], res[1]
        for acc, r in zip(outs, res[2:]):
            acc.append(r)
    return (xp, xs_) + tuple(o[0][None] if depth == 1 else jnp.stack(o) for o in outs)
```

```python
import functools

import jax
import jax.numpy as jnp
from jax import lax
from jax.experimental import pallas as pl
from jax.experimental.pallas import tpu as pltpu

F32 = jnp.float32
BF16 = jnp.bfloat16
I32 = jnp.int32
HIGHEST = lax.Precision.HIGHEST

EPS = 1e-6
N_HEADS = 8
HEAD_D = 128
TOP_K = 4
TOP_K_SHIFT = 2
SWIGLU_LIMIT = 7.0
SWIGLU_ALPHA = 1.702
CHUNK = 64
LANES = 128
EXPERT_ROW_TILE = 256
NEG_BIG = -1e30
VMEM_LIMIT = 56 * 1024 * 1024


def _params(semantics, vmem=VMEM_LIMIT):
    return pltpu.CompilerParams(dimension_semantics=semantics, vmem_limit_bytes=vmem)


def _mm(a, b):
    return jnp.dot(a.astype(BF16), b.astype(BF16), preferred_element_type=F32)


def _mm_nt(a, b):
    return lax.dot_general(a.astype(BF16), b.astype(BF16), (((1,), (1,)), ((), ())),
                           preferred_element_type=F32)


def _rms(x):
    return x * lax.rsqrt(jnp.mean(x * x, axis=-1, keepdims=True) + EPS)


def _silu(x):
    return x * jax.nn.sigmoid(x)


def _softplus(x):
    return jnp.maximum(x, 0.0) + jnp.log1p(jnp.exp(-jnp.abs(x)))


def _mod_kernel(c_ref, w_ref, b_ref, o_ref):
    s = _silu(c_ref[...])
    o_ref[...] = _mm(s, w_ref[...]) + b_ref[...]


def _modulation(c_all, w_mod, b_mod):
    n, d = c_all.shape
    m = w_mod.shape[1]
    tn = 1024
    return pl.pallas_call(
        _mod_kernel,
        out_shape=jax.ShapeDtypeStruct((n, m), F32),
        grid=(m // tn,),
        in_specs=[pl.BlockSpec((n, d), lambda j: (0, 0)),
                  pl.BlockSpec((d, tn), lambda j: (0, j)),
                  pl.BlockSpec((1, tn), lambda j: (0, j))],
        out_specs=pl.BlockSpec((n, tn), lambda j: (0, j)),
        compiler_params=_params(("arbitrary",)),
        name="modulation",
    )(c_all, w_mod, b_mod.reshape(1, m))


def _mod_spec(arr, tm, rows_per_group):
    if arr.ndim == 3:
        tiles = rows_per_group // tm
        return pl.BlockSpec((None, 1, arr.shape[-1]), lambda i, *_: (i // tiles, 0, 0))
    return pl.BlockSpec((tm, arr.shape[-1]), lambda i, *_: (i, 0))


def _proj_kernel(x_ref, sc_ref, sh_ref, g_ref, w_ref, wba_ref, o_ref, ba_ref, h_scr):
    @pl.when(pl.program_id(1) == 0)
    def _():
        h = (_rms(x_ref[...]) * g_ref[...]) * (1.0 + sc_ref[...]) + sh_ref[...]
        hb = h.astype(BF16)
        h_scr[...] = hb
        ba_ref[...] = jnp.dot(hb, wba_ref[...], preferred_element_type=F32)

    o_ref[...] = jnp.dot(h_scr[...], w_ref[...], preferred_element_type=F32)


def _in_proj(x, scale, shift, g, w_main, w_ba, rows_per_group, n):
    m, d = x.shape
    tm = min(m, 1024, rows_per_group if scale.ndim == 3 else m)
    tn = 1024
    return pl.pallas_call(
        _proj_kernel,
        out_shape=(jax.ShapeDtypeStruct((m, n), F32), jax.ShapeDtypeStruct((m, LANES), F32)),
        grid=(m // tm, n // tn),
        in_specs=[pl.BlockSpec((tm, d), lambda i, j: (i, 0)),
                  _mod_spec(scale, tm, rows_per_group),
                  _mod_spec(shift, tm, rows_per_group),
                  pl.BlockSpec((1, d), lambda i, j: (0, 0)),
                  pl.BlockSpec((d, tn), lambda i, j: (0, j)),
                  pl.BlockSpec((d, LANES), lambda i, j: (0, 0))],
        out_specs=(pl.BlockSpec((tm, tn), lambda i, j: (i, j)),
                   pl.BlockSpec((tm, LANES), lambda i, j: (i, 0))),
        scratch_shapes=[pltpu.VMEM((tm, d), BF16)],
        compiler_params=_params(("arbitrary", "arbitrary")),
        name="in_proj",
    )(x, scale, shift, g.reshape(1, d), w_main, w_ba)


PROMPT_SEQS_PER_STEP = 2

def _mixer_prompt_kernel(proj_ref, ba_ref, caw_ref, gcw_ref, alog_ref, dtb_ref, gco_ref, gng_ref,
                         mix_ref, ha_ref, hq_ref, s_ref, extu, extq, qc_scr, s_scr, *, nseq):
    c = CHUNK
    dc = N_HEADS * HEAD_D
    dq = 3 * dc
    t = pl.program_id(1)
    is_last = t == pl.num_programs(1) - 1

    @pl.when(t == 0)
    def _():
        extu[:, 0:8, :] = jnp.zeros((nseq, 8, dc), F32)
        extq[:, 0:8, :] = jnp.zeros((nseq, 8, dq), F32)
        s_scr[...] = jnp.zeros_like(s_scr)

    row = lax.broadcasted_iota(I32, (c, c), 0)
    col = lax.broadcasted_iota(I32, (c, c), 1)
    causal = row >= col
    strict = row > col
    lower = jnp.where(causal, 1.0, 0.0).astype(F32)
    upper = jnp.where(row <= col, 1.0, 0.0).astype(F32)
    caw = caw_ref[...]
    gcw = gcw_ref[...]

    chains = [(sq, h) for sq in range(nseq) for h in range(N_HEADS)]
    heads = range(len(chains))
    qn, kn, vb, kb, kbg, qg, kg, decay, s_decay = ([] for _ in range(9))
    for sq in range(nseq):
        u = proj_ref[sq, :, dc:2 * dc] * proj_ref[sq, :, 2 * dc:3 * dc]
        extu[sq, 8:8 + c, :] = u
        ya = caw[0:1] * extu[sq, 6:6 + c, :] + caw[1:2] * extu[sq, 7:7 + c, :] + caw[2:3] * u
        ya = proj_ref[sq, :, 0:dc] * ya
        mix_ref[sq, :, 0:dc] = (_rms(ya) * gco_ref[...]).astype(BF16)
        last_u = extu[sq, 6 + c:8 + c, :]
        extu[sq, 6:8, :] = last_u

        qkv = proj_ref[sq, :, 3 * dc:3 * dc + dq]
        extq[sq, 8:8 + c, :] = qkv
        qc = (gcw[0:1] * extq[sq, 5:5 + c, :] + gcw[1:2] * extq[sq, 6:6 + c, :]
              + gcw[2:3] * extq[sq, 7:7 + c, :] + gcw[3:4] * qkv)
        qc_scr[sq] = _silu(qc)
        last_q = extq[sq, 5 + c:8 + c, :]
        extq[sq, 5:8, :] = last_q

        @pl.when(is_last)
        def _(sq=sq, last_u=last_u, last_q=last_q):
            ha_ref[sq] = last_u
            hq_ref[sq] = last_q

        ba = ba_ref[sq]
        beta_all = jax.nn.sigmoid(ba)
        g_all = -jnp.exp(alog_ref[...]) * _softplus(ba + dtb_ref[...])
        gc_all = jnp.dot(lower, g_all, precision=HIGHEST, preferred_element_type=F32)
        gc_t = lax.dot_general(g_all, upper, (((0,), (0,)), ((), ())), precision=HIGHEST,
                               preferred_element_type=F32)
        for h in range(N_HEADS):
            lo = h * HEAD_D
            q = qc_scr[sq, :, lo:lo + HEAD_D]
            k = qc_scr[sq, :, dc + lo:dc + lo + HEAD_D]
            v = qc_scr[sq, :, 2 * dc + lo:2 * dc + lo + HEAD_D]
            qn_h = q * lax.rsqrt(jnp.sum(q * q, axis=-1, keepdims=True) + EPS) * (HEAD_D ** -0.5)
            kn_h = k * lax.rsqrt(jnp.sum(k * k, axis=-1, keepdims=True) + EPS)
            beta = beta_all[:, h:h + 1]
            gcc = gc_all[:, N_HEADS + h:N_HEADS + h + 1]
            gcr = gc_t[N_HEADS + h:N_HEADS + h + 1, :]
            gl = gc_all[c - 1:c, N_HEADS + h:N_HEADS + h + 1]
            eg = jnp.exp(gcc)
            kb_h = kn_h * beta
            qn.append(qn_h)
            kn.append(kn_h)
            vb.append(v * beta)
            kb.append(kb_h)
            kbg.append(kb_h * eg)
            qg.append(qn_h * eg)
            kg.append(kn_h * jnp.exp(gl - gcc))
            decay.append(jnp.where(causal, jnp.exp(jnp.minimum(gcc - gcr, 0.0)), 0.0))
            s_decay.append(jnp.exp(gl))

    kq = [_mm_nt(jnp.concatenate([kb[h], qn[h]], axis=0), kn[h]) for h in heads]
    a_mat = [jnp.where(strict, kq[h][:c] * decay[h], 0.0) for h in heads]
    qk = [kq[h][c:] * decay[h] for h in heads]
    n_mat = [-a_mat[h] for h in heads]
    p = a_mat
    size = 2
    while size < c:
        p = [_mm(p[h], p[h]) for h in heads]
        n_p = [_mm(n_mat[h], p[h]) for h in heads]
        n_mat = [n_mat[h] + p[h] + n_p[h] for h in heads]
        size *= 2
    rhs = [jnp.concatenate([vb[h], kbg[h]], axis=-1) for h in heads]
    uw = [rhs[h] + _mm(n_mat[h], rhs[h]) for h in heads]
    s_old = [s_scr[sq, h] for sq, h in chains]
    ws = [_mm(jnp.concatenate([uw[h][:, HEAD_D:], qg[h]], axis=0), s_old[h]) for h in heads]
    v_new = [uw[h][:, :HEAD_D] - ws[h][:c] for h in heads]
    fin = [_mm(jnp.concatenate([qk[h], kg[h].T], axis=0), v_new[h]) for h in heads]
    for i, (sq, h) in enumerate(chains):
        lo = h * HEAD_D
        s_scr[sq, h] = s_old[i] * s_decay[i] + fin[i][c:]
        o = ws[i][c:] + fin[i][:c]
        z = proj_ref[sq, :, 3 * dc + dq + lo:3 * dc + dq + lo + HEAD_D]
        yb = _rms(o) * gng_ref[...] * _silu(z)
        mix_ref[sq, :, dc + lo:dc + lo + HEAD_D] = yb.astype(BF16)

    @pl.when(is_last)
    def _():
        s_ref[...] = s_scr[...]


def _lane_row(vec, offset):
    return jnp.zeros((1, LANES), F32).at[0, offset:offset + vec.shape[0]].set(vec.astype(F32))


def _mixer_prompt(proj, ba, bsz, seq, conv_a_w, gdn_conv_w, a_log, dt_bias, g_conv_out, gdn_norm_g):
    c = CHUNK
    dc = N_HEADS * HEAD_D
    dq = 3 * dc
    dproj = proj.shape[1]
    nt = seq // c
    const = lambda shape: pl.BlockSpec(shape, lambda b, t: (0,) * len(shape))
    nseq = PROMPT_SEQS_PER_STEP if bsz % PROMPT_SEQS_PER_STEP == 0 else 1
    seq_block = lambda *tail: pl.BlockSpec((nseq,) + tail, lambda b, t: (b, t) + (0,) * (len(tail) - 1))
    whole = lambda *tail: pl.BlockSpec((nseq,) + tail, lambda b, t: (b,) + (0,) * len(tail))
    mix, ha, hq, s_fin = pl.pallas_call(
        functools.partial(_mixer_prompt_kernel, nseq=nseq),
        out_shape=(jax.ShapeDtypeStruct((bsz, seq, 2 * dc), BF16),
                   jax.ShapeDtypeStruct((bsz, 2, dc), F32),
                   jax.ShapeDtypeStruct((bsz, 3, dq), F32),
                   jax.ShapeDtypeStruct((bsz, N_HEADS, HEAD_D, HEAD_D), F32)),
        grid=(bsz // nseq, nt),
        in_specs=[seq_block(c, dproj), seq_block(c, LANES),
                  const((3, dc)), const((4, dq)), const((1, LANES)), const((1, LANES)),
                  const((1, dc)), const((1, HEAD_D))],
        out_specs=(seq_block(c, 2 * dc), whole(2, dc), whole(3, dq), whole(N_HEADS, HEAD_D, HEAD_D)),
        scratch_shapes=[pltpu.VMEM((nseq, 8 + c, dc), F32), pltpu.VMEM((nseq, 8 + c, dq), F32),
                        pltpu.VMEM((nseq, c, dq), F32), pltpu.VMEM((nseq, N_HEADS, HEAD_D, HEAD_D), F32)],
        compiler_params=_params(("arbitrary", "arbitrary")),
        name="mixer_prompt",
    )(proj.reshape(bsz, seq, dproj), ba.reshape(bsz, seq, LANES), conv_a_w, gdn_conv_w,
      _lane_row(a_log, N_HEADS), _lane_row(dt_bias, N_HEADS), g_conv_out.reshape(1, dc),
      gdn_norm_g.reshape(1, HEAD_D))
    return mix.reshape(bsz * seq, 2 * dc), ha, hq, s_fin


SAMPLE_GROUP = 16


def _mixer_sample_kernel(proj_ref, ba_ref, hista_ref, histq_ref, s_in_ref, caw_ref, gcw_ref, alog_ref,
                         dtb_ref, gco_ref, gng_ref, mix_ref, ha_ref, hq_ref, s_out_ref, qc_scr, o_scr):
    tb = SAMPLE_GROUP
    dc = N_HEADS * HEAD_D
    dq = 3 * dc

    u = proj_ref[:, dc:2 * dc] * proj_ref[:, 2 * dc:3 * dc]
    caw = caw_ref[...]
    ya = caw[0:1] * hista_ref[:, 0:dc] + caw[1:2] * hista_ref[:, dc:2 * dc] + caw[2:3] * u
    ya = proj_ref[:, 0:dc] * ya
    mix_ref[:, 0:dc] = _rms(ya) * gco_ref[...]
    ha_ref[:, 0:dc] = hista_ref[:, dc:2 * dc]
    ha_ref[:, dc:2 * dc] = u

    qkv = proj_ref[:, 3 * dc:3 * dc + dq]
    gcw = gcw_ref[...]
    qc = (gcw[0:1] * histq_ref[:, 0:dq] + gcw[1:2] * histq_ref[:, dq:2 * dq]
          + gcw[2:3] * histq_ref[:, 2 * dq:3 * dq] + gcw[3:4] * qkv)
    qc_scr[...] = _silu(qc)
    hq_ref[:, 0:dq] = histq_ref[:, dq:2 * dq]
    hq_ref[:, dq:2 * dq] = histq_ref[:, 2 * dq:3 * dq]
    hq_ref[:, 2 * dq:3 * dq] = qkv

    ba = ba_ref[...]
    beta_all = jax.nn.sigmoid(ba)
    eg_all = jnp.exp(-jnp.exp(alog_ref[...]) * _softplus(ba + dtb_ref[...]))

    for h in range(N_HEADS):
        lo = h * HEAD_D
        q = qc_scr[:, lo:lo + HEAD_D]
        k = qc_scr[:, dc + lo:dc + lo + HEAD_D]
        v = qc_scr[:, 2 * dc + lo:2 * dc + lo + HEAD_D]
        qn = q * lax.rsqrt(jnp.sum(q * q, axis=-1, keepdims=True) + EPS) * (HEAD_D ** -0.5)
        kn = k * lax.rsqrt(jnp.sum(k * k, axis=-1, keepdims=True) + EPS)
        qk = jnp.sum(qn * kn, axis=-1, keepdims=True)
        kn_t = kn.T
        qn_t = qn.T
        for b in range(tb):
            s_old = s_in_ref[b, h]
            kc = kn_t[:, b:b + 1]
            e = eg_all[b:b + 1, N_HEADS + h:N_HEADS + h + 1]
            ks = jnp.sum(s_old * kc, axis=0, keepdims=True)
            qs = jnp.sum(s_old * qn_t[:, b:b + 1], axis=0, keepdims=True)
            v_new = beta_all[b:b + 1, h:h + 1] * (v[b:b + 1, :] - e * ks)
            o_scr[b:b + 1, lo:lo + HEAD_D] = e * qs + qk[b:b + 1, :] * v_new
            s_out_ref[b, h] = s_old * e + kc * v_new
        z = proj_ref[:, 3 * dc + dq + lo:3 * dc + dq + lo + HEAD_D]
        o = o_scr[:, lo:lo + HEAD_D]
        mix_ref[:, dc + lo:dc + lo + HEAD_D] = _rms(o) * gng_ref[...] * _silu(z)


def _mixer_sample(proj, ba, hist_a, hist_q, s_in, conv_a_w, gdn_conv_w, a_log, dt_bias, g_conv_out,
                  gdn_norm_g):
    n = proj.shape[0]
    tb = SAMPLE_GROUP
    dc = N_HEADS * HEAD_D
    dq = 3 * dc
    dproj = proj.shape[1]
    const = lambda shape: pl.BlockSpec(shape, lambda i: (0,) * len(shape))
    rows = lambda width: pl.BlockSpec((tb, width), lambda i: (i, 0))
    state = pl.BlockSpec((tb, N_HEADS, HEAD_D, HEAD_D), lambda i: (i, 0, 0, 0))
    return pl.pallas_call(
        _mixer_sample_kernel,
        out_shape=(jax.ShapeDtypeStruct((n, 2 * dc), F32),
                   jax.ShapeDtypeStruct((n, 2 * dc), F32),
                   jax.ShapeDtypeStruct((n, 3 * dq), F32),
                   jax.ShapeDtypeStruct((n, N_HEADS, HEAD_D, HEAD_D), F32)),
        grid=(n // tb,),
        in_specs=[rows(dproj), rows(LANES), rows(2 * dc), rows(3 * dq), state,
                  const((3, dc)), const((4, dq)), const((1, LANES)), const((1, LANES)),
                  const((1, dc)), const((1, HEAD_D))],
        out_specs=(rows(2 * dc), rows(2 * dc), rows(3 * dq), state),
        scratch_shapes=[pltpu.VMEM((tb, dq), F32), pltpu.VMEM((tb, dc), F32)],
        compiler_params=_params(("arbitrary",)),
        name="mixer_sample",
    )(proj, ba, hist_a, hist_q, s_in, conv_a_w, gdn_conv_w, _lane_row(a_log, N_HEADS),
      _lane_row(dt_bias, N_HEADS), g_conv_out.reshape(1, dc), gdn_norm_g.reshape(1, HEAD_D))


POST_MIX_SUB_ROWS = 256

def _post_mix_kernel(mix_ref, x_ref, gate_ref, sc_ref, sh_ref, gpost_ref, gpre_ref, wout_ref,
                     rwh_ref, rwl_ref, rb_ref, x1_ref, h2_ref, lg_ref):
    tm = x_ref.shape[0]
    sub = min(tm, POST_MIX_SUB_ROWS)
    tiles = [slice(r0, r0 + sub) for r0 in range(0, tm, sub)]
    mixes = [jnp.dot(mix_ref[rows].astype(BF16), wout_ref[...], preferred_element_type=F32)
             for rows in tiles]
    for rows, mix in zip(tiles, mixes):
        per_row = lambda ref: ref[rows] if ref.shape[0] == tm else ref[...]
        x1 = x_ref[rows] + per_row(gate_ref) * (_rms(mix) * gpost_ref[...])
        x1_ref[rows] = x1
        h2 = (_rms(x1) * gpre_ref[...]) * (1.0 + per_row(sc_ref)) + per_row(sh_ref)
        h2_ref[rows] = h2
        hi = h2.astype(BF16)
        lo = (h2 - hi.astype(F32)).astype(BF16)
        rwh = rwh_ref[...]
        lg_ref[rows] = (jnp.dot(hi, rwh, preferred_element_type=F32)
                        + jnp.dot(lo, rwh, preferred_element_type=F32)
                        + jnp.dot(hi, rwl_ref[...], preferred_element_type=F32) + rb_ref[...])


def _post_mix(mix_in, x, gate, scale, shift, g_post, g_pre, w_out, rw_hi, rw_lo, rb, rows_per_group):
    m, d = x.shape
    tm = min(m, 512, rows_per_group if gate.ndim == 3 else m)
    const = lambda shape: pl.BlockSpec(shape, lambda i: (0,) * len(shape))
    rows = lambda width: pl.BlockSpec((tm, width), lambda i: (i, 0))
    return pl.pallas_call(
        _post_mix_kernel,
        out_shape=(jax.ShapeDtypeStruct((m, d), F32), jax.ShapeDtypeStruct((m, d), F32),
                   jax.ShapeDtypeStruct((m, LANES), F32)),
        grid=(m // tm,),
        in_specs=[rows(d), rows(d),
                  _mod_spec(gate, tm, rows_per_group),
                  _mod_spec(scale, tm, rows_per_group),
                  _mod_spec(shift, tm, rows_per_group),
                  const((1, d)), const((1, d)), const((d, d)),
                  const((d, LANES)), const((d, LANES)), const((1, LANES))],
        out_specs=(rows(d), rows(d), rows(LANES)),
        compiler_params=_params(("arbitrary",)),
        name="post_mix",
    )(mix_in, x, gate, scale, shift, g_post.reshape(1, d), g_pre.reshape(1, d), w_out, rw_hi, rw_lo, rb)


ROUTE_TOKENS = 128


def _route_kernel(lgp_ref, lgs_ref, idx_ref, p_ref, rank_ref, cnt_ref, carry, *, prompt_tiles):
    tm = lgp_ref.shape[0]

    @pl.when(pl.program_id(0) == 0)
    def _():
        carry[...] = jnp.zeros_like(carry)

    l = jnp.where(pl.program_id(0) < prompt_tiles, lgp_ref[...], lgs_ref[...])
    lane = lax.broadcasted_iota(I32, l.shape, 1)
    lane_f = lane.astype(F32)
    vals, hots = [], []
    idx_out = jnp.zeros(l.shape, F32)
    for k in range(TOP_K):
        m = jnp.max(l, axis=-1, keepdims=True)
        idx = jnp.min(jnp.where(l == m, lane_f, float(LANES)), axis=-1, keepdims=True)
        hot = lane_f == idx
        vals.append(m)
        hots.append(hot)
        idx_out = jnp.where(lane == k, idx, idx_out)
        l = jnp.where(hot, -jnp.inf, l)
    exps = [jnp.exp(v - vals[0]) for v in vals]
    denom = exps[0] + exps[1] + exps[2] + exps[3]
    p_out = jnp.zeros(l.shape, F32)
    for k in range(TOP_K):
        p_out = jnp.where(lane == k, exps[k] / denom, p_out)
    member = jnp.where(hots[0] | hots[1] | hots[2] | hots[3], 1.0, 0.0).astype(F32)
    row = lax.broadcasted_iota(I32, (tm, tm), 0)
    col = lax.broadcasted_iota(I32, (tm, tm), 1)
    before = jnp.where(row > col, 1.0, 0.0).astype(BF16)
    prefix = jnp.dot(before, member.astype(BF16), preferred_element_type=F32) + carry[...]
    rank_out = jnp.zeros(l.shape, F32)
    for k in range(TOP_K):
        r = jnp.sum(jnp.where(hots[k], prefix, 0.0), axis=-1, keepdims=True)
        rank_out = jnp.where(lane == k, r, rank_out)
    carry[...] = carry[...] + jnp.sum(member, axis=0, keepdims=True)
    idx_ref[...] = idx_out.astype(I32)
    p_ref[...] = p_out
    rank_ref[...] = rank_out.astype(I32)
    cnt_ref[...] = carry[...]


def _route(logits_p, logits_s):
    tm = ROUTE_TOKENS
    pt = logits_p.shape[0] // tm
    n = logits_p.shape[0] + logits_s.shape[0]
    tile = pl.BlockSpec((tm, LANES), lambda i: (i, 0))
    return pl.pallas_call(
        functools.partial(_route_kernel, prompt_tiles=pt),
        out_shape=(jax.ShapeDtypeStruct((n, LANES), I32), jax.ShapeDtypeStruct((n, LANES), F32),
                   jax.ShapeDtypeStruct((n, LANES), I32), jax.ShapeDtypeStruct((1, LANES), F32)),
        grid=(n // tm,),
        in_specs=[pl.BlockSpec((tm, LANES), lambda i: (jnp.minimum(i, pt - 1), 0)),
                  pl.BlockSpec((tm, LANES), lambda i: (jnp.maximum(i - pt, 0), 0))],
        out_specs=(tile, tile, tile, pl.BlockSpec((1, LANES), lambda i: (0, 0))),
        scratch_shapes=[pltpu.VMEM((1, LANES), F32)],
        compiler_params=_params(("arbitrary",)),
        name="route",
    )(logits_p, logits_s)


INVERT_TOKENS_MAX = 1024


def _largest_tile(n, unit, cap):
    return max(m for m in range(unit, cap + 1, unit) if n % m == 0)


def _invert_kernel(meta_ref, slot_ref, table_ref, *, pad_token, n_exp, tokens):
    i = pl.program_id(0)
    pairs = tokens * TOP_K

    @pl.when(i == 0)
    def _():
        def fill(r, carry):
            table_ref[r] = pad_token
            return carry

        def fill_group_tail(e, carry):
            lax.fori_loop(meta_ref[e] + meta_ref[2 * n_exp + e], meta_ref[e] + meta_ref[n_exp + e], fill, 0)
            return carry

        lax.fori_loop(0, n_exp, fill_group_tail, 0)
        lax.fori_loop(meta_ref[3 * n_exp] * EXPERT_ROW_TILE, table_ref.shape[0], fill, 0)

    def put(p, carry):
        table_ref[slot_ref[0, p]] = i * tokens + (p >> TOP_K_SHIFT)
        return carry

    lax.fori_loop(0, pairs, put, 0, unroll=16)


def _invert(slots, meta, n_rows, pad_token, n_exp):
    n = slots.shape[0]
    tokens = _largest_tile(n, LANES, INVERT_TOKENS_MAX)
    nt = n // tokens
    pairs = tokens * TOP_K
    return pl.pallas_call(
        functools.partial(_invert_kernel, pad_token=pad_token, n_exp=n_exp, tokens=tokens),
        out_shape=jax.ShapeDtypeStruct((n_rows,), I32),
        grid_spec=pltpu.PrefetchScalarGridSpec(
            num_scalar_prefetch=1,
            grid=(nt,),
            in_specs=[pl.BlockSpec((None, 1, pairs), lambda i, meta: (i, 0, 0), memory_space=pltpu.SMEM)],
            out_specs=pl.BlockSpec(memory_space=pltpu.SMEM)),
        compiler_params=_params(("arbitrary",)),
        name="invert",
    )(meta, slots.reshape(nt, 1, pairs))


ROW_UNROLL = 8
ROW_UNROLL_SHIFT = 3


def _dispatch_kernel(na_ref, split_ref, tok_ref, tok_next_ref, h2p_ref, h2s_ref, xs_ref, buf, sem, *,
                     n_prompt):
    i = pl.program_id(0)
    tile = EXPERT_ROW_TILE
    n_active = na_ref[0]

    def issue(tok, split, slot):
        def prompt_copy(r, priority):
            pltpu.make_async_copy(h2p_ref.at[pl.ds(tok[0, r], 1)], buf.at[slot, pl.ds(r, 1)],
                                  sem.at[slot]).start(priority=priority)

        def sample_copy(r, priority):
            pltpu.make_async_copy(h2s_ref.at[pl.ds(tok[0, r] - n_prompt, 1)], buf.at[slot, pl.ds(r, 1)],
                                  sem.at[slot]).start(priority=priority)

        def rows(lo, hi, fn):
            def body(r, carry):
                fn(r, 0)
                return carry
            lax.fori_loop(lo, hi, body, 0)

        def groups(lo, hi, fn):
            def body(g, carry):
                for u in range(ROW_UNROLL):
                    fn(g * ROW_UNROLL + u, u % 2)
                return carry
            lax.fori_loop(lo, hi, body, 0)

        whole = split >> ROW_UNROLL_SHIFT
        first = (split + ROW_UNROLL - 1) >> ROW_UNROLL_SHIFT
        groups(0, whole, prompt_copy)
        rows(whole * ROW_UNROLL, split, prompt_copy)
        rows(split, first * ROW_UNROLL, sample_copy)
        groups(first, tile // ROW_UNROLL, sample_copy)

    @pl.when(i == 0)
    def _():
        issue(tok_ref, split_ref[0], 0)

    @pl.when(i + 1 < n_active)
    def _():
        issue(tok_next_ref, split_ref[i + 1], (i + 1) & 1)

    @pl.when(i < n_active)
    def _():
        slot = i & 1
        pltpu.make_async_copy(h2p_ref.at[pl.ds(0, tile)], buf.at[slot], sem.at[slot]).wait()
        xs_ref[...] = buf[slot].astype(BF16)

    @pl.when(i >= n_active)
    def _():
        xs_ref[...] = jnp.zeros_like(xs_ref)


def _dispatch(h2_p, h2_s, table, split, n_active):
    d = h2_p.shape[1]
    tile = EXPERT_ROW_TILE
    n_rows = table.shape[0]
    nt = n_rows // tile
    return pl.pallas_call(
        functools.partial(_dispatch_kernel, n_prompt=h2_p.shape[0]),
        out_shape=jax.ShapeDtypeStruct((n_rows, d), BF16),
        grid_spec=pltpu.PrefetchScalarGridSpec(
            num_scalar_prefetch=2,
            grid=(nt,),
            in_specs=[pl.BlockSpec((None, 1, tile), lambda i, na, sp: (i, 0, 0), memory_space=pltpu.SMEM),
                      pl.BlockSpec((None, 1, tile), lambda i, na, sp: (jnp.minimum(i + 1, nt - 1), 0, 0),
                                   memory_space=pltpu.SMEM),
                      pl.BlockSpec(memory_space=pl.ANY), pl.BlockSpec(memory_space=pl.ANY)],
            out_specs=pl.BlockSpec((tile, d), lambda i, na, sp: (i, 0)),
            scratch_shapes=[pltpu.VMEM((2, tile, d), F32), pltpu.SemaphoreType.DMA((2,))]),
        compiler_params=_params(("arbitrary",)),
        name="dispatch",
    )(n_active, split, table.reshape(nt, 1, tile), table.reshape(nt, 1, tile), h2_p, h2_s)


def _expert_run_tiles(first_ref, tiles_ref, meta_ref, src_hbm, dst_hbm, col, in_buf, out_buf, in_sem, out_sem,
                      prepare, compute):
    r = pl.program_id(1)
    tile = EXPERT_ROW_TILE
    width = out_buf.shape[2]
    n_t = tiles_ref[r]
    t0 = first_ref[r]

    def rows(t):
        return pl.ds(pl.multiple_of(t * tile, tile), tile)

    def read(t, slot):
        return pltpu.make_async_copy(src_hbm.at[rows(t0 + t)], in_buf.at[slot], in_sem.at[slot])

    def write(t, slot):
        return pltpu.make_async_copy(out_buf.at[slot], dst_hbm.at[rows(t), pl.ds(col, width)],
                                     out_sem.at[slot])

    @pl.when(n_t > 0)
    def _():
        prepare()
        read(0, 0).start()

        def body(t, carry):
            slot = t & 1
            read(t, slot).wait()

            @pl.when(t + 1 < n_t)
            def _():
                read(t + 1, 1 - slot).start()

            block = compute(in_buf[slot])

            @pl.when(t >= 2)
            def _():
                write(t0 + t - 2, slot).wait()

            out_buf[slot] = block
            write(t0 + t, slot).start()
            return carry

        lax.fori_loop(0, n_t, body, 0)

        @pl.when(n_t >= 2)
        def _():
            write(t0 + n_t - 2, n_t & 1).wait()

        write(t0 + n_t - 1, (n_t - 1) & 1).wait()

    @pl.when(r == pl.num_programs(1) - 1)
    def _():
        out_buf[0] = jnp.zeros(out_buf.shape[1:], out_buf.dtype)

        def start(t, carry):
            write(t, 0).start()
            return carry

        def wait(t, carry):
            write(t, 0).wait()
            return carry

        n_tiles = dst_hbm.shape[0] // tile
        lax.fori_loop(meta_ref[0], n_tiles, start, 0)
        lax.fori_loop(meta_ref[0], n_tiles, wait, 0)


def _gate_up_kernel(expert_ref, first_ref, tiles_ref, meta_ref, wg_ref, wu_ref, bg_ref, bu_ref, xs_hbm, act_hbm,
                    wg_b, wu_b, xbuf, obuf, xsem, osem, *, tn):
    def prepare():
        wg_b[...] = wg_ref[...].astype(BF16)
        wu_b[...] = wu_ref[...].astype(BF16)

    def compute(x):
        gate = jnp.dot(x, wg_b[...], preferred_element_type=F32) + bg_ref[...]
        up = jnp.dot(x, wu_b[...], preferred_element_type=F32) + bu_ref[...]
        gate = jnp.minimum(gate, SWIGLU_LIMIT)
        up = jnp.clip(up, -SWIGLU_LIMIT, SWIGLU_LIMIT)
        return ((up + 1.0) * gate * jax.nn.sigmoid(SWIGLU_ALPHA * gate)).astype(BF16)

    col = pl.multiple_of(pl.program_id(0) * tn, tn)
    _expert_run_tiles(first_ref, tiles_ref, meta_ref, xs_hbm, act_hbm, col, xbuf, obuf, xsem, osem,
                      prepare, compute)


def _expert_gate_up(xs, w_gu, b_gu, runs):
    n_rows, d = xs.shape
    n_exp, _, f2 = w_gu.shape
    f = f2 // 2
    tm = EXPERT_ROW_TILE
    tn = 1024
    nj = f // tn
    weight = lambda c: pl.BlockSpec((None, d, tn), lambda j, r, ex, fi, ti, meta: (ex[r], 0, c * nj + j))
    bias = lambda c: pl.BlockSpec((None, 1, tn), lambda j, r, ex, fi, ti, meta: (ex[r], 0, c * nj + j))
    return pl.pallas_call(
        functools.partial(_gate_up_kernel, tn=tn),
        out_shape=jax.ShapeDtypeStruct((n_rows, f), BF16),
        grid_spec=pltpu.PrefetchScalarGridSpec(
            num_scalar_prefetch=4,
            grid=(nj, n_exp),
            in_specs=[weight(0), weight(1), bias(0), bias(1), pl.BlockSpec(memory_space=pl.ANY)],
            out_specs=pl.BlockSpec(memory_space=pl.ANY),
            scratch_shapes=[pltpu.VMEM((d, tn), BF16), pltpu.VMEM((d, tn), BF16),
                            pltpu.VMEM((2, tm, d), BF16), pltpu.VMEM((2, tm, tn), BF16),
                            pltpu.SemaphoreType.DMA((2,)), pltpu.SemaphoreType.DMA((2,))]),
        compiler_params=_params(("arbitrary", "arbitrary")),
        name="expert_gate_up",
    )(*runs, w_gu, w_gu, b_gu.reshape(n_exp, 1, f2), b_gu.reshape(n_exp, 1, f2), xs)


def _down_kernel(expert_ref, first_ref, tiles_ref, meta_ref, w_ref, b_ref, act_hbm, y_hbm, w_b, abuf, ybuf,
                 asem, ysem):
    def prepare():
        w_b[...] = w_ref[...].astype(BF16)

    def compute(a):
        return jnp.dot(a, w_b[...], preferred_element_type=F32) + b_ref[...]

    _expert_run_tiles(first_ref, tiles_ref, meta_ref, act_hbm, y_hbm, 0, abuf, ybuf, asem, ysem,
                      prepare, compute)


def _expert_down(act, w_d, b_d, runs):
    n_rows, f = act.shape
    n_exp, _, d = w_d.shape
    tm = EXPERT_ROW_TILE
    return pl.pallas_call(
        _down_kernel,
        out_shape=jax.ShapeDtypeStruct((n_rows, d), F32),
        grid_spec=pltpu.PrefetchScalarGridSpec(
            num_scalar_prefetch=4,
            grid=(1, n_exp),
            in_specs=[pl.BlockSpec((None, f, d), lambda j, r, ex, fi, ti, meta: (ex[r], 0, 0)),
                      pl.BlockSpec((None, 1, d), lambda j, r, ex, fi, ti, meta: (ex[r], 0, 0)),
                      pl.BlockSpec(memory_space=pl.ANY)],
            out_specs=pl.BlockSpec(memory_space=pl.ANY),
            scratch_shapes=[pltpu.VMEM((f, d), BF16), pltpu.VMEM((2, tm, f), BF16),
                            pltpu.VMEM((2, tm, d), F32), pltpu.SemaphoreType.DMA((2,)),
                            pltpu.SemaphoreType.DMA((2,))]),
        compiler_params=_params(("arbitrary", "arbitrary")),
        name="expert_down",
    )(*runs, w_d, b_d.reshape(n_exp, 1, d), act)


COMBINE_TOKENS = 128


def _combine_kernel(slot_ref, slot_next_ref, p_ref, x1_ref, gate_ref, g_ref, y_ref, o_ref, buf, sem):
    i = pl.program_id(0)
    tm = COMBINE_TOKENS
    pairs = tm * TOP_K

    def issue(slots, b):
        def body(g, carry):
            for u in range(ROW_UNROLL):
                p = g * ROW_UNROLL + u
                tok = g * (ROW_UNROLL // TOP_K) + u // TOP_K
                pltpu.make_async_copy(y_ref.at[pl.ds(slots[0, p], 1)], buf.at[b, u % TOP_K, pl.ds(tok, 1)],
                                      sem.at[b]).start(priority=u % 2)
            return carry
        lax.fori_loop(0, pairs // ROW_UNROLL, body, 0)

    @pl.when(i == 0)
    def _():
        issue(slot_ref, 0)

    @pl.when(i + 1 < pl.num_programs(0))
    def _():
        issue(slot_next_ref, (i + 1) & 1)

    b = i & 1
    for k in range(TOP_K):
        pltpu.make_async_copy(y_ref.at[pl.ds(0, tm)], buf.at[b, k], sem.at[b]).wait()
    probs = p_ref[...]
    f = probs[:, 0:1] * buf[b, 0]
    for k in range(1, TOP_K):
        f = f + probs[:, k:k + 1] * buf[b, k]
    o_ref[...] = x1_ref[...] + gate_ref[...] * (_rms(f) * g_ref[...])


def _combine(y, slots, probs, x1, gate, g_post, rows_per_group):
    m, d = x1.shape
    tm = COMBINE_TOKENS
    nt = m // tm
    pairs = tm * TOP_K
    return pl.pallas_call(
        _combine_kernel,
        out_shape=jax.ShapeDtypeStruct((m, d), F32),
        grid=(nt,),
        in_specs=[pl.BlockSpec((None, 1, pairs), lambda i: (i, 0, 0), memory_space=pltpu.SMEM),
                  pl.BlockSpec((None, 1, pairs), lambda i: (jnp.minimum(i + 1, nt - 1), 0, 0),
                               memory_space=pltpu.SMEM),
                  pl.BlockSpec((tm, LANES), lambda i: (i, 0)),
                  pl.BlockSpec((tm, d), lambda i: (i, 0)),
                  _mod_spec(gate, tm, rows_per_group),
                  pl.BlockSpec((1, d), lambda i: (0, 0)),
                  pl.BlockSpec(memory_space=pl.ANY)],
        out_specs=pl.BlockSpec((tm, d), lambda i: (i, 0)),
        scratch_shapes=[pltpu.VMEM((2, TOP_K, tm, d), F32), pltpu.SemaphoreType.DMA((2,))],
        compiler_params=_params(("arbitrary",)),
        name="combine",
    )(slots.reshape(nt, 1, pairs), slots.reshape(nt, 1, pairs), probs, x1, gate, g_post.reshape(1, d), y)


def _moe(h2_p, h2_s, logits_p, logits_s, n_exp, w_gu, b_gu, w_d, b_d):
    n = h2_p.shape[0] + h2_s.shape[0]
    tile = EXPERT_ROW_TILE
    top_i, probs, rank, counts = _route(logits_p, logits_s)
    counts = counts[0, :n_exp].astype(I32)
    cap = (counts + tile - 1) // tile * tile
    ends = jnp.cumsum(cap)
    offs = ends - cap
    n_tiles = (n * TOP_K) // tile + n_exp
    n_active = (ends[-1] // tile).astype(I32).reshape(1)
    expert_ids = jnp.arange(n_exp, dtype=I32)
    has_rows = cap > 0
    n_runs = jnp.sum(has_rows.astype(I32))
    run_of = jnp.cumsum(has_rows.astype(I32)) - 1
    hit = has_rows[None, :] & (run_of[None, :] == jnp.minimum(expert_ids, n_runs - 1)[:, None])
    pick = lambda v: jnp.sum(jnp.where(hit, v[None, :], 0), axis=1).astype(I32)
    runs = (pick(expert_ids), pick(offs // tile), jnp.where(expert_ids < n_runs, pick(cap // tile), 0),
            n_active)
    pair_offs = jnp.sum(jnp.where(top_i[:, :TOP_K, None] == expert_ids, offs, 0), axis=-1)
    slots = pair_offs + rank[:, :TOP_K]
    n_prompt = h2_p.shape[0]
    meta = jnp.concatenate([offs, cap, counts, n_active]).astype(I32)
    table = _invert(slots, meta, n_tiles * tile, n_prompt, n_exp)
    split = jnp.sum(table.reshape(n_tiles, tile) < n_prompt, axis=1).astype(I32)
    xs = _dispatch(h2_p, h2_s, table, split, n_active)
    act = _expert_gate_up(xs, w_gu, b_gu, runs)
    y = _expert_down(act, w_d, b_d, runs)
    return y, slots, probs


def _layer(xp, xs_, cp_mod, cs_mod, hist_a, hist_q, s0, g_pre_mix, g_post_mix, g_pre_ffn, g_post_ffn,
           w_in, conv_a_w, gdn_conv_w, a_log, dt_bias, g_conv_out, gdn_norm_g, w_out, router_w, router_b,
           w_gu, b_gu, w_d, b_d):
    bsz, seq, d = xp.shape
    ns = xs_.shape[0]
    n_exp = router_w.shape[1]
    dc = N_HEADS * HEAD_D
    d_main = 3 * dc + 3 * dc + dc
    xp2 = xp.reshape(bsz * seq, d)
    xs2 = xs_.reshape(ns, d)

    w_main = w_in.astype(BF16)
    w_ba = jnp.zeros((d, LANES), BF16).at[:, :2 * N_HEADS].set(w_main[:, d_main:])
    w_out_b = w_out.astype(BF16)
    rw = jnp.zeros((d, LANES), F32).at[:, :n_exp].set(router_w)
    rw_hi = rw.astype(BF16)
    rw_lo = (rw - rw_hi.astype(F32)).astype(BF16)
    rb = jnp.full((1, LANES), NEG_BIG, F32).at[0, :n_exp].set(router_b)

    mp = [cp_mod[:, i * d:(i + 1) * d].reshape(bsz, 1, d) for i in range(6)]
    ms = [cs_mod[:, i * d:(i + 1) * d] for i in range(6)]

    proj_p, ba_p = _in_proj(xp2, mp[1], mp[0], g_pre_mix, w_main, w_ba, seq, d_main)
    proj_s, ba_s = _in_proj(xs2, ms[1], ms[0], g_pre_mix, w_main, w_ba, 1, d_main)

    mix_p, ha_p, hq_p, s_p = _mixer_prompt(proj_p, ba_p, bsz, seq, conv_a_w, gdn_conv_w, a_log, dt_bias,
                                           g_conv_out, gdn_norm_g)
    mix_s, ha_s, hq_s, s_s = _mixer_sample(proj_s, ba_s, hist_a.reshape(ns, 2 * dc),
                                           hist_q.reshape(ns, 9 * dc), s0, conv_a_w, gdn_conv_w, a_log,
                                           dt_bias, g_conv_out, gdn_norm_g)

    x1_p, h2_p, lg_p = _post_mix(mix_p, xp2, mp[2], mp[4], mp[3], g_post_mix, g_pre_ffn, w_out_b, rw_hi,
                                 rw_lo, rb, seq)
    x1_s, h2_s, lg_s = _post_mix(mix_s, xs2, ms[2], ms[4], ms[3], g_post_mix, g_pre_ffn, w_out_b, rw_hi,
                                 rw_lo, rb, 1)

    y, slots, probs = _moe(h2_p, h2_s, lg_p, lg_s, n_exp, w_gu, b_gu, w_d, b_d)
    np_ = bsz * seq
    out_p = _combine(y, slots[:np_], probs[:np_], x1_p, mp[5], g_post_ffn, seq)
    out_s = _combine(y, slots[np_:], probs[np_:], x1_s, ms[5], g_post_ffn, 1)
    return (out_p.reshape(bsz, seq, d), out_s.reshape(ns, 1, d), ha_p, hq_p, s_p,
            ha_s.reshape(ns, 2, dc), hq_s.reshape(ns, 3, 3 * dc), s_s)


def kernel(x_prompt, x_sample, state_conv_a, state_gdn_conv, state_gdn_S, c_prompt, c_sample, w_mod, b_mod, g_pre_mix, g_post_mix, g_pre_ffn, g_post_ffn, w_in, conv_a_w, gdn_conv_w, gdn_a_log, gdn_dt_bias, g_conv_out, gdn_norm_g, w_out, router_w, router_b, exp_w_gate_up, exp_b_gate_up, exp_w_down, exp_b_down):
    depth = w_mod.shape[0]
    bp = x_prompt.shape[0]
    xp, xs_ = x_prompt, x_sample
    outs = [[] for _ in range(6)]
    for l in range(depth):
        mod = _modulation(jnp.concatenate([c_prompt, c_sample], axis=0), w_mod[l], b_mod[l])
        res = _layer(xp, xs_, mod[:bp], mod[bp:], state_conv_a[l], state_gdn_conv[l], state_gdn_S[l],
                     g_pre_mix[l], g_post_mix[l], g_pre_ffn[l], g_post_ffn[l], w_in[l], conv_a_w[l],
                     gdn_conv_w[l], gdn_a_log[l], gdn_dt_bias[l], g_conv_out[l], gdn_norm_g[l], w_out[l],
                     router_w[l], router_b[l], exp_w_gate_up[l], exp_b_gate_up[l], exp_w_down[l],
                     exp_b_down[l])
        xp, xs_ = res[0], res[1]
        for acc, r in zip(outs, res[2:]):
            acc.append(r)
    return (xp, xs_) + tuple(o[0][None] if depth == 1 else jnp.stack(o) for o in outs)
```

```python
import functools

import jax
import jax.numpy as jnp
from jax import lax
from jax.experimental import pallas as pl
from jax.experimental.pallas import tpu as pltpu

F32 = jnp.float32
BF16 = jnp.bfloat16
I32 = jnp.int32
HIGHEST = lax.Precision.HIGHEST

EPS = 1e-6
N_HEADS = 8
HEAD_D = 128
TOP_K = 4
TOP_K_SHIFT = 2
SWIGLU_LIMIT = 7.0
SWIGLU_ALPHA = 1.702
CHUNK = 64
LANES = 128
EXPERT_ROW_TILE = 256
NEG_BIG = -1e30
VMEM_LIMIT = 56 * 1024 * 1024


def _params(semantics, vmem=VMEM_LIMIT):
    return pltpu.CompilerParams(dimension_semantics=semantics, vmem_limit_bytes=vmem)


def _mm(a, b):
    return jnp.dot(a.astype(BF16), b.astype(BF16), preferred_element_type=F32)


def _mm_nt(a, b):
    return lax.dot_general(a.astype(BF16), b.astype(BF16), (((1,), (1,)), ((), ())),
                           preferred_element_type=F32)


def _rms(x):
    return x * lax.rsqrt(jnp.mean(x * x, axis=-1, keepdims=True) + EPS)


def _silu(x):
    return x * jax.nn.sigmoid(x)


def _softplus(x):
    return jnp.maximum(x, 0.0) + jnp.log1p(jnp.exp(-jnp.abs(x)))


def _mod_kernel(c_ref, w_ref, b_ref, o_ref):
    s = _silu(c_ref[...])
    o_ref[...] = _mm(s, w_ref[...]) + b_ref[...]


def _modulation(c_all, w_mod, b_mod):
    n, d = c_all.shape
    m = w_mod.shape[1]
    tn = 1024
    return pl.pallas_call(
        _mod_kernel,
        out_shape=jax.ShapeDtypeStruct((n, m), F32),
        grid=(m // tn,),
        in_specs=[pl.BlockSpec((n, d), lambda j: (0, 0)),
                  pl.BlockSpec((d, tn), lambda j: (0, j)),
                  pl.BlockSpec((1, tn), lambda j: (0, j))],
        out_specs=pl.BlockSpec((n, tn), lambda j: (0, j)),
        compiler_params=_params(("arbitrary",)),
        name="modulation",
    )(c_all, w_mod, b_mod.reshape(1, m))


def _mod_spec(arr, tm, rows_per_group):
    if arr.ndim == 3:
        tiles = rows_per_group // tm
        return pl.BlockSpec((None, 1, arr.shape[-1]), lambda i, *_: (i // tiles, 0, 0))
    return pl.BlockSpec((tm, arr.shape[-1]), lambda i, *_: (i, 0))


def _proj_kernel(x_ref, sc_ref, sh_ref, g_ref, w_ref, wba_ref, o_ref, ba_ref, h_scr):
    @pl.when(pl.program_id(1) == 0)
    def _():
        h = (_rms(x_ref[...]) * g_ref[...]) * (1.0 + sc_ref[...]) + sh_ref[...]
        hb = h.astype(BF16)
        h_scr[...] = hb
        ba_ref[...] = jnp.dot(hb, wba_ref[...], preferred_element_type=F32)

    o_ref[...] = jnp.dot(h_scr[...], w_ref[...], preferred_element_type=F32)


def _in_proj(x, scale, shift, g, w_main, w_ba, rows_per_group, n):
    m, d = x.shape
    tm = min(m, 1024, rows_per_group if scale.ndim == 3 else m)
    tn = 1024
    return pl.pallas_call(
        _proj_kernel,
        out_shape=(jax.ShapeDtypeStruct((m, n), F32), jax.ShapeDtypeStruct((m, LANES), F32)),
        grid=(m // tm, n // tn),
        in_specs=[pl.BlockSpec((tm, d), lambda i, j: (i, 0)),
                  _mod_spec(scale, tm, rows_per_group),
                  _mod_spec(shift, tm, rows_per_group),
                  pl.BlockSpec((1, d), lambda i, j: (0, 0)),
                  pl.BlockSpec((d, tn), lambda i, j: (0, j)),
                  pl.BlockSpec((d, LANES), lambda i, j: (0, 0))],
        out_specs=(pl.BlockSpec((tm, tn), lambda i, j: (i, j)),
                   pl.BlockSpec((tm, LANES), lambda i, j: (i, 0))),
        scratch_shapes=[pltpu.VMEM((tm, d), BF16)],
        compiler_params=_params(("arbitrary", "arbitrary")),
        name="in_proj",
    )(x, scale, shift, g.reshape(1, d), w_main, w_ba)


PROMPT_SEQS_PER_STEP = 2

def _mixer_prompt_kernel(proj_ref, ba_ref, caw_ref, gcw_ref, alog_ref, dtb_ref, gco_ref, gng_ref,
                         mix_ref, ha_ref, hq_ref, s_ref, extu, extq, qc_scr, s_scr, *, nseq):
    c = CHUNK
    dc = N_HEADS * HEAD_D
    dq = 3 * dc
    t = pl.program_id(1)
    is_last = t == pl.num_programs(1) - 1

    @pl.when(t == 0)
    def _():
        extu[:, 0:8, :] = jnp.zeros((nseq, 8, dc), F32)
        extq[:, 0:8, :] = jnp.zeros((nseq, 8, dq), F32)
        s_scr[...] = jnp.zeros_like(s_scr)

    row = lax.broadcasted_iota(I32, (c, c), 0)
    col = lax.broadcasted_iota(I32, (c, c), 1)
    causal = row >= col
    strict = row > col
    lower = jnp.where(causal, 1.0, 0.0).astype(F32)
    upper = jnp.where(row <= col, 1.0, 0.0).astype(F32)
    caw = caw_ref[...]
    gcw = gcw_ref[...]

    chains = [(sq, h) for sq in range(nseq) for h in range(N_HEADS)]
    heads = range(len(chains))
    qn, kn, vb, kb, kbg, qg, kg, decay, s_decay = ([] for _ in range(9))
    for sq in range(nseq):
        u = proj_ref[sq, :, dc:2 * dc] * proj_ref[sq, :, 2 * dc:3 * dc]
        extu[sq, 8:8 + c, :] = u
        ya = caw[0:1] * extu[sq, 6:6 + c, :] + caw[1:2] * extu[sq, 7:7 + c, :] + caw[2:3] * u
        ya = proj_ref[sq, :, 0:dc] * ya
        mix_ref[sq, :, 0:dc] = (_rms(ya) * gco_ref[...]).astype(BF16)
        last_u = extu[sq, 6 + c:8 + c, :]
        extu[sq, 6:8, :] = last_u

        qkv = proj_ref[sq, :, 3 * dc:3 * dc + dq]
        extq[sq, 8:8 + c, :] = qkv
        qc = (gcw[0:1] * extq[sq, 5:5 + c, :] + gcw[1:2] * extq[sq, 6:6 + c, :]
              + gcw[2:3] * extq[sq, 7:7 + c, :] + gcw[3:4] * qkv)
        qc_scr[sq] = _silu(qc)
        last_q = extq[sq, 5 + c:8 + c, :]
        extq[sq, 5:8, :] = last_q

        @pl.when(is_last)
        def _(sq=sq, last_u=last_u, last_q=last_q):
            ha_ref[sq] = last_u
            hq_ref[sq] = last_q

        ba = ba_ref[sq]
        beta_all = jax.nn.sigmoid(ba)
        g_all = -jnp.exp(alog_ref[...]) * _softplus(ba + dtb_ref[...])
        gc_all = jnp.dot(lower, g_all, precision=HIGHEST, preferred_element_type=F32)
        gc_t = lax.dot_general(g_all, upper, (((0,), (0,)), ((), ())), precision=HIGHEST,
                               preferred_element_type=F32)
        for h in range(N_HEADS):
            lo = h * HEAD_D
            q = qc_scr[sq, :, lo:lo + HEAD_D]
            k = qc_scr[sq, :, dc + lo:dc + lo + HEAD_D]
            v = qc_scr[sq, :, 2 * dc + lo:2 * dc + lo + HEAD_D]
            qn_h = q * lax.rsqrt(jnp.sum(q * q, axis=-1, keepdims=True) + EPS) * (HEAD_D ** -0.5)
            kn_h = k * lax.rsqrt(jnp.sum(k * k, axis=-1, keepdims=True) + EPS)
            beta = beta_all[:, h:h + 1]
            gcc = gc_all[:, N_HEADS + h:N_HEADS + h + 1]
            gcr = gc_t[N_HEADS + h:N_HEADS + h + 1, :]
            gl = gc_all[c - 1:c, N_HEADS + h:N_HEADS + h + 1]
            eg = jnp.exp(gcc)
            kb_h = kn_h * beta
            qn.append(qn_h)
            kn.append(kn_h)
            vb.append(v * beta)
            kb.append(kb_h)
            kbg.append(kb_h * eg)
            qg.append(qn_h * eg)
            kg.append(kn_h * jnp.exp(gl - gcc))
            decay.append(jnp.where(causal, jnp.exp(jnp.minimum(gcc - gcr, 0.0)), 0.0))
            s_decay.append(jnp.exp(gl))

    kq = [_mm_nt(jnp.concatenate([kb[h], qn[h]], axis=0), kn[h]) for h in heads]
    a_mat = [jnp.where(strict, kq[h][:c] * decay[h], 0.0) for h in heads]
    qk = [kq[h][c:] * decay[h] for h in heads]
    n_mat = [-a_mat[h] for h in heads]
    p = a_mat
    size = 2
    while size < c:
        p = [_mm(p[h], p[h]) for h in heads]
        n_p = [_mm(n_mat[h], p[h]) for h in heads]
        n_mat = [n_mat[h] + p[h] + n_p[h] for h in heads]
        size *= 2
    rhs = [jnp.concatenate([vb[h], kbg[h]], axis=-1) for h in heads]
    uw = [rhs[h] + _mm(n_mat[h], rhs[h]) for h in heads]
    s_old = [s_scr[sq, h] for sq, h in chains]
    ws = [_mm(jnp.concatenate([uw[h][:, HEAD_D:], qg[h]], axis=0), s_old[h]) for h in heads]
    v_new = [uw[h][:, :HEAD_D] - ws[h][:c] for h in heads]
    fin = [_mm(jnp.concatenate([qk[h], kg[h].T], axis=0), v_new[h]) for h in heads]
    for i, (sq, h) in enumerate(chains):
        lo = h * HEAD_D
        s_scr[sq, h] = s_old[i] * s_decay[i] + fin[i][c:]
        o = ws[i][c:] + fin[i][:c]
        z = proj_ref[sq, :, 3 * dc + dq + lo:3 * dc + dq + lo + HEAD_D]
        yb = _rms(o) * gng_ref[...] * _silu(z)
        mix_ref[sq, :, dc + lo:dc + lo + HEAD_D] = yb.astype(BF16)

    @pl.when(is_last)
    def _():
        s_ref[...] = s_scr[...]


def _lane_row(vec, offset):
    return jnp.zeros((1, LANES), F32).at[0, offset:offset + vec.shape[0]].set(vec.astype(F32))


def _mixer_prompt(proj, ba, bsz, seq, conv_a_w, gdn_conv_w, a_log, dt_bias, g_conv_out, gdn_norm_g):
    c = CHUNK
    dc = N_HEADS * HEAD_D
    dq = 3 * dc
    dproj = proj.shape[1]
    nt = seq // c
    const = lambda shape: pl.BlockSpec(shape, lambda b, t: (0,) * len(shape))
    nseq = PROMPT_SEQS_PER_STEP if bsz % PROMPT_SEQS_PER_STEP == 0 else 1
    seq_block = lambda *tail: pl.BlockSpec((nseq,) + tail, lambda b, t: (b, t) + (0,) * (len(tail) - 1))
    whole = lambda *tail: pl.BlockSpec((nseq,) + tail, lambda b, t: (b,) + (0,) * len(tail))
    mix, ha, hq, s_fin = pl.pallas_call(
        functools.partial(_mixer_prompt_kernel, nseq=nseq),
        out_shape=(jax.ShapeDtypeStruct((bsz, seq, 2 * dc), BF16),
                   jax.ShapeDtypeStruct((bsz, 2, dc), F32),
                   jax.ShapeDtypeStruct((bsz, 3, dq), F32),
                   jax.ShapeDtypeStruct((bsz, N_HEADS, HEAD_D, HEAD_D), F32)),
        grid=(bsz // nseq, nt),
        in_specs=[seq_block(c, dproj), seq_block(c, LANES),
                  const((3, dc)), const((4, dq)), const((1, LANES)), const((1, LANES)),
                  const((1, dc)), const((1, HEAD_D))],
        out_specs=(seq_block(c, 2 * dc), whole(2, dc), whole(3, dq), whole(N_HEADS, HEAD_D, HEAD_D)),
        scratch_shapes=[pltpu.VMEM((nseq, 8 + c, dc), F32), pltpu.VMEM((nseq, 8 + c, dq), F32),
                        pltpu.VMEM((nseq, c, dq), F32), pltpu.VMEM((nseq, N_HEADS, HEAD_D, HEAD_D), F32)],
        compiler_params=_params(("arbitrary", "arbitrary")),
        name="mixer_prompt",
    )(proj.reshape(bsz, seq, dproj), ba.reshape(bsz, seq, LANES), conv_a_w, gdn_conv_w,
      _lane_row(a_log, N_HEADS), _lane_row(dt_bias, N_HEADS), g_conv_out.reshape(1, dc),
      gdn_norm_g.reshape(1, HEAD_D))
    return mix.reshape(bsz * seq, 2 * dc), ha, hq, s_fin


SAMPLE_GROUP = 16


def _mixer_sample_kernel(proj_ref, ba_ref, hista_ref, histq_ref, s_in_ref, caw_ref, gcw_ref, alog_ref,
                         dtb_ref, gco_ref, gng_ref, mix_ref, ha_ref, hq_ref, s_out_ref, qc_scr, o_scr):
    tb = SAMPLE_GROUP
    dc = N_HEADS * HEAD_D
    dq = 3 * dc

    u = proj_ref[:, dc:2 * dc] * proj_ref[:, 2 * dc:3 * dc]
    caw = caw_ref[...]
    ya = caw[0:1] * hista_ref[:, 0:dc] + caw[1:2] * hista_ref[:, dc:2 * dc] + caw[2:3] * u
    ya = proj_ref[:, 0:dc] * ya
    mix_ref[:, 0:dc] = _rms(ya) * gco_ref[...]
    ha_ref[:, 0:dc] = hista_ref[:, dc:2 * dc]
    ha_ref[:, dc:2 * dc] = u

    qkv = proj_ref[:, 3 * dc:3 * dc + dq]
    gcw = gcw_ref[...]
    qc = (gcw[0:1] * histq_ref[:, 0:dq] + gcw[1:2] * histq_ref[:, dq:2 * dq]
          + gcw[2:3] * histq_ref[:, 2 * dq:3 * dq] + gcw[3:4] * qkv)
    qc_scr[...] = _silu(qc)
    hq_ref[:, 0:dq] = histq_ref[:, dq:2 * dq]
    hq_ref[:, dq:2 * dq] = histq_ref[:, 2 * dq:3 * dq]
    hq_ref[:, 2 * dq:3 * dq] = qkv

    ba = ba_ref[...]
    beta_all = jax.nn.sigmoid(ba)
    eg_all = jnp.exp(-jnp.exp(alog_ref[...]) * _softplus(ba + dtb_ref[...]))

    for h in range(N_HEADS):
        lo = h * HEAD_D
        q = qc_scr[:, lo:lo + HEAD_D]
        k = qc_scr[:, dc + lo:dc + lo + HEAD_D]
        v = qc_scr[:, 2 * dc + lo:2 * dc + lo + HEAD_D]
        qn = q * lax.rsqrt(jnp.sum(q * q, axis=-1, keepdims=True) + EPS) * (HEAD_D ** -0.5)
        kn = k * lax.rsqrt(jnp.sum(k * k, axis=-1, keepdims=True) + EPS)
        qk = jnp.sum(qn * kn, axis=-1, keepdims=True)
        kn_t = kn.T
        qn_t = qn.T
        for b in range(tb):
            s_old = s_in_ref[b, h]
            kc = kn_t[:, b:b + 1]
            e = eg_all[b:b + 1, N_HEADS + h:N_HEADS + h + 1]
            ks = jnp.sum(s_old * kc, axis=0, keepdims=True)
            qs = jnp.sum(s_old * qn_t[:, b:b + 1], axis=0, keepdims=True)
            v_new = beta_all[b:b + 1, h:h + 1] * (v[b:b + 1, :] - e * ks)
            o_scr[b:b + 1, lo:lo + HEAD_D] = e * qs + qk[b:b + 1, :] * v_new
            s_out_ref[b, h] = s_old * e + kc * v_new
        z = proj_ref[:, 3 * dc + dq + lo:3 * dc + dq + lo + HEAD_D]
        o = o_scr[:, lo:lo + HEAD_D]
        mix_ref[:, dc + lo:dc + lo + HEAD_D] = _rms(o) * gng_ref[...] * _silu(z)


def _mixer_sample(proj, ba, hist_a, hist_q, s_in, conv_a_w, gdn_conv_w, a_log, dt_bias, g_conv_out,
                  gdn_norm_g):
    n = proj.shape[0]
    tb = SAMPLE_GROUP
    dc = N_HEADS * HEAD_D
    dq = 3 * dc
    dproj = proj.shape[1]
    const = lambda shape: pl.BlockSpec(shape, lambda i: (0,) * len(shape))
    rows = lambda width: pl.BlockSpec((tb, width), lambda i: (i, 0))
    state = pl.BlockSpec((tb, N_HEADS, HEAD_D, HEAD_D), lambda i: (i, 0, 0, 0))
    return pl.pallas_call(
        _mixer_sample_kernel,
        out_shape=(jax.ShapeDtypeStruct((n, 2 * dc), F32),
                   jax.ShapeDtypeStruct((n, 2 * dc), F32),
                   jax.ShapeDtypeStruct((n, 3 * dq), F32),
                   jax.ShapeDtypeStruct((n, N_HEADS, HEAD_D, HEAD_D), F32)),
        grid=(n // tb,),
        in_specs=[rows(dproj), rows(LANES), rows(2 * dc), rows(3 * dq), state,
                  const((3, dc)), const((4, dq)), const((1, LANES)), const((1, LANES)),
                  const((1, dc)), const((1, HEAD_D))],
        out_specs=(rows(2 * dc), rows(2 * dc), rows(3 * dq), state),
        scratch_shapes=[pltpu.VMEM((tb, dq), F32), pltpu.VMEM((tb, dc), F32)],
        compiler_params=_params(("arbitrary",)),
        name="mixer_sample",
    )(proj, ba, hist_a, hist_q, s_in, conv_a_w, gdn_conv_w, _lane_row(a_log, N_HEADS),
      _lane_row(dt_bias, N_HEADS), g_conv_out.reshape(1, dc), gdn_norm_g.reshape(1, HEAD_D))


POST_MIX_SUB_ROWS = 256

def _post_mix_kernel(mix_ref, x_ref, gate_ref, sc_ref, sh_ref, gpost_ref, gpre_ref, wout_ref,
                     rwh_ref, rwl_ref, rb_ref, x1_ref, h2_ref, lg_ref):
    tm = x_ref.shape[0]
    sub = min(tm, POST_MIX_SUB_ROWS)
    tiles = [slice(r0, r0 + sub) for r0 in range(0, tm, sub)]
    mixes = [jnp.dot(mix_ref[rows].astype(BF16), wout_ref[...], preferred_element_type=F32)
             for rows in tiles]
    for rows, mix in zip(tiles, mixes):
        per_row = lambda ref: ref[rows] if ref.shape[0] == tm else ref[...]
        x1 = x_ref[rows] + per_row(gate_ref) * (_rms(mix) * gpost_ref[...])
        x1_ref[rows] = x1
        h2 = (_rms(x1) * gpre_ref[...]) * (1.0 + per_row(sc_ref)) + per_row(sh_ref)
        h2_ref[rows] = h2
        hi = h2.astype(BF16)
        lo = (h2 - hi.astype(F32)).astype(BF16)
        rwh = rwh_ref[...]
        lg_ref[rows] = (jnp.dot(hi, rwh, preferred_element_type=F32)
                        + jnp.dot(lo, rwh, preferred_element_type=F32)
                        + jnp.dot(hi, rwl_ref[...], preferred_element_type=F32) + rb_ref[...])


def _post_mix(mix_in, x, gate, scale, shift, g_post, g_pre, w_out, rw_hi, rw_lo, rb, rows_per_group):
    m, d = x.shape
    tm = min(m, 512, rows_per_group if gate.ndim == 3 else m)
    const = lambda shape: pl.BlockSpec(shape, lambda i: (0,) * len(shape))
    rows = lambda width: pl.BlockSpec((tm, width), lambda i: (i, 0))
    return pl.pallas_call(
        _post_mix_kernel,
        out_shape=(jax.ShapeDtypeStruct((m, d), F32), jax.ShapeDtypeStruct((m, d), F32),
                   jax.ShapeDtypeStruct((m, LANES), F32)),
        grid=(m // tm,),
        in_specs=[rows(d), rows(d),
                  _mod_spec(gate, tm, rows_per_group),
                  _mod_spec(scale, tm, rows_per_group),
                  _mod_spec(shift, tm, rows_per_group),
                  const((1, d)), const((1, d)), const((d, d)),
                  const((d, LANES)), const((d, LANES)), const((1, LANES))],
        out_specs=(rows(d), rows(d), rows(LANES)),
        compiler_params=_params(("arbitrary",)),
        name="post_mix",
    )(mix_in, x, gate, scale, shift, g_post.reshape(1, d), g_pre.reshape(1, d), w_out, rw_hi, rw_lo, rb)


ROUTE_TOKENS = 128


def _route_kernel(lgp_ref, lgs_ref, idx_ref, p_ref, rank_ref, cnt_ref, carry, *, prompt_tiles):
    tm = lgp_ref.shape[0]

    @pl.when(pl.program_id(0) == 0)
    def _():
        carry[...] = jnp.zeros_like(carry)

    l = jnp.where(pl.program_id(0) < prompt_tiles, lgp_ref[...], lgs_ref[...])
    lane = lax.broadcasted_iota(I32, l.shape, 1)
    lane_f = lane.astype(F32)
    vals, hots = [], []
    idx_out = jnp.zeros(l.shape, F32)
    for k in range(TOP_K):
        m = jnp.max(l, axis=-1, keepdims=True)
        idx = jnp.min(jnp.where(l == m, lane_f, float(LANES)), axis=-1, keepdims=True)
        hot = lane_f == idx
        vals.append(m)
        hots.append(hot)
        idx_out = jnp.where(lane == k, idx, idx_out)
        l = jnp.where(hot, -jnp.inf, l)
    exps = [jnp.exp(v - vals[0]) for v in vals]
    denom = exps[0] + exps[1] + exps[2] + exps[3]
    p_out = jnp.zeros(l.shape, F32)
    for k in range(TOP_K):
        p_out = jnp.where(lane == k, exps[k] / denom, p_out)
    member = jnp.where(hots[0] | hots[1] | hots[2] | hots[3], 1.0, 0.0).astype(F32)
    row = lax.broadcasted_iota(I32, (tm, tm), 0)
    col = lax.broadcasted_iota(I32, (tm, tm), 1)
    before = jnp.where(row > col, 1.0, 0.0).astype(BF16)
    prefix = jnp.dot(before, member.astype(BF16), preferred_element_type=F32) + carry[...]
    rank_out = jnp.zeros(l.shape, F32)
    for k in range(TOP_K):
        r = jnp.sum(jnp.where(hots[k], prefix, 0.0), axis=-1, keepdims=True)
        rank_out = jnp.where(lane == k, r, rank_out)
    carry[...] = carry[...] + jnp.sum(member, axis=0, keepdims=True)
    idx_ref[...] = idx_out.astype(I32)
    p_ref[...] = p_out
    rank_ref[...] = rank_out.astype(I32)
    cnt_ref[...] = carry[...]


def _route(logits_p, logits_s):
    tm = ROUTE_TOKENS
    pt = logits_p.shape[0] // tm
    n = logits_p.shape[0] + logits_s.shape[0]
    tile = pl.BlockSpec((tm, LANES), lambda i: (i, 0))
    return pl.pallas_call(
        functools.partial(_route_kernel, prompt_tiles=pt),
        out_shape=(jax.ShapeDtypeStruct((n, LANES), I32), jax.ShapeDtypeStruct((n, LANES), F32),
                   jax.ShapeDtypeStruct((n, LANES), I32), jax.ShapeDtypeStruct((1, LANES), F32)),
        grid=(n // tm,),
        in_specs=[pl.BlockSpec((tm, LANES), lambda i: (jnp.minimum(i, pt - 1), 0)),
                  pl.BlockSpec((tm, LANES), lambda i: (jnp.maximum(i - pt, 0), 0))],
        out_specs=(tile, tile, tile, pl.BlockSpec((1, LANES), lambda i: (0, 0))),
        scratch_shapes=[pltpu.VMEM((1, LANES), F32)],
        compiler_params=_params(("arbitrary",)),
        name="route",
    )(logits_p, logits_s)


INVERT_TOKENS_MAX = 1024


def _largest_tile(n, unit, cap):
    return max(m for m in range(unit, cap + 1, unit) if n % m == 0)


def _invert_kernel(meta_ref, slot_ref, table_ref, *, pad_token, n_exp, tokens):
    i = pl.program_id(0)
    pairs = tokens * TOP_K

    @pl.when(i == 0)
    def _():
        def fill(r, carry):
            table_ref[r] = pad_token
            return carry

        def fill_group_tail(e, carry):
            lax.fori_loop(meta_ref[e] + meta_ref[2 * n_exp + e], meta_ref[e] + meta_ref[n_exp + e], fill, 0)
            return carry

        lax.fori_loop(0, n_exp, fill_group_tail, 0)
        lax.fori_loop(meta_ref[3 * n_exp] * EXPERT_ROW_TILE, table_ref.shape[0], fill, 0)

    def put(p, carry):
        table_ref[slot_ref[0, p]] = i * tokens + (p >> TOP_K_SHIFT)
        return carry

    lax.fori_loop(0, pairs, put, 0, unroll=16)


def _invert(slots, meta, n_rows, pad_token, n_exp):
    n = slots.shape[0]
    tokens = _largest_tile(n, LANES, INVERT_TOKENS_MAX)
    nt = n // tokens
    pairs = tokens * TOP_K
    return pl.pallas_call(
        functools.partial(_invert_kernel, pad_token=pad_token, n_exp=n_exp, tokens=tokens),
        out_shape=jax.ShapeDtypeStruct((n_rows,), I32),
        grid_spec=pltpu.PrefetchScalarGridSpec(
            num_scalar_prefetch=1,
            grid=(nt,),
            in_specs=[pl.BlockSpec((None, 1, pairs), lambda i, meta: (i, 0, 0), memory_space=pltpu.SMEM)],
            out_specs=pl.BlockSpec(memory_space=pltpu.SMEM)),
        compiler_params=_params(("arbitrary",)),
        name="invert",
    )(meta, slots.reshape(nt, 1, pairs))


ROW_UNROLL = 8
ROW_UNROLL_SHIFT = 3


def _dispatch_kernel(na_ref, split_ref, tok_ref, tok_next_ref, h2p_ref, h2s_ref, xs_ref, buf, sem, *,
                     n_prompt):
    i = pl.program_id(0)
    tile = EXPERT_ROW_TILE
    n_active = na_ref[0]

    def issue(tok, split, slot):
        def prompt_copy(r):
            pltpu.make_async_copy(h2p_ref.at[pl.ds(tok[0, r], 1)], buf.at[slot, pl.ds(r, 1)],
                                  sem.at[slot]).start()

        def sample_copy(r):
            pltpu.make_async_copy(h2s_ref.at[pl.ds(tok[0, r] - n_prompt, 1)], buf.at[slot, pl.ds(r, 1)],
                                  sem.at[slot]).start()

        def rows(lo, hi, fn):
            def body(r, carry):
                fn(r)
                return carry
            lax.fori_loop(lo, hi, body, 0)

        def groups(lo, hi, fn):
            def body(g, carry):
                for u in range(ROW_UNROLL):
                    fn(g * ROW_UNROLL + u)
                return carry
            lax.fori_loop(lo, hi, body, 0)

        whole = split >> ROW_UNROLL_SHIFT
        first = (split + ROW_UNROLL - 1) >> ROW_UNROLL_SHIFT
        groups(0, whole, prompt_copy)
        rows(whole * ROW_UNROLL, split, prompt_copy)
        rows(split, first * ROW_UNROLL, sample_copy)
        groups(first, tile // ROW_UNROLL, sample_copy)

    @pl.when(i == 0)
    def _():
        issue(tok_ref, split_ref[0], 0)

    @pl.when(i + 1 < n_active)
    def _():
        issue(tok_next_ref, split_ref[i + 1], (i + 1) & 1)

    @pl.when(i < n_active)
    def _():
        slot = i & 1
        pltpu.make_async_copy(h2p_ref.at[pl.ds(0, tile)], buf.at[slot], sem.at[slot]).wait()
        xs_ref[...] = buf[slot].astype(BF16)

    @pl.when(i >= n_active)
    def _():
        xs_ref[...] = jnp.zeros_like(xs_ref)


def _dispatch(h2_p, h2_s, table, split, n_active):
    d = h2_p.shape[1]
    tile = EXPERT_ROW_TILE
    n_rows = table.shape[0]
    nt = n_rows // tile
    return pl.pallas_call(
        functools.partial(_dispatch_kernel, n_prompt=h2_p.shape[0]),
        out_shape=jax.ShapeDtypeStruct((n_rows, d), BF16),
        grid_spec=pltpu.PrefetchScalarGridSpec(
            num_scalar_prefetch=2,
            grid=(nt,),
            in_specs=[pl.BlockSpec((None, 1, tile), lambda i, na, sp: (i, 0, 0), memory_space=pltpu.SMEM),
                      pl.BlockSpec((None, 1, tile), lambda i, na, sp: (jnp.minimum(i + 1, nt - 1), 0, 0),
                                   memory_space=pltpu.SMEM),
                      pl.BlockSpec(memory_space=pl.ANY), pl.BlockSpec(memory_space=pl.ANY)],
            out_specs=pl.BlockSpec((tile, d), lambda i, na, sp: (i, 0)),
            scratch_shapes=[pltpu.VMEM((2, tile, d), F32), pltpu.SemaphoreType.DMA((2,))]),
        compiler_params=_params(("arbitrary",)),
        name="dispatch",
    )(n_active, split, table.reshape(nt, 1, tile), table.reshape(nt, 1, tile), h2_p, h2_s)


def _expert_run_tiles(first_ref, tiles_ref, meta_ref, src_hbm, dst_hbm, col, in_buf, out_buf, in_sem, out_sem,
                      prepare, compute):
    r = pl.program_id(1)
    tile = EXPERT_ROW_TILE
    tile_priority = 1
    width = out_buf.shape[2]
    n_t = tiles_ref[r]
    t0 = first_ref[r]

    def rows(t):
        return pl.ds(pl.multiple_of(t * tile, tile), tile)

    def read(t, slot):
        return pltpu.make_async_copy(src_hbm.at[rows(t0 + t)], in_buf.at[slot], in_sem.at[slot])

    def write(t, slot):
        return pltpu.make_async_copy(out_buf.at[slot], dst_hbm.at[rows(t), pl.ds(col, width)],
                                     out_sem.at[slot])

    @pl.when(n_t > 0)
    def _():
        prepare()
        read(0, 0).start(priority=tile_priority)

        def body(t, carry):
            slot = t & 1
            read(t, slot).wait()

            @pl.when(t + 1 < n_t)
            def _():
                read(t + 1, 1 - slot).start(priority=tile_priority)

            block = compute(in_buf[slot])

            @pl.when(t >= 2)
            def _():
                write(t0 + t - 2, slot).wait()

            out_buf[slot] = block
            write(t0 + t, slot).start(priority=tile_priority)
            return carry

        lax.fori_loop(0, n_t, body, 0)

        @pl.when(n_t >= 2)
        def _():
            write(t0 + n_t - 2, n_t & 1).wait()

        write(t0 + n_t - 1, (n_t - 1) & 1).wait()

    @pl.when(r == pl.num_programs(1) - 1)
    def _():
        out_buf[0] = jnp.zeros(out_buf.shape[1:], out_buf.dtype)

        def start(t, carry):
            write(t, 0).start()
            return carry

        def wait(t, carry):
            write(t, 0).wait()
            return carry

        n_tiles = dst_hbm.shape[0] // tile
        lax.fori_loop(meta_ref[0], n_tiles, start, 0)
        lax.fori_loop(meta_ref[0], n_tiles, wait, 0)


def _gate_up_kernel(expert_ref, first_ref, tiles_ref, meta_ref, wg_ref, wu_ref, bg_ref, bu_ref, xs_hbm, act_hbm,
                    wg_b, wu_b, xbuf, obuf, xsem, osem, *, tn):
    def prepare():
        wg_b[...] = wg_ref[...].astype(BF16)
        wu_b[...] = wu_ref[...].astype(BF16)

    def compute(x):
        gate = jnp.dot(x, wg_b[...], preferred_element_type=F32) + bg_ref[...]
        up = jnp.dot(x, wu_b[...], preferred_element_type=F32) + bu_ref[...]
        gate = jnp.minimum(gate, SWIGLU_LIMIT)
        up = jnp.clip(up, -SWIGLU_LIMIT, SWIGLU_LIMIT)
        return ((up + 1.0) * gate * jax.nn.sigmoid(SWIGLU_ALPHA * gate)).astype(BF16)

    col = pl.multiple_of(pl.program_id(0) * tn, tn)
    _expert_run_tiles(first_ref, tiles_ref, meta_ref, xs_hbm, act_hbm, col, xbuf, obuf, xsem, osem,
                      prepare, compute)


def _expert_gate_up(xs, w_gu, b_gu, runs):
    n_rows, d = xs.shape
    n_exp, _, f2 = w_gu.shape
    f = f2 // 2
    tm = EXPERT_ROW_TILE
    tn = 1024
    nj = f // tn
    weight = lambda c: pl.BlockSpec((None, d, tn), lambda j, r, ex, fi, ti, meta: (ex[r], 0, c * nj + j))
    bias = lambda c: pl.BlockSpec((None, 1, tn), lambda j, r, ex, fi, ti, meta: (ex[r], 0, c * nj + j))
    return pl.pallas_call(
        functools.partial(_gate_up_kernel, tn=tn),
        out_shape=jax.ShapeDtypeStruct((n_rows, f), BF16),
        grid_spec=pltpu.PrefetchScalarGridSpec(
            num_scalar_prefetch=4,
            grid=(nj, n_exp),
            in_specs=[weight(0), weight(1), bias(0), bias(1), pl.BlockSpec(memory_space=pl.ANY)],
            out_specs=pl.BlockSpec(memory_space=pl.ANY),
            scratch_shapes=[pltpu.VMEM((d, tn), BF16), pltpu.VMEM((d, tn), BF16),
                            pltpu.VMEM((2, tm, d), BF16), pltpu.VMEM((2, tm, tn), BF16),
                            pltpu.SemaphoreType.DMA((2,)), pltpu.SemaphoreType.DMA((2,))]),
        compiler_params=_params(("arbitrary", "arbitrary")),
        name="expert_gate_up",
    )(*runs, w_gu, w_gu, b_gu.reshape(n_exp, 1, f2), b_gu.reshape(n_exp, 1, f2), xs)


def _down_kernel(expert_ref, first_ref, tiles_ref, meta_ref, w_ref, b_ref, act_hbm, y_hbm, w_b, abuf, ybuf,
                 asem, ysem):
    def prepare():
        w_b[...] = w_ref[...].astype(BF16)

    def compute(a):
        return jnp.dot(a, w_b[...], preferred_element_type=F32) + b_ref[...]

    _expert_run_tiles(first_ref, tiles_ref, meta_ref, act_hbm, y_hbm, 0, abuf, ybuf, asem, ysem,
                      prepare, compute)


def _expert_down(act, w_d, b_d, runs):
    n_rows, f = act.shape
    n_exp, _, d = w_d.shape
    tm = EXPERT_ROW_TILE
    return pl.pallas_call(
        _down_kernel,
        out_shape=jax.ShapeDtypeStruct((n_rows, d), F32),
        grid_spec=pltpu.PrefetchScalarGridSpec(
            num_scalar_prefetch=4,
            grid=(1, n_exp),
            in_specs=[pl.BlockSpec((None, f, d), lambda j, r, ex, fi, ti, meta: (ex[r], 0, 0)),
                      pl.BlockSpec((None, 1, d), lambda j, r, ex, fi, ti, meta: (ex[r], 0, 0)),
                      pl.BlockSpec(memory_space=pl.ANY)],
            out_specs=pl.BlockSpec(memory_space=pl.ANY),
            scratch_shapes=[pltpu.VMEM((f, d), BF16), pltpu.VMEM((2, tm, f), BF16),
                            pltpu.VMEM((2, tm, d), F32), pltpu.SemaphoreType.DMA((2,)),
                            pltpu.SemaphoreType.DMA((2,))]),
        compiler_params=_params(("arbitrary", "arbitrary")),
        name="expert_down",
    )(*runs, w_d, b_d.reshape(n_exp, 1, d), act)


COMBINE_TOKENS = 128


def _combine_kernel(slot_ref, slot_next_ref, p_ref, x1_ref, gate_ref, g_ref, y_ref, o_ref, buf, sem):
    i = pl.program_id(0)
    tm = COMBINE_TOKENS
    pairs = tm * TOP_K

    def issue(slots, b):
        def body(g, carry):
            for u in range(ROW_UNROLL):
                p = g * ROW_UNROLL + u
                tok = g * (ROW_UNROLL // TOP_K) + u // TOP_K
                pltpu.make_async_copy(y_ref.at[pl.ds(slots[0, p], 1)], buf.at[b, u % TOP_K, pl.ds(tok, 1)],
                                      sem.at[b]).start()
            return carry
        lax.fori_loop(0, pairs // ROW_UNROLL, body, 0)

    @pl.when(i == 0)
    def _():
        issue(slot_ref, 0)

    @pl.when(i + 1 < pl.num_programs(0))
    def _():
        issue(slot_next_ref, (i + 1) & 1)

    b = i & 1
    for k in range(TOP_K):
        pltpu.make_async_copy(y_ref.at[pl.ds(0, tm)], buf.at[b, k], sem.at[b]).wait()
    probs = p_ref[...]
    f = probs[:, 0:1] * buf[b, 0]
    for k in range(1, TOP_K):
        f = f + probs[:, k:k + 1] * buf[b, k]
    o_ref[...] = x1_ref[...] + gate_ref[...] * (_rms(f) * g_ref[...])


def _combine(y, slots, probs, x1, gate, g_post, rows_per_group):
    m, d = x1.shape
    tm = COMBINE_TOKENS
    nt = m // tm
    pairs = tm * TOP_K
    return pl.pallas_call(
        _combine_kernel,
        out_shape=jax.ShapeDtypeStruct((m, d), F32),
        grid=(nt,),
        in_specs=[pl.BlockSpec((None, 1, pairs), lambda i: (i, 0, 0), memory_space=pltpu.SMEM),
                  pl.BlockSpec((None, 1, pairs), lambda i: (jnp.minimum(i + 1, nt - 1), 0, 0),
                               memory_space=pltpu.SMEM),
                  pl.BlockSpec((tm, LANES), lambda i: (i, 0)),
                  pl.BlockSpec((tm, d), lambda i: (i, 0)),
                  _mod_spec(gate, tm, rows_per_group),
                  pl.BlockSpec((1, d), lambda i: (0, 0)),
                  pl.BlockSpec(memory_space=pl.ANY)],
        out_specs=pl.BlockSpec((tm, d), lambda i: (i, 0)),
        scratch_shapes=[pltpu.VMEM((2, TOP_K, tm, d), F32), pltpu.SemaphoreType.DMA((2,))],
        compiler_params=_params(("arbitrary",)),
        name="combine",
    )(slots.reshape(nt, 1, pairs), slots.reshape(nt, 1, pairs), probs, x1, gate, g_post.reshape(1, d), y)


def _moe(h2_p, h2_s, logits_p, logits_s, n_exp, w_gu, b_gu, w_d, b_d):
    n = h2_p.shape[0] + h2_s.shape[0]
    tile = EXPERT_ROW_TILE
    top_i, probs, rank, counts = _route(logits_p, logits_s)
    counts = counts[0, :n_exp].astype(I32)
    cap = (counts + tile - 1) // tile * tile
    ends = jnp.cumsum(cap)
    offs = ends - cap
    n_tiles = (n * TOP_K) // tile + n_exp
    n_active = (ends[-1] // tile).astype(I32).reshape(1)
    expert_ids = jnp.arange(n_exp, dtype=I32)
    has_rows = cap > 0
    n_runs = jnp.sum(has_rows.astype(I32))
    run_of = jnp.cumsum(has_rows.astype(I32)) - 1
    hit = has_rows[None, :] & (run_of[None, :] == jnp.minimum(expert_ids, n_runs - 1)[:, None])
    pick = lambda v: jnp.sum(jnp.where(hit, v[None, :], 0), axis=1).astype(I32)
    runs = (pick(expert_ids), pick(offs // tile), jnp.where(expert_ids < n_runs, pick(cap // tile), 0),
            n_active)
    pair_offs = jnp.sum(jnp.where(top_i[:, :TOP_K, None] == expert_ids, offs, 0), axis=-1)
    slots = pair_offs + rank[:, :TOP_K]
    n_prompt = h2_p.shape[0]
    meta = jnp.concatenate([offs, cap, counts, n_active]).astype(I32)
    table = _invert(slots, meta, n_tiles * tile, n_prompt, n_exp)
    split = jnp.sum(table.reshape(n_tiles, tile) < n_prompt, axis=1).astype(I32)
    xs = _dispatch(h2_p, h2_s, table, split, n_active)
    act = _expert_gate_up(xs, w_gu, b_gu, runs)
    y = _expert_down(act, w_d, b_d, runs)
    return y, slots, probs


def _layer(xp, xs_, cp_mod, cs_mod, hist_a, hist_q, s0, g_pre_mix, g_post_mix, g_pre_ffn, g_post_ffn,
           w_in, conv_a_w, gdn_conv_w, a_log, dt_bias, g_conv_out, gdn_norm_g, w_out, router_w, router_b,
           w_gu, b_gu, w_d, b_d):
    bsz, seq, d = xp.shape
    ns = xs_.shape[0]
    n_exp = router_w.shape[1]
    dc = N_HEADS * HEAD_D
    d_main = 3 * dc + 3 * dc + dc
    xp2 = xp.reshape(bsz * seq, d)
    xs2 = xs_.reshape(ns, d)

    w_main = w_in.astype(BF16)
    w_ba = jnp.zeros((d, LANES), BF16).at[:, :2 * N_HEADS].set(w_main[:, d_main:])
    w_out_b = w_out.astype(BF16)
    rw = jnp.zeros((d, LANES), F32).at[:, :n_exp].set(router_w)
    rw_hi = rw.astype(BF16)
    rw_lo = (rw - rw_hi.astype(F32)).astype(BF16)
    rb = jnp.full((1, LANES), NEG_BIG, F32).at[0, :n_exp].set(router_b)

    mp = [cp_mod[:, i * d:(i + 1) * d].reshape(bsz, 1, d) for i in range(6)]
    ms = [cs_mod[:, i * d:(i + 1) * d] for i in range(6)]

    proj_p, ba_p = _in_proj(xp2, mp[1], mp[0], g_pre_mix, w_main, w_ba, seq, d_main)
    proj_s, ba_s = _in_proj(xs2, ms[1], ms[0], g_pre_mix, w_main, w_ba, 1, d_main)

    mix_p, ha_p, hq_p, s_p = _mixer_prompt(proj_p, ba_p, bsz, seq, conv_a_w, gdn_conv_w, a_log, dt_bias,
                                           g_conv_out, gdn_norm_g)
    mix_s, ha_s, hq_s, s_s = _mixer_sample(proj_s, ba_s, hist_a.reshape(ns, 2 * dc),
                                           hist_q.reshape(ns, 9 * dc), s0, conv_a_w, gdn_conv_w, a_log,
                                           dt_bias, g_conv_out, gdn_norm_g)

    x1_p, h2_p, lg_p = _post_mix(mix_p, xp2, mp[2], mp[4], mp[3], g_post_mix, g_pre_ffn, w_out_b, rw_hi,
                                 rw_lo, rb, seq)
    x1_s, h2_s, lg_s = _post_mix(mix_s, xs2, ms[2], ms[4], ms[3], g_post_mix, g_pre_ffn, w_out_b, rw_hi,
                                 rw_lo, rb, 1)

    y, slots, probs = _moe(h2_p, h2_s, lg_p, lg_s, n_exp, w_gu, b_gu, w_d, b_d)
    np_ = bsz * seq
    out_p = _combine(y, slots[:np_], probs[:np_], x1_p, mp[5], g_post_ffn, seq)
    out_s = _combine(y, slots[np_:], probs[np_:], x1_s, ms[5], g_post_ffn, 1)
    return (out_p.reshape(bsz, seq, d), out_s.reshape(ns, 1, d), ha_p, hq_p, s_p,
            ha_s.reshape(ns, 2, dc), hq_s.reshape(ns, 3, 3 * dc), s_s)


def kernel(x_prompt, x_sample, state_conv_a, state_gdn_conv, state_gdn_S, c_prompt, c_sample, w_mod, b_mod, g_pre_mix, g_post_mix, g_pre_ffn, g_post_ffn, w_in, conv_a_w, gdn_conv_w, gdn_a_log, gdn_dt_bias, g_conv_out, gdn_norm_g, w_out, router_w, router_b, exp_w_gate_up, exp_b_gate_up, exp_w_down, exp_b_down):
    depth = w_mod.shape[0]
    bp = x_prompt.shape[0]
    xp, xs_ = x_prompt, x_sample
    outs = [[] for _ in range(6)]
    for l in range(depth):
        mod = _modulation(jnp.concatenate([c_prompt, c_sample], axis=0), w_mod[l], b_mod[l])
        res = _layer(xp, xs_, mod[:bp], mod[bp:], state_conv_a[l], state_gdn_conv[l], state_gdn_S[l],
                     g_pre_mix[l], g_post_mix[l], g_pre_ffn[l], g_post_ffn[l], w_in[l], conv_a_w[l],
                     gdn_conv_w[l], gdn_a_log[l], gdn_dt_bias[l], g_conv_out[l], gdn_norm_g[l], w_out[l],
                     router_w[l], router_b[l], exp_w_gate_up[l], exp_b_gate_up[l], exp_w_down[l],
                     exp_b_down[l])
        xp, xs_ = res[0], res[1]
        for acc, r in zip(outs, res[2:]):
            acc.append(r)
    return (xp, xs_) + tuple(o[0][None] if depth == 1 else jnp.stack(o) for o in outs)
```

```python
import functools

import jax
import jax.numpy as jnp
from jax import lax
from jax.experimental import pallas as pl
from jax.experimental.pallas import tpu as pltpu

F32 = jnp.float32
BF16 = jnp.bfloat16
I32 = jnp.int32
HIGHEST = lax.Precision.HIGHEST

EPS = 1e-6
N_HEADS = 8
HEAD_D = 128
TOP_K = 4
TOP_K_SHIFT = 2
SWIGLU_LIMIT = 7.0
SWIGLU_ALPHA = 1.702
CHUNK = 64
LANES = 128
EXPERT_ROW_TILE = 256
NEG_BIG = -1e30
VMEM_LIMIT = 56 * 1024 * 1024


def _params(semantics, vmem=VMEM_LIMIT):
    return pltpu.CompilerParams(dimension_semantics=semantics, vmem_limit_bytes=vmem)


def _mm(a, b):
    return jnp.dot(a.astype(BF16), b.astype(BF16), preferred_element_type=F32)


def _mm_nt(a, b):
    return lax.dot_general(a.astype(BF16), b.astype(BF16), (((1,), (1,)), ((), ())),
                           preferred_element_type=F32)


def _rms(x):
    return x * lax.rsqrt(jnp.mean(x * x, axis=-1, keepdims=True) + EPS)


def _silu(x):
    return x * jax.nn.sigmoid(x)


def _softplus(x):
    return jnp.maximum(x, 0.0) + jnp.log1p(jnp.exp(-jnp.abs(x)))


def _mod_kernel(c_ref, w_ref, b_ref, o_ref):
    s = _silu(c_ref[...])
    o_ref[...] = _mm(s, w_ref[...]) + b_ref[...]


def _modulation(c_all, w_mod, b_mod):
    n, d = c_all.shape
    m = w_mod.shape[1]
    tn = 1024
    return pl.pallas_call(
        _mod_kernel,
        out_shape=jax.ShapeDtypeStruct((n, m), F32),
        grid=(m // tn,),
        in_specs=[pl.BlockSpec((n, d), lambda j: (0, 0)),
                  pl.BlockSpec((d, tn), lambda j: (0, j)),
                  pl.BlockSpec((1, tn), lambda j: (0, j))],
        out_specs=pl.BlockSpec((n, tn), lambda j: (0, j)),
        compiler_params=_params(("arbitrary",)),
        name="modulation",
    )(c_all, w_mod, b_mod.reshape(1, m))


def _mod_spec(arr, tm, rows_per_group):
    if arr.ndim == 3:
        tiles = rows_per_group // tm
        return pl.BlockSpec((None, 1, arr.shape[-1]), lambda i, *_: (i // tiles, 0, 0))
    return pl.BlockSpec((tm, arr.shape[-1]), lambda i, *_: (i, 0))


def _proj_kernel(x_ref, sc_ref, sh_ref, g_ref, w_ref, wba_ref, o_ref, ba_ref, h_scr):
    @pl.when(pl.program_id(1) == 0)
    def _():
        h = (_rms(x_ref[...]) * g_ref[...]) * (1.0 + sc_ref[...]) + sh_ref[...]
        hb = h.astype(BF16)
        h_scr[...] = hb
        ba_ref[...] = jnp.dot(hb, wba_ref[...], preferred_element_type=F32)

    o_ref[...] = jnp.dot(h_scr[...], w_ref[...], preferred_element_type=F32)


def _in_proj(x, scale, shift, g, w_main, w_ba, rows_per_group, n):
    m, d = x.shape
    tm = min(m, 1024, rows_per_group if scale.ndim == 3 else m)
    tn = 1024
    return pl.pallas_call(
        _proj_kernel,
        out_shape=(jax.ShapeDtypeStruct((m, n), F32), jax.ShapeDtypeStruct((m, LANES), F32)),
        grid=(m // tm, n // tn),
        in_specs=[pl.BlockSpec((tm, d), lambda i, j: (i, 0)),
                  _mod_spec(scale, tm, rows_per_group),
                  _mod_spec(shift, tm, rows_per_group),
                  pl.BlockSpec((1, d), lambda i, j: (0, 0)),
                  pl.BlockSpec((d, tn), lambda i, j: (0, j)),
                  pl.BlockSpec((d, LANES), lambda i, j: (0, 0))],
        out_specs=(pl.BlockSpec((tm, tn), lambda i, j: (i, j)),
                   pl.BlockSpec((tm, LANES), lambda i, j: (i, 0))),
        scratch_shapes=[pltpu.VMEM((tm, d), BF16)],
        compiler_params=_params(("arbitrary", "arbitrary")),
        name="in_proj",
    )(x, scale, shift, g.reshape(1, d), w_main, w_ba)


PROMPT_SEQS_PER_STEP = 2

def _mixer_prompt_kernel(proj_ref, ba_ref, caw_ref, gcw_ref, alog_ref, dtb_ref, gco_ref, gng_ref,
                         mix_ref, ha_ref, hq_ref, s_ref, extu, extq, qc_scr, s_scr, *, nseq):
    c = CHUNK
    dc = N_HEADS * HEAD_D
    dq = 3 * dc
    t = pl.program_id(1)
    is_last = t == pl.num_programs(1) - 1

    @pl.when(t == 0)
    def _():
        extu[:, 0:8, :] = jnp.zeros((nseq, 8, dc), F32)
        extq[:, 0:8, :] = jnp.zeros((nseq, 8, dq), F32)
        s_scr[...] = jnp.zeros_like(s_scr)

    row = lax.broadcasted_iota(I32, (c, c), 0)
    col = lax.broadcasted_iota(I32, (c, c), 1)
    causal = row >= col
    strict = row > col
    lower = jnp.where(causal, 1.0, 0.0).astype(F32)
    upper = jnp.where(row <= col, 1.0, 0.0).astype(F32)
    caw = caw_ref[...]
    gcw = gcw_ref[...]

    chains = [(sq, h) for sq in range(nseq) for h in range(N_HEADS)]
    heads = range(len(chains))
    qn, kn, vb, kb, kbg, qg, kg, decay, s_decay = ([] for _ in range(9))
    for sq in range(nseq):
        u = proj_ref[sq, :, dc:2 * dc] * proj_ref[sq, :, 2 * dc:3 * dc]
        extu[sq, 8:8 + c, :] = u
        ya = caw[0:1] * extu[sq, 6:6 + c, :] + caw[1:2] * extu[sq, 7:7 + c, :] + caw[2:3] * u
        ya = proj_ref[sq, :, 0:dc] * ya
        mix_ref[sq, :, 0:dc] = (_rms(ya) * gco_ref[...]).astype(BF16)
        last_u = extu[sq, 6 + c:8 + c, :]
        extu[sq, 6:8, :] = last_u

        qkv = proj_ref[sq, :, 3 * dc:3 * dc + dq]
        extq[sq, 8:8 + c, :] = qkv
        qc = (gcw[0:1] * extq[sq, 5:5 + c, :] + gcw[1:2] * extq[sq, 6:6 + c, :]
              + gcw[2:3] * extq[sq, 7:7 + c, :] + gcw[3:4] * qkv)
        qc_scr[sq] = _silu(qc)
        last_q = extq[sq, 5 + c:8 + c, :]
        extq[sq, 5:8, :] = last_q

        @pl.when(is_last)
        def _(sq=sq, last_u=last_u, last_q=last_q):
            ha_ref[sq] = last_u
            hq_ref[sq] = last_q

        ba = ba_ref[sq]
        beta_all = jax.nn.sigmoid(ba)
        g_all = -jnp.exp(alog_ref[...]) * _softplus(ba + dtb_ref[...])
        gc_all = jnp.dot(lower, g_all, precision=HIGHEST, preferred_element_type=F32)
        gc_t = lax.dot_general(g_all, upper, (((0,), (0,)), ((), ())), precision=HIGHEST,
                               preferred_element_type=F32)
        for h in range(N_HEADS):
            lo = h * HEAD_D
            q = qc_scr[sq, :, lo:lo + HEAD_D]
            k = qc_scr[sq, :, dc + lo:dc + lo + HEAD_D]
            v = qc_scr[sq, :, 2 * dc + lo:2 * dc + lo + HEAD_D]
            qn_h = q * lax.rsqrt(jnp.sum(q * q, axis=-1, keepdims=True) + EPS) * (HEAD_D ** -0.5)
            kn_h = k * lax.rsqrt(jnp.sum(k * k, axis=-1, keepdims=True) + EPS)
            beta = beta_all[:, h:h + 1]
            gcc = gc_all[:, N_HEADS + h:N_HEADS + h + 1]
            gcr = gc_t[N_HEADS + h:N_HEADS + h + 1, :]
            gl = gc_all[c - 1:c, N_HEADS + h:N_HEADS + h + 1]
            eg = jnp.exp(gcc)
            kb_h = kn_h * beta
            qn.append(qn_h)
            kn.append(kn_h)
            vb.append(v * beta)
            kb.append(kb_h)
            kbg.append(kb_h * eg)
            qg.append(qn_h * eg)
            kg.append(kn_h * jnp.exp(gl - gcc))
            decay.append(jnp.where(causal, jnp.exp(jnp.minimum(gcc - gcr, 0.0)), 0.0))
            s_decay.append(jnp.exp(gl))

    kq = [_mm_nt(jnp.concatenate([kb[h], qn[h]], axis=0), kn[h]) for h in heads]
    a_mat = [jnp.where(strict, kq[h][:c] * decay[h], 0.0) for h in heads]
    qk = [kq[h][c:] * decay[h] for h in heads]
    n_mat = [-a_mat[h] for h in heads]
    p = a_mat
    size = 2
    while size < c:
        p = [_mm(p[h], p[h]) for h in heads]
        n_p = [_mm(n_mat[h], p[h]) for h in heads]
        n_mat = [n_mat[h] + p[h] + n_p[h] for h in heads]
        size *= 2
    rhs = [jnp.concatenate([vb[h], kbg[h]], axis=-1) for h in heads]
    uw = [rhs[h] + _mm(n_mat[h], rhs[h]) for h in heads]
    s_old = [s_scr[sq, h] for sq, h in chains]
    ws = [_mm(jnp.concatenate([uw[h][:, HEAD_D:], qg[h]], axis=0), s_old[h]) for h in heads]
    v_new = [uw[h][:, :HEAD_D] - ws[h][:c] for h in heads]
    fin = [_mm(jnp.concatenate([qk[h], kg[h].T], axis=0), v_new[h]) for h in heads]
    for i, (sq, h) in enumerate(chains):
        lo = h * HEAD_D
        s_scr[sq, h] = s_old[i] * s_decay[i] + fin[i][c:]
        o = ws[i][c:] + fin[i][:c]
        z = proj_ref[sq, :, 3 * dc + dq + lo:3 * dc + dq + lo + HEAD_D]
        yb = _rms(o) * gng_ref[...] * _silu(z)
        mix_ref[sq, :, dc + lo:dc + lo + HEAD_D] = yb.astype(BF16)

    @pl.when(is_last)
    def _():
        s_ref[...] = s_scr[...]


def _lane_row(vec, offset):
    return jnp.zeros((1, LANES), F32).at[0, offset:offset + vec.shape[0]].set(vec.astype(F32))


def _mixer_prompt(proj, ba, bsz, seq, conv_a_w, gdn_conv_w, a_log, dt_bias, g_conv_out, gdn_norm_g):
    c = CHUNK
    dc = N_HEADS * HEAD_D
    dq = 3 * dc
    dproj = proj.shape[1]
    nt = seq // c
    const = lambda shape: pl.BlockSpec(shape, lambda b, t: (0,) * len(shape))
    nseq = PROMPT_SEQS_PER_STEP if bsz % PROMPT_SEQS_PER_STEP == 0 else 1
    seq_block = lambda *tail: pl.BlockSpec((nseq,) + tail, lambda b, t: (b, t) + (0,) * (len(tail) - 1))
    whole = lambda *tail: pl.BlockSpec((nseq,) + tail, lambda b, t: (b,) + (0,) * len(tail))
    mix, ha, hq, s_fin = pl.pallas_call(
        functools.partial(_mixer_prompt_kernel, nseq=nseq),
        out_shape=(jax.ShapeDtypeStruct((bsz, seq, 2 * dc), BF16),
                   jax.ShapeDtypeStruct((bsz, 2, dc), F32),
                   jax.ShapeDtypeStruct((bsz, 3, dq), F32),
                   jax.ShapeDtypeStruct((bsz, N_HEADS, HEAD_D, HEAD_D), F32)),
        grid=(bsz // nseq, nt),
        in_specs=[seq_block(c, dproj), seq_block(c, LANES),
                  const((3, dc)), const((4, dq)), const((1, LANES)), const((1, LANES)),
                  const((1, dc)), const((1, HEAD_D))],
        out_specs=(seq_block(c, 2 * dc), whole(2, dc), whole(3, dq), whole(N_HEADS, HEAD_D, HEAD_D)),
        scratch_shapes=[pltpu.VMEM((nseq, 8 + c, dc), F32), pltpu.VMEM((nseq, 8 + c, dq), F32),
                        pltpu.VMEM((nseq, c, dq), F32), pltpu.VMEM((nseq, N_HEADS, HEAD_D, HEAD_D), F32)],
        compiler_params=_params(("arbitrary", "arbitrary")),
        name="mixer_prompt",
    )(proj.reshape(bsz, seq, dproj), ba.reshape(bsz, seq, LANES), conv_a_w, gdn_conv_w,
      _lane_row(a_log, N_HEADS), _lane_row(dt_bias, N_HEADS), g_conv_out.reshape(1, dc),
      gdn_norm_g.reshape(1, HEAD_D))
    return mix.reshape(bsz * seq, 2 * dc), ha, hq, s_fin


SAMPLE_GROUP = 16


def _mixer_sample_kernel(proj_ref, ba_ref, hista_ref, histq_ref, s_in_ref, caw_ref, gcw_ref, alog_ref,
                         dtb_ref, gco_ref, gng_ref, mix_ref, ha_ref, hq_ref, s_out_ref, qc_scr, o_scr):
    tb = SAMPLE_GROUP
    dc = N_HEADS * HEAD_D
    dq = 3 * dc

    u = proj_ref[:, dc:2 * dc] * proj_ref[:, 2 * dc:3 * dc]
    caw = caw_ref[...]
    ya = caw[0:1] * hista_ref[:, 0:dc] + caw[1:2] * hista_ref[:, dc:2 * dc] + caw[2:3] * u
    ya = proj_ref[:, 0:dc] * ya
    mix_ref[:, 0:dc] = _rms(ya) * gco_ref[...]
    ha_ref[:, 0:dc] = hista_ref[:, dc:2 * dc]
    ha_ref[:, dc:2 * dc] = u

    qkv = proj_ref[:, 3 * dc:3 * dc + dq]
    gcw = gcw_ref[...]
    qc = (gcw[0:1] * histq_ref[:, 0:dq] + gcw[1:2] * histq_ref[:, dq:2 * dq]
          + gcw[2:3] * histq_ref[:, 2 * dq:3 * dq] + gcw[3:4] * qkv)
    qc_scr[...] = _silu(qc)
    hq_ref[:, 0:dq] = histq_ref[:, dq:2 * dq]
    hq_ref[:, dq:2 * dq] = histq_ref[:, 2 * dq:3 * dq]
    hq_ref[:, 2 * dq:3 * dq] = qkv

    ba = ba_ref[...]
    beta_all = jax.nn.sigmoid(ba)
    eg_all = jnp.exp(-jnp.exp(alog_ref[...]) * _softplus(ba + dtb_ref[...]))

    for h in range(N_HEADS):
        lo = h * HEAD_D
        q = qc_scr[:, lo:lo + HEAD_D]
        k = qc_scr[:, dc + lo:dc + lo + HEAD_D]
        v = qc_scr[:, 2 * dc + lo:2 * dc + lo + HEAD_D]
        qn = q * lax.rsqrt(jnp.sum(q * q, axis=-1, keepdims=True) + EPS) * (HEAD_D ** -0.5)
        kn = k * lax.rsqrt(jnp.sum(k * k, axis=-1, keepdims=True) + EPS)
        qk = jnp.sum(qn * kn, axis=-1, keepdims=True)
        kn_t = kn.T
        qn_t = qn.T
        for b in range(tb):
            s_old = s_in_ref[b, h]
            kc = kn_t[:, b:b + 1]
            e = eg_all[b:b + 1, N_HEADS + h:N_HEADS + h + 1]
            ks = jnp.sum(s_old * kc, axis=0, keepdims=True)
            qs = jnp.sum(s_old * qn_t[:, b:b + 1], axis=0, keepdims=True)
            v_new = beta_all[b:b + 1, h:h + 1] * (v[b:b + 1, :] - e * ks)
            o_scr[b:b + 1, lo:lo + HEAD_D] = e * qs + qk[b:b + 1, :] * v_new
            s_out_ref[b, h] = s_old * e + kc * v_new
        z = proj_ref[:, 3 * dc + dq + lo:3 * dc + dq + lo + HEAD_D]
        o = o_scr[:, lo:lo + HEAD_D]
        mix_ref[:, dc + lo:dc + lo + HEAD_D] = _rms(o) * gng_ref[...] * _silu(z)


def _mixer_sample(proj, ba, hist_a, hist_q, s_in, conv_a_w, gdn_conv_w, a_log, dt_bias, g_conv_out,
                  gdn_norm_g):
    n = proj.shape[0]
    tb = SAMPLE_GROUP
    dc = N_HEADS * HEAD_D
    dq = 3 * dc
    dproj = proj.shape[1]
    const = lambda shape: pl.BlockSpec(shape, lambda i: (0,) * len(shape))
    rows = lambda width: pl.BlockSpec((tb, width), lambda i: (i, 0))
    state = pl.BlockSpec((tb, N_HEADS, HEAD_D, HEAD_D), lambda i: (i, 0, 0, 0))
    return pl.pallas_call(
        _mixer_sample_kernel,
        out_shape=(jax.ShapeDtypeStruct((n, 2 * dc), F32),
                   jax.ShapeDtypeStruct((n, 2 * dc), F32),
                   jax.ShapeDtypeStruct((n, 3 * dq), F32),
                   jax.ShapeDtypeStruct((n, N_HEADS, HEAD_D, HEAD_D), F32)),
        grid=(n // tb,),
        in_specs=[rows(dproj), rows(LANES), rows(2 * dc), rows(3 * dq), state,
                  const((3, dc)), const((4, dq)), const((1, LANES)), const((1, LANES)),
                  const((1, dc)), const((1, HEAD_D))],
        out_specs=(rows(2 * dc), rows(2 * dc), rows(3 * dq), state),
        scratch_shapes=[pltpu.VMEM((tb, dq), F32), pltpu.VMEM((tb, dc), F32)],
        compiler_params=_params(("arbitrary",)),
        name="mixer_sample",
    )(proj, ba, hist_a, hist_q, s_in, conv_a_w, gdn_conv_w, _lane_row(a_log, N_HEADS),
      _lane_row(dt_bias, N_HEADS), g_conv_out.reshape(1, dc), gdn_norm_g.reshape(1, HEAD_D))


POST_MIX_SUB_ROWS = 256

def _post_mix_kernel(mix_ref, x_ref, gate_ref, sc_ref, sh_ref, gpost_ref, gpre_ref, wout_ref,
                     rwh_ref, rwl_ref, rb_ref, x1_ref, h2_ref, lg_ref):
    tm = x_ref.shape[0]
    sub = min(tm, POST_MIX_SUB_ROWS)
    tiles = [slice(r0, r0 + sub) for r0 in range(0, tm, sub)]
    mixes = [jnp.dot(mix_ref[rows].astype(BF16), wout_ref[...], preferred_element_type=F32)
             for rows in tiles]
    for rows, mix in zip(tiles, mixes):
        per_row = lambda ref: ref[rows] if ref.shape[0] == tm else ref[...]
        x1 = x_ref[rows] + per_row(gate_ref) * (_rms(mix) * gpost_ref[...])
        x1_ref[rows] = x1
        h2 = (_rms(x1) * gpre_ref[...]) * (1.0 + per_row(sc_ref)) + per_row(sh_ref)
        h2_ref[rows] = h2
        hi = h2.astype(BF16)
        lo = (h2 - hi.astype(F32)).astype(BF16)
        rwh = rwh_ref[...]
        lg_ref[rows] = (jnp.dot(hi, rwh, preferred_element_type=F32)
                        + jnp.dot(lo, rwh, preferred_element_type=F32)
                        + jnp.dot(hi, rwl_ref[...], preferred_element_type=F32) + rb_ref[...])


def _post_mix(mix_in, x, gate, scale, shift, g_post, g_pre, w_out, rw_hi, rw_lo, rb, rows_per_group):
    m, d = x.shape
    tm = min(m, 512, rows_per_group if gate.ndim == 3 else m)
    const = lambda shape: pl.BlockSpec(shape, lambda i: (0,) * len(shape))
    rows = lambda width: pl.BlockSpec((tm, width), lambda i: (i, 0))
    return pl.pallas_call(
        _post_mix_kernel,
        out_shape=(jax.ShapeDtypeStruct((m, d), F32), jax.ShapeDtypeStruct((m, d), F32),
                   jax.ShapeDtypeStruct((m, LANES), F32)),
        grid=(m // tm,),
        in_specs=[rows(d), rows(d),
                  _mod_spec(gate, tm, rows_per_group),
                  _mod_spec(scale, tm, rows_per_group),
                  _mod_spec(shift, tm, rows_per_group),
                  const((1, d)), const((1, d)), const((d, d)),
                  const((d, LANES)), const((d, LANES)), const((1, LANES))],
        out_specs=(rows(d), rows(d), rows(LANES)),
        compiler_params=_params(("arbitrary",)),
        name="post_mix",
    )(mix_in, x, gate, scale, shift, g_post.reshape(1, d), g_pre.reshape(1, d), w_out, rw_hi, rw_lo, rb)


ROUTE_TOKENS = 128


def _route_kernel(lgp_ref, lgs_ref, idx_ref, p_ref, rank_ref, cnt_ref, carry, *, prompt_tiles):
    tm = lgp_ref.shape[0]

    @pl.when(pl.program_id(0) == 0)
    def _():
        carry[...] = jnp.zeros_like(carry)

    l = jnp.where(pl.program_id(0) < prompt_tiles, lgp_ref[...], lgs_ref[...])
    lane = lax.broadcasted_iota(I32, l.shape, 1)
    lane_f = lane.astype(F32)
    vals, hots = [], []
    idx_out = jnp.zeros(l.shape, F32)
    for k in range(TOP_K):
        m = jnp.max(l, axis=-1, keepdims=True)
        idx = jnp.min(jnp.where(l == m, lane_f, float(LANES)), axis=-1, keepdims=True)
        hot = lane_f == idx
        vals.append(m)
        hots.append(hot)
        idx_out = jnp.where(lane == k, idx, idx_out)
        l = jnp.where(hot, -jnp.inf, l)
    exps = [jnp.exp(v - vals[0]) for v in vals]
    denom = exps[0] + exps[1] + exps[2] + exps[3]
    p_out = jnp.zeros(l.shape, F32)
    for k in range(TOP_K):
        p_out = jnp.where(lane == k, exps[k] / denom, p_out)
    member = jnp.where(hots[0] | hots[1] | hots[2] | hots[3], 1.0, 0.0).astype(F32)
    row = lax.broadcasted_iota(I32, (tm, tm), 0)
    col = lax.broadcasted_iota(I32, (tm, tm), 1)
    before = jnp.where(row > col, 1.0, 0.0).astype(BF16)
    prefix = jnp.dot(before, member.astype(BF16), preferred_element_type=F32) + carry[...]
    rank_out = jnp.zeros(l.shape, F32)
    for k in range(TOP_K):
        r = jnp.sum(jnp.where(hots[k], prefix, 0.0), axis=-1, keepdims=True)
        rank_out = jnp.where(lane == k, r, rank_out)
    carry[...] = carry[...] + jnp.sum(member, axis=0, keepdims=True)
    idx_ref[...] = idx_out.astype(I32)
    p_ref[...] = p_out
    rank_ref[...] = rank_out.astype(I32)
    cnt_ref[...] = carry[...]


def _route(logits_p, logits_s):
    tm = ROUTE_TOKENS
    pt = logits_p.shape[0] // tm
    n = logits_p.shape[0] + logits_s.shape[0]
    tile = pl.BlockSpec((tm, LANES), lambda i: (i, 0))
    return pl.pallas_call(
        functools.partial(_route_kernel, prompt_tiles=pt),
        out_shape=(jax.ShapeDtypeStruct((n, LANES), I32), jax.ShapeDtypeStruct((n, LANES), F32),
                   jax.ShapeDtypeStruct((n, LANES), I32), jax.ShapeDtypeStruct((1, LANES), F32)),
        grid=(n // tm,),
        in_specs=[pl.BlockSpec((tm, LANES), lambda i: (jnp.minimum(i, pt - 1), 0)),
                  pl.BlockSpec((tm, LANES), lambda i: (jnp.maximum(i - pt, 0), 0))],
        out_specs=(tile, tile, tile, pl.BlockSpec((1, LANES), lambda i: (0, 0))),
        scratch_shapes=[pltpu.VMEM((1, LANES), F32)],
        compiler_params=_params(("arbitrary",)),
        name="route",
    )(logits_p, logits_s)


INVERT_TOKENS_MAX = 1024


def _largest_tile(n, unit, cap):
    return max(m for m in range(unit, cap + 1, unit) if n % m == 0)


def _invert_kernel(meta_ref, slot_ref, table_ref, *, pad_token, n_exp, tokens):
    i = pl.program_id(0)
    pairs = tokens * TOP_K

    @pl.when(i == 0)
    def _():
        def fill(r, carry):
            table_ref[r] = pad_token
            return carry

        def fill_group_tail(e, carry):
            lax.fori_loop(meta_ref[e] + meta_ref[2 * n_exp + e], meta_ref[e] + meta_ref[n_exp + e], fill, 0)
            return carry

        lax.fori_loop(0, n_exp, fill_group_tail, 0)
        lax.fori_loop(meta_ref[3 * n_exp] * EXPERT_ROW_TILE, table_ref.shape[0], fill, 0)

    def put(p, carry):
        table_ref[slot_ref[0, p]] = i * tokens + (p >> TOP_K_SHIFT)
        return carry

    lax.fori_loop(0, pairs, put, 0, unroll=16)


def _invert(slots, meta, n_rows, pad_token, n_exp):
    n = slots.shape[0]
    tokens = _largest_tile(n, LANES, INVERT_TOKENS_MAX)
    nt = n // tokens
    pairs = tokens * TOP_K
    return pl.pallas_call(
        functools.partial(_invert_kernel, pad_token=pad_token, n_exp=n_exp, tokens=tokens),
        out_shape=jax.ShapeDtypeStruct((n_rows,), I32),
        grid_spec=pltpu.PrefetchScalarGridSpec(
            num_scalar_prefetch=1,
            grid=(nt,),
            in_specs=[pl.BlockSpec((None, 1, pairs), lambda i, meta: (i, 0, 0), memory_space=pltpu.SMEM)],
            out_specs=pl.BlockSpec(memory_space=pltpu.SMEM)),
        compiler_params=_params(("arbitrary",)),
        name="invert",
    )(meta, slots.reshape(nt, 1, pairs))


ROW_UNROLL = 8
ROW_UNROLL_SHIFT = 3


def _dispatch_kernel(na_ref, split_ref, tok_ref, tok_next_ref, h2p_ref, h2s_ref, xs_ref, buf, sem, *,
                     n_prompt):
    i = pl.program_id(0)
    tile = EXPERT_ROW_TILE
    n_active = na_ref[0]

    def issue(tok, split, slot):
        def prompt_copy(r):
            pltpu.make_async_copy(h2p_ref.at[pl.ds(tok[0, r], 1)], buf.at[slot, pl.ds(r, 1)],
                                  sem.at[slot]).start()

        def sample_copy(r):
            pltpu.make_async_copy(h2s_ref.at[pl.ds(tok[0, r] - n_prompt, 1)], buf.at[slot, pl.ds(r, 1)],
                                  sem.at[slot]).start()

        def rows(lo, hi, fn):
            def body(r, carry):
                fn(r)
                return carry
            lax.fori_loop(lo, hi, body, 0)

        def groups(lo, hi, fn):
            def body(g, carry):
                for u in range(ROW_UNROLL):
                    fn(g * ROW_UNROLL + u)
                return carry
            lax.fori_loop(lo, hi, body, 0)

        whole = split >> ROW_UNROLL_SHIFT
        first = (split + ROW_UNROLL - 1) >> ROW_UNROLL_SHIFT
        groups(0, whole, prompt_copy)
        rows(whole * ROW_UNROLL, split, prompt_copy)
        rows(split, first * ROW_UNROLL, sample_copy)
        groups(first, tile // ROW_UNROLL, sample_copy)

    @pl.when(i == 0)
    def _():
        issue(tok_ref, split_ref[0], 0)

    @pl.when(i + 1 < n_active)
    def _():
        issue(tok_next_ref, split_ref[i + 1], (i + 1) & 1)

    @pl.when(i < n_active)
    def _():
        slot = i & 1
        pltpu.make_async_copy(h2p_ref.at[pl.ds(0, tile)], buf.at[slot], sem.at[slot]).wait()
        xs_ref[...] = buf[slot].astype(BF16)

    @pl.when(i >= n_active)
    def _():
        xs_ref[...] = jnp.zeros_like(xs_ref)


def _dispatch(h2_p, h2_s, table, split, n_active):
    d = h2_p.shape[1]
    tile = EXPERT_ROW_TILE
    n_rows = table.shape[0]
    nt = n_rows // tile
    return pl.pallas_call(
        functools.partial(_dispatch_kernel, n_prompt=h2_p.shape[0]),
        out_shape=jax.ShapeDtypeStruct((n_rows, d), BF16),
        grid_spec=pltpu.PrefetchScalarGridSpec(
            num_scalar_prefetch=2,
            grid=(nt,),
            in_specs=[pl.BlockSpec((None, 1, tile), lambda i, na, sp: (i, 0, 0), memory_space=pltpu.SMEM),
                      pl.BlockSpec((None, 1, tile), lambda i, na, sp: (jnp.minimum(i + 1, nt - 1), 0, 0),
                                   memory_space=pltpu.SMEM),
                      pl.BlockSpec(memory_space=pl.ANY), pl.BlockSpec(memory_space=pl.ANY)],
            out_specs=pl.BlockSpec((tile, d), lambda i, na, sp: (i, 0)),
            scratch_shapes=[pltpu.VMEM((2, tile, d), F32), pltpu.SemaphoreType.DMA((2,))]),
        compiler_params=_params(("arbitrary",)),
        name="dispatch",
    )(n_active, split, table.reshape(nt, 1, tile), table.reshape(nt, 1, tile), h2_p, h2_s)


def _expert_run_tiles(expert_ref, first_ref, tiles_ref, meta_ref, n_col_blocks, weight_chunks, src_hbm, dst_hbm,
                      col, in_buf, out_buf, in_sem, out_sem, prepare, compute):
    j = pl.program_id(0)
    r = pl.program_id(1)
    tile = EXPERT_ROW_TILE
    width = out_buf.shape[2]
    n_runs = meta_ref[1]
    n_t = tiles_ref[r]
    t0 = first_ref[r]
    w_slot = (j * n_runs + r) & 1
    last_run = r == n_runs - 1
    r_next = jnp.where(last_run, 0, r + 1)
    j_next = jnp.where(last_run, j + 1, j)
    has_next = j_next < n_col_blocks
    cur_chunks = weight_chunks(expert_ref[r], j, w_slot)
    next_chunks = weight_chunks(expert_ref[r_next], j_next, 1 - w_slot)

    def rows(t):
        return pl.ds(pl.multiple_of(t * tile, tile), tile)

    def read(t, slot):
        return pltpu.make_async_copy(src_hbm.at[rows(t0 + t)], in_buf.at[slot], in_sem.at[slot])

    def write(t, slot):
        return pltpu.make_async_copy(out_buf.at[slot], dst_hbm.at[rows(t), pl.ds(col, width)],
                                     out_sem.at[slot])

    @pl.when(jnp.logical_and(j == 0, r == 0))
    def _():
        for cp in cur_chunks:
            cp.start()

    @pl.when(r < n_runs)
    def _():
        read(0, 0).start()
        for cp in cur_chunks:
            cp.wait()
        prepare(w_slot)

        def body(t, carry):
            slot = t & 1
            read(t, slot).wait()

            @pl.when(t + 1 < n_t)
            def _():
                read(t + 1, 1 - slot).start()

            for k, cp in enumerate(next_chunks):
                @pl.when(jnp.logical_and(has_next, t == k))
                def _(cp=cp):
                    cp.start()

            block = compute(in_buf[slot])

            @pl.when(t >= 2)
            def _():
                write(t0 + t - 2, slot).wait()

            out_buf[slot] = block
            write(t0 + t, slot).start()
            return carry

        lax.fori_loop(0, n_t, body, 0)

        for k, cp in enumerate(next_chunks):
            @pl.when(jnp.logical_and(has_next, n_t <= k))
            def _(cp=cp):
                cp.start()

        @pl.when(n_t >= 2)
        def _():
            write(t0 + n_t - 2, n_t & 1).wait()

        write(t0 + n_t - 1, (n_t - 1) & 1).wait()

    @pl.when(r == pl.num_programs(1) - 1)
    def _():
        out_buf[0] = jnp.zeros(out_buf.shape[1:], out_buf.dtype)

        def start(t, carry):
            write(t, 0).start()
            return carry

        def wait(t, carry):
            write(t, 0).wait()
            return carry

        n_tiles = dst_hbm.shape[0] // tile
        lax.fori_loop(meta_ref[0], n_tiles, start, 0)
        lax.fori_loop(meta_ref[0], n_tiles, wait, 0)


WEIGHT_ROW_CHUNKS = 2


def _gate_up_kernel(expert_ref, first_ref, tiles_ref, meta_ref, bg_ref, bu_ref, w_hbm, xs_hbm, act_hbm,
                    wbuf, wg_b, wu_b, xbuf, obuf, wsem, xsem, osem, *, nj, tn):
    piece = wbuf.shape[2] // WEIGHT_ROW_CHUNKS

    def weight_chunks(e, jj, slot):
        chunks = []
        for c in range(2):
            cols = pl.ds(pl.multiple_of((c * nj + jj) * tn, tn), tn)
            for h in range(WEIGHT_ROW_CHUNKS):
                part = pl.ds(h * piece, piece)
                chunks.append(pltpu.make_async_copy(w_hbm.at[e, part, cols], wbuf.at[slot, c, part],
                                                    wsem.at[slot]))
        return chunks

    def prepare(slot):
        wg_b[...] = wbuf[slot, 0].astype(BF16)
        wu_b[...] = wbuf[slot, 1].astype(BF16)

    def compute(x):
        gate = jnp.dot(x, wg_b[...], preferred_element_type=F32) + bg_ref[...]
        up = jnp.dot(x, wu_b[...], preferred_element_type=F32) + bu_ref[...]
        gate = jnp.minimum(gate, SWIGLU_LIMIT)
        up = jnp.clip(up, -SWIGLU_LIMIT, SWIGLU_LIMIT)
        return ((up + 1.0) * gate * jax.nn.sigmoid(SWIGLU_ALPHA * gate)).astype(BF16)

    col = pl.multiple_of(pl.program_id(0) * tn, tn)
    _expert_run_tiles(expert_ref, first_ref, tiles_ref, meta_ref, nj, weight_chunks, xs_hbm, act_hbm, col,
                      xbuf, obuf, xsem, osem, prepare, compute)


def _expert_gate_up(xs, w_gu, b_gu, runs):
    n_rows, d = xs.shape
    n_exp, _, f2 = w_gu.shape
    f = f2 // 2
    tm = EXPERT_ROW_TILE
    tn = 1024
    nj = f // tn
    bias = lambda c: pl.BlockSpec((None, 1, tn), lambda j, r, ex, fi, ti, meta: (ex[r], 0, c * nj + j))
    return pl.pallas_call(
        functools.partial(_gate_up_kernel, nj=nj, tn=tn),
        out_shape=jax.ShapeDtypeStruct((n_rows, f), BF16),
        grid_spec=pltpu.PrefetchScalarGridSpec(
            num_scalar_prefetch=4,
            grid=(nj, n_exp),
            in_specs=[bias(0), bias(1), pl.BlockSpec(memory_space=pl.ANY), pl.BlockSpec(memory_space=pl.ANY)],
            out_specs=pl.BlockSpec(memory_space=pl.ANY),
            scratch_shapes=[pltpu.VMEM((2, 2, d, tn), F32), pltpu.VMEM((d, tn), BF16),
                            pltpu.VMEM((d, tn), BF16), pltpu.VMEM((2, tm, d), BF16),
                            pltpu.VMEM((2, tm, tn), BF16), pltpu.SemaphoreType.DMA((2,)),
                            pltpu.SemaphoreType.DMA((2,)), pltpu.SemaphoreType.DMA((2,))]),
        compiler_params=_params(("arbitrary", "arbitrary")),
        name="expert_gate_up",
    )(*runs, b_gu.reshape(n_exp, 1, f2), b_gu.reshape(n_exp, 1, f2), w_gu, xs)


def _down_kernel(expert_ref, first_ref, tiles_ref, meta_ref, b_ref, w_hbm, act_hbm, y_hbm, wbuf, w_b, abuf, ybuf,
                 wsem, asem, ysem):
    n_pieces = 2 * WEIGHT_ROW_CHUNKS
    piece = wbuf.shape[1] // n_pieces

    def weight_chunks(e, jj, slot):
        return [pltpu.make_async_copy(w_hbm.at[e, pl.ds(h * piece, piece)], wbuf.at[slot, pl.ds(h * piece, piece)],
                                      wsem.at[slot]) for h in range(n_pieces)]

    def prepare(slot):
        w_b[...] = wbuf[slot].astype(BF16)

    def compute(a):
        return jnp.dot(a, w_b[...], preferred_element_type=F32) + b_ref[...]

    _expert_run_tiles(expert_ref, first_ref, tiles_ref, meta_ref, 1, weight_chunks, act_hbm, y_hbm, 0,
                      abuf, ybuf, asem, ysem, prepare, compute)


def _expert_down(act, w_d, b_d, runs):
    n_rows, f = act.shape
    n_exp, _, d = w_d.shape
    tm = EXPERT_ROW_TILE
    return pl.pallas_call(
        _down_kernel,
        out_shape=jax.ShapeDtypeStruct((n_rows, d), F32),
        grid_spec=pltpu.PrefetchScalarGridSpec(
            num_scalar_prefetch=4,
            grid=(1, n_exp),
            in_specs=[pl.BlockSpec((None, 1, d), lambda j, r, ex, fi, ti, meta: (ex[r], 0, 0)),
                      pl.BlockSpec(memory_space=pl.ANY), pl.BlockSpec(memory_space=pl.ANY)],
            out_specs=pl.BlockSpec(memory_space=pl.ANY),
            scratch_shapes=[pltpu.VMEM((2, f, d), F32), pltpu.VMEM((f, d), BF16),
                            pltpu.VMEM((2, tm, f), BF16), pltpu.VMEM((2, tm, d), F32),
                            pltpu.SemaphoreType.DMA((2,)), pltpu.SemaphoreType.DMA((2,)),
                            pltpu.SemaphoreType.DMA((2,))]),
        compiler_params=_params(("arbitrary", "arbitrary")),
        name="expert_down",
    )(*runs, b_d.reshape(n_exp, 1, d), w_d, act)


COMBINE_TOKENS = 128


def _combine_kernel(slot_ref, slot_next_ref, p_ref, x1_ref, gate_ref, g_ref, y_ref, o_ref, buf, sem):
    i = pl.program_id(0)
    tm = COMBINE_TOKENS
    pairs = tm * TOP_K

    def issue(slots, b):
        def body(g, carry):
            for u in range(ROW_UNROLL):
                p = g * ROW_UNROLL + u
                tok = g * (ROW_UNROLL // TOP_K) + u // TOP_K
                pltpu.make_async_copy(y_ref.at[pl.ds(slots[0, p], 1)], buf.at[b, u % TOP_K, pl.ds(tok, 1)],
                                      sem.at[b]).start()
            return carry
        lax.fori_loop(0, pairs // ROW_UNROLL, body, 0)

    @pl.when(i == 0)
    def _():
        issue(slot_ref, 0)

    @pl.when(i + 1 < pl.num_programs(0))
    def _():
        issue(slot_next_ref, (i + 1) & 1)

    b = i & 1
    for k in range(TOP_K):
        pltpu.make_async_copy(y_ref.at[pl.ds(0, tm)], buf.at[b, k], sem.at[b]).wait()
    probs = p_ref[...]
    f = probs[:, 0:1] * buf[b, 0]
    for k in range(1, TOP_K):
        f = f + probs[:, k:k + 1] * buf[b, k]
    o_ref[...] = x1_ref[...] + gate_ref[...] * (_rms(f) * g_ref[...])


def _combine(y, slots, probs, x1, gate, g_post, rows_per_group):
    m, d = x1.shape
    tm = COMBINE_TOKENS
    nt = m // tm
    pairs = tm * TOP_K
    return pl.pallas_call(
        _combine_kernel,
        out_shape=jax.ShapeDtypeStruct((m, d), F32),
        grid=(nt,),
        in_specs=[pl.BlockSpec((None, 1, pairs), lambda i: (i, 0, 0), memory_space=pltpu.SMEM),
                  pl.BlockSpec((None, 1, pairs), lambda i: (jnp.minimum(i + 1, nt - 1), 0, 0),
                               memory_space=pltpu.SMEM),
                  pl.BlockSpec((tm, LANES), lambda i: (i, 0)),
                  pl.BlockSpec((tm, d), lambda i: (i, 0)),
                  _mod_spec(gate, tm, rows_per_group),
                  pl.BlockSpec((1, d), lambda i: (0, 0)),
                  pl.BlockSpec(memory_space=pl.ANY)],
        out_specs=pl.BlockSpec((tm, d), lambda i: (i, 0)),
        scratch_shapes=[pltpu.VMEM((2, TOP_K, tm, d), F32), pltpu.SemaphoreType.DMA((2,))],
        compiler_params=_params(("arbitrary",)),
        name="combine",
    )(slots.reshape(nt, 1, pairs), slots.reshape(nt, 1, pairs), probs, x1, gate, g_post.reshape(1, d), y)


def _moe(h2_p, h2_s, logits_p, logits_s, n_exp, w_gu, b_gu, w_d, b_d):
    n = h2_p.shape[0] + h2_s.shape[0]
    tile = EXPERT_ROW_TILE
    top_i, probs, rank, counts = _route(logits_p, logits_s)
    counts = counts[0, :n_exp].astype(I32)
    cap = (counts + tile - 1) // tile * tile
    ends = jnp.cumsum(cap)
    offs = ends - cap
    n_tiles = (n * TOP_K) // tile + n_exp
    n_active = (ends[-1] // tile).astype(I32).reshape(1)
    expert_ids = jnp.arange(n_exp, dtype=I32)
    has_rows = cap > 0
    n_runs = jnp.sum(has_rows.astype(I32))
    run_of = jnp.cumsum(has_rows.astype(I32)) - 1
    hit = has_rows[None, :] & (run_of[None, :] == jnp.minimum(expert_ids, n_runs - 1)[:, None])
    pick = lambda v: jnp.sum(jnp.where(hit, v[None, :], 0), axis=1).astype(I32)
    runs = (pick(expert_ids), pick(offs // tile), jnp.where(expert_ids < n_runs, pick(cap // tile), 0),
            jnp.stack([n_active[0], n_runs]).astype(I32))
    pair_offs = jnp.sum(jnp.where(top_i[:, :TOP_K, None] == expert_ids, offs, 0), axis=-1)
    slots = pair_offs + rank[:, :TOP_K]
    n_prompt = h2_p.shape[0]
    meta = jnp.concatenate([offs, cap, counts, n_active]).astype(I32)
    table = _invert(slots, meta, n_tiles * tile, n_prompt, n_exp)
    split = jnp.sum(table.reshape(n_tiles, tile) < n_prompt, axis=1).astype(I32)
    xs = _dispatch(h2_p, h2_s, table, split, n_active)
    act = _expert_gate_up(xs, w_gu, b_gu, runs)
    y = _expert_down(act, w_d, b_d, runs)
    return y, slots, probs


def _layer(xp, xs_, cp_mod, cs_mod, hist_a, hist_q, s0, g_pre_mix, g_post_mix, g_pre_ffn, g_post_ffn,
           w_in, conv_a_w, gdn_conv_w, a_log, dt_bias, g_conv_out, gdn_norm_g, w_out, router_w, router_b,
           w_gu, b_gu, w_d, b_d):
    bsz, seq, d = xp.shape
    ns = xs_.shape[0]
    n_exp = router_w.shape[1]
    dc = N_HEADS * HEAD_D
    d_main = 3 * dc + 3 * dc + dc
    xp2 = xp.reshape(bsz * seq, d)
    xs2 = xs_.reshape(ns, d)

    w_main = w_in.astype(BF16)
    w_ba = jnp.zeros((d, LANES), BF16).at[:, :2 * N_HEADS].set(w_main[:, d_main:])
    w_out_b = w_out.astype(BF16)
    rw = jnp.zeros((d, LANES), F32).at[:, :n_exp].set(router_w)
    rw_hi = rw.astype(BF16)
    rw_lo = (rw - rw_hi.astype(F32)).astype(BF16)
    rb = jnp.full((1, LANES), NEG_BIG, F32).at[0, :n_exp].set(router_b)

    mp = [cp_mod[:, i * d:(i + 1) * d].reshape(bsz, 1, d) for i in range(6)]
    ms = [cs_mod[:, i * d:(i + 1) * d] for i in range(6)]

    proj_p, ba_p = _in_proj(xp2, mp[1], mp[0], g_pre_mix, w_main, w_ba, seq, d_main)
    proj_s, ba_s = _in_proj(xs2, ms[1], ms[0], g_pre_mix, w_main, w_ba, 1, d_main)

    mix_p, ha_p, hq_p, s_p = _mixer_prompt(proj_p, ba_p, bsz, seq, conv_a_w, gdn_conv_w, a_log, dt_bias,
                                           g_conv_out, gdn_norm_g)
    mix_s, ha_s, hq_s, s_s = _mixer_sample(proj_s, ba_s, hist_a.reshape(ns, 2 * dc),
                                           hist_q.reshape(ns, 9 * dc), s0, conv_a_w, gdn_conv_w, a_log,
                                           dt_bias, g_conv_out, gdn_norm_g)

    x1_p, h2_p, lg_p = _post_mix(mix_p, xp2, mp[2], mp[4], mp[3], g_post_mix, g_pre_ffn, w_out_b, rw_hi,
                                 rw_lo, rb, seq)
    x1_s, h2_s, lg_s = _post_mix(mix_s, xs2, ms[2], ms[4], ms[3], g_post_mix, g_pre_ffn, w_out_b, rw_hi,
                                 rw_lo, rb, 1)

    y, slots, probs = _moe(h2_p, h2_s, lg_p, lg_s, n_exp, w_gu, b_gu, w_d, b_d)
    np_ = bsz * seq
    out_p = _combine(y, slots[:np_], probs[:np_], x1_p, mp[5], g_post_ffn, seq)
    out_s = _combine(y, slots[np_:], probs[np_:], x1_s, ms[5], g_post_ffn, 1)
    return (out_p.reshape(bsz, seq, d), out_s.reshape(ns, 1, d), ha_p, hq_p, s_p,
            ha_s.reshape(ns, 2, dc), hq_s.reshape(ns, 3, 3 * dc), s_s)


def kernel(x_prompt, x_sample, state_conv_a, state_gdn_conv, state_gdn_S, c_prompt, c_sample, w_mod, b_mod, g_pre_mix, g_post_mix, g_pre_ffn, g_post_ffn, w_in, conv_a_w, gdn_conv_w, gdn_a_log, gdn_dt_bias, g_conv_out, gdn_norm_g, w_out, router_w, router_b, exp_w_gate_up, exp_b_gate_up, exp_w_down, exp_b_down):
    depth = w_mod.shape[0]
    bp = x_prompt.shape[0]
    xp, xs_ = x_prompt, x_sample
    outs = [[] for _ in range(6)]
    for l in range(depth):
        mod = _modulation(jnp.concatenate([c_prompt, c_sample], axis=0), w_mod[l], b_mod[l])
        res = _layer(xp, xs_, mod[:bp], mod[bp:], state_conv_a[l], state_gdn_conv[l], state_gdn_S[l],
                     g_pre_mix[l], g_post_mix[l], g_pre_ffn[l], g_post_ffn[l], w_in[l], conv_a_w[l],
                     gdn_conv_w[l], gdn_a_log[l], gdn_dt_bias[l], g_conv_out[l], gdn_norm_g[l], w_out[l],
                     router_w[l], router_b[l], exp_w_gate_up[l], exp_b_gate_up[l], exp_w_down[l],
                     exp_b_down[l])
        xp, xs_ = res[0], res[1]
        for acc, r in zip(outs, res[2:]):
            acc.append(r)
    return (xp, xs_) + tuple(o[0][None] if depth == 1 else jnp.stack(o) for o in outs)
```

```python
import functools

import jax
import jax.numpy as jnp
from jax import lax
from jax.experimental import pallas as pl
from jax.experimental.pallas import tpu as pltpu

F32 = jnp.float32
BF16 = jnp.bfloat16
I32 = jnp.int32
HIGHEST = lax.Precision.HIGHEST

EPS = 1e-6
N_HEADS = 8
HEAD_D = 128
TOP_K = 4
TOP_K_SHIFT = 2
SWIGLU_LIMIT = 7.0
SWIGLU_ALPHA = 1.702
CHUNK = 64
LANES = 128
EXPERT_ROW_TILE = 256
NEG_BIG = -1e30
VMEM_LIMIT = 56 * 1024 * 1024


def _params(semantics, vmem=VMEM_LIMIT):
    return pltpu.CompilerParams(dimension_semantics=semantics, vmem_limit_bytes=vmem)


def _mm(a, b):
    return jnp.dot(a.astype(BF16), b.astype(BF16), preferred_element_type=F32)


def _mm_nt(a, b):
    return lax.dot_general(a.astype(BF16), b.astype(BF16), (((1,), (1,)), ((), ())),
                           preferred_element_type=F32)


def _rms(x):
    return x * lax.rsqrt(jnp.mean(x * x, axis=-1, keepdims=True) + EPS)


def _silu(x):
    return x * jax.nn.sigmoid(x)


def _softplus(x):
    return jnp.maximum(x, 0.0) + jnp.log1p(jnp.exp(-jnp.abs(x)))


def _mod_kernel(c_ref, w_ref, b_ref, o_ref):
    s = _silu(c_ref[...])
    o_ref[...] = _mm(s, w_ref[...]) + b_ref[...]


def _modulation(c_all, w_mod, b_mod):
    n, d = c_all.shape
    m = w_mod.shape[1]
    tn = 1024
    return pl.pallas_call(
        _mod_kernel,
        out_shape=jax.ShapeDtypeStruct((n, m), F32),
        grid=(m // tn,),
        in_specs=[pl.BlockSpec((n, d), lambda j: (0, 0)),
                  pl.BlockSpec((d, tn), lambda j: (0, j)),
                  pl.BlockSpec((1, tn), lambda j: (0, j))],
        out_specs=pl.BlockSpec((n, tn), lambda j: (0, j)),
        compiler_params=_params(("arbitrary",)),
        name="modulation",
    )(c_all, w_mod, b_mod.reshape(1, m))


def _mod_spec(arr, tm, rows_per_group):
    if arr.ndim == 3:
        tiles = rows_per_group // tm
        return pl.BlockSpec((None, 1, arr.shape[-1]), lambda i, *_: (i // tiles, 0, 0))
    return pl.BlockSpec((tm, arr.shape[-1]), lambda i, *_: (i, 0))


def _proj_kernel(x_ref, sc_ref, sh_ref, g_ref, w_ref, wba_ref, o_ref, ba_ref, h_scr):
    @pl.when(pl.program_id(1) == 0)
    def _():
        h = (_rms(x_ref[...]) * g_ref[...]) * (1.0 + sc_ref[...]) + sh_ref[...]
        hb = h.astype(BF16)
        h_scr[...] = hb
        ba_ref[...] = jnp.dot(hb, wba_ref[...], preferred_element_type=F32)

    o_ref[...] = jnp.dot(h_scr[...], w_ref[...], preferred_element_type=F32)


def _in_proj(x, scale, shift, g, w_main, w_ba, rows_per_group, n):
    m, d = x.shape
    tm = min(m, 1024, rows_per_group if scale.ndim == 3 else m)
    tn = 1024
    return pl.pallas_call(
        _proj_kernel,
        out_shape=(jax.ShapeDtypeStruct((m, n), F32), jax.ShapeDtypeStruct((m, LANES), F32)),
        grid=(m // tm, n // tn),
        in_specs=[pl.BlockSpec((tm, d), lambda i, j: (i, 0)),
                  _mod_spec(scale, tm, rows_per_group),
                  _mod_spec(shift, tm, rows_per_group),
                  pl.BlockSpec((1, d), lambda i, j: (0, 0)),
                  pl.BlockSpec((d, tn), lambda i, j: (0, j)),
                  pl.BlockSpec((d, LANES), lambda i, j: (0, 0))],
        out_specs=(pl.BlockSpec((tm, tn), lambda i, j: (i, j)),
                   pl.BlockSpec((tm, LANES), lambda i, j: (i, 0))),
        scratch_shapes=[pltpu.VMEM((tm, d), BF16)],
        compiler_params=_params(("arbitrary", "arbitrary")),
        name="in_proj",
    )(x, scale, shift, g.reshape(1, d), w_main, w_ba)


PROMPT_SEQS_PER_STEP = 4


def _mixer_prompt_kernel(proj_ref, ba_ref, caw_ref, gcw_ref, alog_ref, dtb_ref, gco_ref, gng_ref,
                         mix_ref, ha_ref, hq_ref, s_ref, extu, extq, qc_scr, s_scr, *, nseq):
    c = CHUNK
    dc = N_HEADS * HEAD_D
    dq = 3 * dc
    t = pl.program_id(1)
    is_last = t == pl.num_programs(1) - 1

    @pl.when(t == 0)
    def _():
        extu[:, 0:8, :] = jnp.zeros((nseq, 8, dc), F32)
        extq[:, 0:8, :] = jnp.zeros((nseq, 8, dq), F32)
        s_scr[...] = jnp.zeros_like(s_scr)

    row = lax.broadcasted_iota(I32, (c, c), 0)
    col = lax.broadcasted_iota(I32, (c, c), 1)
    causal = row >= col
    strict = row > col
    lower = jnp.where(causal, 1.0, 0.0).astype(F32)
    upper = jnp.where(row <= col, 1.0, 0.0).astype(F32)
    caw = caw_ref[...]
    gcw = gcw_ref[...]

    chains = [(sq, h) for sq in range(nseq) for h in range(N_HEADS)]
    heads = range(len(chains))
    qn, kn, vb, kb, kbg, qg, kg, decay, s_decay = ([] for _ in range(9))
    for sq in range(nseq):
        u = proj_ref[sq, :, dc:2 * dc] * proj_ref[sq, :, 2 * dc:3 * dc]
        extu[sq, 8:8 + c, :] = u
        ya = caw[0:1] * extu[sq, 6:6 + c, :] + caw[1:2] * extu[sq, 7:7 + c, :] + caw[2:3] * u
        ya = proj_ref[sq, :, 0:dc] * ya
        mix_ref[sq, :, 0:dc] = (_rms(ya) * gco_ref[...]).astype(BF16)
        last_u = extu[sq, 6 + c:8 + c, :]
        extu[sq, 6:8, :] = last_u

        qkv = proj_ref[sq, :, 3 * dc:3 * dc + dq]
        extq[sq, 8:8 + c, :] = qkv
        qc = (gcw[0:1] * extq[sq, 5:5 + c, :] + gcw[1:2] * extq[sq, 6:6 + c, :]
              + gcw[2:3] * extq[sq, 7:7 + c, :] + gcw[3:4] * qkv)
        qc_scr[sq] = _silu(qc)
        last_q = extq[sq, 5 + c:8 + c, :]
        extq[sq, 5:8, :] = last_q

        @pl.when(is_last)
        def _(sq=sq, last_u=last_u, last_q=last_q):
            ha_ref[sq] = last_u
            hq_ref[sq] = last_q

        ba = ba_ref[sq]
        beta_all = jax.nn.sigmoid(ba)
        g_all = -jnp.exp(alog_ref[...]) * _softplus(ba + dtb_ref[...])
        gc_all = jnp.dot(lower, g_all, precision=HIGHEST, preferred_element_type=F32)
        gc_t = lax.dot_general(g_all, upper, (((0,), (0,)), ((), ())), precision=HIGHEST,
                               preferred_element_type=F32)
        for h in range(N_HEADS):
            lo = h * HEAD_D
            q = qc_scr[sq, :, lo:lo + HEAD_D]
            k = qc_scr[sq, :, dc + lo:dc + lo + HEAD_D]
            v = qc_scr[sq, :, 2 * dc + lo:2 * dc + lo + HEAD_D]
            qn_h = q * lax.rsqrt(jnp.sum(q * q, axis=-1, keepdims=True) + EPS) * (HEAD_D ** -0.5)
            kn_h = k * lax.rsqrt(jnp.sum(k * k, axis=-1, keepdims=True) + EPS)
            beta = beta_all[:, h:h + 1]
            gcc = gc_all[:, N_HEADS + h:N_HEADS + h + 1]
            gcr = gc_t[N_HEADS + h:N_HEADS + h + 1, :]
            gl = gc_all[c - 1:c, N_HEADS + h:N_HEADS + h + 1]
            eg = jnp.exp(gcc)
            kb_h = kn_h * beta
            qn.append(qn_h)
            kn.append(kn_h)
            vb.append(v * beta)
            kb.append(kb_h)
            kbg.append(kb_h * eg)
            qg.append(qn_h * eg)
            kg.append(kn_h * jnp.exp(gl - gcc))
            decay.append(jnp.where(causal, jnp.exp(jnp.minimum(gcc - gcr, 0.0)), 0.0))
            s_decay.append(jnp.exp(gl))

    kq = [_mm_nt(jnp.concatenate([kb[h], qn[h]], axis=0), kn[h]) for h in heads]
    a_mat = [jnp.where(strict, kq[h][:c] * decay[h], 0.0) for h in heads]
    qk = [kq[h][c:] * decay[h] for h in heads]
    n_mat = [-a_mat[h] for h in heads]
    p = a_mat
    size = 2
    while size < c:
        p = [_mm(p[h], p[h]) for h in heads]
        n_p = [_mm(n_mat[h], p[h]) for h in heads]
        n_mat = [n_mat[h] + p[h] + n_p[h] for h in heads]
        size *= 2
    rhs = [jnp.concatenate([vb[h], kbg[h]], axis=-1) for h in heads]
    uw = [rhs[h] + _mm(n_mat[h], rhs[h]) for h in heads]
    s_old = [s_scr[sq, h] for sq, h in chains]
    ws = [_mm(jnp.concatenate([uw[h][:, HEAD_D:], qg[h]], axis=0), s_old[h]) for h in heads]
    v_new = [uw[h][:, :HEAD_D] - ws[h][:c] for h in heads]
    fin = [_mm(jnp.concatenate([qk[h], kg[h].T], axis=0), v_new[h]) for h in heads]
    for i, (sq, h) in enumerate(chains):
        lo = h * HEAD_D
        s_scr[sq, h] = s_old[i] * s_decay[i] + fin[i][c:]
        o = ws[i][c:] + fin[i][:c]
        z = proj_ref[sq, :, 3 * dc + dq + lo:3 * dc + dq + lo + HEAD_D]
        yb = _rms(o) * gng_ref[...] * _silu(z)
        mix_ref[sq, :, dc + lo:dc + lo + HEAD_D] = yb.astype(BF16)

    @pl.when(is_last)
    def _():
        s_ref[...] = s_scr[...]


def _lane_row(vec, offset):
    return jnp.zeros((1, LANES), F32).at[0, offset:offset + vec.shape[0]].set(vec.astype(F32))


def _mixer_prompt(proj, ba, bsz, seq, conv_a_w, gdn_conv_w, a_log, dt_bias, g_conv_out, gdn_norm_g):
    c = CHUNK
    dc = N_HEADS * HEAD_D
    dq = 3 * dc
    dproj = proj.shape[1]
    nt = seq // c
    const = lambda shape: pl.BlockSpec(shape, lambda b, t: (0,) * len(shape))
    nseq = PROMPT_SEQS_PER_STEP if bsz % PROMPT_SEQS_PER_STEP == 0 else 1
    seq_block = lambda *tail: pl.BlockSpec((nseq,) + tail, lambda b, t: (b, t) + (0,) * (len(tail) - 1))
    whole = lambda *tail: pl.BlockSpec((nseq,) + tail, lambda b, t: (b,) + (0,) * len(tail))
    mix, ha, hq, s_fin = pl.pallas_call(
        functools.partial(_mixer_prompt_kernel, nseq=nseq),
        out_shape=(jax.ShapeDtypeStruct((bsz, seq, 2 * dc), BF16),
                   jax.ShapeDtypeStruct((bsz, 2, dc), F32),
                   jax.ShapeDtypeStruct((bsz, 3, dq), F32),
                   jax.ShapeDtypeStruct((bsz, N_HEADS, HEAD_D, HEAD_D), F32)),
        grid=(bsz // nseq, nt),
        in_specs=[seq_block(c, dproj), seq_block(c, LANES),
                  const((3, dc)), const((4, dq)), const((1, LANES)), const((1, LANES)),
                  const((1, dc)), const((1, HEAD_D))],
        out_specs=(seq_block(c, 2 * dc), whole(2, dc), whole(3, dq), whole(N_HEADS, HEAD_D, HEAD_D)),
        scratch_shapes=[pltpu.VMEM((nseq, 8 + c, dc), F32), pltpu.VMEM((nseq, 8 + c, dq), F32),
                        pltpu.VMEM((nseq, c, dq), F32), pltpu.VMEM((nseq, N_HEADS, HEAD_D, HEAD_D), F32)],
        compiler_params=_params(("arbitrary", "arbitrary")),
        name="mixer_prompt",
    )(proj.reshape(bsz, seq, dproj), ba.reshape(bsz, seq, LANES), conv_a_w, gdn_conv_w,
      _lane_row(a_log, N_HEADS), _lane_row(dt_bias, N_HEADS), g_conv_out.reshape(1, dc),
      gdn_norm_g.reshape(1, HEAD_D))
    return mix.reshape(bsz * seq, 2 * dc), ha, hq, s_fin


SAMPLE_GROUP = 16


def _mixer_sample_kernel(proj_ref, ba_ref, hista_ref, histq_ref, s_in_ref, caw_ref, gcw_ref, alog_ref,
                         dtb_ref, gco_ref, gng_ref, mix_ref, ha_ref, hq_ref, s_out_ref, qc_scr, o_scr):
    tb = SAMPLE_GROUP
    dc = N_HEADS * HEAD_D
    dq = 3 * dc

    u = proj_ref[:, dc:2 * dc] * proj_ref[:, 2 * dc:3 * dc]
    caw = caw_ref[...]
    ya = caw[0:1] * hista_ref[:, 0:dc] + caw[1:2] * hista_ref[:, dc:2 * dc] + caw[2:3] * u
    ya = proj_ref[:, 0:dc] * ya
    mix_ref[:, 0:dc] = _rms(ya) * gco_ref[...]
    ha_ref[:, 0:dc] = hista_ref[:, dc:2 * dc]
    ha_ref[:, dc:2 * dc] = u

    qkv = proj_ref[:, 3 * dc:3 * dc + dq]
    gcw = gcw_ref[...]
    qc = (gcw[0:1] * histq_ref[:, 0:dq] + gcw[1:2] * histq_ref[:, dq:2 * dq]
          + gcw[2:3] * histq_ref[:, 2 * dq:3 * dq] + gcw[3:4] * qkv)
    qc_scr[...] = _silu(qc)
    hq_ref[:, 0:dq] = histq_ref[:, dq:2 * dq]
    hq_ref[:, dq:2 * dq] = histq_ref[:, 2 * dq:3 * dq]
    hq_ref[:, 2 * dq:3 * dq] = qkv

    ba = ba_ref[...]
    beta_all = jax.nn.sigmoid(ba)
    eg_all = jnp.exp(-jnp.exp(alog_ref[...]) * _softplus(ba + dtb_ref[...]))

    for h in range(N_HEADS):
        lo = h * HEAD_D
        q = qc_scr[:, lo:lo + HEAD_D]
        k = qc_scr[:, dc + lo:dc + lo + HEAD_D]
        v = qc_scr[:, 2 * dc + lo:2 * dc + lo + HEAD_D]
        qn = q * lax.rsqrt(jnp.sum(q * q, axis=-1, keepdims=True) + EPS) * (HEAD_D ** -0.5)
        kn = k * lax.rsqrt(jnp.sum(k * k, axis=-1, keepdims=True) + EPS)
        qk = jnp.sum(qn * kn, axis=-1, keepdims=True)
        kn_t = kn.T
        qn_t = qn.T
        for b in range(tb):
            s_old = s_in_ref[b, h]
            kc = kn_t[:, b:b + 1]
            e = eg_all[b:b + 1, N_HEADS + h:N_HEADS + h + 1]
            ks = jnp.sum(s_old * kc, axis=0, keepdims=True)
            qs = jnp.sum(s_old * qn_t[:, b:b + 1], axis=0, keepdims=True)
            v_new = beta_all[b:b + 1, h:h + 1] * (v[b:b + 1, :] - e * ks)
            o_scr[b:b + 1, lo:lo + HEAD_D] = e * qs + qk[b:b + 1, :] * v_new
            s_out_ref[b, h] = s_old * e + kc * v_new
        z = proj_ref[:, 3 * dc + dq + lo:3 * dc + dq + lo + HEAD_D]
        o = o_scr[:, lo:lo + HEAD_D]
        mix_ref[:, dc + lo:dc + lo + HEAD_D] = _rms(o) * gng_ref[...] * _silu(z)


def _mixer_sample(proj, ba, hist_a, hist_q, s_in, conv_a_w, gdn_conv_w, a_log, dt_bias, g_conv_out,
                  gdn_norm_g):
    n = proj.shape[0]
    tb = SAMPLE_GROUP
    dc = N_HEADS * HEAD_D
    dq = 3 * dc
    dproj = proj.shape[1]
    const = lambda shape: pl.BlockSpec(shape, lambda i: (0,) * len(shape))
    rows = lambda width: pl.BlockSpec((tb, width), lambda i: (i, 0))
    state = pl.BlockSpec((tb, N_HEADS, HEAD_D, HEAD_D), lambda i: (i, 0, 0, 0))
    return pl.pallas_call(
        _mixer_sample_kernel,
        out_shape=(jax.ShapeDtypeStruct((n, 2 * dc), F32),
                   jax.ShapeDtypeStruct((n, 2 * dc), F32),
                   jax.ShapeDtypeStruct((n, 3 * dq), F32),
                   jax.ShapeDtypeStruct((n, N_HEADS, HEAD_D, HEAD_D), F32)),
        grid=(n // tb,),
        in_specs=[rows(dproj), rows(LANES), rows(2 * dc), rows(3 * dq), state,
                  const((3, dc)), const((4, dq)), const((1, LANES)), const((1, LANES)),
                  const((1, dc)), const((1, HEAD_D))],
        out_specs=(rows(2 * dc), rows(2 * dc), rows(3 * dq), state),
        scratch_shapes=[pltpu.VMEM((tb, dq), F32), pltpu.VMEM((tb, dc), F32)],
        compiler_params=_params(("arbitrary",)),
        name="mixer_sample",
    )(proj, ba, hist_a, hist_q, s_in, conv_a_w, gdn_conv_w, _lane_row(a_log, N_HEADS),
      _lane_row(dt_bias, N_HEADS), g_conv_out.reshape(1, dc), gdn_norm_g.reshape(1, HEAD_D))


POST_MIX_SUB_ROWS = 256


def _post_mix_kernel(mix_ref, x_ref, gate_ref, sc_ref, sh_ref, gpost_ref, gpre_ref, wout_ref,
                     rwh_ref, rwl_ref, rb_ref, x1_ref, h2_ref, lg_ref):
    tm = x_ref.shape[0]
    sub = min(tm, POST_MIX_SUB_ROWS)
    tiles = [slice(r0, r0 + sub) for r0 in range(0, tm, sub)]
    mixes = [jnp.dot(mix_ref[rows].astype(BF16), wout_ref[...], preferred_element_type=F32)
             for rows in tiles]
    for rows, mix in zip(tiles, mixes):
        per_row = lambda ref: ref[rows] if ref.shape[0] == tm else ref[...]
        x1 = x_ref[rows] + per_row(gate_ref) * (_rms(mix) * gpost_ref[...])
        x1_ref[rows] = x1
        h2 = (_rms(x1) * gpre_ref[...]) * (1.0 + per_row(sc_ref)) + per_row(sh_ref)
        h2_ref[rows] = h2
        hi = h2.astype(BF16)
        lo = (h2 - hi.astype(F32)).astype(BF16)
        rwh = rwh_ref[...]
        lg_ref[rows] = (jnp.dot(hi, rwh, preferred_element_type=F32)
                        + jnp.dot(lo, rwh, preferred_element_type=F32)
                        + jnp.dot(hi, rwl_ref[...], preferred_element_type=F32) + rb_ref[...])


def _post_mix(mix_in, x, gate, scale, shift, g_post, g_pre, w_out, rw_hi, rw_lo, rb, rows_per_group):
    m, d = x.shape
    tm = min(m, 512, rows_per_group if gate.ndim == 3 else m)
    const = lambda shape: pl.BlockSpec(shape, lambda i: (0,) * len(shape))
    rows = lambda width: pl.BlockSpec((tm, width), lambda i: (i, 0))
    return pl.pallas_call(
        _post_mix_kernel,
        out_shape=(jax.ShapeDtypeStruct((m, d), F32), jax.ShapeDtypeStruct((m, d), F32),
                   jax.ShapeDtypeStruct((m, LANES), F32)),
        grid=(m // tm,),
        in_specs=[rows(d), rows(d),
                  _mod_spec(gate, tm, rows_per_group),
                  _mod_spec(scale, tm, rows_per_group),
                  _mod_spec(shift, tm, rows_per_group),
                  const((1, d)), const((1, d)), const((d, d)),
                  const((d, LANES)), const((d, LANES)), const((1, LANES))],
        out_specs=(rows(d), rows(d), rows(LANES)),
        compiler_params=_params(("arbitrary",)),
        name="post_mix",
    )(mix_in, x, gate, scale, shift, g_post.reshape(1, d), g_pre.reshape(1, d), w_out, rw_hi, rw_lo, rb)


ROUTE_TOKENS = 128


def _route_kernel(lgp_ref, lgs_ref, idx_ref, p_ref, rank_ref, cnt_ref, carry, *, prompt_tiles):
    tm = lgp_ref.shape[0]

    @pl.when(pl.program_id(0) == 0)
    def _():
        carry[...] = jnp.zeros_like(carry)

    l = jnp.where(pl.program_id(0) < prompt_tiles, lgp_ref[...], lgs_ref[...])
    lane = lax.broadcasted_iota(I32, l.shape, 1)
    lane_f = lane.astype(F32)
    vals, hots = [], []
    idx_out = jnp.zeros(l.shape, F32)
    for k in range(TOP_K):
        m = jnp.max(l, axis=-1, keepdims=True)
        idx = jnp.min(jnp.where(l == m, lane_f, float(LANES)), axis=-1, keepdims=True)
        hot = lane_f == idx
        vals.append(m)
        hots.append(hot)
        idx_out = jnp.where(lane == k, idx, idx_out)
        l = jnp.where(hot, -jnp.inf, l)
    exps = [jnp.exp(v - vals[0]) for v in vals]
    denom = exps[0] + exps[1] + exps[2] + exps[3]
    p_out = jnp.zeros(l.shape, F32)
    for k in range(TOP_K):
        p_out = jnp.where(lane == k, exps[k] / denom, p_out)
    member = jnp.where(hots[0] | hots[1] | hots[2] | hots[3], 1.0, 0.0).astype(F32)
    row = lax.broadcasted_iota(I32, (tm, tm), 0)
    col = lax.broadcasted_iota(I32, (tm, tm), 1)
    before = jnp.where(row > col, 1.0, 0.0).astype(BF16)
    prefix = jnp.dot(before, member.astype(BF16), preferred_element_type=F32) + carry[...]
    rank_out = jnp.zeros(l.shape, F32)
    for k in range(TOP_K):
        r = jnp.sum(jnp.where(hots[k], prefix, 0.0), axis=-1, keepdims=True)
        rank_out = jnp.where(lane == k, r, rank_out)
    carry[...] = carry[...] + jnp.sum(member, axis=0, keepdims=True)
    idx_ref[...] = idx_out.astype(I32)
    p_ref[...] = p_out
    rank_ref[...] = rank_out.astype(I32)
    cnt_ref[...] = carry[...]


def _route(logits_p, logits_s):
    tm = ROUTE_TOKENS
    pt = logits_p.shape[0] // tm
    n = logits_p.shape[0] + logits_s.shape[0]
    tile = pl.BlockSpec((tm, LANES), lambda i: (i, 0))
    return pl.pallas_call(
        functools.partial(_route_kernel, prompt_tiles=pt),
        out_shape=(jax.ShapeDtypeStruct((n, LANES), I32), jax.ShapeDtypeStruct((n, LANES), F32),
                   jax.ShapeDtypeStruct((n, LANES), I32), jax.ShapeDtypeStruct((1, LANES), F32)),
        grid=(n // tm,),
        in_specs=[pl.BlockSpec((tm, LANES), lambda i: (jnp.minimum(i, pt - 1), 0)),
                  pl.BlockSpec((tm, LANES), lambda i: (jnp.maximum(i - pt, 0), 0))],
        out_specs=(tile, tile, tile, pl.BlockSpec((1, LANES), lambda i: (0, 0))),
        scratch_shapes=[pltpu.VMEM((1, LANES), F32)],
        compiler_params=_params(("arbitrary",)),
        name="route",
    )(logits_p, logits_s)


INVERT_TOKENS_MAX = 1024


def _largest_tile(n, unit, cap):
    return max(m for m in range(unit, cap + 1, unit) if n % m == 0)


def _invert_kernel(meta_ref, slot_ref, table_ref, *, pad_token, n_exp, tokens):
    i = pl.program_id(0)
    pairs = tokens * TOP_K

    @pl.when(i == 0)
    def _():
        def fill(r, carry):
            table_ref[r] = pad_token
            return carry

        def fill_group_tail(e, carry):
            lax.fori_loop(meta_ref[e] + meta_ref[2 * n_exp + e], meta_ref[e] + meta_ref[n_exp + e], fill, 0)
            return carry

        lax.fori_loop(0, n_exp, fill_group_tail, 0)
        lax.fori_loop(meta_ref[3 * n_exp] * EXPERT_ROW_TILE, table_ref.shape[0], fill, 0)

    def put(p, carry):
        table_ref[slot_ref[0, p]] = i * tokens + (p >> TOP_K_SHIFT)
        return carry

    lax.fori_loop(0, pairs, put, 0, unroll=16)


def _invert(slots, meta, n_rows, pad_token, n_exp):
    n = slots.shape[0]
    tokens = _largest_tile(n, LANES, INVERT_TOKENS_MAX)
    nt = n // tokens
    pairs = tokens * TOP_K
    return pl.pallas_call(
        functools.partial(_invert_kernel, pad_token=pad_token, n_exp=n_exp, tokens=tokens),
        out_shape=jax.ShapeDtypeStruct((n_rows,), I32),
        grid_spec=pltpu.PrefetchScalarGridSpec(
            num_scalar_prefetch=1,
            grid=(nt,),
            in_specs=[pl.BlockSpec((None, 1, pairs), lambda i, meta: (i, 0, 0), memory_space=pltpu.SMEM)],
            out_specs=pl.BlockSpec(memory_space=pltpu.SMEM)),
        compiler_params=_params(("arbitrary",)),
        name="invert",
    )(meta, slots.reshape(nt, 1, pairs))


ROW_UNROLL = 8
ROW_UNROLL_SHIFT = 3


def _dispatch_kernel(na_ref, split_ref, tok_ref, tok_next_ref, h2p_ref, h2s_ref, xs_ref, buf, sem, *,
                     n_prompt):
    i = pl.program_id(0)
    tile = EXPERT_ROW_TILE
    n_active = na_ref[0]

    def issue(tok, split, slot):
        def prompt_copy(r):
            pltpu.make_async_copy(h2p_ref.at[pl.ds(tok[0, r], 1)], buf.at[slot, pl.ds(r, 1)],
                                  sem.at[slot]).start()

        def sample_copy(r):
            pltpu.make_async_copy(h2s_ref.at[pl.ds(tok[0, r] - n_prompt, 1)], buf.at[slot, pl.ds(r, 1)],
                                  sem.at[slot]).start()

        def rows(lo, hi, fn):
            def body(r, carry):
                fn(r)
                return carry
            lax.fori_loop(lo, hi, body, 0)

        def groups(lo, hi, fn):
            def body(g, carry):
                for u in range(ROW_UNROLL):
                    fn(g * ROW_UNROLL + u)
                return carry
            lax.fori_loop(lo, hi, body, 0)

        whole = split >> ROW_UNROLL_SHIFT
        first = (split + ROW_UNROLL - 1) >> ROW_UNROLL_SHIFT
        groups(0, whole, prompt_copy)
        rows(whole * ROW_UNROLL, split, prompt_copy)
        rows(split, first * ROW_UNROLL, sample_copy)
        groups(first, tile // ROW_UNROLL, sample_copy)

    @pl.when(i == 0)
    def _():
        issue(tok_ref, split_ref[0], 0)

    @pl.when(i + 1 < n_active)
    def _():
        issue(tok_next_ref, split_ref[i + 1], (i + 1) & 1)

    @pl.when(i < n_active)
    def _():
        slot = i & 1
        pltpu.make_async_copy(h2p_ref.at[pl.ds(0, tile)], buf.at[slot], sem.at[slot]).wait()
        xs_ref[...] = buf[slot].astype(BF16)

    @pl.when(i >= n_active)
    def _():
        xs_ref[...] = jnp.zeros_like(xs_ref)


def _dispatch(h2_p, h2_s, table, split, n_active):
    d = h2_p.shape[1]
    tile = EXPERT_ROW_TILE
    n_rows = table.shape[0]
    nt = n_rows // tile
    return pl.pallas_call(
        functools.partial(_dispatch_kernel, n_prompt=h2_p.shape[0]),
        out_shape=jax.ShapeDtypeStruct((n_rows, d), BF16),
        grid_spec=pltpu.PrefetchScalarGridSpec(
            num_scalar_prefetch=2,
            grid=(nt,),
            in_specs=[pl.BlockSpec((None, 1, tile), lambda i, na, sp: (i, 0, 0), memory_space=pltpu.SMEM),
                      pl.BlockSpec((None, 1, tile), lambda i, na, sp: (jnp.minimum(i + 1, nt - 1), 0, 0),
                                   memory_space=pltpu.SMEM),
                      pl.BlockSpec(memory_space=pl.ANY), pl.BlockSpec(memory_space=pl.ANY)],
            out_specs=pl.BlockSpec((tile, d), lambda i, na, sp: (i, 0)),
            scratch_shapes=[pltpu.VMEM((2, tile, d), F32), pltpu.SemaphoreType.DMA((2,))]),
        compiler_params=_params(("arbitrary",)),
        name="dispatch",
    )(n_active, split, table.reshape(nt, 1, tile), table.reshape(nt, 1, tile), h2_p, h2_s)


def _stream_expert_weights(te_ref, first_ref, run_ref, next_ref, meta_ref, n_col_blocks, copies, consume):
    j = pl.program_id(0)
    i = pl.program_id(1)
    n_active = meta_ref[0]
    n_runs = meta_ref[1]

    @pl.when(jnp.logical_and(j == 0, i == 0))
    def _():
        for cp in copies(te_ref[0], 0, 0):
            cp.start()

    @pl.when(jnp.logical_and(i < n_active, first_ref[i] == 1))
    def _():
        slot = (j * n_runs + run_ref[i]) & 1
        for cp in copies(te_ref[i], j, slot):
            cp.wait()
        last = run_ref[i] == n_runs - 1
        e_next = jnp.where(last, te_ref[0], next_ref[i])
        j_next = jnp.where(last, j + 1, j)

        @pl.when(j_next < n_col_blocks)
        def _():
            for cp in copies(e_next, j_next, 1 - slot):
                cp.start()

        consume(slot)


def _gate_up_kernel(te_ref, first_ref, run_ref, next_ref, meta_ref, x_ref, bg_ref, bu_ref, w_hbm, o_ref,
                    wbuf, wg_b, wu_b, sem, *, nj, tn):
    i = pl.program_id(1)
    active = i < meta_ref[0]

    def copies(e, jj, slot):
        return [pltpu.make_async_copy(w_hbm.at[e, :, pl.ds(pl.multiple_of((c * nj + jj) * tn, tn), tn)],
                                      wbuf.at[slot, c], sem.at[slot, c]) for c in range(2)]

    def consume(slot):
        wg_b[...] = wbuf[slot, 0].astype(BF16)
        wu_b[...] = wbuf[slot, 1].astype(BF16)

    _stream_expert_weights(te_ref, first_ref, run_ref, next_ref, meta_ref, nj, copies, consume)

    @pl.when(active)
    def _():
        x = x_ref[...]
        gate = jnp.dot(x, wg_b[...], preferred_element_type=F32) + bg_ref[...]
        up = jnp.dot(x, wu_b[...], preferred_element_type=F32) + bu_ref[...]
        gate = jnp.minimum(gate, SWIGLU_LIMIT)
        up = jnp.clip(up, -SWIGLU_LIMIT, SWIGLU_LIMIT)
        o_ref[...] = ((up + 1.0) * gate * jax.nn.sigmoid(SWIGLU_ALPHA * gate)).astype(BF16)

    @pl.when(jnp.logical_not(active))
    def _():
        o_ref[...] = jnp.zeros_like(o_ref)


def _expert_gate_up(xs, w_gu, b_gu, sched):
    n_rows, d = xs.shape
    n_exp, _, f2 = w_gu.shape
    f = f2 // 2
    tm = EXPERT_ROW_TILE
    tn = 1024
    nj = f // tn
    row = lambda i, meta: jnp.minimum(i, meta[0] - 1)
    return pl.pallas_call(
        functools.partial(_gate_up_kernel, nj=nj, tn=tn),
        out_shape=jax.ShapeDtypeStruct((n_rows, f), BF16),
        grid_spec=pltpu.PrefetchScalarGridSpec(
            num_scalar_prefetch=5,
            grid=(nj, n_rows // tm),
            in_specs=[pl.BlockSpec((tm, d), lambda j, i, te, fi, ru, nx, meta: (row(i, meta), 0)),
                      pl.BlockSpec((None, 1, tn), lambda j, i, te, fi, ru, nx, meta: (te[row(i, meta)], 0, j)),
                      pl.BlockSpec((None, 1, tn),
                                   lambda j, i, te, fi, ru, nx, meta: (te[row(i, meta)], 0, nj + j)),
                      pl.BlockSpec(memory_space=pl.ANY)],
            out_specs=pl.BlockSpec((tm, tn), lambda j, i, te, fi, ru, nx, meta: (i, j)),
            scratch_shapes=[pltpu.VMEM((2, 2, d, tn), F32), pltpu.VMEM((d, tn), BF16),
                            pltpu.VMEM((d, tn), BF16), pltpu.SemaphoreType.DMA((2, 2))]),
        compiler_params=_params(("arbitrary", "arbitrary")),
        name="expert_gate_up",
    )(*sched, xs, b_gu.reshape(n_exp, 1, f2), b_gu.reshape(n_exp, 1, f2), w_gu)


def _down_kernel(te_ref, first_ref, run_ref, next_ref, meta_ref, a_ref, b_ref, w_hbm, o_ref, wbuf, w_b, sem):
    i = pl.program_id(1)
    active = i < meta_ref[0]

    def copies(e, jj, slot):
        return [pltpu.make_async_copy(w_hbm.at[e], wbuf.at[slot], sem.at[slot])]

    def consume(slot):
        w_b[...] = wbuf[slot].astype(BF16)

    _stream_expert_weights(te_ref, first_ref, run_ref, next_ref, meta_ref, 1, copies, consume)

    @pl.when(active)
    def _():
        o_ref[...] = jnp.dot(a_ref[...], w_b[...], preferred_element_type=F32) + b_ref[...]

    @pl.when(jnp.logical_not(active))
    def _():
        o_ref[...] = jnp.zeros_like(o_ref)


def _expert_down(act, w_d, b_d, sched):
    n_rows, f = act.shape
    n_exp, _, d = w_d.shape
    tm = EXPERT_ROW_TILE
    row = lambda i, meta: jnp.minimum(i, meta[0] - 1)
    return pl.pallas_call(
        _down_kernel,
        out_shape=jax.ShapeDtypeStruct((n_rows, d), F32),
        grid_spec=pltpu.PrefetchScalarGridSpec(
            num_scalar_prefetch=5,
            grid=(1, n_rows // tm),
            in_specs=[pl.BlockSpec((tm, f), lambda j, i, te, fi, ru, nx, meta: (row(i, meta), 0)),
                      pl.BlockSpec((None, 1, d), lambda j, i, te, fi, ru, nx, meta: (te[row(i, meta)], 0, 0)),
                      pl.BlockSpec(memory_space=pl.ANY)],
            out_specs=pl.BlockSpec((tm, d), lambda j, i, te, fi, ru, nx, meta: (i, 0)),
            scratch_shapes=[pltpu.VMEM((2, f, d), F32), pltpu.VMEM((f, d), BF16),
                            pltpu.SemaphoreType.DMA((2,))]),
        compiler_params=_params(("arbitrary", "arbitrary")),
        name="expert_down",
    )(*sched, act, b_d.reshape(n_exp, 1, d), w_d)


COMBINE_TOKENS = 256


def _combine_kernel(slot_ref, slot_next_ref, p_ref, x1_ref, gate_ref, g_ref, y_ref, o_ref, buf, sem):
    i = pl.program_id(0)
    tm = x1_ref.shape[0]
    pairs = tm * TOP_K

    def issue(slots, b):
        def body(g, carry):
            for u in range(ROW_UNROLL):
                p = g * ROW_UNROLL + u
                tok = g * (ROW_UNROLL // TOP_K) + u // TOP_K
                pltpu.make_async_copy(y_ref.at[pl.ds(slots[0, p], 1)], buf.at[b, u % TOP_K, pl.ds(tok, 1)],
                                      sem.at[b]).start()
            return carry
        lax.fori_loop(0, pairs // ROW_UNROLL, body, 0)

    @pl.when(i == 0)
    def _():
        issue(slot_ref, 0)

    @pl.when(i + 1 < pl.num_programs(0))
    def _():
        issue(slot_next_ref, (i + 1) & 1)

    b = i & 1
    for k in range(TOP_K):
        pltpu.make_async_copy(y_ref.at[pl.ds(0, tm)], buf.at[b, k], sem.at[b]).wait()
    probs = p_ref[...]
    f = probs[:, 0:1] * buf[b, 0]
    for k in range(1, TOP_K):
        f = f + probs[:, k:k + 1] * buf[b, k]
    o_ref[...] = x1_ref[...] + gate_ref[...] * (_rms(f) * g_ref[...])


def _combine(y, slots, probs, x1, gate, g_post, rows_per_group):
    m, d = x1.shape
    tm = min(m, COMBINE_TOKENS)
    nt = m // tm
    pairs = tm * TOP_K
    return pl.pallas_call(
        _combine_kernel,
        out_shape=jax.ShapeDtypeStruct((m, d), F32),
        grid=(nt,),
        in_specs=[pl.BlockSpec((None, 1, pairs), lambda i: (i, 0, 0), memory_space=pltpu.SMEM),
                  pl.BlockSpec((None, 1, pairs), lambda i: (jnp.minimum(i + 1, nt - 1), 0, 0),
                               memory_space=pltpu.SMEM),
                  pl.BlockSpec((tm, LANES), lambda i: (i, 0)),
                  pl.BlockSpec((tm, d), lambda i: (i, 0)),
                  _mod_spec(gate, tm, rows_per_group),
                  pl.BlockSpec((1, d), lambda i: (0, 0)),
                  pl.BlockSpec(memory_space=pl.ANY)],
        out_specs=pl.BlockSpec((tm, d), lambda i: (i, 0)),
        scratch_shapes=[pltpu.VMEM((2, TOP_K, tm, d), F32), pltpu.SemaphoreType.DMA((2,))],
        compiler_params=_params(("arbitrary",)),
        name="combine",
    )(slots.reshape(nt, 1, pairs), slots.reshape(nt, 1, pairs), probs, x1, gate, g_post.reshape(1, d), y)


def _moe(h2_p, h2_s, logits_p, logits_s, n_exp, w_gu, b_gu, w_d, b_d):
    n = h2_p.shape[0] + h2_s.shape[0]
    tile = EXPERT_ROW_TILE
    top_i, probs, rank, counts = _route(logits_p, logits_s)
    counts = counts[0, :n_exp].astype(I32)
    cap = (counts + tile - 1) // tile * tile
    ends = jnp.cumsum(cap)
    offs = ends - cap
    n_tiles = (n * TOP_K) // tile + n_exp
    n_active = (ends[-1] // tile).astype(I32).reshape(1)
    tile_id = jnp.arange(n_tiles, dtype=I32)
    tile_expert = jnp.minimum(jnp.sum(ends[None, :] <= tile_id[:, None] * tile, axis=1), n_exp - 1).astype(I32)
    first = ((tile_id == 0) | (tile_expert != jnp.roll(tile_expert, 1))) & (tile_id < n_active[0])
    run = jnp.cumsum(first.astype(I32)) - 1
    next_expert = tile_expert[jnp.minimum(ends[tile_expert] // tile, n_tiles - 1)]
    sched = (tile_expert, first.astype(I32), run.astype(I32), next_expert.astype(I32),
             jnp.stack([n_active[0], jnp.sum(first.astype(I32))]).astype(I32))
    expert_ids = jnp.arange(n_exp, dtype=I32)
    pair_offs = jnp.sum(jnp.where(top_i[:, :TOP_K, None] == expert_ids, offs, 0), axis=-1)
    slots = pair_offs + rank[:, :TOP_K]
    n_prompt = h2_p.shape[0]
    meta = jnp.concatenate([offs, cap, counts, n_active]).astype(I32)
    table = _invert(slots, meta, n_tiles * tile, n_prompt, n_exp)
    split = jnp.sum(table.reshape(n_tiles, tile) < n_prompt, axis=1).astype(I32)
    xs = _dispatch(h2_p, h2_s, table, split, n_active)
    act = _expert_gate_up(xs, w_gu, b_gu, sched)
    y = _expert_down(act, w_d, b_d, sched)
    return y, slots, probs


def _layer(xp, xs_, cp_mod, cs_mod, hist_a, hist_q, s0, g_pre_mix, g_post_mix, g_pre_ffn, g_post_ffn,
           w_in, conv_a_w, gdn_conv_w, a_log, dt_bias, g_conv_out, gdn_norm_g, w_out, router_w, router_b,
           w_gu, b_gu, w_d, b_d):
    bsz, seq, d = xp.shape
    ns = xs_.shape[0]
    n_exp = router_w.shape[1]
    dc = N_HEADS * HEAD_D
    d_main = 3 * dc + 3 * dc + dc
    xp2 = xp.reshape(bsz * seq, d)
    xs2 = xs_.reshape(ns, d)

    w_main = w_in.astype(BF16)
    w_ba = jnp.zeros((d, LANES), BF16).at[:, :2 * N_HEADS].set(w_main[:, d_main:])
    w_out_b = w_out.astype(BF16)
    rw = jnp.zeros((d, LANES), F32).at[:, :n_exp].set(router_w)
    rw_hi = rw.astype(BF16)
    rw_lo = (rw - rw_hi.astype(F32)).astype(BF16)
    rb = jnp.full((1, LANES), NEG_BIG, F32).at[0, :n_exp].set(router_b)

    mp = [cp_mod[:, i * d:(i + 1) * d].reshape(bsz, 1, d) for i in range(6)]
    ms = [cs_mod[:, i * d:(i + 1) * d] for i in range(6)]

    proj_p, ba_p = _in_proj(xp2, mp[1], mp[0], g_pre_mix, w_main, w_ba, seq, d_main)
    proj_s, ba_s = _in_proj(xs2, ms[1], ms[0], g_pre_mix, w_main, w_ba, 1, d_main)

    mix_p, ha_p, hq_p, s_p = _mixer_prompt(proj_p, ba_p, bsz, seq, conv_a_w, gdn_conv_w, a_log, dt_bias,
                                           g_conv_out, gdn_norm_g)
    mix_s, ha_s, hq_s, s_s = _mixer_sample(proj_s, ba_s, hist_a.reshape(ns, 2 * dc),
                                           hist_q.reshape(ns, 9 * dc), s0, conv_a_w, gdn_conv_w, a_log,
                                           dt_bias, g_conv_out, gdn_norm_g)

    x1_p, h2_p, lg_p = _post_mix(mix_p, xp2, mp[2], mp[4], mp[3], g_post_mix, g_pre_ffn, w_out_b, rw_hi,
                                 rw_lo, rb, seq)
    x1_s, h2_s, lg_s = _post_mix(mix_s, xs2, ms[2], ms[4], ms[3], g_post_mix, g_pre_ffn, w_out_b, rw_hi,
                                 rw_lo, rb, 1)

    y, slots, probs = _moe(h2_p, h2_s, lg_p, lg_s, n_exp, w_gu, b_gu, w_d, b_d)
    np_ = bsz * seq
    out_p = _combine(y, slots[:np_], probs[:np_], x1_p, mp[5], g_post_ffn, seq)
    out_s = _combine(y, slots[np_:], probs[np_:], x1_s, ms[5], g_post_ffn, 1)
    return (out_p.reshape(bsz, seq, d), out_s.reshape(ns, 1, d), ha_p, hq_p, s_p,
            ha_s.reshape(ns, 2, dc), hq_s.reshape(ns, 3, 3 * dc), s_s)


def kernel(x_prompt, x_sample, state_conv_a, state_gdn_conv, state_gdn_S, c_prompt, c_sample, w_mod, b_mod, g_pre_mix, g_post_mix, g_pre_ffn, g_post_ffn, w_in, conv_a_w, gdn_conv_w, gdn_a_log, gdn_dt_bias, g_conv_out, gdn_norm_g, w_out, router_w, router_b, exp_w_gate_up, exp_b_gate_up, exp_w_down, exp_b_down):
    depth = w_mod.shape[0]
    bp = x_prompt.shape[0]
    xp, xs_ = x_prompt, x_sample
    outs = [[] for _ in range(6)]
    for l in range(depth):
        mod = _modulation(jnp.concatenate([c_prompt, c_sample], axis=0), w_mod[l], b_mod[l])
        res = _layer(xp, xs_, mod[:bp], mod[bp:], state_conv_a[l], state_gdn_conv[l], state_gdn_S[l],
                     g_pre_mix[l], g_post_mix[l], g_pre_ffn[l], g_post_ffn[l], w_in[l], conv_a_w[l],
                     gdn_conv_w[l], gdn_a_log[l], gdn_dt_bias[l], g_conv_out[l], gdn_norm_g[l], w_out[l],
                     router_w[l], router_b[l], exp_w_gate_up[l], exp_b_gate_up[l], exp_w_down[l],
                     exp_b_down[l])
        xp, xs_ = res[0], res[1]
        for acc, r in zip(outs, res[2:]):
            acc.append(r)
    return (xp, xs_) + tuple(o[0][None] if depth == 1 else jnp.stack(o) for o in outs)
```

```python
import functools

import jax
import jax.numpy as jnp
from jax import lax
from jax.experimental import pallas as pl
from jax.experimental.pallas import tpu as pltpu

F32 = jnp.float32
BF16 = jnp.bfloat16
I32 = jnp.int32
HIGHEST = lax.Precision.HIGHEST

EPS = 1e-6
N_HEADS = 8
HEAD_D = 128
TOP_K = 4
TOP_K_SHIFT = 2
SWIGLU_LIMIT = 7.0
SWIGLU_ALPHA = 1.702
CHUNK = 64
LANES = 128
EXPERT_ROW_TILE = 256
MATMUL_COLS = 1024
IN_PROJ_ROWS = 1024
POST_MIX_ROWS = 512
HALO = 8
NEG_BIG = -1e30
VMEM_LIMIT = 56 * 1024 * 1024


def _params(semantics, vmem=VMEM_LIMIT):
    return pltpu.CompilerParams(dimension_semantics=semantics, vmem_limit_bytes=vmem)


def _mm(a, b):
    return jnp.dot(a.astype(BF16), b.astype(BF16), preferred_element_type=F32)


def _mm_nt(a, b):
    return lax.dot_general(a.astype(BF16), b.astype(BF16), (((1,), (1,)), ((), ())),
                           preferred_element_type=F32)


def _rms(x):
    return x * lax.rsqrt(jnp.mean(x * x, axis=-1, keepdims=True) + EPS)


def _silu(x):
    return x * jax.nn.sigmoid(x)


def _softplus(x):
    return jnp.maximum(x, 0.0) + jnp.log1p(jnp.exp(-jnp.abs(x)))


def _mod_kernel(c_ref, w_ref, b_ref, o_ref):
    s = _silu(c_ref[...])
    o_ref[...] = _mm(s, w_ref[...]) + b_ref[...]


def _modulation(c_all, w_mod, b_mod):
    n, d = c_all.shape
    m = w_mod.shape[1]
    tn = MATMUL_COLS
    return pl.pallas_call(
        _mod_kernel,
        out_shape=jax.ShapeDtypeStruct((n, m), F32),
        grid=(m // tn,),
        in_specs=[pl.BlockSpec((n, d), lambda j: (0, 0)),
                  pl.BlockSpec((d, tn), lambda j: (0, j)),
                  pl.BlockSpec((1, tn), lambda j: (0, j))],
        out_specs=pl.BlockSpec((n, tn), lambda j: (0, j)),
        compiler_params=_params(("arbitrary",)),
        name="modulation",
    )(c_all, w_mod, b_mod.reshape(1, m))


def _mod_spec(arr, tm, rows_per_group):
    if arr.ndim == 3:
        tiles = rows_per_group // tm
        return pl.BlockSpec((None, 1, arr.shape[-1]), lambda i, *_: (i // tiles, 0, 0))
    return pl.BlockSpec((tm, arr.shape[-1]), lambda i, *_: (i, 0))


def _proj_kernel(x_ref, sc_ref, sh_ref, g_ref, w_ref, wba_ref, o_ref, ba_ref, h_scr):
    @pl.when(pl.program_id(1) == 0)
    def _():
        h = (_rms(x_ref[...]) * g_ref[...]) * (1.0 + sc_ref[...]) + sh_ref[...]
        hb = h.astype(BF16)
        h_scr[...] = hb
        ba_ref[...] = jnp.dot(hb, wba_ref[...], preferred_element_type=F32)

    o_ref[...] = jnp.dot(h_scr[...], w_ref[...], preferred_element_type=F32)


def _in_proj(x, scale, shift, g, w_main, w_ba, rows_per_group, n):
    m, d = x.shape
    tm = min(m, IN_PROJ_ROWS, rows_per_group if scale.ndim == 3 else m)
    tn = MATMUL_COLS
    return pl.pallas_call(
        _proj_kernel,
        out_shape=(jax.ShapeDtypeStruct((m, n), F32), jax.ShapeDtypeStruct((m, LANES), F32)),
        grid=(m // tm, n // tn),
        in_specs=[pl.BlockSpec((tm, d), lambda i, j: (i, 0)),
                  _mod_spec(scale, tm, rows_per_group),
                  _mod_spec(shift, tm, rows_per_group),
                  pl.BlockSpec((1, d), lambda i, j: (0, 0)),
                  pl.BlockSpec((d, tn), lambda i, j: (0, j)),
                  pl.BlockSpec((d, LANES), lambda i, j: (0, 0))],
        out_specs=(pl.BlockSpec((tm, tn), lambda i, j: (i, j)),
                   pl.BlockSpec((tm, LANES), lambda i, j: (i, 0))),
        scratch_shapes=[pltpu.VMEM((tm, d), BF16)],
        compiler_params=_params(("arbitrary", "arbitrary")),
        name="in_proj",
    )(x, scale, shift, g.reshape(1, d), w_main, w_ba)


PROMPT_SEQS_PER_STEP = 4


def _mixer_prompt_kernel(proj_ref, ba_ref, caw_ref, gcw_ref, alog_ref, dtb_ref, gco_ref, gng_ref,
                         mix_ref, ha_ref, hq_ref, s_ref, extu, extq, qc_scr, s_scr, *, nseq):
    c = CHUNK
    dc = N_HEADS * HEAD_D
    dq = 3 * dc
    t = pl.program_id(1)
    is_last = t == pl.num_programs(1) - 1

    @pl.when(t == 0)
    def _():
        extu[:, 0:HALO, :] = jnp.zeros((nseq, HALO, dc), F32)
        extq[:, 0:HALO, :] = jnp.zeros((nseq, HALO, dq), F32)
        s_scr[...] = jnp.zeros_like(s_scr)

    row = lax.broadcasted_iota(I32, (c, c), 0)
    col = lax.broadcasted_iota(I32, (c, c), 1)
    causal = row >= col
    strict = row > col
    lower = jnp.where(causal, 1.0, 0.0).astype(F32)
    upper = jnp.where(row <= col, 1.0, 0.0).astype(F32)
    caw = caw_ref[...]
    gcw = gcw_ref[...]

    chains = [(sq, h) for sq in range(nseq) for h in range(N_HEADS)]
    heads = range(len(chains))
    qn, kn, vb, kb, kbg, qg, kg, decay, s_decay = ([] for _ in range(9))
    for sq in range(nseq):
        u = proj_ref[sq, :, dc:2 * dc] * proj_ref[sq, :, 2 * dc:3 * dc]
        extu[sq, HALO:HALO + c, :] = u
        ya = (caw[0:1] * extu[sq, HALO - 2:HALO - 2 + c, :] + caw[1:2] * extu[sq, HALO - 1:HALO - 1 + c, :]
              + caw[2:3] * u)
        ya = proj_ref[sq, :, 0:dc] * ya
        mix_ref[sq, :, 0:dc] = (_rms(ya) * gco_ref[...]).astype(BF16)
        last_u = extu[sq, HALO - 2 + c:HALO + c, :]
        extu[sq, HALO - 2:HALO, :] = last_u

        qkv = proj_ref[sq, :, 3 * dc:3 * dc + dq]
        extq[sq, HALO:HALO + c, :] = qkv
        qc = (gcw[0:1] * extq[sq, HALO - 3:HALO - 3 + c, :] + gcw[1:2] * extq[sq, HALO - 2:HALO - 2 + c, :]
              + gcw[2:3] * extq[sq, HALO - 1:HALO - 1 + c, :] + gcw[3:4] * qkv)
        qc_scr[sq] = _silu(qc)
        last_q = extq[sq, HALO - 3 + c:HALO + c, :]
        extq[sq, HALO - 3:HALO, :] = last_q

        @pl.when(is_last)
        def _(sq=sq, last_u=last_u, last_q=last_q):
            ha_ref[sq] = last_u
            hq_ref[sq] = last_q

        ba = ba_ref[sq]
        beta_all = jax.nn.sigmoid(ba)
        g_all = -jnp.exp(alog_ref[...]) * _softplus(ba + dtb_ref[...])
        gc_all = jnp.dot(lower, g_all, precision=HIGHEST, preferred_element_type=F32)
        gc_t = lax.dot_general(g_all, upper, (((0,), (0,)), ((), ())), precision=HIGHEST,
                               preferred_element_type=F32)
        for h in range(N_HEADS):
            lo = h * HEAD_D
            q = qc_scr[sq, :, lo:lo + HEAD_D]
            k = qc_scr[sq, :, dc + lo:dc + lo + HEAD_D]
            v = qc_scr[sq, :, 2 * dc + lo:2 * dc + lo + HEAD_D]
            qn_h = q * lax.rsqrt(jnp.sum(q * q, axis=-1, keepdims=True) + EPS) * (HEAD_D ** -0.5)
            kn_h = k * lax.rsqrt(jnp.sum(k * k, axis=-1, keepdims=True) + EPS)
            beta = beta_all[:, h:h + 1]
            gcc = gc_all[:, N_HEADS + h:N_HEADS + h + 1]
            gcr = gc_t[N_HEADS + h:N_HEADS + h + 1, :]
            gl = gc_all[c - 1:c, N_HEADS + h:N_HEADS + h + 1]
            eg = jnp.exp(gcc)
            kb_h = kn_h * beta
            qn.append(qn_h)
            kn.append(kn_h)
            vb.append(v * beta)
            kb.append(kb_h)
            kbg.append(kb_h * eg)
            qg.append(qn_h * eg)
            kg.append(kn_h * jnp.exp(gl - gcc))
            decay.append(jnp.where(causal, jnp.exp(jnp.minimum(gcc - gcr, 0.0)), 0.0))
            s_decay.append(jnp.exp(gl))

    kq = [_mm_nt(jnp.concatenate([kb[h], qn[h]], axis=0), kn[h]) for h in heads]
    a_mat = [jnp.where(strict, kq[h][:c] * decay[h], 0.0) for h in heads]
    qk = [kq[h][c:] * decay[h] for h in heads]
    n_mat = [-a_mat[h] for h in heads]
    p = a_mat
    size = 2
    while size < c:
        p = [_mm(p[h], p[h]) for h in heads]
        n_p = [_mm(n_mat[h], p[h]) for h in heads]
        n_mat = [n_mat[h] + p[h] + n_p[h] for h in heads]
        size *= 2
    rhs = [jnp.concatenate([vb[h], kbg[h]], axis=-1) for h in heads]
    uw = [rhs[h] + _mm(n_mat[h], rhs[h]) for h in heads]
    s_old = [s_scr[sq, h] for sq, h in chains]
    ws = [_mm(jnp.concatenate([uw[h][:, HEAD_D:], qg[h]], axis=0), s_old[h]) for h in heads]
    v_new = [uw[h][:, :HEAD_D] - ws[h][:c] for h in heads]
    fin = [_mm(jnp.concatenate([qk[h], kg[h].T], axis=0), v_new[h]) for h in heads]
    for i, (sq, h) in enumerate(chains):
        lo = h * HEAD_D
        s_scr[sq, h] = s_old[i] * s_decay[i] + fin[i][c:]
        o = ws[i][c:] + fin[i][:c]
        z = proj_ref[sq, :, 3 * dc + dq + lo:3 * dc + dq + lo + HEAD_D]
        yb = _rms(o) * gng_ref[...] * _silu(z)
        mix_ref[sq, :, dc + lo:dc + lo + HEAD_D] = yb.astype(BF16)

    @pl.when(is_last)
    def _():
        s_ref[...] = s_scr[...]


def _lane_row(vec, offset):
    return jnp.zeros((1, LANES), F32).at[0, offset:offset + vec.shape[0]].set(vec.astype(F32))


def _mixer_prompt(proj, ba, bsz, seq, conv_a_w, gdn_conv_w, a_log, dt_bias, g_conv_out, gdn_norm_g):
    c = CHUNK
    dc = N_HEADS * HEAD_D
    dq = 3 * dc
    dproj = proj.shape[1]
    nt = seq // c
    const = lambda shape: pl.BlockSpec(shape, lambda b, t: (0,) * len(shape))
    nseq = PROMPT_SEQS_PER_STEP if bsz % PROMPT_SEQS_PER_STEP == 0 else 1
    seq_block = lambda *tail: pl.BlockSpec((nseq,) + tail, lambda b, t: (b, t) + (0,) * (len(tail) - 1))
    whole = lambda *tail: pl.BlockSpec((nseq,) + tail, lambda b, t: (b,) + (0,) * len(tail))
    mix, ha, hq, s_fin = pl.pallas_call(
        functools.partial(_mixer_prompt_kernel, nseq=nseq),
        out_shape=(jax.ShapeDtypeStruct((bsz, seq, 2 * dc), BF16),
                   jax.ShapeDtypeStruct((bsz, 2, dc), F32),
                   jax.ShapeDtypeStruct((bsz, 3, dq), F32),
                   jax.ShapeDtypeStruct((bsz, N_HEADS, HEAD_D, HEAD_D), F32)),
        grid=(bsz // nseq, nt),
        in_specs=[seq_block(c, dproj), seq_block(c, LANES),
                  const((3, dc)), const((4, dq)), const((1, LANES)), const((1, LANES)),
                  const((1, dc)), const((1, HEAD_D))],
        out_specs=(seq_block(c, 2 * dc), whole(2, dc), whole(3, dq), whole(N_HEADS, HEAD_D, HEAD_D)),
        scratch_shapes=[pltpu.VMEM((nseq, HALO + c, dc), F32), pltpu.VMEM((nseq, HALO + c, dq), F32),
                        pltpu.VMEM((nseq, c, dq), F32), pltpu.VMEM((nseq, N_HEADS, HEAD_D, HEAD_D), F32)],
        compiler_params=_params(("arbitrary", "arbitrary")),
        name="mixer_prompt",
    )(proj.reshape(bsz, seq, dproj), ba.reshape(bsz, seq, LANES), conv_a_w, gdn_conv_w,
      _lane_row(a_log, N_HEADS), _lane_row(dt_bias, N_HEADS), g_conv_out.reshape(1, dc),
      gdn_norm_g.reshape(1, HEAD_D))
    return mix.reshape(bsz * seq, 2 * dc), ha, hq, s_fin


SAMPLE_GROUP = 16


def _mixer_sample_kernel(proj_ref, ba_ref, hista_ref, histq_ref, s_in_ref, caw_ref, gcw_ref, alog_ref,
                         dtb_ref, gco_ref, gng_ref, mix_ref, ha_ref, hq_ref, s_out_ref, qc_scr, o_scr):
    tb = SAMPLE_GROUP
    dc = N_HEADS * HEAD_D
    dq = 3 * dc

    u = proj_ref[:, dc:2 * dc] * proj_ref[:, 2 * dc:3 * dc]
    caw = caw_ref[...]
    ya = caw[0:1] * hista_ref[:, 0:dc] + caw[1:2] * hista_ref[:, dc:2 * dc] + caw[2:3] * u
    ya = proj_ref[:, 0:dc] * ya
    mix_ref[:, 0:dc] = _rms(ya) * gco_ref[...]
    ha_ref[:, 0:dc] = hista_ref[:, dc:2 * dc]
    ha_ref[:, dc:2 * dc] = u

    qkv = proj_ref[:, 3 * dc:3 * dc + dq]
    gcw = gcw_ref[...]
    qc = (gcw[0:1] * histq_ref[:, 0:dq] + gcw[1:2] * histq_ref[:, dq:2 * dq]
          + gcw[2:3] * histq_ref[:, 2 * dq:3 * dq] + gcw[3:4] * qkv)
    qc_scr[...] = _silu(qc)
    hq_ref[:, 0:dq] = histq_ref[:, dq:2 * dq]
    hq_ref[:, dq:2 * dq] = histq_ref[:, 2 * dq:3 * dq]
    hq_ref[:, 2 * dq:3 * dq] = qkv

    ba = ba_ref[...]
    beta_all = jax.nn.sigmoid(ba)
    eg_all = jnp.exp(-jnp.exp(alog_ref[...]) * _softplus(ba + dtb_ref[...]))

    for h in range(N_HEADS):
        lo = h * HEAD_D
        q = qc_scr[:, lo:lo + HEAD_D]
        k = qc_scr[:, dc + lo:dc + lo + HEAD_D]
        v = qc_scr[:, 2 * dc + lo:2 * dc + lo + HEAD_D]
        qn = q * lax.rsqrt(jnp.sum(q * q, axis=-1, keepdims=True) + EPS) * (HEAD_D ** -0.5)
        kn = k * lax.rsqrt(jnp.sum(k * k, axis=-1, keepdims=True) + EPS)
        qk = jnp.sum(qn * kn, axis=-1, keepdims=True)
        kn_t = kn.T
        qn_t = qn.T
        for b in range(tb):
            s_old = s_in_ref[b, h]
            kc = jnp.broadcast_to(kn_t[:, b:b + 1], s_old.shape)
            e = eg_all[b:b + 1, N_HEADS + h:N_HEADS + h + 1]
            ks = jnp.sum(s_old * kc, axis=0, keepdims=True)
            qs = jnp.sum(s_old * qn_t[:, b:b + 1], axis=0, keepdims=True)
            v_new = beta_all[b:b + 1, h:h + 1] * (v[b:b + 1, :] - e * ks)
            o_scr[b:b + 1, lo:lo + HEAD_D] = e * qs + qk[b:b + 1, :] * v_new
            s_out_ref[b, h] = s_old * e + kc * v_new
        z = proj_ref[:, 3 * dc + dq + lo:3 * dc + dq + lo + HEAD_D]
        o = o_scr[:, lo:lo + HEAD_D]
        mix_ref[:, dc + lo:dc + lo + HEAD_D] = _rms(o) * gng_ref[...] * _silu(z)


def _mixer_sample(proj, ba, hist_a, hist_q, s_in, conv_a_w, gdn_conv_w, a_log, dt_bias, g_conv_out,
                  gdn_norm_g):
    n = proj.shape[0]
    tb = SAMPLE_GROUP
    dc = N_HEADS * HEAD_D
    dq = 3 * dc
    dproj = proj.shape[1]
    const = lambda shape: pl.BlockSpec(shape, lambda i: (0,) * len(shape))
    rows = lambda width: pl.BlockSpec((tb, width), lambda i: (i, 0))
    state = pl.BlockSpec((tb, N_HEADS, HEAD_D, HEAD_D), lambda i: (i, 0, 0, 0))
    return pl.pallas_call(
        _mixer_sample_kernel,
        out_shape=(jax.ShapeDtypeStruct((n, 2 * dc), F32),
                   jax.ShapeDtypeStruct((n, 2 * dc), F32),
                   jax.ShapeDtypeStruct((n, 3 * dq), F32),
                   jax.ShapeDtypeStruct((n, N_HEADS, HEAD_D, HEAD_D), F32)),
        grid=(n // tb,),
        in_specs=[rows(dproj), rows(LANES), rows(2 * dc), rows(3 * dq), state,
                  const((3, dc)), const((4, dq)), const((1, LANES)), const((1, LANES)),
                  const((1, dc)), const((1, HEAD_D))],
        out_specs=(rows(2 * dc), rows(2 * dc), rows(3 * dq), state),
        scratch_shapes=[pltpu.VMEM((tb, dq), F32), pltpu.VMEM((tb, dc), F32)],
        compiler_params=_params(("arbitrary",)),
        name="mixer_sample",
    )(proj, ba, hist_a, hist_q, s_in, conv_a_w, gdn_conv_w, _lane_row(a_log, N_HEADS),
      _lane_row(dt_bias, N_HEADS), g_conv_out.reshape(1, dc), gdn_norm_g.reshape(1, HEAD_D))


POST_MIX_SUB_ROWS = 256


def _post_mix_kernel(mix_ref, x_ref, gate_ref, sc_ref, sh_ref, gpost_ref, gpre_ref, wout_ref,
                     rwh_ref, rwl_ref, rb_ref, x1_ref, h2_ref, lg_ref):
    tm = x_ref.shape[0]
    sub = min(tm, POST_MIX_SUB_ROWS)
    tiles = [slice(r0, r0 + sub) for r0 in range(0, tm, sub)]
    mixes = [jnp.dot(mix_ref[rows].astype(BF16), wout_ref[...], preferred_element_type=F32)
             for rows in tiles]
    for rows, mix in zip(tiles, mixes):
        per_row = lambda ref: ref[rows] if ref.shape[0] == tm else ref[...]
        x1 = x_ref[rows] + per_row(gate_ref) * (_rms(mix) * gpost_ref[...])
        x1_ref[rows] = x1
        h2 = (_rms(x1) * gpre_ref[...]) * (1.0 + per_row(sc_ref)) + per_row(sh_ref)
        h2_ref[rows] = h2
        hi = h2.astype(BF16)
        lo = (h2 - hi.astype(F32)).astype(BF16)
        rwh = rwh_ref[...]
        lg_ref[rows] = (jnp.dot(hi, rwh, preferred_element_type=F32)
                        + jnp.dot(lo, rwh, preferred_element_type=F32)
                        + jnp.dot(hi, rwl_ref[...], preferred_element_type=F32) + rb_ref[...])


def _post_mix(mix_in, x, gate, scale, shift, g_post, g_pre, w_out, rw_hi, rw_lo, rb, rows_per_group):
    m, d = x.shape
    tm = min(m, POST_MIX_ROWS, rows_per_group if gate.ndim == 3 else m)
    const = lambda shape: pl.BlockSpec(shape, lambda i: (0,) * len(shape))
    rows = lambda width: pl.BlockSpec((tm, width), lambda i: (i, 0))
    return pl.pallas_call(
        _post_mix_kernel,
        out_shape=(jax.ShapeDtypeStruct((m, d), F32), jax.ShapeDtypeStruct((m, d), F32),
                   jax.ShapeDtypeStruct((m, LANES), F32)),
        grid=(m // tm,),
        in_specs=[rows(d), rows(d),
                  _mod_spec(gate, tm, rows_per_group),
                  _mod_spec(scale, tm, rows_per_group),
                  _mod_spec(shift, tm, rows_per_group),
                  const((1, d)), const((1, d)), const((d, d)),
                  const((d, LANES)), const((d, LANES)), const((1, LANES))],
        out_specs=(rows(d), rows(d), rows(LANES)),
        compiler_params=_params(("arbitrary",)),
        name="post_mix",
    )(mix_in, x, gate, scale, shift, g_post.reshape(1, d), g_pre.reshape(1, d), w_out, rw_hi, rw_lo, rb)


ROUTE_TOKENS = 128


def _route_kernel(lgp_ref, lgs_ref, idx_ref, p_ref, rank_ref, cnt_ref, carry, *, prompt_tiles):
    tm = lgp_ref.shape[0]

    @pl.when(pl.program_id(0) == 0)
    def _():
        carry[...] = jnp.zeros_like(carry)

    l = jnp.where(pl.program_id(0) < prompt_tiles, lgp_ref[...], lgs_ref[...])
    lane = lax.broadcasted_iota(I32, l.shape, 1)
    lane_f = lane.astype(F32)
    vals, hots = [], []
    idx_out = jnp.zeros(l.shape, F32)
    for k in range(TOP_K):
        m = jnp.max(l, axis=-1, keepdims=True)
        idx = jnp.min(jnp.where(l == m, lane_f, float(LANES)), axis=-1, keepdims=True)
        hot = lane_f == idx
        vals.append(m)
        hots.append(hot)
        idx_out = jnp.where(lane == k, idx, idx_out)
        l = jnp.where(hot, -jnp.inf, l)
    exps = [jnp.exp(v - vals[0]) for v in vals]
    denom = exps[0] + exps[1] + exps[2] + exps[3]
    p_out = jnp.zeros(l.shape, F32)
    for k in range(TOP_K):
        p_out = jnp.where(lane == k, exps[k] / denom, p_out)
    member = jnp.where(hots[0] | hots[1] | hots[2] | hots[3], 1.0, 0.0).astype(F32)
    row = lax.broadcasted_iota(I32, (tm, tm), 0)
    col = lax.broadcasted_iota(I32, (tm, tm), 1)
    before = jnp.where(row > col, 1.0, 0.0).astype(BF16)
    prefix = jnp.dot(before, member.astype(BF16), preferred_element_type=F32) + carry[...]
    rank_out = jnp.zeros(l.shape, F32)
    for k in range(TOP_K):
        r = jnp.sum(jnp.where(hots[k], prefix, 0.0), axis=-1, keepdims=True)
        rank_out = jnp.where(lane == k, r, rank_out)
    carry[...] = carry[...] + jnp.sum(member, axis=0, keepdims=True)
    idx_ref[...] = idx_out.astype(I32)
    p_ref[...] = p_out
    rank_ref[...] = rank_out.astype(I32)
    cnt_ref[...] = carry[...]


def _route(logits_p, logits_s):
    tm = ROUTE_TOKENS
    pt = logits_p.shape[0] // tm
    n = logits_p.shape[0] + logits_s.shape[0]
    tile = pl.BlockSpec((tm, LANES), lambda i: (i, 0))
    return pl.pallas_call(
        functools.partial(_route_kernel, prompt_tiles=pt),
        out_shape=(jax.ShapeDtypeStruct((n, LANES), I32), jax.ShapeDtypeStruct((n, LANES), F32),
                   jax.ShapeDtypeStruct((n, LANES), I32), jax.ShapeDtypeStruct((1, LANES), F32)),
        grid=(n // tm,),
        in_specs=[pl.BlockSpec((tm, LANES), lambda i: (jnp.minimum(i, pt - 1), 0)),
                  pl.BlockSpec((tm, LANES), lambda i: (jnp.maximum(i - pt, 0), 0))],
        out_specs=(tile, tile, tile, pl.BlockSpec((1, LANES), lambda i: (0, 0))),
        scratch_shapes=[pltpu.VMEM((1, LANES), F32)],
        compiler_params=_params(("arbitrary",)),
        name="route",
    )(logits_p, logits_s)


INVERT_TOKENS_MAX = 1024


def _largest_tile(n, unit, cap):
    return max(m for m in range(unit, cap + 1, unit) if n % m == 0)


def _invert_kernel(meta_ref, slot_ref, table_ref, *, pad_token, n_exp, tokens):
    i = pl.program_id(0)
    pairs = tokens * TOP_K

    @pl.when(i == 0)
    def _():
        def fill(r, carry):
            table_ref[r] = pad_token
            return carry

        def fill_group_tail(e, carry):
            lax.fori_loop(meta_ref[e] + meta_ref[2 * n_exp + e], meta_ref[e] + meta_ref[n_exp + e], fill, 0)
            return carry

        lax.fori_loop(0, n_exp, fill_group_tail, 0)
        lax.fori_loop(meta_ref[3 * n_exp] * EXPERT_ROW_TILE, table_ref.shape[0], fill, 0)

    def put(p, carry):
        table_ref[slot_ref[0, p]] = i * tokens + (p >> TOP_K_SHIFT)
        return carry

    lax.fori_loop(0, pairs, put, 0, unroll=16)


def _invert(slots, meta, n_rows, pad_token, n_exp):
    n = slots.shape[0]
    tokens = _largest_tile(n, LANES, INVERT_TOKENS_MAX)
    nt = n // tokens
    pairs = tokens * TOP_K
    return pl.pallas_call(
        functools.partial(_invert_kernel, pad_token=pad_token, n_exp=n_exp, tokens=tokens),
        out_shape=jax.ShapeDtypeStruct((n_rows,), I32),
        grid_spec=pltpu.PrefetchScalarGridSpec(
            num_scalar_prefetch=1,
            grid=(nt,),
            in_specs=[pl.BlockSpec((None, 1, pairs), lambda i, meta: (i, 0, 0), memory_space=pltpu.SMEM)],
            out_specs=pl.BlockSpec(memory_space=pltpu.SMEM)),
        compiler_params=_params(("arbitrary",)),
        name="invert",
    )(meta, slots.reshape(nt, 1, pairs))


ROW_UNROLL = 8
ROW_UNROLL_SHIFT = 3


def _dispatch_kernel(na_ref, split_ref, tok_ref, tok_next_ref, h2p_ref, h2s_ref, xs_ref, buf, sem, *,
                     n_prompt):
    i = pl.program_id(0)
    tile = EXPERT_ROW_TILE
    n_active = na_ref[0]

    def issue(tok, split, slot):
        def prompt_copy(r):
            pltpu.make_async_copy(h2p_ref.at[pl.ds(tok[0, r], 1)], buf.at[slot, pl.ds(r, 1)],
                                  sem.at[slot]).start()

        def sample_copy(r):
            pltpu.make_async_copy(h2s_ref.at[pl.ds(tok[0, r] - n_prompt, 1)], buf.at[slot, pl.ds(r, 1)],
                                  sem.at[slot]).start()

        def rows(lo, hi, fn):
            def body(r, carry):
                fn(r)
                return carry
            lax.fori_loop(lo, hi, body, 0)

        def groups(lo, hi, fn):
            def body(g, carry):
                for u in range(ROW_UNROLL):
                    fn(g * ROW_UNROLL + u)
                return carry
            lax.fori_loop(lo, hi, body, 0)

        whole = split >> ROW_UNROLL_SHIFT
        first = (split + ROW_UNROLL - 1) >> ROW_UNROLL_SHIFT
        groups(0, whole, prompt_copy)
        rows(whole * ROW_UNROLL, split, prompt_copy)
        rows(split, first * ROW_UNROLL, sample_copy)
        groups(first, tile // ROW_UNROLL, sample_copy)

    @pl.when(i == 0)
    def _():
        issue(tok_ref, split_ref[0], 0)

    @pl.when(i + 1 < n_active)
    def _():
        issue(tok_next_ref, split_ref[i + 1], (i + 1) & 1)

    @pl.when(i < n_active)
    def _():
        slot = i & 1
        pltpu.make_async_copy(h2p_ref.at[pl.ds(0, tile)], buf.at[slot], sem.at[slot]).wait()
        xs_ref[...] = buf[slot].astype(BF16)

    @pl.when(i >= n_active)
    def _():
        xs_ref[...] = jnp.zeros_like(xs_ref)


def _dispatch(h2_p, h2_s, table, split, n_active):
    d = h2_p.shape[1]
    tile = EXPERT_ROW_TILE
    n_rows = table.shape[0]
    nt = n_rows // tile
    return pl.pallas_call(
        functools.partial(_dispatch_kernel, n_prompt=h2_p.shape[0]),
        out_shape=jax.ShapeDtypeStruct((n_rows, d), BF16),
        grid_spec=pltpu.PrefetchScalarGridSpec(
            num_scalar_prefetch=2,
            grid=(nt,),
            in_specs=[pl.BlockSpec((None, 1, tile), lambda i, na, sp: (i, 0, 0), memory_space=pltpu.SMEM),
                      pl.BlockSpec((None, 1, tile), lambda i, na, sp: (jnp.minimum(i + 1, nt - 1), 0, 0),
                                   memory_space=pltpu.SMEM),
                      pl.BlockSpec(memory_space=pl.ANY), pl.BlockSpec(memory_space=pl.ANY)],
            out_specs=pl.BlockSpec((tile, d), lambda i, na, sp: (i, 0)),
            scratch_shapes=[pltpu.VMEM((2, tile, d), F32), pltpu.SemaphoreType.DMA((2,))]),
        compiler_params=_params(("arbitrary",)),
        name="dispatch",
    )(n_active, split, table.reshape(nt, 1, tile), table.reshape(nt, 1, tile), h2_p, h2_s)


def _stream_expert_weights(te_ref, first_ref, run_ref, next_ref, meta_ref, n_col_blocks, copies, consume):
    j = pl.program_id(0)
    i = pl.program_id(1)
    n_active = meta_ref[0]
    n_runs = meta_ref[1]

    @pl.when(jnp.logical_and(j == 0, i == 0))
    def _():
        for cp in copies(te_ref[0], 0, 0):
            cp.start()

    @pl.when(jnp.logical_and(i < n_active, first_ref[i] == 1))
    def _():
        slot = (j * n_runs + run_ref[i]) & 1
        for cp in copies(te_ref[i], j, slot):
            cp.wait()
        last = run_ref[i] == n_runs - 1
        e_next = jnp.where(last, te_ref[0], next_ref[i])
        j_next = jnp.where(last, j + 1, j)

        @pl.when(j_next < n_col_blocks)
        def _():
            for cp in copies(e_next, j_next, 1 - slot):
                cp.start()

        consume(slot)


def _gate_up_kernel(te_ref, first_ref, run_ref, next_ref, meta_ref, x_ref, bg_ref, bu_ref, w_hbm, o_ref,
                    wbuf, wg_b, wu_b, sem, *, nj, tn):
    i = pl.program_id(1)
    active = i < meta_ref[0]

    def copies(e, jj, slot):
        return [pltpu.make_async_copy(w_hbm.at[e, :, pl.ds(pl.multiple_of((c * nj + jj) * tn, tn), tn)],
                                      wbuf.at[slot, c], sem.at[slot, c]) for c in range(2)]

    def consume(slot):
        wg_b[...] = wbuf[slot, 0].astype(BF16)
        wu_b[...] = wbuf[slot, 1].astype(BF16)

    _stream_expert_weights(te_ref, first_ref, run_ref, next_ref, meta_ref, nj, copies, consume)

    @pl.when(active)
    def _():
        x = x_ref[...]
        gate = jnp.dot(x, wg_b[...], preferred_element_type=F32) + bg_ref[...]
        up = jnp.dot(x, wu_b[...], preferred_element_type=F32) + bu_ref[...]
        gate = jnp.minimum(gate, SWIGLU_LIMIT)
        up = jnp.clip(up, -SWIGLU_LIMIT, SWIGLU_LIMIT)
        o_ref[...] = ((up + 1.0) * gate * jax.nn.sigmoid(SWIGLU_ALPHA * gate)).astype(BF16)

    @pl.when(jnp.logical_not(active))
    def _():
        o_ref[...] = jnp.zeros_like(o_ref)


def _expert_gate_up(xs, w_gu, b_gu, sched):
    n_rows, d = xs.shape
    n_exp, _, f2 = w_gu.shape
    f = f2 // 2
    tm = EXPERT_ROW_TILE
    tn = MATMUL_COLS
    nj = f // tn
    row = lambda i, meta: jnp.minimum(i, meta[0] - 1)
    return pl.pallas_call(
        functools.partial(_gate_up_kernel, nj=nj, tn=tn),
        out_shape=jax.ShapeDtypeStruct((n_rows, f), BF16),
        grid_spec=pltpu.PrefetchScalarGridSpec(
            num_scalar_prefetch=5,
            grid=(nj, n_rows // tm),
            in_specs=[pl.BlockSpec((tm, d), lambda j, i, te, fi, ru, nx, meta: (row(i, meta), 0)),
                      pl.BlockSpec((None, 1, tn), lambda j, i, te, fi, ru, nx, meta: (te[row(i, meta)], 0, j)),
                      pl.BlockSpec((None, 1, tn),
                                   lambda j, i, te, fi, ru, nx, meta: (te[row(i, meta)], 0, nj + j)),
                      pl.BlockSpec(memory_space=pl.ANY)],
            out_specs=pl.BlockSpec((tm, tn), lambda j, i, te, fi, ru, nx, meta: (i, j)),
            scratch_shapes=[pltpu.VMEM((2, 2, d, tn), F32), pltpu.VMEM((d, tn), BF16),
                            pltpu.VMEM((d, tn), BF16), pltpu.SemaphoreType.DMA((2, 2))]),
        compiler_params=_params(("arbitrary", "arbitrary")),
        name="expert_gate_up",
    )(*sched, xs, b_gu.reshape(n_exp, 1, f2), b_gu.reshape(n_exp, 1, f2), w_gu)


def _down_kernel(te_ref, first_ref, run_ref, next_ref, meta_ref, a_ref, b_ref, w_hbm, o_ref, wbuf, w_b, sem):
    i = pl.program_id(1)
    active = i < meta_ref[0]

    def copies(e, jj, slot):
        return [pltpu.make_async_copy(w_hbm.at[e], wbuf.at[slot], sem.at[slot])]

    def consume(slot):
        w_b[...] = wbuf[slot].astype(BF16)

    _stream_expert_weights(te_ref, first_ref, run_ref, next_ref, meta_ref, 1, copies, consume)

    @pl.when(active)
    def _():
        o_ref[...] = jnp.dot(a_ref[...], w_b[...], preferred_element_type=F32) + b_ref[...]

    @pl.when(jnp.logical_not(active))
    def _():
        o_ref[...] = jnp.zeros_like(o_ref)


def _expert_down(act, w_d, b_d, sched):
    n_rows, f = act.shape
    n_exp, _, d = w_d.shape
    tm = EXPERT_ROW_TILE
    row = lambda i, meta: jnp.minimum(i, meta[0] - 1)
    return pl.pallas_call(
        _down_kernel,
        out_shape=jax.ShapeDtypeStruct((n_rows, d), F32),
        grid_spec=pltpu.PrefetchScalarGridSpec(
            num_scalar_prefetch=5,
            grid=(1, n_rows // tm),
            in_specs=[pl.BlockSpec((tm, f), lambda j, i, te, fi, ru, nx, meta: (row(i, meta), 0)),
                      pl.BlockSpec((None, 1, d), lambda j, i, te, fi, ru, nx, meta: (te[row(i, meta)], 0, 0)),
                      pl.BlockSpec(memory_space=pl.ANY)],
            out_specs=pl.BlockSpec((tm, d), lambda j, i, te, fi, ru, nx, meta: (i, 0)),
            scratch_shapes=[pltpu.VMEM((2, f, d), F32), pltpu.VMEM((f, d), BF16),
                            pltpu.SemaphoreType.DMA((2,))]),
        compiler_params=_params(("arbitrary", "arbitrary")),
        name="expert_down",
    )(*sched, act, b_d.reshape(n_exp, 1, d), w_d)


COMBINE_TOKENS = 256


def _combine_kernel(slot_ref, slot_next_ref, p_ref, x1_ref, gate_ref, g_ref, y_ref, o_ref, buf, sem):
    i = pl.program_id(0)
    tm = x1_ref.shape[0]
    pairs = tm * TOP_K

    def issue(slots, b):
        def body(g, carry):
            for u in range(ROW_UNROLL):
                p = g * ROW_UNROLL + u
                tok = g * (ROW_UNROLL // TOP_K) + u // TOP_K
                pltpu.make_async_copy(y_ref.at[pl.ds(slots[0, p], 1)], buf.at[b, u % TOP_K, pl.ds(tok, 1)],
                                      sem.at[b]).start()
            return carry
        lax.fori_loop(0, pairs // ROW_UNROLL, body, 0)

    @pl.when(i == 0)
    def _():
        issue(slot_ref, 0)

    @pl.when(i + 1 < pl.num_programs(0))
    def _():
        issue(slot_next_ref, (i + 1) & 1)

    b = i & 1
    for k in range(TOP_K):
        pltpu.make_async_copy(y_ref.at[pl.ds(0, tm)], buf.at[b, k], sem.at[b]).wait()
    probs = p_ref[...]
    f = probs[:, 0:1] * buf[b, 0]
    for k in range(1, TOP_K):
        f = f + probs[:, k:k + 1] * buf[b, k]
    o_ref[...] = x1_ref[...] + gate_ref[...] * (_rms(f) * g_ref[...])


def _combine(y, slots, probs, x1, gate, g_post, rows_per_group):
    m, d = x1.shape
    tm = min(m, COMBINE_TOKENS)
    nt = m // tm
    pairs = tm * TOP_K
    return pl.pallas_call(
        _combine_kernel,
        out_shape=jax.ShapeDtypeStruct((m, d), F32),
        grid=(nt,),
        in_specs=[pl.BlockSpec((None, 1, pairs), lambda i: (i, 0, 0), memory_space=pltpu.SMEM),
                  pl.BlockSpec((None, 1, pairs), lambda i: (jnp.minimum(i + 1, nt - 1), 0, 0),
                               memory_space=pltpu.SMEM),
                  pl.BlockSpec((tm, LANES), lambda i: (i, 0)),
                  pl.BlockSpec((tm, d), lambda i: (i, 0)),
                  _mod_spec(gate, tm, rows_per_group),
                  pl.BlockSpec((1, d), lambda i: (0, 0)),
                  pl.BlockSpec(memory_space=pl.ANY)],
        out_specs=pl.BlockSpec((tm, d), lambda i: (i, 0)),
        scratch_shapes=[pltpu.VMEM((2, TOP_K, tm, d), F32), pltpu.SemaphoreType.DMA((2,))],
        compiler_params=_params(("arbitrary",)),
        name="combine",
    )(slots.reshape(nt, 1, pairs), slots.reshape(nt, 1, pairs), probs, x1, gate, g_post.reshape(1, d), y)


def _moe(h2_p, h2_s, logits_p, logits_s, n_exp, w_gu, b_gu, w_d, b_d):
    n = h2_p.shape[0] + h2_s.shape[0]
    tile = EXPERT_ROW_TILE
    top_i, probs, rank, counts = _route(logits_p, logits_s)
    counts = counts[0, :n_exp].astype(I32)
    cap = (counts + tile - 1) // tile * tile
    ends = jnp.cumsum(cap)
    offs = ends - cap
    n_tiles = (n * TOP_K) // tile + n_exp
    n_active = (ends[-1] // tile).astype(I32).reshape(1)
    tile_id = jnp.arange(n_tiles, dtype=I32)
    tile_expert = jnp.minimum(jnp.sum(ends[None, :] <= tile_id[:, None] * tile, axis=1), n_exp - 1).astype(I32)
    first = ((tile_id == 0) | (tile_expert != jnp.roll(tile_expert, 1))) & (tile_id < n_active[0])
    run = jnp.cumsum(first.astype(I32)) - 1
    next_expert = tile_expert[jnp.minimum(ends[tile_expert] // tile, n_tiles - 1)]
    sched = (tile_expert, first.astype(I32), run.astype(I32), next_expert.astype(I32),
             jnp.stack([n_active[0], jnp.sum(first.astype(I32))]).astype(I32))
    expert_ids = jnp.arange(n_exp, dtype=I32)
    pair_offs = jnp.sum(jnp.where(top_i[:, :TOP_K, None] == expert_ids, offs, 0), axis=-1)
    slots = pair_offs + rank[:, :TOP_K]
    n_prompt = h2_p.shape[0]
    meta = jnp.concatenate([offs, cap, counts, n_active]).astype(I32)
    table = _invert(slots, meta, n_tiles * tile, n_prompt, n_exp)
    split = jnp.sum(table.reshape(n_tiles, tile) < n_prompt, axis=1).astype(I32)
    xs = _dispatch(h2_p, h2_s, table, split, n_active)
    act = _expert_gate_up(xs, w_gu, b_gu, sched)
    y = _expert_down(act, w_d, b_d, sched)
    return y, slots, probs


def _layer(xp, xs_, cp_mod, cs_mod, hist_a, hist_q, s0, g_pre_mix, g_post_mix, g_pre_ffn, g_post_ffn,
           w_in, conv_a_w, gdn_conv_w, a_log, dt_bias, g_conv_out, gdn_norm_g, w_out, router_w, router_b,
           w_gu, b_gu, w_d, b_d):
    bsz, seq, d = xp.shape
    ns = xs_.shape[0]
    n_exp = router_w.shape[1]
    dc = N_HEADS * HEAD_D
    d_main = 3 * dc + 3 * dc + dc
    xp2 = xp.reshape(bsz * seq, d)
    xs2 = xs_.reshape(ns, d)

    w_main = w_in.astype(BF16)
    w_ba = jnp.zeros((d, LANES), BF16).at[:, :2 * N_HEADS].set(w_main[:, d_main:])
    w_out_b = w_out.astype(BF16)
    rw = jnp.zeros((d, LANES), F32).at[:, :n_exp].set(router_w)
    rw_hi = rw.astype(BF16)
    rw_lo = (rw - rw_hi.astype(F32)).astype(BF16)
    rb = jnp.full((1, LANES), NEG_BIG, F32).at[0, :n_exp].set(router_b)

    mp = [cp_mod[:, i * d:(i + 1) * d].reshape(bsz, 1, d) for i in range(6)]
    ms = [cs_mod[:, i * d:(i + 1) * d] for i in range(6)]

    proj_p, ba_p = _in_proj(xp2, mp[1], mp[0], g_pre_mix, w_main, w_ba, seq, d_main)
    proj_s, ba_s = _in_proj(xs2, ms[1], ms[0], g_pre_mix, w_main, w_ba, 1, d_main)

    mix_p, ha_p, hq_p, s_p = _mixer_prompt(proj_p, ba_p, bsz, seq, conv_a_w, gdn_conv_w, a_log, dt_bias,
                                           g_conv_out, gdn_norm_g)
    mix_s, ha_s, hq_s, s_s = _mixer_sample(proj_s, ba_s, hist_a.reshape(ns, 2 * dc),
                                           hist_q.reshape(ns, 9 * dc), s0, conv_a_w, gdn_conv_w, a_log,
                                           dt_bias, g_conv_out, gdn_norm_g)

    x1_p, h2_p, lg_p = _post_mix(mix_p, xp2, mp[2], mp[4], mp[3], g_post_mix, g_pre_ffn, w_out_b, rw_hi,
                                 rw_lo, rb, seq)
    x1_s, h2_s, lg_s = _post_mix(mix_s, xs2, ms[2], ms[4], ms[3], g_post_mix, g_pre_ffn, w_out_b, rw_hi,
                                 rw_lo, rb, 1)

    y, slots, probs = _moe(h2_p, h2_s, lg_p, lg_s, n_exp, w_gu, b_gu, w_d, b_d)
    np_ = bsz * seq
    out_p = _combine(y, slots[:np_], probs[:np_], x1_p, mp[5], g_post_ffn, seq)
    out_s = _combine(y, slots[np_:], probs[np_:], x1_s, ms[5], g_post_ffn, 1)
    return (out_p.reshape(bsz, seq, d), out_s.reshape(ns, 1, d), ha_p, hq_p, s_p,
            ha_s.reshape(ns, 2, dc), hq_s.reshape(ns, 3, 3 * dc), s_s)


def kernel(x_prompt, x_sample, state_conv_a, state_gdn_conv, state_gdn_S, c_prompt, c_sample, w_mod, b_mod, g_pre_mix, g_post_mix, g_pre_ffn, g_post_ffn, w_in, conv_a_w, gdn_conv_w, gdn_a_log, gdn_dt_bias, g_conv_out, gdn_norm_g, w_out, router_w, router_b, exp_w_gate_up, exp_b_gate_up, exp_w_down, exp_b_down):
    depth = w_mod.shape[0]
    bp = x_prompt.shape[0]
    xp, xs_ = x_prompt, x_sample
    outs = [[] for _ in range(6)]
    for l in range(depth):
        mod = _modulation(jnp.concatenate([c_prompt, c_sample], axis=0), w_mod[l], b_mod[l])
        res = _layer(xp, xs_, mod[:bp], mod[bp:], state_conv_a[l], state_gdn_conv[l], state_gdn_S[l],
                     g_pre_mix[l], g_post_mix[l], g_pre_ffn[l], g_post_ffn[l], w_in[l], conv_a_w[l],
                     gdn_conv_w[l], gdn_a_log[l], gdn_dt_bias[l], g_conv_out[l], gdn_norm_g[l], w_out[l],
                     router_w[l], router_b[l], exp_w_gate_up[l], exp_b_gate_up[l], exp_w_down[l],
                     exp_b_down[l])
        xp, xs_ = res[0], res[1]
        for acc, r in zip(outs, res[2:]):
            acc.append(r)
    return (xp, xs_) + tuple(o[0][None] if depth == 1 else jnp.stack(o) for o in outs)
```

```python
import functools

import jax
import jax.numpy as jnp
from jax import lax
from jax.experimental import pallas as pl
from jax.experimental.pallas import tpu as pltpu

F32 = jnp.float32
BF16 = jnp.bfloat16
I32 = jnp.int32
HIGHEST = lax.Precision.HIGHEST

EPS = 1e-6
N_HEADS = 8
HEAD_D = 128
TOP_K = 4
TOP_K_SHIFT = 2
SWIGLU_LIMIT = 7.0
SWIGLU_ALPHA = 1.702
CHUNK = 64
LANES = 128
EXPERT_ROW_TILE = 256
MATMUL_COLS = 1024
IN_PROJ_ROWS = 1024
POST_MIX_ROWS = 512
HALO = 8
NEG_BIG = -1e30
VMEM_LIMIT = 56 * 1024 * 1024


def _params(semantics, vmem=VMEM_LIMIT):
    return pltpu.CompilerParams(dimension_semantics=semantics, vmem_limit_bytes=vmem)


def _mm(a, b):
    return jnp.dot(a.astype(BF16), b.astype(BF16), preferred_element_type=F32)


def _mm_nt(a, b):
    return lax.dot_general(a.astype(BF16), b.astype(BF16), (((1,), (1,)), ((), ())),
                           preferred_element_type=F32)


def _rms(x):
    return x * lax.rsqrt(jnp.mean(x * x, axis=-1, keepdims=True) + EPS)


def _silu(x):
    return x * jax.nn.sigmoid(x)


def _softplus(x):
    return jnp.maximum(x, 0.0) + jnp.log1p(jnp.exp(-jnp.abs(x)))


def _mod_kernel(c_ref, w_ref, b_ref, o_ref):
    s = _silu(c_ref[...])
    o_ref[...] = _mm(s, w_ref[...]) + b_ref[...]


def _modulation(c_all, w_mod, b_mod):
    n, d = c_all.shape
    m = w_mod.shape[1]
    tn = MATMUL_COLS
    return pl.pallas_call(
        _mod_kernel,
        out_shape=jax.ShapeDtypeStruct((n, m), F32),
        grid=(m // tn,),
        in_specs=[pl.BlockSpec((n, d), lambda j: (0, 0)),
                  pl.BlockSpec((d, tn), lambda j: (0, j)),
                  pl.BlockSpec((1, tn), lambda j: (0, j))],
        out_specs=pl.BlockSpec((n, tn), lambda j: (0, j)),
        compiler_params=_params(("arbitrary",)),
        name="modulation",
    )(c_all, w_mod, b_mod.reshape(1, m))


def _mod_spec(arr, tm, rows_per_group):
    if arr.ndim == 3:
        tiles = rows_per_group // tm
        return pl.BlockSpec((None, 1, arr.shape[-1]), lambda i, *_: (i // tiles, 0, 0))
    return pl.BlockSpec((tm, arr.shape[-1]), lambda i, *_: (i, 0))


def _proj_kernel(x_ref, sc_ref, sh_ref, g_ref, w_ref, wba_ref, o_ref, ba_ref, h_scr):
    @pl.when(pl.program_id(1) == 0)
    def _():
        h = (_rms(x_ref[...]) * g_ref[...]) * (1.0 + sc_ref[...]) + sh_ref[...]
        hb = h.astype(BF16)
        h_scr[...] = hb
        ba_ref[...] = jnp.dot(hb, wba_ref[...], preferred_element_type=F32)

    o_ref[...] = jnp.dot(h_scr[...], w_ref[...], preferred_element_type=F32)


def _in_proj(x, scale, shift, g, w_main, w_ba, rows_per_group, n):
    m, d = x.shape
    tm = min(m, IN_PROJ_ROWS, rows_per_group if scale.ndim == 3 else m)
    tn = MATMUL_COLS
    return pl.pallas_call(
        _proj_kernel,
        out_shape=(jax.ShapeDtypeStruct((m, n), F32), jax.ShapeDtypeStruct((m, LANES), F32)),
        grid=(m // tm, n // tn),
        in_specs=[pl.BlockSpec((tm, d), lambda i, j: (i, 0)),
                  _mod_spec(scale, tm, rows_per_group),
                  _mod_spec(shift, tm, rows_per_group),
                  pl.BlockSpec((1, d), lambda i, j: (0, 0)),
                  pl.BlockSpec((d, tn), lambda i, j: (0, j)),
                  pl.BlockSpec((d, LANES), lambda i, j: (0, 0))],
        out_specs=(pl.BlockSpec((tm, tn), lambda i, j: (i, j)),
                   pl.BlockSpec((tm, LANES), lambda i, j: (i, 0))),
        scratch_shapes=[pltpu.VMEM((tm, d), BF16)],
        compiler_params=_params(("arbitrary", "arbitrary")),
        name="in_proj",
    )(x, scale, shift, g.reshape(1, d), w_main, w_ba)


PROMPT_SEQS_PER_STEP = 4


def _mixer_prompt_kernel(proj_ref, ba_ref, caw_ref, gcw_ref, alog_ref, dtb_ref, gco_ref, gng_ref,
                         mix_ref, ha_ref, hq_ref, s_ref, extu, extq, qc_scr, s_scr, *, nseq):
    c = CHUNK
    dc = N_HEADS * HEAD_D
    dq = 3 * dc
    t = pl.program_id(1)
    is_last = t == pl.num_programs(1) - 1

    @pl.when(t == 0)
    def _():
        extu[:, 0:HALO, :] = jnp.zeros((nseq, HALO, dc), F32)
        extq[:, 0:HALO, :] = jnp.zeros((nseq, HALO, dq), F32)
        s_scr[...] = jnp.zeros_like(s_scr)

    row = lax.broadcasted_iota(I32, (c, c), 0)
    col = lax.broadcasted_iota(I32, (c, c), 1)
    causal = row >= col
    strict = row > col
    lower = jnp.where(causal, 1.0, 0.0).astype(F32)
    upper = jnp.where(row <= col, 1.0, 0.0).astype(F32)
    caw = caw_ref[...]
    gcw = gcw_ref[...]

    chains = [(sq, h) for sq in range(nseq) for h in range(N_HEADS)]
    heads = range(len(chains))
    qn, kn, vb, kb, kbg, qg, kg, decay, s_decay = ([] for _ in range(9))
    for sq in range(nseq):
        u = proj_ref[sq, :, dc:2 * dc] * proj_ref[sq, :, 2 * dc:3 * dc]
        extu[sq, HALO:HALO + c, :] = u
        ya = (caw[0:1] * extu[sq, HALO - 2:HALO - 2 + c, :] + caw[1:2] * extu[sq, HALO - 1:HALO - 1 + c, :]
              + caw[2:3] * u)
        ya = proj_ref[sq, :, 0:dc] * ya
        mix_ref[sq, :, 0:dc] = (_rms(ya) * gco_ref[...]).astype(BF16)
        last_u = extu[sq, HALO - 2 + c:HALO + c, :]
        extu[sq, HALO - 2:HALO, :] = last_u

        qkv = proj_ref[sq, :, 3 * dc:3 * dc + dq]
        extq[sq, HALO:HALO + c, :] = qkv
        qc = (gcw[0:1] * extq[sq, HALO - 3:HALO - 3 + c, :] + gcw[1:2] * extq[sq, HALO - 2:HALO - 2 + c, :]
              + gcw[2:3] * extq[sq, HALO - 1:HALO - 1 + c, :] + gcw[3:4] * qkv)
        qc_scr[sq] = _silu(qc)
        last_q = extq[sq, HALO - 3 + c:HALO + c, :]
        extq[sq, HALO - 3:HALO, :] = last_q

        @pl.when(is_last)
        def _(sq=sq, last_u=last_u, last_q=last_q):
            ha_ref[sq] = last_u
            hq_ref[sq] = last_q

        ba = ba_ref[sq]
        beta_all = jax.nn.sigmoid(ba)
        g_all = -jnp.exp(alog_ref[...]) * _softplus(ba + dtb_ref[...])
        gc_all = jnp.dot(lower, g_all, precision=HIGHEST, preferred_element_type=F32)
        gc_t = lax.dot_general(g_all, upper, (((0,), (0,)), ((), ())), precision=HIGHEST,
                               preferred_element_type=F32)
        for h in range(N_HEADS):
            lo = h * HEAD_D
            q = qc_scr[sq, :, lo:lo + HEAD_D]
            k = qc_scr[sq, :, dc + lo:dc + lo + HEAD_D]
            v = qc_scr[sq, :, 2 * dc + lo:2 * dc + lo + HEAD_D]
            qn_h = q * lax.rsqrt(jnp.sum(q * q, axis=-1, keepdims=True) + EPS) * (HEAD_D ** -0.5)
            kn_h = k * lax.rsqrt(jnp.sum(k * k, axis=-1, keepdims=True) + EPS)
            beta = beta_all[:, h:h + 1]
            gcc = gc_all[:, N_HEADS + h:N_HEADS + h + 1]
            gcr = gc_t[N_HEADS + h:N_HEADS + h + 1, :]
            gl = gc_all[c - 1:c, N_HEADS + h:N_HEADS + h + 1]
            eg = jnp.exp(gcc)
            kb_h = kn_h * beta
            qn.append(qn_h)
            kn.append(kn_h)
            vb.append(v * beta)
            kb.append(kb_h)
            kbg.append(kb_h * eg)
            qg.append(qn_h * eg)
            kg.append(kn_h * jnp.exp(gl - gcc))
            decay.append(jnp.where(causal, jnp.exp(jnp.minimum(gcc - gcr, 0.0)), 0.0))
            s_decay.append(jnp.exp(gl))

    kq = [_mm_nt(jnp.concatenate([kb[h], qn[h]], axis=0), kn[h]) for h in heads]
    a_mat = [jnp.where(strict, kq[h][:c] * decay[h], 0.0) for h in heads]
    qk = [kq[h][c:] * decay[h] for h in heads]
    n_mat = [-a_mat[h] for h in heads]
    p = a_mat
    size = 2
    while size < c:
        p = [_mm(p[h], p[h]) for h in heads]
        n_p = [_mm(n_mat[h], p[h]) for h in heads]
        n_mat = [n_mat[h] + p[h] + n_p[h] for h in heads]
        size *= 2
    rhs = [jnp.concatenate([vb[h], kbg[h]], axis=-1) for h in heads]
    uw = [rhs[h] + _mm(n_mat[h], rhs[h]) for h in heads]
    s_old = [s_scr[sq, h] for sq, h in chains]
    ws = [_mm(jnp.concatenate([uw[h][:, HEAD_D:], qg[h]], axis=0), s_old[h]) for h in heads]
    v_new = [uw[h][:, :HEAD_D] - ws[h][:c] for h in heads]
    fin = [_mm(jnp.concatenate([qk[h], kg[h].T], axis=0), v_new[h]) for h in heads]
    for i, (sq, h) in enumerate(chains):
        lo = h * HEAD_D
        s_scr[sq, h] = s_old[i] * s_decay[i] + fin[i][c:]
        o = ws[i][c:] + fin[i][:c]
        z = proj_ref[sq, :, 3 * dc + dq + lo:3 * dc + dq + lo + HEAD_D]
        yb = _rms(o) * gng_ref[...] * _silu(z)
        mix_ref[sq, :, dc + lo:dc + lo + HEAD_D] = yb.astype(BF16)

    @pl.when(is_last)
    def _():
        s_ref[...] = s_scr[...]


def _lane_row(vec, offset):
    return jnp.zeros((1, LANES), F32).at[0, offset:offset + vec.shape[0]].set(vec.astype(F32))


def _mixer_prompt(proj, ba, bsz, seq, conv_a_w, gdn_conv_w, a_log, dt_bias, g_conv_out, gdn_norm_g):
    c = CHUNK
    dc = N_HEADS * HEAD_D
    dq = 3 * dc
    dproj = proj.shape[1]
    nt = seq // c
    const = lambda shape: pl.BlockSpec(shape, lambda b, t: (0,) * len(shape))
    nseq = PROMPT_SEQS_PER_STEP if bsz % PROMPT_SEQS_PER_STEP == 0 else 1
    seq_block = lambda *tail: pl.BlockSpec((nseq,) + tail, lambda b, t: (b, t) + (0,) * (len(tail) - 1))
    whole = lambda *tail: pl.BlockSpec((nseq,) + tail, lambda b, t: (b,) + (0,) * len(tail))
    mix, ha, hq, s_fin = pl.pallas_call(
        functools.partial(_mixer_prompt_kernel, nseq=nseq),
        out_shape=(jax.ShapeDtypeStruct((bsz, seq, 2 * dc), BF16),
                   jax.ShapeDtypeStruct((bsz, 2, dc), F32),
                   jax.ShapeDtypeStruct((bsz, 3, dq), F32),
                   jax.ShapeDtypeStruct((bsz, N_HEADS, HEAD_D, HEAD_D), F32)),
        grid=(bsz // nseq, nt),
        in_specs=[seq_block(c, dproj), seq_block(c, LANES),
                  const((3, dc)), const((4, dq)), const((1, LANES)), const((1, LANES)),
                  const((1, dc)), const((1, HEAD_D))],
        out_specs=(seq_block(c, 2 * dc), whole(2, dc), whole(3, dq), whole(N_HEADS, HEAD_D, HEAD_D)),
        scratch_shapes=[pltpu.VMEM((nseq, HALO + c, dc), F32), pltpu.VMEM((nseq, HALO + c, dq), F32),
                        pltpu.VMEM((nseq, c, dq), F32), pltpu.VMEM((nseq, N_HEADS, HEAD_D, HEAD_D), F32)],
        compiler_params=_params(("arbitrary", "arbitrary")),
        name="mixer_prompt",
    )(proj.reshape(bsz, seq, dproj), ba.reshape(bsz, seq, LANES), conv_a_w, gdn_conv_w,
      _lane_row(a_log, N_HEADS), _lane_row(dt_bias, N_HEADS), g_conv_out.reshape(1, dc),
      gdn_norm_g.reshape(1, HEAD_D))
    return mix.reshape(bsz * seq, 2 * dc), ha, hq, s_fin


SAMPLE_GROUP = 16


def _mixer_sample_kernel(proj_ref, ba_ref, hista_ref, histq_ref, s_in_ref, caw_ref, gcw_ref, alog_ref,
                         dtb_ref, gco_ref, gng_ref, mix_ref, ha_ref, hq_ref, s_out_ref, qc_scr, o_scr):
    tb = SAMPLE_GROUP
    dc = N_HEADS * HEAD_D
    dq = 3 * dc

    u = proj_ref[:, dc:2 * dc] * proj_ref[:, 2 * dc:3 * dc]
    caw = caw_ref[...]
    ya = caw[0:1] * hista_ref[:, 0:dc] + caw[1:2] * hista_ref[:, dc:2 * dc] + caw[2:3] * u
    ya = proj_ref[:, 0:dc] * ya
    mix_ref[:, 0:dc] = _rms(ya) * gco_ref[...]
    ha_ref[:, 0:dc] = hista_ref[:, dc:2 * dc]
    ha_ref[:, dc:2 * dc] = u

    qkv = proj_ref[:, 3 * dc:3 * dc + dq]
    gcw = gcw_ref[...]
    qc = (gcw[0:1] * histq_ref[:, 0:dq] + gcw[1:2] * histq_ref[:, dq:2 * dq]
          + gcw[2:3] * histq_ref[:, 2 * dq:3 * dq] + gcw[3:4] * qkv)
    qc_scr[...] = _silu(qc)
    hq_ref[:, 0:dq] = histq_ref[:, dq:2 * dq]
    hq_ref[:, dq:2 * dq] = histq_ref[:, 2 * dq:3 * dq]
    hq_ref[:, 2 * dq:3 * dq] = qkv

    ba = ba_ref[...]
    beta_all = jax.nn.sigmoid(ba)
    eg_all = jnp.exp(-jnp.exp(alog_ref[...]) * _softplus(ba + dtb_ref[...]))

    for h in range(N_HEADS):
        lo = h * HEAD_D
        q = qc_scr[:, lo:lo + HEAD_D]
        k = qc_scr[:, dc + lo:dc + lo + HEAD_D]
        v = qc_scr[:, 2 * dc + lo:2 * dc + lo + HEAD_D]
        qn = q * lax.rsqrt(jnp.sum(q * q, axis=-1, keepdims=True) + EPS) * (HEAD_D ** -0.5)
        kn = k * lax.rsqrt(jnp.sum(k * k, axis=-1, keepdims=True) + EPS)
        qk = jnp.sum(qn * kn, axis=-1, keepdims=True)
        kn_t = kn.T
        qn_t = qn.T
        for b in range(tb):
            s_old = s_in_ref[b, h]
            kc = jnp.broadcast_to(kn_t[:, b:b + 1], s_old.shape)
            e = eg_all[b:b + 1, N_HEADS + h:N_HEADS + h + 1]
            ks = jnp.sum(s_old * kc, axis=0, keepdims=True)
            qs = jnp.sum(s_old * qn_t[:, b:b + 1], axis=0, keepdims=True)
            v_new = beta_all[b:b + 1, h:h + 1] * (v[b:b + 1, :] - e * ks)
            o_scr[b:b + 1, lo:lo + HEAD_D] = e * qs + qk[b:b + 1, :] * v_new
            s_out_ref[b, h] = s_old * e + kc * v_new
        z = proj_ref[:, 3 * dc + dq + lo:3 * dc + dq + lo + HEAD_D]
        o = o_scr[:, lo:lo + HEAD_D]
        mix_ref[:, dc + lo:dc + lo + HEAD_D] = _rms(o) * gng_ref[...] * _silu(z)


def _mixer_sample(proj, ba, hist_a, hist_q, s_in, conv_a_w, gdn_conv_w, a_log, dt_bias, g_conv_out,
                  gdn_norm_g):
    n = proj.shape[0]
    tb = SAMPLE_GROUP
    dc = N_HEADS * HEAD_D
    dq = 3 * dc
    dproj = proj.shape[1]
    const = lambda shape: pl.BlockSpec(shape, lambda i: (0,) * len(shape))
    rows = lambda width: pl.BlockSpec((tb, width), lambda i: (i, 0))
    state = pl.BlockSpec((tb, N_HEADS, HEAD_D, HEAD_D), lambda i: (i, 0, 0, 0))
    return pl.pallas_call(
        _mixer_sample_kernel,
        out_shape=(jax.ShapeDtypeStruct((n, 2 * dc), F32),
                   jax.ShapeDtypeStruct((n, 2 * dc), F32),
                   jax.ShapeDtypeStruct((n, 3 * dq), F32),
                   jax.ShapeDtypeStruct((n, N_HEADS, HEAD_D, HEAD_D), F32)),
        grid=(n // tb,),
        in_specs=[rows(dproj), rows(LANES), rows(2 * dc), rows(3 * dq), state,
                  const((3, dc)), const((4, dq)), const((1, LANES)), const((1, LANES)),
                  const((1, dc)), const((1, HEAD_D))],
        out_specs=(rows(2 * dc), rows(2 * dc), rows(3 * dq), state),
        scratch_shapes=[pltpu.VMEM((tb, dq), F32), pltpu.VMEM((tb, dc), F32)],
        compiler_params=_params(("arbitrary",)),
        name="mixer_sample",
    )(proj, ba, hist_a, hist_q, s_in, conv_a_w, gdn_conv_w, _lane_row(a_log, N_HEADS),
      _lane_row(dt_bias, N_HEADS), g_conv_out.reshape(1, dc), gdn_norm_g.reshape(1, HEAD_D))


POST_MIX_SUB_ROWS = 256


def _post_mix_kernel(mix_ref, x_ref, gate_ref, sc_ref, sh_ref, gpost_ref, gpre_ref, wout_ref,
                     rwh_ref, rwl_ref, rb_ref, x1_ref, h2_ref, lg_ref):
    tm = x_ref.shape[0]
    sub = min(tm, POST_MIX_SUB_ROWS)
    tiles = [slice(r0, r0 + sub) for r0 in range(0, tm, sub)]
    mixes = [jnp.dot(mix_ref[rows].astype(BF16), wout_ref[...], preferred_element_type=F32)
             for rows in tiles]
    for rows, mix in zip(tiles, mixes):
        per_row = lambda ref: ref[rows] if ref.shape[0] == tm else ref[...]
        x1 = x_ref[rows] + per_row(gate_ref) * (_rms(mix) * gpost_ref[...])
        x1_ref[rows] = x1
        h2 = (_rms(x1) * gpre_ref[...]) * (1.0 + per_row(sc_ref)) + per_row(sh_ref)
        pieces = h2.shape[1] // LANES
        for s in range(pieces):
            h2_ref[pl.ds(rows.start * pieces + s, sub, stride=pieces), :] = h2[:, s * LANES:(s + 1) * LANES]
        hi = h2.astype(BF16)
        lo = (h2 - hi.astype(F32)).astype(BF16)
        rwh = rwh_ref[...]
        lg_ref[rows] = (jnp.dot(hi, rwh, preferred_element_type=F32)
                        + jnp.dot(lo, rwh, preferred_element_type=F32)
                        + jnp.dot(hi, rwl_ref[...], preferred_element_type=F32) + rb_ref[...])


def _post_mix(mix_in, x, gate, scale, shift, g_post, g_pre, w_out, rw_hi, rw_lo, rb, rows_per_group):
    m, d = x.shape
    tm = min(m, POST_MIX_ROWS, rows_per_group if gate.ndim == 3 else m)
    const = lambda shape: pl.BlockSpec(shape, lambda i: (0,) * len(shape))
    rows = lambda width: pl.BlockSpec((tm, width), lambda i: (i, 0))
    pieces = d // LANES
    return pl.pallas_call(
        _post_mix_kernel,
        out_shape=(jax.ShapeDtypeStruct((m, d), F32), jax.ShapeDtypeStruct((m * pieces, LANES), F32),
                   jax.ShapeDtypeStruct((m, LANES), F32)),
        grid=(m // tm,),
        in_specs=[rows(d), rows(d),
                  _mod_spec(gate, tm, rows_per_group),
                  _mod_spec(scale, tm, rows_per_group),
                  _mod_spec(shift, tm, rows_per_group),
                  const((1, d)), const((1, d)), const((d, d)),
                  const((d, LANES)), const((d, LANES)), const((1, LANES))],
        out_specs=(rows(d), pl.BlockSpec((tm * pieces, LANES), lambda i: (i, 0)), rows(LANES)),
        compiler_params=_params(("arbitrary",)),
        name="post_mix",
    )(mix_in, x, gate, scale, shift, g_post.reshape(1, d), g_pre.reshape(1, d), w_out, rw_hi, rw_lo, rb)


ROUTE_TOKENS = 128


def _route_kernel(lgp_ref, lgs_ref, idx_ref, p_ref, rank_ref, cnt_ref, carry, *, prompt_tiles):
    tm = lgp_ref.shape[0]

    @pl.when(pl.program_id(0) == 0)
    def _():
        carry[...] = jnp.zeros_like(carry)

    l = jnp.where(pl.program_id(0) < prompt_tiles, lgp_ref[...], lgs_ref[...])
    lane = lax.broadcasted_iota(I32, l.shape, 1)
    lane_f = lane.astype(F32)
    vals, hots = [], []
    idx_out = jnp.zeros(l.shape, F32)
    for k in range(TOP_K):
        m = jnp.max(l, axis=-1, keepdims=True)
        idx = jnp.min(jnp.where(l == m, lane_f, float(LANES)), axis=-1, keepdims=True)
        hot = lane_f == idx
        vals.append(m)
        hots.append(hot)
        idx_out = jnp.where(lane == k, idx, idx_out)
        l = jnp.where(hot, -jnp.inf, l)
    exps = [jnp.exp(v - vals[0]) for v in vals]
    denom = exps[0] + exps[1] + exps[2] + exps[3]
    p_out = jnp.zeros(l.shape, F32)
    for k in range(TOP_K):
        p_out = jnp.where(lane == k, exps[k] / denom, p_out)
    member = jnp.where(hots[0] | hots[1] | hots[2] | hots[3], 1.0, 0.0).astype(F32)
    row = lax.broadcasted_iota(I32, (tm, tm), 0)
    col = lax.broadcasted_iota(I32, (tm, tm), 1)
    before = jnp.where(row > col, 1.0, 0.0).astype(BF16)
    prefix = jnp.dot(before, member.astype(BF16), preferred_element_type=F32) + carry[...]
    rank_out = jnp.zeros(l.shape, F32)
    for k in range(TOP_K):
        r = jnp.sum(jnp.where(hots[k], prefix, 0.0), axis=-1, keepdims=True)
        rank_out = jnp.where(lane == k, r, rank_out)
    carry[...] = carry[...] + jnp.sum(member, axis=0, keepdims=True)
    idx_ref[...] = idx_out.astype(I32)
    p_ref[...] = p_out
    rank_ref[...] = rank_out.astype(I32)
    cnt_ref[...] = carry[...]


def _route(logits_p, logits_s):
    tm = ROUTE_TOKENS
    pt = logits_p.shape[0] // tm
    n = logits_p.shape[0] + logits_s.shape[0]
    tile = pl.BlockSpec((tm, LANES), lambda i: (i, 0))
    return pl.pallas_call(
        functools.partial(_route_kernel, prompt_tiles=pt),
        out_shape=(jax.ShapeDtypeStruct((n, LANES), I32), jax.ShapeDtypeStruct((n, LANES), F32),
                   jax.ShapeDtypeStruct((n, LANES), I32), jax.ShapeDtypeStruct((1, LANES), F32)),
        grid=(n // tm,),
        in_specs=[pl.BlockSpec((tm, LANES), lambda i: (jnp.minimum(i, pt - 1), 0)),
                  pl.BlockSpec((tm, LANES), lambda i: (jnp.maximum(i - pt, 0), 0))],
        out_specs=(tile, tile, tile, pl.BlockSpec((1, LANES), lambda i: (0, 0))),
        scratch_shapes=[pltpu.VMEM((1, LANES), F32)],
        compiler_params=_params(("arbitrary",)),
        name="route",
    )(logits_p, logits_s)


INVERT_TOKENS_MAX = 1024


def _largest_tile(n, unit, cap):
    return max(m for m in range(unit, cap + 1, unit) if n % m == 0)


def _invert_kernel(meta_ref, slot_ref, table_ref, *, pad_token, n_exp, tokens):
    i = pl.program_id(0)
    pairs = tokens * TOP_K

    @pl.when(i == 0)
    def _():
        def fill(r, carry):
            table_ref[r] = pad_token
            return carry

        def fill_group_tail(e, carry):
            lax.fori_loop(meta_ref[e] + meta_ref[2 * n_exp + e], meta_ref[e] + meta_ref[n_exp + e], fill, 0)
            return carry

        lax.fori_loop(0, n_exp, fill_group_tail, 0)
        lax.fori_loop(meta_ref[3 * n_exp] * EXPERT_ROW_TILE, table_ref.shape[0], fill, 0)

    def put(p, carry):
        table_ref[slot_ref[0, p]] = i * tokens + (p >> TOP_K_SHIFT)
        return carry

    lax.fori_loop(0, pairs, put, 0, unroll=16)


def _invert(slots, meta, n_rows, pad_token, n_exp):
    n = slots.shape[0]
    tokens = _largest_tile(n, LANES, INVERT_TOKENS_MAX)
    nt = n // tokens
    pairs = tokens * TOP_K
    return pl.pallas_call(
        functools.partial(_invert_kernel, pad_token=pad_token, n_exp=n_exp, tokens=tokens),
        out_shape=jax.ShapeDtypeStruct((n_rows,), I32),
        grid_spec=pltpu.PrefetchScalarGridSpec(
            num_scalar_prefetch=1,
            grid=(nt,),
            in_specs=[pl.BlockSpec((None, 1, pairs), lambda i, meta: (i, 0, 0), memory_space=pltpu.SMEM)],
            out_specs=pl.BlockSpec(memory_space=pltpu.SMEM)),
        compiler_params=_params(("arbitrary",)),
        name="invert",
    )(meta, slots.reshape(nt, 1, pairs))


ROW_UNROLL = 8
ROW_UNROLL_SHIFT = 3


def _dispatch_kernel(na_ref, split_ref, tok_ref, tok_next_ref, h2p_ref, h2s_ref, xs_ref, buf, sem, *,
                     n_prompt):
    i = pl.program_id(0)
    tile = EXPERT_ROW_TILE
    n_active = na_ref[0]
    pieces = xs_ref.shape[1] // LANES

    def token_rows(t):
        return pl.ds(pl.multiple_of(t * pieces, pieces), pieces)

    def issue(tok, split, slot):
        def prompt_copy(r):
            pltpu.make_async_copy(h2p_ref.at[token_rows(tok[0, r])], buf.at[slot, token_rows(r)],
                                  sem.at[slot]).start()

        def sample_copy(r):
            pltpu.make_async_copy(h2s_ref.at[token_rows(tok[0, r] - n_prompt)], buf.at[slot, token_rows(r)],
                                  sem.at[slot]).start()

        def rows(lo, hi, fn):
            def body(r, carry):
                fn(r)
                return carry
            lax.fori_loop(lo, hi, body, 0)

        def groups(lo, hi, fn):
            def body(g, carry):
                for u in range(ROW_UNROLL):
                    fn(g * ROW_UNROLL + u)
                return carry
            lax.fori_loop(lo, hi, body, 0)

        whole = split >> ROW_UNROLL_SHIFT
        first = (split + ROW_UNROLL - 1) >> ROW_UNROLL_SHIFT
        groups(0, whole, prompt_copy)
        rows(whole * ROW_UNROLL, split, prompt_copy)
        rows(split, first * ROW_UNROLL, sample_copy)
        groups(first, tile // ROW_UNROLL, sample_copy)

    @pl.when(i == 0)
    def _():
        issue(tok_ref, split_ref[0], 0)

    @pl.when(i + 1 < n_active)
    def _():
        issue(tok_next_ref, split_ref[i + 1], (i + 1) & 1)

    @pl.when(i < n_active)
    def _():
        slot = i & 1
        pltpu.make_async_copy(h2p_ref.at[pl.ds(0, tile * pieces)], buf.at[slot], sem.at[slot]).wait()
        for s in range(pieces):
            xs_ref[:, s * LANES:(s + 1) * LANES] = buf[slot, pl.ds(s, tile, stride=pieces), :].astype(BF16)

    @pl.when(i >= n_active)
    def _():
        xs_ref[...] = jnp.zeros_like(xs_ref)


def _dispatch(h2_p, h2_s, table, split, n_active, d):
    pieces = d // LANES
    tile = EXPERT_ROW_TILE
    n_rows = table.shape[0]
    nt = n_rows // tile
    return pl.pallas_call(
        functools.partial(_dispatch_kernel, n_prompt=h2_p.shape[0] // pieces),
        out_shape=jax.ShapeDtypeStruct((n_rows, d), BF16),
        grid_spec=pltpu.PrefetchScalarGridSpec(
            num_scalar_prefetch=2,
            grid=(nt,),
            in_specs=[pl.BlockSpec((None, 1, tile), lambda i, na, sp: (i, 0, 0), memory_space=pltpu.SMEM),
                      pl.BlockSpec((None, 1, tile), lambda i, na, sp: (jnp.minimum(i + 1, nt - 1), 0, 0),
                                   memory_space=pltpu.SMEM),
                      pl.BlockSpec(memory_space=pl.ANY), pl.BlockSpec(memory_space=pl.ANY)],
            out_specs=pl.BlockSpec((tile, d), lambda i, na, sp: (i, 0)),
            scratch_shapes=[pltpu.VMEM((2, tile * pieces, LANES), F32), pltpu.SemaphoreType.DMA((2,))]),
        compiler_params=_params(("arbitrary",)),
        name="dispatch",
    )(n_active, split, table.reshape(nt, 1, tile), table.reshape(nt, 1, tile), h2_p, h2_s)


def _stream_expert_weights(te_ref, first_ref, run_ref, next_ref, meta_ref, n_col_blocks, copies, consume):
    j = pl.program_id(0)
    i = pl.program_id(1)
    n_active = meta_ref[0]
    n_runs = meta_ref[1]

    @pl.when(jnp.logical_and(j == 0, i == 0))
    def _():
        for cp in copies(te_ref[0], 0, 0):
            cp.start()

    @pl.when(jnp.logical_and(i < n_active, first_ref[i] == 1))
    def _():
        slot = (j * n_runs + run_ref[i]) & 1
        for cp in copies(te_ref[i], j, slot):
            cp.wait()
        last = run_ref[i] == n_runs - 1
        e_next = jnp.where(last, te_ref[0], next_ref[i])
        j_next = jnp.where(last, j + 1, j)

        @pl.when(j_next < n_col_blocks)
        def _():
            for cp in copies(e_next, j_next, 1 - slot):
                cp.start()

        consume(slot)


def _gate_up_kernel(te_ref, first_ref, run_ref, next_ref, meta_ref, x_ref, bg_ref, bu_ref, w_hbm, o_ref,
                    wbuf, wg_b, wu_b, sem, *, nj, tn):
    i = pl.program_id(1)
    active = i < meta_ref[0]

    def copies(e, jj, slot):
        return [pltpu.make_async_copy(w_hbm.at[e, :, pl.ds(pl.multiple_of((c * nj + jj) * tn, tn), tn)],
                                      wbuf.at[slot, c], sem.at[slot, c]) for c in range(2)]

    def consume(slot):
        wg_b[...] = wbuf[slot, 0].astype(BF16)
        wu_b[...] = wbuf[slot, 1].astype(BF16)

    _stream_expert_weights(te_ref, first_ref, run_ref, next_ref, meta_ref, nj, copies, consume)

    @pl.when(active)
    def _():
        x = x_ref[...]
        gate = jnp.dot(x, wg_b[...], preferred_element_type=F32) + bg_ref[...]
        up = jnp.dot(x, wu_b[...], preferred_element_type=F32) + bu_ref[...]
        gate = jnp.minimum(gate, SWIGLU_LIMIT)
        up = jnp.clip(up, -SWIGLU_LIMIT, SWIGLU_LIMIT)
        o_ref[...] = ((up + 1.0) * gate * jax.nn.sigmoid(SWIGLU_ALPHA * gate)).astype(BF16)

    @pl.when(jnp.logical_not(active))
    def _():
        o_ref[...] = jnp.zeros_like(o_ref)


def _expert_gate_up(xs, w_gu, b_gu, sched):
    n_rows, d = xs.shape
    n_exp, _, f2 = w_gu.shape
    f = f2 // 2
    tm = EXPERT_ROW_TILE
    tn = MATMUL_COLS
    nj = f // tn
    row = lambda i, meta: jnp.minimum(i, meta[0] - 1)
    return pl.pallas_call(
        functools.partial(_gate_up_kernel, nj=nj, tn=tn),
        out_shape=jax.ShapeDtypeStruct((n_rows, f), BF16),
        grid_spec=pltpu.PrefetchScalarGridSpec(
            num_scalar_prefetch=5,
            grid=(nj, n_rows // tm),
            in_specs=[pl.BlockSpec((tm, d), lambda j, i, te, fi, ru, nx, meta: (row(i, meta), 0)),
                      pl.BlockSpec((None, 1, tn), lambda j, i, te, fi, ru, nx, meta: (te[row(i, meta)], 0, j)),
                      pl.BlockSpec((None, 1, tn),
                                   lambda j, i, te, fi, ru, nx, meta: (te[row(i, meta)], 0, nj + j)),
                      pl.BlockSpec(memory_space=pl.ANY)],
            out_specs=pl.BlockSpec((tm, tn), lambda j, i, te, fi, ru, nx, meta: (i, j)),
            scratch_shapes=[pltpu.VMEM((2, 2, d, tn), F32), pltpu.VMEM((d, tn), BF16),
                            pltpu.VMEM((d, tn), BF16), pltpu.SemaphoreType.DMA((2, 2))]),
        compiler_params=_params(("arbitrary", "arbitrary")),
        name="expert_gate_up",
    )(*sched, xs, b_gu.reshape(n_exp, 1, f2), b_gu.reshape(n_exp, 1, f2), w_gu)


def _down_kernel(te_ref, first_ref, run_ref, next_ref, meta_ref, a_ref, b_ref, w_hbm, o_ref, wbuf, w_b, sem):
    i = pl.program_id(1)
    active = i < meta_ref[0]

    def copies(e, jj, slot):
        return [pltpu.make_async_copy(w_hbm.at[e], wbuf.at[slot], sem.at[slot])]

    def consume(slot):
        w_b[...] = wbuf[slot].astype(BF16)

    _stream_expert_weights(te_ref, first_ref, run_ref, next_ref, meta_ref, 1, copies, consume)

    @pl.when(active)
    def _():
        o_ref[...] = jnp.dot(a_ref[...], w_b[...], preferred_element_type=F32) + b_ref[...]

    @pl.when(jnp.logical_not(active))
    def _():
        o_ref[...] = jnp.zeros_like(o_ref)


def _expert_down(act, w_d, b_d, sched):
    n_rows, f = act.shape
    n_exp, _, d = w_d.shape
    tm = EXPERT_ROW_TILE
    row = lambda i, meta: jnp.minimum(i, meta[0] - 1)
    return pl.pallas_call(
        _down_kernel,
        out_shape=jax.ShapeDtypeStruct((n_rows, d), F32),
        grid_spec=pltpu.PrefetchScalarGridSpec(
            num_scalar_prefetch=5,
            grid=(1, n_rows // tm),
            in_specs=[pl.BlockSpec((tm, f), lambda j, i, te, fi, ru, nx, meta: (row(i, meta), 0)),
                      pl.BlockSpec((None, 1, d), lambda j, i, te, fi, ru, nx, meta: (te[row(i, meta)], 0, 0)),
                      pl.BlockSpec(memory_space=pl.ANY)],
            out_specs=pl.BlockSpec((tm, d), lambda j, i, te, fi, ru, nx, meta: (i, 0)),
            scratch_shapes=[pltpu.VMEM((2, f, d), F32), pltpu.VMEM((f, d), BF16),
                            pltpu.SemaphoreType.DMA((2,))]),
        compiler_params=_params(("arbitrary", "arbitrary")),
        name="expert_down",
    )(*sched, act, b_d.reshape(n_exp, 1, d), w_d)


COMBINE_TOKENS = 256


def _combine_kernel(slot_ref, slot_next_ref, p_ref, x1_ref, gate_ref, g_ref, y_ref, o_ref, buf, sem):
    i = pl.program_id(0)
    tm = x1_ref.shape[0]
    pairs = tm * TOP_K

    def issue(slots, b):
        def body(g, carry):
            for u in range(ROW_UNROLL):
                p = g * ROW_UNROLL + u
                tok = g * (ROW_UNROLL // TOP_K) + u // TOP_K
                pltpu.make_async_copy(y_ref.at[pl.ds(slots[0, p], 1)], buf.at[b, u % TOP_K, pl.ds(tok, 1)],
                                      sem.at[b]).start()
            return carry
        lax.fori_loop(0, pairs // ROW_UNROLL, body, 0)

    @pl.when(i == 0)
    def _():
        issue(slot_ref, 0)

    @pl.when(i + 1 < pl.num_programs(0))
    def _():
        issue(slot_next_ref, (i + 1) & 1)

    b = i & 1
    for k in range(TOP_K):
        pltpu.make_async_copy(y_ref.at[pl.ds(0, tm)], buf.at[b, k], sem.at[b]).wait()
    probs = p_ref[...]
    f = probs[:, 0:1] * buf[b, 0]
    for k in range(1, TOP_K):
        f = f + probs[:, k:k + 1] * buf[b, k]
    o_ref[...] = x1_ref[...] + gate_ref[...] * (_rms(f) * g_ref[...])


def _combine(y, slots, probs, x1, gate, g_post, rows_per_group):
    m, d = x1.shape
    tm = min(m, COMBINE_TOKENS)
    nt = m // tm
    pairs = tm * TOP_K
    return pl.pallas_call(
        _combine_kernel,
        out_shape=jax.ShapeDtypeStruct((m, d), F32),
        grid=(nt,),
        in_specs=[pl.BlockSpec((None, 1, pairs), lambda i: (i, 0, 0), memory_space=pltpu.SMEM),
                  pl.BlockSpec((None, 1, pairs), lambda i: (jnp.minimum(i + 1, nt - 1), 0, 0),
                               memory_space=pltpu.SMEM),
                  pl.BlockSpec((tm, LANES), lambda i: (i, 0)),
                  pl.BlockSpec((tm, d), lambda i: (i, 0)),
                  _mod_spec(gate, tm, rows_per_group),
                  pl.BlockSpec((1, d), lambda i: (0, 0)),
                  pl.BlockSpec(memory_space=pl.ANY)],
        out_specs=pl.BlockSpec((tm, d), lambda i: (i, 0)),
        scratch_shapes=[pltpu.VMEM((2, TOP_K, tm, d), F32), pltpu.SemaphoreType.DMA((2,))],
        compiler_params=_params(("arbitrary",)),
        name="combine",
    )(slots.reshape(nt, 1, pairs), slots.reshape(nt, 1, pairs), probs, x1, gate, g_post.reshape(1, d), y)


def _moe(h2_p, h2_s, logits_p, logits_s, n_exp, w_gu, b_gu, w_d, b_d):
    d = w_gu.shape[1]
    n_prompt = logits_p.shape[0]
    n = n_prompt + logits_s.shape[0]
    tile = EXPERT_ROW_TILE
    top_i, probs, rank, counts = _route(logits_p, logits_s)
    counts = counts[0, :n_exp].astype(I32)
    cap = (counts + tile - 1) // tile * tile
    ends = jnp.cumsum(cap)
    offs = ends - cap
    n_tiles = (n * TOP_K) // tile + n_exp
    n_active = (ends[-1] // tile).astype(I32).reshape(1)
    tile_id = jnp.arange(n_tiles, dtype=I32)
    tile_expert = jnp.minimum(jnp.sum(ends[None, :] <= tile_id[:, None] * tile, axis=1), n_exp - 1).astype(I32)
    first = ((tile_id == 0) | (tile_expert != jnp.roll(tile_expert, 1))) & (tile_id < n_active[0])
    run = jnp.cumsum(first.astype(I32)) - 1
    next_expert = tile_expert[jnp.minimum(ends[tile_expert] // tile, n_tiles - 1)]
    sched = (tile_expert, first.astype(I32), run.astype(I32), next_expert.astype(I32),
             jnp.stack([n_active[0], jnp.sum(first.astype(I32))]).astype(I32))
    expert_ids = jnp.arange(n_exp, dtype=I32)
    pair_offs = jnp.sum(jnp.where(top_i[:, :TOP_K, None] == expert_ids, offs, 0), axis=-1)
    slots = pair_offs + rank[:, :TOP_K]
    meta = jnp.concatenate([offs, cap, counts, n_active]).astype(I32)
    table = _invert(slots, meta, n_tiles * tile, n_prompt, n_exp)
    split = jnp.sum(table.reshape(n_tiles, tile) < n_prompt, axis=1).astype(I32)
    xs = _dispatch(h2_p, h2_s, table, split, n_active, d)
    act = _expert_gate_up(xs, w_gu, b_gu, sched)
    y = _expert_down(act, w_d, b_d, sched)
    return y, slots, probs


def _layer(xp, xs_, cp_mod, cs_mod, hist_a, hist_q, s0, g_pre_mix, g_post_mix, g_pre_ffn, g_post_ffn,
           w_in, conv_a_w, gdn_conv_w, a_log, dt_bias, g_conv_out, gdn_norm_g, w_out, router_w, router_b,
           w_gu, b_gu, w_d, b_d):
    bsz, seq, d = xp.shape
    ns = xs_.shape[0]
    n_exp = router_w.shape[1]
    dc = N_HEADS * HEAD_D
    d_main = 3 * dc + 3 * dc + dc
    xp2 = xp.reshape(bsz * seq, d)
    xs2 = xs_.reshape(ns, d)

    w_main = w_in.astype(BF16)
    w_ba = jnp.zeros((d, LANES), BF16).at[:, :2 * N_HEADS].set(w_main[:, d_main:])
    w_out_b = w_out.astype(BF16)
    rw = jnp.zeros((d, LANES), F32).at[:, :n_exp].set(router_w)
    rw_hi = rw.astype(BF16)
    rw_lo = (rw - rw_hi.astype(F32)).astype(BF16)
    rb = jnp.full((1, LANES), NEG_BIG, F32).at[0, :n_exp].set(router_b)

    mp = [cp_mod[:, i * d:(i + 1) * d].reshape(bsz, 1, d) for i in range(6)]
    ms = [cs_mod[:, i * d:(i + 1) * d] for i in range(6)]

    proj_p, ba_p = _in_proj(xp2, mp[1], mp[0], g_pre_mix, w_main, w_ba, seq, d_main)
    proj_s, ba_s = _in_proj(xs2, ms[1], ms[0], g_pre_mix, w_main, w_ba, 1, d_main)

    mix_p, ha_p, hq_p, s_p = _mixer_prompt(proj_p, ba_p, bsz, seq, conv_a_w, gdn_conv_w, a_log, dt_bias,
                                           g_conv_out, gdn_norm_g)
    mix_s, ha_s, hq_s, s_s = _mixer_sample(proj_s, ba_s, hist_a.reshape(ns, 2 * dc),
                                           hist_q.reshape(ns, 9 * dc), s0, conv_a_w, gdn_conv_w, a_log,
                                           dt_bias, g_conv_out, gdn_norm_g)

    x1_p, h2_p, lg_p = _post_mix(mix_p, xp2, mp[2], mp[4], mp[3], g_post_mix, g_pre_ffn, w_out_b, rw_hi,
                                 rw_lo, rb, seq)
    x1_s, h2_s, lg_s = _post_mix(mix_s, xs2, ms[2], ms[4], ms[3], g_post_mix, g_pre_ffn, w_out_b, rw_hi,
                                 rw_lo, rb, 1)

    y, slots, probs = _moe(h2_p, h2_s, lg_p, lg_s, n_exp, w_gu, b_gu, w_d, b_d)
    np_ = bsz * seq
    out_p = _combine(y, slots[:np_], probs[:np_], x1_p, mp[5], g_post_ffn, seq)
    out_s = _combine(y, slots[np_:], probs[np_:], x1_s, ms[5], g_post_ffn, 1)
    return (out_p.reshape(bsz, seq, d), out_s.reshape(ns, 1, d), ha_p, hq_p, s_p,
            ha_s.reshape(ns, 2, dc), hq_s.reshape(ns, 3, 3 * dc), s_s)


def kernel(x_prompt, x_sample, state_conv_a, state_gdn_conv, state_gdn_S, c_prompt, c_sample, w_mod, b_mod, g_pre_mix, g_post_mix, g_pre_ffn, g_post_ffn, w_in, conv_a_w, gdn_conv_w, gdn_a_log, gdn_dt_bias, g_conv_out, gdn_norm_g, w_out, router_w, router_b, exp_w_gate_up, exp_b_gate_up, exp_w_down, exp_b_down):
    depth = w_mod.shape[0]
    bp = x_prompt.shape[0]
    xp, xs_ = x_prompt, x_sample
    outs = [[] for _ in range(6)]
    for l in range(depth):
        mod = _modulation(jnp.concatenate([c_prompt, c_sample], axis=0), w_mod[l], b_mod[l])
        res = _layer(xp, xs_, mod[:bp], mod[bp:], state_conv_a[l], state_gdn_conv[l], state_gdn_S[l],
                     g_pre_mix[l], g_post_mix[l], g_pre_ffn[l], g_post_ffn[l], w_in[l], conv_a_w[l],
                     gdn_conv_w[l], gdn_a_log[l], gdn_dt_bias[l], g_conv_out[l], gdn_norm_g[l], w_out[l],
                     router_w[l], router_b[l], exp_w_gate_up[l], exp_b_gate_up[l], exp_w_down[l],
                     exp_b_down[l])
        xp, xs_ = res[0], res[1]
        for acc, r in zip(outs, res[2:]):
            acc.append(r)
    return (xp, xs_) + tuple(o[0][None] if depth == 1 else jnp.stack(o) for o in outs)
```

```python
import functools

import jax
import jax.numpy as jnp
from jax import lax
from jax.experimental import pallas as pl
from jax.experimental.pallas import tpu as pltpu

F32 = jnp.float32
BF16 = jnp.bfloat16
I32 = jnp.int32
HIGHEST = lax.Precision.HIGHEST

EPS = 1e-6
N_HEADS = 8
HEAD_D = 128
TOP_K = 4
TOP_K_SHIFT = 2
SWIGLU_LIMIT = 7.0
SWIGLU_ALPHA = 1.702
CHUNK = 64
LANES = 128
EXPERT_ROW_TILE = 256
MATMUL_COLS = 1024
IN_PROJ_ROWS = 1024
POST_MIX_ROWS = 512
HALO = 8
NEG_BIG = -1e30
VMEM_LIMIT = 56 * 1024 * 1024


def _params(semantics, vmem=VMEM_LIMIT):
    return pltpu.CompilerParams(dimension_semantics=semantics, vmem_limit_bytes=vmem)


def _mm(a, b):
    return jnp.dot(a.astype(BF16), b.astype(BF16), preferred_element_type=F32)


def _mm_nt(a, b):
    return lax.dot_general(a.astype(BF16), b.astype(BF16), (((1,), (1,)), ((), ())),
                           preferred_element_type=F32)


def _rms(x):
    return x * lax.rsqrt(jnp.mean(x * x, axis=-1, keepdims=True) + EPS)


def _silu(x):
    return x * jax.nn.sigmoid(x)


def _softplus(x):
    return jnp.maximum(x, 0.0) + jnp.log1p(jnp.exp(-jnp.abs(x)))


def _mod_kernel(c_ref, w_ref, b_ref, o_ref):
    s = _silu(c_ref[...])
    o_ref[...] = _mm(s, w_ref[...]) + b_ref[...]


def _modulation(c_all, w_mod, b_mod):
    n, d = c_all.shape
    m = w_mod.shape[1]
    tn = MATMUL_COLS
    return pl.pallas_call(
        _mod_kernel,
        out_shape=jax.ShapeDtypeStruct((n, m), F32),
        grid=(m // tn,),
        in_specs=[pl.BlockSpec((n, d), lambda j: (0, 0)),
                  pl.BlockSpec((d, tn), lambda j: (0, j)),
                  pl.BlockSpec((1, tn), lambda j: (0, j))],
        out_specs=pl.BlockSpec((n, tn), lambda j: (0, j)),
        compiler_params=_params(("arbitrary",)),
        name="modulation",
    )(c_all, w_mod, b_mod.reshape(1, m))


def _mod_spec(arr, tm, rows_per_group):
    if arr.ndim == 3:
        tiles = rows_per_group // tm
        return pl.BlockSpec((None, 1, arr.shape[-1]), lambda i, *_: (i // tiles, 0, 0))
    return pl.BlockSpec((tm, arr.shape[-1]), lambda i, *_: (i, 0))


def _proj_kernel(x_ref, sc_ref, sh_ref, g_ref, w_ref, wba_ref, o_ref, ba_ref, h_scr):
    @pl.when(pl.program_id(1) == 0)
    def _():
        h = (_rms(x_ref[...]) * g_ref[...]) * (1.0 + sc_ref[...]) + sh_ref[...]
        hb = h.astype(BF16)
        h_scr[...] = hb
        ba_ref[...] = jnp.dot(hb, wba_ref[...], preferred_element_type=F32)

    o_ref[...] = jnp.dot(h_scr[...], w_ref[...], preferred_element_type=F32)


def _in_proj(x, scale, shift, g, w_main, w_ba, rows_per_group, n):
    m, d = x.shape
    tm = min(m, IN_PROJ_ROWS, rows_per_group if scale.ndim == 3 else m)
    tn = MATMUL_COLS
    return pl.pallas_call(
        _proj_kernel,
        out_shape=(jax.ShapeDtypeStruct((m, n), F32), jax.ShapeDtypeStruct((m, LANES), F32)),
        grid=(m // tm, n // tn),
        in_specs=[pl.BlockSpec((tm, d), lambda i, j: (i, 0)),
                  _mod_spec(scale, tm, rows_per_group),
                  _mod_spec(shift, tm, rows_per_group),
                  pl.BlockSpec((1, d), lambda i, j: (0, 0)),
                  pl.BlockSpec((d, tn), lambda i, j: (0, j)),
                  pl.BlockSpec((d, LANES), lambda i, j: (0, 0))],
        out_specs=(pl.BlockSpec((tm, tn), lambda i, j: (i, j)),
                   pl.BlockSpec((tm, LANES), lambda i, j: (i, 0))),
        scratch_shapes=[pltpu.VMEM((tm, d), BF16)],
        compiler_params=_params(("arbitrary", "arbitrary")),
        name="in_proj",
    )(x, scale, shift, g.reshape(1, d), w_main, w_ba)


PROMPT_SEQS_PER_STEP = 4


def _mixer_prompt_kernel(proj_ref, ba_ref, caw_ref, gcw_ref, alog_ref, dtb_ref, gco_ref, gng_ref,
                         mix_ref, ha_ref, hq_ref, s_ref, extu, extq, qc_scr, s_scr, *, nseq):
    c = CHUNK
    dc = N_HEADS * HEAD_D
    dq = 3 * dc
    t = pl.program_id(1)
    is_last = t == pl.num_programs(1) - 1

    @pl.when(t == 0)
    def _():
        extu[:, 0:HALO, :] = jnp.zeros((nseq, HALO, dc), F32)
        extq[:, 0:HALO, :] = jnp.zeros((nseq, HALO, dq), F32)
        s_scr[...] = jnp.zeros_like(s_scr)

    row = lax.broadcasted_iota(I32, (c, c), 0)
    col = lax.broadcasted_iota(I32, (c, c), 1)
    causal = row >= col
    strict = row > col
    lower = jnp.where(causal, 1.0, 0.0).astype(F32)
    upper = jnp.where(row <= col, 1.0, 0.0).astype(F32)
    caw = caw_ref[...]
    gcw = gcw_ref[...]

    chains = [(sq, h) for sq in range(nseq) for h in range(N_HEADS)]
    heads = range(len(chains))
    qn, kn, vb, kb, kbg, qg, kg, decay, s_decay = ([] for _ in range(9))
    for sq in range(nseq):
        u = proj_ref[sq, :, dc:2 * dc] * proj_ref[sq, :, 2 * dc:3 * dc]
        extu[sq, HALO:HALO + c, :] = u
        ya = (caw[0:1] * extu[sq, HALO - 2:HALO - 2 + c, :] + caw[1:2] * extu[sq, HALO - 1:HALO - 1 + c, :]
              + caw[2:3] * u)
        ya = proj_ref[sq, :, 0:dc] * ya
        mix_ref[sq, :, 0:dc] = (_rms(ya) * gco_ref[...]).astype(BF16)
        last_u = extu[sq, HALO - 2 + c:HALO + c, :]
        extu[sq, HALO - 2:HALO, :] = last_u

        qkv = proj_ref[sq, :, 3 * dc:3 * dc + dq]
        extq[sq, HALO:HALO + c, :] = qkv
        qc = (gcw[0:1] * extq[sq, HALO - 3:HALO - 3 + c, :] + gcw[1:2] * extq[sq, HALO - 2:HALO - 2 + c, :]
              + gcw[2:3] * extq[sq, HALO - 1:HALO - 1 + c, :] + gcw[3:4] * qkv)
        qc_scr[sq] = _silu(qc)
        last_q = extq[sq, HALO - 3 + c:HALO + c, :]
        extq[sq, HALO - 3:HALO, :] = last_q

        @pl.when(is_last)
        def _(sq=sq, last_u=last_u, last_q=last_q):
            ha_ref[sq] = last_u
            hq_ref[sq] = last_q

        ba = ba_ref[sq]
        beta_all = jax.nn.sigmoid(ba)
        g_all = -jnp.exp(alog_ref[...]) * _softplus(ba + dtb_ref[...])
        gc_all = jnp.dot(lower, g_all, precision=HIGHEST, preferred_element_type=F32)
        gc_t = lax.dot_general(g_all, upper, (((0,), (0,)), ((), ())), precision=HIGHEST,
                               preferred_element_type=F32)
        for h in range(N_HEADS):
            lo = h * HEAD_D
            q = qc_scr[sq, :, lo:lo + HEAD_D]
            k = qc_scr[sq, :, dc + lo:dc + lo + HEAD_D]
            v = qc_scr[sq, :, 2 * dc + lo:2 * dc + lo + HEAD_D]
            qn_h = q * lax.rsqrt(jnp.sum(q * q, axis=-1, keepdims=True) + EPS) * (HEAD_D ** -0.5)
            kn_h = k * lax.rsqrt(jnp.sum(k * k, axis=-1, keepdims=True) + EPS)
            beta = beta_all[:, h:h + 1]
            gcc = gc_all[:, N_HEADS + h:N_HEADS + h + 1]
            gcr = gc_t[N_HEADS + h:N_HEADS + h + 1, :]
            gl = gc_all[c - 1:c, N_HEADS + h:N_HEADS + h + 1]
            eg = jnp.exp(gcc)
            kb_h = kn_h * beta
            qn.append(qn_h)
            kn.append(kn_h)
            vb.append(v * beta)
            kb.append(kb_h)
            kbg.append(kb_h * eg)
            qg.append(qn_h * eg)
            kg.append(kn_h * jnp.exp(gl - gcc))
            decay.append(jnp.where(causal, jnp.exp(jnp.minimum(gcc - gcr, 0.0)), 0.0))
            s_decay.append(jnp.exp(gl))

    kq = [_mm_nt(jnp.concatenate([kb[h], qn[h]], axis=0), kn[h]) for h in heads]
    a_mat = [jnp.where(strict, kq[h][:c] * decay[h], 0.0) for h in heads]
    qk = [kq[h][c:] * decay[h] for h in heads]
    n_mat = [-a_mat[h] for h in heads]
    p = a_mat
    size = 2
    while size < c:
        p = [_mm(p[h], p[h]) for h in heads]
        n_p = [_mm(n_mat[h], p[h]) for h in heads]
        n_mat = [n_mat[h] + p[h] + n_p[h] for h in heads]
        size *= 2
    rhs = [jnp.concatenate([vb[h], kbg[h]], axis=-1) for h in heads]
    uw = [rhs[h] + _mm(n_mat[h], rhs[h]) for h in heads]
    s_old = [s_scr[sq, h] for sq, h in chains]
    ws = [_mm(jnp.concatenate([uw[h][:, HEAD_D:], qg[h]], axis=0), s_old[h]) for h in heads]
    v_new = [uw[h][:, :HEAD_D] - ws[h][:c] for h in heads]
    fin = [_mm(jnp.concatenate([qk[h], kg[h].T], axis=0), v_new[h]) for h in heads]
    for i, (sq, h) in enumerate(chains):
        lo = h * HEAD_D
        s_scr[sq, h] = s_old[i] * s_decay[i] + fin[i][c:]
        o = ws[i][c:] + fin[i][:c]
        z = proj_ref[sq, :, 3 * dc + dq + lo:3 * dc + dq + lo + HEAD_D]
        yb = _rms(o) * gng_ref[...] * _silu(z)
        mix_ref[sq, :, dc + lo:dc + lo + HEAD_D] = yb.astype(BF16)

    @pl.when(is_last)
    def _():
        s_ref[...] = s_scr[...]


def _lane_row(vec, offset):
    return jnp.zeros((1, LANES), F32).at[0, offset:offset + vec.shape[0]].set(vec.astype(F32))


def _mixer_prompt(proj, ba, bsz, seq, conv_a_w, gdn_conv_w, a_log, dt_bias, g_conv_out, gdn_norm_g):
    c = CHUNK
    dc = N_HEADS * HEAD_D
    dq = 3 * dc
    dproj = proj.shape[1]
    nt = seq // c
    const = lambda shape: pl.BlockSpec(shape, lambda b, t: (0,) * len(shape))
    nseq = PROMPT_SEQS_PER_STEP if bsz % PROMPT_SEQS_PER_STEP == 0 else 1
    seq_block = lambda *tail: pl.BlockSpec((nseq,) + tail, lambda b, t: (b, t) + (0,) * (len(tail) - 1))
    whole = lambda *tail: pl.BlockSpec((nseq,) + tail, lambda b, t: (b,) + (0,) * len(tail))
    mix, ha, hq, s_fin = pl.pallas_call(
        functools.partial(_mixer_prompt_kernel, nseq=nseq),
        out_shape=(jax.ShapeDtypeStruct((bsz, seq, 2 * dc), BF16),
                   jax.ShapeDtypeStruct((bsz, 2, dc), F32),
                   jax.ShapeDtypeStruct((bsz, 3, dq), F32),
                   jax.ShapeDtypeStruct((bsz, N_HEADS, HEAD_D, HEAD_D), F32)),
        grid=(bsz // nseq, nt),
        in_specs=[seq_block(c, dproj), seq_block(c, LANES),
                  const((3, dc)), const((4, dq)), const((1, LANES)), const((1, LANES)),
                  const((1, dc)), const((1, HEAD_D))],
        out_specs=(seq_block(c, 2 * dc), whole(2, dc), whole(3, dq), whole(N_HEADS, HEAD_D, HEAD_D)),
        scratch_shapes=[pltpu.VMEM((nseq, HALO + c, dc), F32), pltpu.VMEM((nseq, HALO + c, dq), F32),
                        pltpu.VMEM((nseq, c, dq), F32), pltpu.VMEM((nseq, N_HEADS, HEAD_D, HEAD_D), F32)],
        compiler_params=_params(("arbitrary", "arbitrary")),
        name="mixer_prompt",
    )(proj.reshape(bsz, seq, dproj), ba.reshape(bsz, seq, LANES), conv_a_w, gdn_conv_w,
      _lane_row(a_log, N_HEADS), _lane_row(dt_bias, N_HEADS), g_conv_out.reshape(1, dc),
      gdn_norm_g.reshape(1, HEAD_D))
    return mix.reshape(bsz * seq, 2 * dc), ha, hq, s_fin


SAMPLE_GROUP = 16


def _mixer_sample_kernel(proj_ref, ba_ref, hista_ref, histq_ref, s_in_ref, caw_ref, gcw_ref, alog_ref,
                         dtb_ref, gco_ref, gng_ref, mix_ref, ha_ref, hq_ref, s_out_ref, qc_scr, o_scr):
    tb = SAMPLE_GROUP
    dc = N_HEADS * HEAD_D
    dq = 3 * dc

    u = proj_ref[:, dc:2 * dc] * proj_ref[:, 2 * dc:3 * dc]
    caw = caw_ref[...]
    ya = caw[0:1] * hista_ref[:, 0:dc] + caw[1:2] * hista_ref[:, dc:2 * dc] + caw[2:3] * u
    ya = proj_ref[:, 0:dc] * ya
    mix_ref[:, 0:dc] = _rms(ya) * gco_ref[...]
    ha_ref[:, 0:dc] = hista_ref[:, dc:2 * dc]
    ha_ref[:, dc:2 * dc] = u

    qkv = proj_ref[:, 3 * dc:3 * dc + dq]
    gcw = gcw_ref[...]
    qc = (gcw[0:1] * histq_ref[:, 0:dq] + gcw[1:2] * histq_ref[:, dq:2 * dq]
          + gcw[2:3] * histq_ref[:, 2 * dq:3 * dq] + gcw[3:4] * qkv)
    qc_scr[...] = _silu(qc)
    hq_ref[:, 0:dq] = histq_ref[:, dq:2 * dq]
    hq_ref[:, dq:2 * dq] = histq_ref[:, 2 * dq:3 * dq]
    hq_ref[:, 2 * dq:3 * dq] = qkv

    ba = ba_ref[...]
    beta_all = jax.nn.sigmoid(ba)
    eg_all = jnp.exp(-jnp.exp(alog_ref[...]) * _softplus(ba + dtb_ref[...]))

    for h in range(N_HEADS):
        lo = h * HEAD_D
        q = qc_scr[:, lo:lo + HEAD_D]
        k = qc_scr[:, dc + lo:dc + lo + HEAD_D]
        v = qc_scr[:, 2 * dc + lo:2 * dc + lo + HEAD_D]
        qn = q * lax.rsqrt(jnp.sum(q * q, axis=-1, keepdims=True) + EPS) * (HEAD_D ** -0.5)
        kn = k * lax.rsqrt(jnp.sum(k * k, axis=-1, keepdims=True) + EPS)
        qk = jnp.sum(qn * kn, axis=-1, keepdims=True)
        kn_t = kn.T
        qn_t = qn.T
        for b in range(tb):
            s_old = s_in_ref[b, h]
            kc = jnp.broadcast_to(kn_t[:, b:b + 1], s_old.shape)
            e = eg_all[b:b + 1, N_HEADS + h:N_HEADS + h + 1]
            ks = jnp.sum(s_old * kc, axis=0, keepdims=True)
            qs = jnp.sum(s_old * qn_t[:, b:b + 1], axis=0, keepdims=True)
            v_new = beta_all[b:b + 1, h:h + 1] * (v[b:b + 1, :] - e * ks)
            o_scr[b:b + 1, lo:lo + HEAD_D] = e * qs + qk[b:b + 1, :] * v_new
            s_out_ref[b, h] = s_old * e + kc * v_new
        z = proj_ref[:, 3 * dc + dq + lo:3 * dc + dq + lo + HEAD_D]
        o = o_scr[:, lo:lo + HEAD_D]
        mix_ref[:, dc + lo:dc + lo + HEAD_D] = _rms(o) * gng_ref[...] * _silu(z)


def _mixer_sample(proj, ba, hist_a, hist_q, s_in, conv_a_w, gdn_conv_w, a_log, dt_bias, g_conv_out,
                  gdn_norm_g):
    n = proj.shape[0]
    tb = SAMPLE_GROUP
    dc = N_HEADS * HEAD_D
    dq = 3 * dc
    dproj = proj.shape[1]
    const = lambda shape: pl.BlockSpec(shape, lambda i: (0,) * len(shape))
    rows = lambda width: pl.BlockSpec((tb, width), lambda i: (i, 0))
    state = pl.BlockSpec((tb, N_HEADS, HEAD_D, HEAD_D), lambda i: (i, 0, 0, 0))
    return pl.pallas_call(
        _mixer_sample_kernel,
        out_shape=(jax.ShapeDtypeStruct((n, 2 * dc), F32),
                   jax.ShapeDtypeStruct((n, 2 * dc), F32),
                   jax.ShapeDtypeStruct((n, 3 * dq), F32),
                   jax.ShapeDtypeStruct((n, N_HEADS, HEAD_D, HEAD_D), F32)),
        grid=(n // tb,),
        in_specs=[rows(dproj), rows(LANES), rows(2 * dc), rows(3 * dq), state,
                  const((3, dc)), const((4, dq)), const((1, LANES)), const((1, LANES)),
                  const((1, dc)), const((1, HEAD_D))],
        out_specs=(rows(2 * dc), rows(2 * dc), rows(3 * dq), state),
        scratch_shapes=[pltpu.VMEM((tb, dq), F32), pltpu.VMEM((tb, dc), F32)],
        compiler_params=_params(("arbitrary",)),
        name="mixer_sample",
    )(proj, ba, hist_a, hist_q, s_in, conv_a_w, gdn_conv_w, _lane_row(a_log, N_HEADS),
      _lane_row(dt_bias, N_HEADS), g_conv_out.reshape(1, dc), gdn_norm_g.reshape(1, HEAD_D))


POST_MIX_SUB_ROWS = 256


def _post_mix_kernel(mix_ref, x_ref, gate_ref, sc_ref, sh_ref, gpost_ref, gpre_ref, wout_ref,
                     rwh_ref, rwl_ref, rb_ref, x1_ref, h2_ref, lg_ref):
    tm = x_ref.shape[0]
    sub = min(tm, POST_MIX_SUB_ROWS)
    tiles = [slice(r0, r0 + sub) for r0 in range(0, tm, sub)]
    mixes = [jnp.dot(mix_ref[rows].astype(BF16), wout_ref[...], preferred_element_type=F32)
             for rows in tiles]
    for rows, mix in zip(tiles, mixes):
        per_row = lambda ref: ref[rows] if ref.shape[0] == tm else ref[...]
        x1 = x_ref[rows] + per_row(gate_ref) * (_rms(mix) * gpost_ref[...])
        x1_ref[rows] = x1
        h2 = (_rms(x1) * gpre_ref[...]) * (1.0 + per_row(sc_ref)) + per_row(sh_ref)
        h2_ref[rows] = h2
        hi = h2.astype(BF16)
        lo = (h2 - hi.astype(F32)).astype(BF16)
        rwh = rwh_ref[...]
        lg_ref[rows] = (jnp.dot(hi, rwh, preferred_element_type=F32)
                        + jnp.dot(lo, rwh, preferred_element_type=F32)
                        + jnp.dot(hi, rwl_ref[...], preferred_element_type=F32) + rb_ref[...])


def _post_mix(mix_in, x, gate, scale, shift, g_post, g_pre, w_out, rw_hi, rw_lo, rb, rows_per_group):
    m, d = x.shape
    tm = min(m, POST_MIX_ROWS, rows_per_group if gate.ndim == 3 else m)
    const = lambda shape: pl.BlockSpec(shape, lambda i: (0,) * len(shape))
    rows = lambda width: pl.BlockSpec((tm, width), lambda i: (i, 0))
    return pl.pallas_call(
        _post_mix_kernel,
        out_shape=(jax.ShapeDtypeStruct((m, d), F32), jax.ShapeDtypeStruct((m, d), F32),
                   jax.ShapeDtypeStruct((m, LANES), F32)),
        grid=(m // tm,),
        in_specs=[rows(d), rows(d),
                  _mod_spec(gate, tm, rows_per_group),
                  _mod_spec(scale, tm, rows_per_group),
                  _mod_spec(shift, tm, rows_per_group),
                  const((1, d)), const((1, d)), const((d, d)),
                  const((d, LANES)), const((d, LANES)), const((1, LANES))],
        out_specs=(rows(d), rows(d), rows(LANES)),
        compiler_params=_params(("arbitrary",)),
        name="post_mix",
    )(mix_in, x, gate, scale, shift, g_post.reshape(1, d), g_pre.reshape(1, d), w_out, rw_hi, rw_lo, rb)


ROUTE_TOKENS = 128


def _route_kernel(lgp_ref, lgs_ref, idx_ref, p_ref, rank_ref, cnt_ref, carry, *, prompt_tiles):
    tm = lgp_ref.shape[0]

    @pl.when(pl.program_id(0) == 0)
    def _():
        carry[...] = jnp.zeros_like(carry)

    l = jnp.where(pl.program_id(0) < prompt_tiles, lgp_ref[...], lgs_ref[...])
    lane = lax.broadcasted_iota(I32, l.shape, 1)
    lane_f = lane.astype(F32)
    vals, hots = [], []
    idx_out = jnp.zeros(l.shape, F32)
    for k in range(TOP_K):
        m = jnp.max(l, axis=-1, keepdims=True)
        idx = jnp.min(jnp.where(l == m, lane_f, float(LANES)), axis=-1, keepdims=True)
        hot = lane_f == idx
        vals.append(m)
        hots.append(hot)
        idx_out = jnp.where(lane == k, idx, idx_out)
        l = jnp.where(hot, -jnp.inf, l)
    exps = [jnp.exp(v - vals[0]) for v in vals]
    denom = exps[0] + exps[1] + exps[2] + exps[3]
    p_out = jnp.zeros(l.shape, F32)
    for k in range(TOP_K):
        p_out = jnp.where(lane == k, exps[k] / denom, p_out)
    member = jnp.where(hots[0] | hots[1] | hots[2] | hots[3], 1.0, 0.0).astype(F32)
    row = lax.broadcasted_iota(I32, (tm, tm), 0)
    col = lax.broadcasted_iota(I32, (tm, tm), 1)
    before = jnp.where(row > col, 1.0, 0.0).astype(BF16)
    prefix = jnp.dot(before, member.astype(BF16), preferred_element_type=F32) + carry[...]
    rank_out = jnp.zeros(l.shape, F32)
    for k in range(TOP_K):
        r = jnp.sum(jnp.where(hots[k], prefix, 0.0), axis=-1, keepdims=True)
        rank_out = jnp.where(lane == k, r, rank_out)
    carry[...] = carry[...] + jnp.sum(member, axis=0, keepdims=True)
    idx_ref[...] = idx_out.astype(I32)
    p_ref[...] = p_out
    rank_ref[...] = rank_out.astype(I32)
    cnt_ref[...] = carry[...]


def _route(logits_p, logits_s):
    tm = ROUTE_TOKENS
    pt = logits_p.shape[0] // tm
    n = logits_p.shape[0] + logits_s.shape[0]
    tile = pl.BlockSpec((tm, LANES), lambda i: (i, 0))
    return pl.pallas_call(
        functools.partial(_route_kernel, prompt_tiles=pt),
        out_shape=(jax.ShapeDtypeStruct((n, LANES), I32), jax.ShapeDtypeStruct((n, LANES), F32),
                   jax.ShapeDtypeStruct((n, LANES), I32), jax.ShapeDtypeStruct((1, LANES), F32)),
        grid=(n // tm,),
        in_specs=[pl.BlockSpec((tm, LANES), lambda i: (jnp.minimum(i, pt - 1), 0)),
                  pl.BlockSpec((tm, LANES), lambda i: (jnp.maximum(i - pt, 0), 0))],
        out_specs=(tile, tile, tile, pl.BlockSpec((1, LANES), lambda i: (0, 0))),
        scratch_shapes=[pltpu.VMEM((1, LANES), F32)],
        compiler_params=_params(("arbitrary",)),
        name="route",
    )(logits_p, logits_s)


INVERT_TOKENS_MAX = 1024


def _largest_tile(n, unit, cap):
    return max(m for m in range(unit, cap + 1, unit) if n % m == 0)


def _invert_kernel(meta_ref, slot_ref, table_ref, *, pad_token, n_exp, tokens):
    i = pl.program_id(0)
    pairs = tokens * TOP_K

    @pl.when(i == 0)
    def _():
        def fill(r, carry):
            table_ref[r] = pad_token
            return carry

        def fill_group_tail(e, carry):
            lax.fori_loop(meta_ref[e] + meta_ref[2 * n_exp + e], meta_ref[e] + meta_ref[n_exp + e], fill, 0)
            return carry

        lax.fori_loop(0, n_exp, fill_group_tail, 0)
        lax.fori_loop(meta_ref[3 * n_exp] * EXPERT_ROW_TILE, table_ref.shape[0], fill, 0)

    def put(p, carry):
        table_ref[slot_ref[0, p]] = i * tokens + (p >> TOP_K_SHIFT)
        return carry

    lax.fori_loop(0, pairs, put, 0, unroll=16)


def _invert(slots, meta, n_rows, pad_token, n_exp):
    n = slots.shape[0]
    tokens = _largest_tile(n, LANES, INVERT_TOKENS_MAX)
    nt = n // tokens
    pairs = tokens * TOP_K
    return pl.pallas_call(
        functools.partial(_invert_kernel, pad_token=pad_token, n_exp=n_exp, tokens=tokens),
        out_shape=jax.ShapeDtypeStruct((n_rows,), I32),
        grid_spec=pltpu.PrefetchScalarGridSpec(
            num_scalar_prefetch=1,
            grid=(nt,),
            in_specs=[pl.BlockSpec((None, 1, pairs), lambda i, meta: (i, 0, 0), memory_space=pltpu.SMEM)],
            out_specs=pl.BlockSpec(memory_space=pltpu.SMEM)),
        compiler_params=_params(("arbitrary",)),
        name="invert",
    )(meta, slots.reshape(nt, 1, pairs))


ROW_UNROLL = 8
ROW_UNROLL_SHIFT = 3
DISPATCH_TILES = 6


def _dispatch_kernel(na_ref, split_ref, tok_ref, tok_next_ref, h2p_ref, h2s_ref, xs_ref, buf, sem, *,
                     n_prompt):
    i = pl.program_id(0)
    tile = EXPERT_ROW_TILE
    n_active = na_ref[0]

    def issue_tile(tok, t, sub, slot):
        split = split_ref[t]
        base = sub * tile

        def prompt_copy(r):
            pltpu.make_async_copy(h2p_ref.at[pl.ds(tok[0, base + r], 1)], buf.at[slot, pl.ds(base + r, 1)],
                                  sem.at[slot, sub]).start()

        def sample_copy(r):
            pltpu.make_async_copy(h2s_ref.at[pl.ds(tok[0, base + r] - n_prompt, 1)],
                                  buf.at[slot, pl.ds(base + r, 1)], sem.at[slot, sub]).start()

        def rows(lo, hi, fn):
            def body(r, carry):
                fn(r)
                return carry
            lax.fori_loop(lo, hi, body, 0)

        def groups(lo, hi, fn):
            def body(g, carry):
                for u in range(ROW_UNROLL):
                    fn(g * ROW_UNROLL + u)
                return carry
            lax.fori_loop(lo, hi, body, 0)

        whole = split >> ROW_UNROLL_SHIFT
        first = (split + ROW_UNROLL - 1) >> ROW_UNROLL_SHIFT
        groups(0, whole, prompt_copy)
        rows(whole * ROW_UNROLL, split, prompt_copy)
        rows(split, first * ROW_UNROLL, sample_copy)
        groups(first, tile // ROW_UNROLL, sample_copy)

    def issue(tok, step, slot):
        for sub in range(DISPATCH_TILES):
            t = step * DISPATCH_TILES + sub

            @pl.when(t < n_active)
            def _(t=t, sub=sub):
                issue_tile(tok, t, sub, slot)

    @pl.when(i == 0)
    def _():
        issue(tok_ref, 0, 0)

    @pl.when((i + 1) * DISPATCH_TILES < n_active)
    def _():
        issue(tok_next_ref, i + 1, (i + 1) & 1)

    slot = i & 1
    for sub in range(DISPATCH_TILES):
        t = i * DISPATCH_TILES + sub
        out_rows = slice(sub * tile, (sub + 1) * tile)

        @pl.when(t < n_active)
        def _(sub=sub, out_rows=out_rows):
            pltpu.make_async_copy(h2p_ref.at[pl.ds(0, tile)], buf.at[slot, out_rows], sem.at[slot, sub]).wait()
            xs_ref[out_rows] = buf[slot, out_rows].astype(BF16)

        @pl.when(t >= n_active)
        def _(out_rows=out_rows):
            xs_ref[out_rows] = jnp.zeros((tile, xs_ref.shape[1]), BF16)


def _dispatch(h2_p, h2_s, table, split, n_active):
    d = h2_p.shape[1]
    rows = EXPERT_ROW_TILE * DISPATCH_TILES
    n_rows = table.shape[0]
    assert n_rows % rows == 0, (n_rows, rows)
    nt = n_rows // rows
    return pl.pallas_call(
        functools.partial(_dispatch_kernel, n_prompt=h2_p.shape[0]),
        out_shape=jax.ShapeDtypeStruct((n_rows, d), BF16),
        grid_spec=pltpu.PrefetchScalarGridSpec(
            num_scalar_prefetch=2,
            grid=(nt,),
            in_specs=[pl.BlockSpec((None, 1, rows), lambda i, na, sp: (i, 0, 0), memory_space=pltpu.SMEM),
                      pl.BlockSpec((None, 1, rows), lambda i, na, sp: (jnp.minimum(i + 1, nt - 1), 0, 0),
                                   memory_space=pltpu.SMEM),
                      pl.BlockSpec(memory_space=pl.ANY), pl.BlockSpec(memory_space=pl.ANY)],
            out_specs=pl.BlockSpec((rows, d), lambda i, na, sp: (i, 0)),
            scratch_shapes=[pltpu.VMEM((2, rows, d), F32), pltpu.SemaphoreType.DMA((2, DISPATCH_TILES))]),
        compiler_params=_params(("arbitrary",)),
        name="dispatch",
    )(n_active, split, table.reshape(nt, 1, rows), table.reshape(nt, 1, rows), h2_p, h2_s)


def _stream_expert_weights(te_ref, first_ref, run_ref, next_ref, meta_ref, n_col_blocks, copies, consume):
    j = pl.program_id(0)
    i = pl.program_id(1)
    n_active = meta_ref[0]
    n_runs = meta_ref[1]

    @pl.when(jnp.logical_and(j == 0, i == 0))
    def _():
        for cp in copies(te_ref[0], 0, 0):
            cp.start()

    @pl.when(jnp.logical_and(i < n_active, first_ref[i] == 1))
    def _():
        slot = (j * n_runs + run_ref[i]) & 1
        for cp in copies(te_ref[i], j, slot):
            cp.wait()
        last = run_ref[i] == n_runs - 1
        e_next = jnp.where(last, te_ref[0], next_ref[i])
        j_next = jnp.where(last, j + 1, j)

        @pl.when(j_next < n_col_blocks)
        def _():
            for cp in copies(e_next, j_next, 1 - slot):
                cp.start()

        consume(slot)


def _gate_up_kernel(te_ref, first_ref, run_ref, next_ref, meta_ref, x_ref, bg_ref, bu_ref, w_hbm, o_ref,
                    wbuf, wg_b, wu_b, sem, *, nj, tn):
    i = pl.program_id(1)
    active = i < meta_ref[0]

    def copies(e, jj, slot):
        return [pltpu.make_async_copy(w_hbm.at[e, :, pl.ds(pl.multiple_of((c * nj + jj) * tn, tn), tn)],
                                      wbuf.at[slot, c], sem.at[slot, c]) for c in range(2)]

    def consume(slot):
        wg_b[...] = wbuf[slot, 0].astype(BF16)
        wu_b[...] = wbuf[slot, 1].astype(BF16)

    _stream_expert_weights(te_ref, first_ref, run_ref, next_ref, meta_ref, nj, copies, consume)

    @pl.when(active)
    def _():
        x = x_ref[...]
        gate = jnp.dot(x, wg_b[...], preferred_element_type=F32) + bg_ref[...]
        up = jnp.dot(x, wu_b[...], preferred_element_type=F32) + bu_ref[...]
        gate = jnp.minimum(gate, SWIGLU_LIMIT)
        up = jnp.clip(up, -SWIGLU_LIMIT, SWIGLU_LIMIT)
        o_ref[...] = ((up + 1.0) * gate * jax.nn.sigmoid(SWIGLU_ALPHA * gate)).astype(BF16)

    @pl.when(jnp.logical_not(active))
    def _():
        o_ref[...] = jnp.zeros_like(o_ref)


def _expert_gate_up(xs, w_gu, b_gu, sched):
    n_rows, d = xs.shape
    n_exp, _, f2 = w_gu.shape
    f = f2 // 2
    tm = EXPERT_ROW_TILE
    tn = MATMUL_COLS
    nj = f // tn
    row = lambda i, meta: jnp.minimum(i, meta[0] - 1)
    return pl.pallas_call(
        functools.partial(_gate_up_kernel, nj=nj, tn=tn),
        out_shape=jax.ShapeDtypeStruct((n_rows, f), BF16),
        grid_spec=pltpu.PrefetchScalarGridSpec(
            num_scalar_prefetch=5,
            grid=(nj, n_rows // tm),
            in_specs=[pl.BlockSpec((tm, d), lambda j, i, te, fi, ru, nx, meta: (row(i, meta), 0)),
                      pl.BlockSpec((None, 1, tn), lambda j, i, te, fi, ru, nx, meta: (te[row(i, meta)], 0, j)),
                      pl.BlockSpec((None, 1, tn),
                                   lambda j, i, te, fi, ru, nx, meta: (te[row(i, meta)], 0, nj + j)),
                      pl.BlockSpec(memory_space=pl.ANY)],
            out_specs=pl.BlockSpec((tm, tn), lambda j, i, te, fi, ru, nx, meta: (i, j)),
            scratch_shapes=[pltpu.VMEM((2, 2, d, tn), F32), pltpu.VMEM((d, tn), BF16),
                            pltpu.VMEM((d, tn), BF16), pltpu.SemaphoreType.DMA((2, 2))]),
        compiler_params=_params(("arbitrary", "arbitrary")),
        name="expert_gate_up",
    )(*sched, xs, b_gu.reshape(n_exp, 1, f2), b_gu.reshape(n_exp, 1, f2), w_gu)


def _down_kernel(te_ref, first_ref, run_ref, next_ref, meta_ref, a_ref, b_ref, w_hbm, o_ref, wbuf, w_b, sem):
    i = pl.program_id(1)
    active = i < meta_ref[0]

    def copies(e, jj, slot):
        return [pltpu.make_async_copy(w_hbm.at[e], wbuf.at[slot], sem.at[slot])]

    def consume(slot):
        w_b[...] = wbuf[slot].astype(BF16)

    _stream_expert_weights(te_ref, first_ref, run_ref, next_ref, meta_ref, 1, copies, consume)

    @pl.when(active)
    def _():
        o_ref[...] = jnp.dot(a_ref[...], w_b[...], preferred_element_type=F32) + b_ref[...]

    @pl.when(jnp.logical_not(active))
    def _():
        o_ref[...] = jnp.zeros_like(o_ref)


def _expert_down(act, w_d, b_d, sched):
    n_rows, f = act.shape
    n_exp, _, d = w_d.shape
    tm = EXPERT_ROW_TILE
    row = lambda i, meta: jnp.minimum(i, meta[0] - 1)
    return pl.pallas_call(
        _down_kernel,
        out_shape=jax.ShapeDtypeStruct((n_rows, d), F32),
        grid_spec=pltpu.PrefetchScalarGridSpec(
            num_scalar_prefetch=5,
            grid=(1, n_rows // tm),
            in_specs=[pl.BlockSpec((tm, f), lambda j, i, te, fi, ru, nx, meta: (row(i, meta), 0)),
                      pl.BlockSpec((None, 1, d), lambda j, i, te, fi, ru, nx, meta: (te[row(i, meta)], 0, 0)),
                      pl.BlockSpec(memory_space=pl.ANY)],
            out_specs=pl.BlockSpec((tm, d), lambda j, i, te, fi, ru, nx, meta: (i, 0)),
            scratch_shapes=[pltpu.VMEM((2, f, d), F32), pltpu.VMEM((f, d), BF16),
                            pltpu.SemaphoreType.DMA((2,))]),
        compiler_params=_params(("arbitrary", "arbitrary")),
        name="expert_down",
    )(*sched, act, b_d.reshape(n_exp, 1, d), w_d)


COMBINE_TOKENS = 256


def _combine_kernel(slot_ref, slot_next_ref, p_ref, x1_ref, gate_ref, g_ref, y_ref, o_ref, buf, sem):
    i = pl.program_id(0)
    tm = x1_ref.shape[0]
    pairs = tm * TOP_K

    def issue(slots, b):
        def body(g, carry):
            for u in range(ROW_UNROLL):
                p = g * ROW_UNROLL + u
                tok = g * (ROW_UNROLL // TOP_K) + u // TOP_K
                pltpu.make_async_copy(y_ref.at[pl.ds(slots[0, p], 1)], buf.at[b, u % TOP_K, pl.ds(tok, 1)],
                                      sem.at[b]).start()
            return carry
        lax.fori_loop(0, pairs // ROW_UNROLL, body, 0)

    @pl.when(i == 0)
    def _():
        issue(slot_ref, 0)

    @pl.when(i + 1 < pl.num_programs(0))
    def _():
        issue(slot_next_ref, (i + 1) & 1)

    b = i & 1
    for k in range(TOP_K):
        pltpu.make_async_copy(y_ref.at[pl.ds(0, tm)], buf.at[b, k], sem.at[b]).wait()
    probs = p_ref[...]
    f = probs[:, 0:1] * buf[b, 0]
    for k in range(1, TOP_K):
        f = f + probs[:, k:k + 1] * buf[b, k]
    o_ref[...] = x1_ref[...] + gate_ref[...] * (_rms(f) * g_ref[...])


def _combine(y, slots, probs, x1, gate, g_post, rows_per_group):
    m, d = x1.shape
    tm = min(m, COMBINE_TOKENS)
    nt = m // tm
    pairs = tm * TOP_K
    return pl.pallas_call(
        _combine_kernel,
        out_shape=jax.ShapeDtypeStruct((m, d), F32),
        grid=(nt,),
        in_specs=[pl.BlockSpec((None, 1, pairs), lambda i: (i, 0, 0), memory_space=pltpu.SMEM),
                  pl.BlockSpec((None, 1, pairs), lambda i: (jnp.minimum(i + 1, nt - 1), 0, 0),
                               memory_space=pltpu.SMEM),
                  pl.BlockSpec((tm, LANES), lambda i: (i, 0)),
                  pl.BlockSpec((tm, d), lambda i: (i, 0)),
                  _mod_spec(gate, tm, rows_per_group),
                  pl.BlockSpec((1, d), lambda i: (0, 0)),
                  pl.BlockSpec(memory_space=pl.ANY)],
        out_specs=pl.BlockSpec((tm, d), lambda i: (i, 0)),
        scratch_shapes=[pltpu.VMEM((2, TOP_K, tm, d), F32), pltpu.SemaphoreType.DMA((2,))],
        compiler_params=_params(("arbitrary",)),
        name="combine",
    )(slots.reshape(nt, 1, pairs), slots.reshape(nt, 1, pairs), probs, x1, gate, g_post.reshape(1, d), y)


def _moe(h2_p, h2_s, logits_p, logits_s, n_exp, w_gu, b_gu, w_d, b_d):
    n = h2_p.shape[0] + h2_s.shape[0]
    tile = EXPERT_ROW_TILE
    top_i, probs, rank, counts = _route(logits_p, logits_s)
    counts = counts[0, :n_exp].astype(I32)
    cap = (counts + tile - 1) // tile * tile
    ends = jnp.cumsum(cap)
    offs = ends - cap
    n_tiles = (n * TOP_K) // tile + n_exp
    n_active = (ends[-1] // tile).astype(I32).reshape(1)
    tile_id = jnp.arange(n_tiles, dtype=I32)
    tile_expert = jnp.minimum(jnp.sum(ends[None, :] <= tile_id[:, None] * tile, axis=1), n_exp - 1).astype(I32)
    first = ((tile_id == 0) | (tile_expert != jnp.roll(tile_expert, 1))) & (tile_id < n_active[0])
    run = jnp.cumsum(first.astype(I32)) - 1
    next_expert = tile_expert[jnp.minimum(ends[tile_expert] // tile, n_tiles - 1)]
    sched = (tile_expert, first.astype(I32), run.astype(I32), next_expert.astype(I32),
             jnp.stack([n_active[0], jnp.sum(first.astype(I32))]).astype(I32))
    expert_ids = jnp.arange(n_exp, dtype=I32)
    pair_offs = jnp.sum(jnp.where(top_i[:, :TOP_K, None] == expert_ids, offs, 0), axis=-1)
    slots = pair_offs + rank[:, :TOP_K]
    n_prompt = h2_p.shape[0]
    meta = jnp.concatenate([offs, cap, counts, n_active]).astype(I32)
    table = _invert(slots, meta, n_tiles * tile, n_prompt, n_exp)
    split = jnp.sum(table.reshape(n_tiles, tile) < n_prompt, axis=1).astype(I32)
    xs = _dispatch(h2_p, h2_s, table, split, n_active)
    act = _expert_gate_up(xs, w_gu, b_gu, sched)
    y = _expert_down(act, w_d, b_d, sched)
    return y, slots, probs


def _layer(xp, xs_, cp_mod, cs_mod, hist_a, hist_q, s0, g_pre_mix, g_post_mix, g_pre_ffn, g_post_ffn,
           w_in, conv_a_w, gdn_conv_w, a_log, dt_bias, g_conv_out, gdn_norm_g, w_out, router_w, router_b,
           w_gu, b_gu, w_d, b_d):
    bsz, seq, d = xp.shape
    ns = xs_.shape[0]
    n_exp = router_w.shape[1]
    dc = N_HEADS * HEAD_D
    d_main = 3 * dc + 3 * dc + dc
    xp2 = xp.reshape(bsz * seq, d)
    xs2 = xs_.reshape(ns, d)

    w_main = w_in.astype(BF16)
    w_ba = jnp.zeros((d, LANES), BF16).at[:, :2 * N_HEADS].set(w_main[:, d_main:])
    w_out_b = w_out.astype(BF16)
    rw = jnp.zeros((d, LANES), F32).at[:, :n_exp].set(router_w)
    rw_hi = rw.astype(BF16)
    rw_lo = (rw - rw_hi.astype(F32)).astype(BF16)
    rb = jnp.full((1, LANES), NEG_BIG, F32).at[0, :n_exp].set(router_b)

    mp = [cp_mod[:, i * d:(i + 1) * d].reshape(bsz, 1, d) for i in range(6)]
    ms = [cs_mod[:, i * d:(i + 1) * d] for i in range(6)]

    proj_p, ba_p = _in_proj(xp2, mp[1], mp[0], g_pre_mix, w_main, w_ba, seq, d_main)
    proj_s, ba_s = _in_proj(xs2, ms[1], ms[0], g_pre_mix, w_main, w_ba, 1, d_main)

    mix_p, ha_p, hq_p, s_p = _mixer_prompt(proj_p, ba_p, bsz, seq, conv_a_w, gdn_conv_w, a_log, dt_bias,
                                           g_conv_out, gdn_norm_g)
    mix_s, ha_s, hq_s, s_s = _mixer_sample(proj_s, ba_s, hist_a.reshape(ns, 2 * dc),
                                           hist_q.reshape(ns, 9 * dc), s0, conv_a_w, gdn_conv_w, a_log,
                                           dt_bias, g_conv_out, gdn_norm_g)

    x1_p, h2_p, lg_p = _post_mix(mix_p, xp2, mp[2], mp[4], mp[3], g_post_mix, g_pre_ffn, w_out_b, rw_hi,
                                 rw_lo, rb, seq)
    x1_s, h2_s, lg_s = _post_mix(mix_s, xs2, ms[2], ms[4], ms[3], g_post_mix, g_pre_ffn, w_out_b, rw_hi,
                                 rw_lo, rb, 1)

    y, slots, probs = _moe(h2_p, h2_s, lg_p, lg_s, n_exp, w_gu, b_gu, w_d, b_d)
    np_ = bsz * seq
    out_p = _combine(y, slots[:np_], probs[:np_], x1_p, mp[5], g_post_ffn, seq)
    out_s = _combine(y, slots[np_:], probs[np_:], x1_s, ms[5], g_post_ffn, 1)
    return (out_p.reshape(bsz, seq, d), out_s.reshape(ns, 1, d), ha_p, hq_p, s_p,
            ha_s.reshape(ns, 2, dc), hq_s.reshape(ns, 3, 3 * dc), s_s)


def kernel(x_prompt, x_sample, state_conv_a, state_gdn_conv, state_gdn_S, c_prompt, c_sample, w_mod, b_mod, g_pre_mix, g_post_mix, g_pre_ffn, g_post_ffn, w_in, conv_a_w, gdn_conv_w, gdn_a_log, gdn_dt_bias, g_conv_out, gdn_norm_g, w_out, router_w, router_b, exp_w_gate_up, exp_b_gate_up, exp_w_down, exp_b_down):
    depth = w_mod.shape[0]
    bp = x_prompt.shape[0]
    xp, xs_ = x_prompt, x_sample
    outs = [[] for _ in range(6)]
    for l in range(depth):
        mod = _modulation(jnp.concatenate([c_prompt, c_sample], axis=0), w_mod[l], b_mod[l])
        res = _layer(xp, xs_, mod[:bp], mod[bp:], state_conv_a[l], state_gdn_conv[l], state_gdn_S[l],
                     g_pre_mix[l], g_post_mix[l], g_pre_ffn[l], g_post_ffn[l], w_in[l], conv_a_w[l],
                     gdn_conv_w[l], gdn_a_log[l], gdn_dt_bias[l], g_conv_out[l], gdn_norm_g[l], w_out[l],
                     router_w[l], router_b[l], exp_w_gate_up[l], exp_b_gate_up[l], exp_w_down[l],
                     exp_b_down[l])
        xp, xs_ = res[0], res[1]
        for acc, r in zip(outs, res[2:]):
            acc.append(r)
    return (xp, xs_) + tuple(o[0][None] if depth == 1 else jnp.stack(o) for o in outs)
```

```python
import functools

import jax
import jax.numpy as jnp
from jax import lax
from jax.experimental import pallas as pl
from jax.experimental.pallas import tpu as pltpu

F32 = jnp.float32
BF16 = jnp.bfloat16
I32 = jnp.int32
HIGHEST = lax.Precision.HIGHEST

EPS = 1e-6
N_HEADS = 8
HEAD_D = 128
TOP_K = 4
TOP_K_SHIFT = 2
SWIGLU_LIMIT = 7.0
SWIGLU_ALPHA = 1.702
CHUNK = 64
LANES = 128
EXPERT_ROW_TILE = 256
MATMUL_COLS = 1024
IN_PROJ_ROWS = 1024
POST_MIX_ROWS = 512
HALO = 8
NEG_BIG = -1e30
VMEM_LIMIT = 56 * 1024 * 1024


def _params(semantics, vmem=VMEM_LIMIT):
    return pltpu.CompilerParams(dimension_semantics=semantics, vmem_limit_bytes=vmem)


def _mm(a, b):
    return jnp.dot(a.astype(BF16), b.astype(BF16), preferred_element_type=F32)


def _mm_nt(a, b):
    return lax.dot_general(a.astype(BF16), b.astype(BF16), (((1,), (1,)), ((), ())),
                           preferred_element_type=F32)


def _rms(x):
    return x * lax.rsqrt(jnp.mean(x * x, axis=-1, keepdims=True) + EPS)


def _silu(x):
    return x * jax.nn.sigmoid(x)


def _softplus(x):
    return jnp.maximum(x, 0.0) + jnp.log1p(jnp.exp(-jnp.abs(x)))


def _mod_kernel(c_ref, w_ref, b_ref, o_ref):
    s = _silu(c_ref[...])
    o_ref[...] = _mm(s, w_ref[...]) + b_ref[...]


def _modulation(c_all, w_mod, b_mod):
    n, d = c_all.shape
    m = w_mod.shape[1]
    tn = MATMUL_COLS
    return pl.pallas_call(
        _mod_kernel,
        out_shape=jax.ShapeDtypeStruct((n, m), F32),
        grid=(m // tn,),
        in_specs=[pl.BlockSpec((n, d), lambda j: (0, 0)),
                  pl.BlockSpec((d, tn), lambda j: (0, j)),
                  pl.BlockSpec((1, tn), lambda j: (0, j))],
        out_specs=pl.BlockSpec((n, tn), lambda j: (0, j)),
        compiler_params=_params(("arbitrary",)),
        name="modulation",
    )(c_all, w_mod, b_mod.reshape(1, m))


def _mod_spec(arr, tm, rows_per_group):
    if arr.ndim == 3:
        tiles = rows_per_group // tm
        return pl.BlockSpec((None, 1, arr.shape[-1]), lambda i, *_: (i // tiles, 0, 0))
    return pl.BlockSpec((tm, arr.shape[-1]), lambda i, *_: (i, 0))


def _proj_kernel(x_ref, sc_ref, sh_ref, g_ref, w_ref, wba_ref, o_ref, ba_ref, h_scr):
    @pl.when(pl.program_id(1) == 0)
    def _():
        h = (_rms(x_ref[...]) * g_ref[...]) * (1.0 + sc_ref[...]) + sh_ref[...]
        hb = h.astype(BF16)
        h_scr[...] = hb
        ba_ref[...] = jnp.dot(hb, wba_ref[...], preferred_element_type=F32)

    o_ref[...] = jnp.dot(h_scr[...], w_ref[...], preferred_element_type=F32)


def _in_proj(x, scale, shift, g, w_main, w_ba, rows_per_group, n):
    m, d = x.shape
    tm = min(m, IN_PROJ_ROWS, rows_per_group if scale.ndim == 3 else m)
    tn = MATMUL_COLS
    return pl.pallas_call(
        _proj_kernel,
        out_shape=(jax.ShapeDtypeStruct((m, n), F32), jax.ShapeDtypeStruct((m, LANES), F32)),
        grid=(m // tm, n // tn),
        in_specs=[pl.BlockSpec((tm, d), lambda i, j: (i, 0)),
                  _mod_spec(scale, tm, rows_per_group),
                  _mod_spec(shift, tm, rows_per_group),
                  pl.BlockSpec((1, d), lambda i, j: (0, 0)),
                  pl.BlockSpec((d, tn), lambda i, j: (0, j)),
                  pl.BlockSpec((d, LANES), lambda i, j: (0, 0))],
        out_specs=(pl.BlockSpec((tm, tn), lambda i, j: (i, j)),
                   pl.BlockSpec((tm, LANES), lambda i, j: (i, 0))),
        scratch_shapes=[pltpu.VMEM((tm, d), BF16)],
        compiler_params=_params(("arbitrary", "arbitrary")),
        name="in_proj",
    )(x, scale, shift, g.reshape(1, d), w_main, w_ba)


PROMPT_SEQS_PER_STEP = 4


def _mixer_prompt_kernel(proj_ref, ba_ref, caw_ref, gcw_ref, alog_ref, dtb_ref, gco_ref, gng_ref,
                         mix_ref, ha_ref, hq_ref, s_ref, extu, extq, qc_scr, s_scr, *, nseq):
    c = CHUNK
    dc = N_HEADS * HEAD_D
    dq = 3 * dc
    t = pl.program_id(1)
    is_last = t == pl.num_programs(1) - 1

    @pl.when(t == 0)
    def _():
        extu[:, 0:HALO, :] = jnp.zeros((nseq, HALO, dc), F32)
        extq[:, 0:HALO, :] = jnp.zeros((nseq, HALO, dq), F32)
        s_scr[...] = jnp.zeros_like(s_scr)

    row = lax.broadcasted_iota(I32, (c, c), 0)
    col = lax.broadcasted_iota(I32, (c, c), 1)
    causal = row >= col
    strict = row > col
    lower = jnp.where(causal, 1.0, 0.0).astype(F32)
    upper = jnp.where(row <= col, 1.0, 0.0).astype(F32)
    caw = caw_ref[...]
    gcw = gcw_ref[...]

    chains = [(sq, h) for sq in range(nseq) for h in range(N_HEADS)]
    heads = range(len(chains))
    qn, kn, vb, kb, kbg, qg, kg, decay, s_decay = ([] for _ in range(9))
    for sq in range(nseq):
        u = proj_ref[sq, :, dc:2 * dc] * proj_ref[sq, :, 2 * dc:3 * dc]
        extu[sq, HALO:HALO + c, :] = u
        ya = (caw[0:1] * extu[sq, HALO - 2:HALO - 2 + c, :] + caw[1:2] * extu[sq, HALO - 1:HALO - 1 + c, :]
              + caw[2:3] * u)
        ya = proj_ref[sq, :, 0:dc] * ya
        mix_ref[sq, :, 0:dc] = (_rms(ya) * gco_ref[...]).astype(BF16)
        last_u = extu[sq, HALO - 2 + c:HALO + c, :]
        extu[sq, HALO - 2:HALO, :] = last_u

        qkv = proj_ref[sq, :, 3 * dc:3 * dc + dq]
        extq[sq, HALO:HALO + c, :] = qkv
        qc = (gcw[0:1] * extq[sq, HALO - 3:HALO - 3 + c, :] + gcw[1:2] * extq[sq, HALO - 2:HALO - 2 + c, :]
              + gcw[2:3] * extq[sq, HALO - 1:HALO - 1 + c, :] + gcw[3:4] * qkv)
        qc_scr[sq] = _silu(qc)
        last_q = extq[sq, HALO - 3 + c:HALO + c, :]
        extq[sq, HALO - 3:HALO, :] = last_q

        @pl.when(is_last)
        def _(sq=sq, last_u=last_u, last_q=last_q):
            ha_ref[sq] = last_u
            hq_ref[sq] = last_q

        ba = ba_ref[sq]
        beta_all = jax.nn.sigmoid(ba)
        g_all = -jnp.exp(alog_ref[...]) * _softplus(ba + dtb_ref[...])
        gc_all = jnp.dot(lower, g_all, precision=HIGHEST, preferred_element_type=F32)
        gc_t = lax.dot_general(g_all, upper, (((0,), (0,)), ((), ())), precision=HIGHEST,
                               preferred_element_type=F32)
        for h in range(N_HEADS):
            lo = h * HEAD_D
            q = qc_scr[sq, :, lo:lo + HEAD_D]
            k = qc_scr[sq, :, dc + lo:dc + lo + HEAD_D]
            v = qc_scr[sq, :, 2 * dc + lo:2 * dc + lo + HEAD_D]
            qn_h = q * lax.rsqrt(jnp.sum(q * q, axis=-1, keepdims=True) + EPS) * (HEAD_D ** -0.5)
            kn_h = k * lax.rsqrt(jnp.sum(k * k, axis=-1, keepdims=True) + EPS)
            beta = beta_all[:, h:h + 1]
            gcc = gc_all[:, N_HEADS + h:N_HEADS + h + 1]
            gcr = gc_t[N_HEADS + h:N_HEADS + h + 1, :]
            gl = gc_all[c - 1:c, N_HEADS + h:N_HEADS + h + 1]
            eg = jnp.exp(gcc)
            kb_h = kn_h * beta
            qn.append(qn_h)
            kn.append(kn_h)
            vb.append(v * beta)
            kb.append(kb_h)
            kbg.append(kb_h * eg)
            qg.append(qn_h * eg)
            kg.append(kn_h * jnp.exp(gl - gcc))
            decay.append(jnp.where(causal, jnp.exp(jnp.minimum(gcc - gcr, 0.0)), 0.0))
            s_decay.append(jnp.exp(gl))

    kq = [_mm_nt(jnp.concatenate([kb[h], qn[h]], axis=0), kn[h]) for h in heads]
    a_mat = [jnp.where(strict, kq[h][:c] * decay[h], 0.0) for h in heads]
    qk = [kq[h][c:] * decay[h] for h in heads]
    n_mat = [-a_mat[h] for h in heads]
    p = a_mat
    size = 2
    while size < c:
        p = [_mm(p[h], p[h]) for h in heads]
        n_p = [_mm(n_mat[h], p[h]) for h in heads]
        n_mat = [n_mat[h] + p[h] + n_p[h] for h in heads]
        size *= 2
    rhs = [jnp.concatenate([vb[h], kbg[h]], axis=-1) for h in heads]
    uw = [rhs[h] + _mm(n_mat[h], rhs[h]) for h in heads]
    s_old = [s_scr[sq, h] for sq, h in chains]
    ws = [_mm(jnp.concatenate([uw[h][:, HEAD_D:], qg[h]], axis=0), s_old[h]) for h in heads]
    v_new = [uw[h][:, :HEAD_D] - ws[h][:c] for h in heads]
    fin = [_mm(jnp.concatenate([qk[h], kg[h].T], axis=0), v_new[h]) for h in heads]
    for i, (sq, h) in enumerate(chains):
        lo = h * HEAD_D
        s_scr[sq, h] = s_old[i] * s_decay[i] + fin[i][c:]
        o = ws[i][c:] + fin[i][:c]
        z = proj_ref[sq, :, 3 * dc + dq + lo:3 * dc + dq + lo + HEAD_D]
        yb = _rms(o) * gng_ref[...] * _silu(z)
        mix_ref[sq, :, dc + lo:dc + lo + HEAD_D] = yb.astype(BF16)

    @pl.when(is_last)
    def _():
        s_ref[...] = s_scr[...]


def _lane_row(vec, offset):
    return jnp.zeros((1, LANES), F32).at[0, offset:offset + vec.shape[0]].set(vec.astype(F32))


def _mixer_prompt(proj, ba, bsz, seq, conv_a_w, gdn_conv_w, a_log, dt_bias, g_conv_out, gdn_norm_g):
    c = CHUNK
    dc = N_HEADS * HEAD_D
    dq = 3 * dc
    dproj = proj.shape[1]
    nt = seq // c
    const = lambda shape: pl.BlockSpec(shape, lambda b, t: (0,) * len(shape))
    nseq = PROMPT_SEQS_PER_STEP if bsz % PROMPT_SEQS_PER_STEP == 0 else 1
    seq_block = lambda *tail: pl.BlockSpec((nseq,) + tail, lambda b, t: (b, t) + (0,) * (len(tail) - 1))
    whole = lambda *tail: pl.BlockSpec((nseq,) + tail, lambda b, t: (b,) + (0,) * len(tail))
    mix, ha, hq, s_fin = pl.pallas_call(
        functools.partial(_mixer_prompt_kernel, nseq=nseq),
        out_shape=(jax.ShapeDtypeStruct((bsz, seq, 2 * dc), BF16),
                   jax.ShapeDtypeStruct((bsz, 2, dc), F32),
                   jax.ShapeDtypeStruct((bsz, 3, dq), F32),
                   jax.ShapeDtypeStruct((bsz, N_HEADS, HEAD_D, HEAD_D), F32)),
        grid=(bsz // nseq, nt),
        in_specs=[seq_block(c, dproj), seq_block(c, LANES),
                  const((3, dc)), const((4, dq)), const((1, LANES)), const((1, LANES)),
                  const((1, dc)), const((1, HEAD_D))],
        out_specs=(seq_block(c, 2 * dc), whole(2, dc), whole(3, dq), whole(N_HEADS, HEAD_D, HEAD_D)),
        scratch_shapes=[pltpu.VMEM((nseq, HALO + c, dc), F32), pltpu.VMEM((nseq, HALO + c, dq), F32),
                        pltpu.VMEM((nseq, c, dq), F32), pltpu.VMEM((nseq, N_HEADS, HEAD_D, HEAD_D), F32)],
        compiler_params=_params(("arbitrary", "arbitrary")),
        name="mixer_prompt",
    )(proj.reshape(bsz, seq, dproj), ba.reshape(bsz, seq, LANES), conv_a_w, gdn_conv_w,
      _lane_row(a_log, N_HEADS), _lane_row(dt_bias, N_HEADS), g_conv_out.reshape(1, dc),
      gdn_norm_g.reshape(1, HEAD_D))
    return mix.reshape(bsz * seq, 2 * dc), ha, hq, s_fin


SAMPLE_GROUP = 16


def _mixer_sample_kernel(proj_ref, ba_ref, hista_ref, histq_ref, s_in_ref, caw_ref, gcw_ref, alog_ref,
                         dtb_ref, gco_ref, gng_ref, mix_ref, ha_ref, hq_ref, s_out_ref, qc_scr, o_scr):
    tb = SAMPLE_GROUP
    dc = N_HEADS * HEAD_D
    dq = 3 * dc

    u = proj_ref[:, dc:2 * dc] * proj_ref[:, 2 * dc:3 * dc]
    caw = caw_ref[...]
    ya = caw[0:1] * hista_ref[:, 0:dc] + caw[1:2] * hista_ref[:, dc:2 * dc] + caw[2:3] * u
    ya = proj_ref[:, 0:dc] * ya
    mix_ref[:, 0:dc] = _rms(ya) * gco_ref[...]
    ha_ref[:, 0:dc] = hista_ref[:, dc:2 * dc]
    ha_ref[:, dc:2 * dc] = u

    qkv = proj_ref[:, 3 * dc:3 * dc + dq]
    gcw = gcw_ref[...]
    qc = (gcw[0:1] * histq_ref[:, 0:dq] + gcw[1:2] * histq_ref[:, dq:2 * dq]
          + gcw[2:3] * histq_ref[:, 2 * dq:3 * dq] + gcw[3:4] * qkv)
    qc_scr[...] = _silu(qc)
    hq_ref[:, 0:dq] = histq_ref[:, dq:2 * dq]
    hq_ref[:, dq:2 * dq] = histq_ref[:, 2 * dq:3 * dq]
    hq_ref[:, 2 * dq:3 * dq] = qkv

    ba = ba_ref[...]
    beta_all = jax.nn.sigmoid(ba)
    eg_all = jnp.exp(-jnp.exp(alog_ref[...]) * _softplus(ba + dtb_ref[...]))

    for h in range(N_HEADS):
        lo = h * HEAD_D
        q = qc_scr[:, lo:lo + HEAD_D]
        k = qc_scr[:, dc + lo:dc + lo + HEAD_D]
        v = qc_scr[:, 2 * dc + lo:2 * dc + lo + HEAD_D]
        qn = q * lax.rsqrt(jnp.sum(q * q, axis=-1, keepdims=True) + EPS) * (HEAD_D ** -0.5)
        kn = k * lax.rsqrt(jnp.sum(k * k, axis=-1, keepdims=True) + EPS)
        qk = jnp.sum(qn * kn, axis=-1, keepdims=True)
        kn_t = kn.T
        qn_t = qn.T
        for b in range(tb):
            s_old = s_in_ref[b, h]
            kc = jnp.broadcast_to(kn_t[:, b:b + 1], s_old.shape)
            e = eg_all[b:b + 1, N_HEADS + h:N_HEADS + h + 1]
            ks = jnp.sum(s_old * kc, axis=0, keepdims=True)
            qs = jnp.sum(s_old * qn_t[:, b:b + 1], axis=0, keepdims=True)
            v_new = beta_all[b:b + 1, h:h + 1] * (v[b:b + 1, :] - e * ks)
            o_scr[b:b + 1, lo:lo + HEAD_D] = e * qs + qk[b:b + 1, :] * v_new
            s_out_ref[b, h] = s_old * e + kc * v_new
        z = proj_ref[:, 3 * dc + dq + lo:3 * dc + dq + lo + HEAD_D]
        o = o_scr[:, lo:lo + HEAD_D]
        mix_ref[:, dc + lo:dc + lo + HEAD_D] = _rms(o) * gng_ref[...] * _silu(z)


def _mixer_sample(proj, ba, hist_a, hist_q, s_in, conv_a_w, gdn_conv_w, a_log, dt_bias, g_conv_out,
                  gdn_norm_g):
    n = proj.shape[0]
    tb = SAMPLE_GROUP
    dc = N_HEADS * HEAD_D
    dq = 3 * dc
    dproj = proj.shape[1]
    const = lambda shape: pl.BlockSpec(shape, lambda i: (0,) * len(shape))
    rows = lambda width: pl.BlockSpec((tb, width), lambda i: (i, 0))
    state = pl.BlockSpec((tb, N_HEADS, HEAD_D, HEAD_D), lambda i: (i, 0, 0, 0))
    return pl.pallas_call(
        _mixer_sample_kernel,
        out_shape=(jax.ShapeDtypeStruct((n, 2 * dc), F32),
                   jax.ShapeDtypeStruct((n, 2 * dc), F32),
                   jax.ShapeDtypeStruct((n, 3 * dq), F32),
                   jax.ShapeDtypeStruct((n, N_HEADS, HEAD_D, HEAD_D), F32)),
        grid=(n // tb,),
        in_specs=[rows(dproj), rows(LANES), rows(2 * dc), rows(3 * dq), state,
                  const((3, dc)), const((4, dq)), const((1, LANES)), const((1, LANES)),
                  const((1, dc)), const((1, HEAD_D))],
        out_specs=(rows(2 * dc), rows(2 * dc), rows(3 * dq), state),
        scratch_shapes=[pltpu.VMEM((tb, dq), F32), pltpu.VMEM((tb, dc), F32)],
        compiler_params=_params(("arbitrary",)),
        name="mixer_sample",
    )(proj, ba, hist_a, hist_q, s_in, conv_a_w, gdn_conv_w, _lane_row(a_log, N_HEADS),
      _lane_row(dt_bias, N_HEADS), g_conv_out.reshape(1, dc), gdn_norm_g.reshape(1, HEAD_D))


POST_MIX_SUB_ROWS = 256


def _post_mix_kernel(mix_ref, x_ref, gate_ref, sc_ref, sh_ref, gpost_ref, gpre_ref, wout_ref,
                     rwh_ref, rwl_ref, rb_ref, x1_ref, h2_ref, lg_ref):
    tm = x_ref.shape[0]
    sub = min(tm, POST_MIX_SUB_ROWS)
    tiles = [slice(r0, r0 + sub) for r0 in range(0, tm, sub)]
    mixes = [jnp.dot(mix_ref[rows].astype(BF16), wout_ref[...], preferred_element_type=F32)
             for rows in tiles]
    for rows, mix in zip(tiles, mixes):
        per_row = lambda ref: ref[rows] if ref.shape[0] == tm else ref[...]
        x1 = x_ref[rows] + per_row(gate_ref) * (_rms(mix) * gpost_ref[...])
        x1_ref[rows] = x1
        h2 = (_rms(x1) * gpre_ref[...]) * (1.0 + per_row(sc_ref)) + per_row(sh_ref)
        h2_ref[rows] = h2
        hi = h2.astype(BF16)
        lo = (h2 - hi.astype(F32)).astype(BF16)
        rwh = rwh_ref[...]
        lg_ref[rows] = (jnp.dot(hi, rwh, preferred_element_type=F32)
                        + jnp.dot(lo, rwh, preferred_element_type=F32)
                        + jnp.dot(hi, rwl_ref[...], preferred_element_type=F32) + rb_ref[...])


def _post_mix(mix_in, x, gate, scale, shift, g_post, g_pre, w_out, rw_hi, rw_lo, rb, rows_per_group):
    m, d = x.shape
    tm = min(m, POST_MIX_ROWS, rows_per_group if gate.ndim == 3 else m)
    const = lambda shape: pl.BlockSpec(shape, lambda i: (0,) * len(shape))
    rows = lambda width: pl.BlockSpec((tm, width), lambda i: (i, 0))
    return pl.pallas_call(
        _post_mix_kernel,
        out_shape=(jax.ShapeDtypeStruct((m, d), F32), jax.ShapeDtypeStruct((m, d), F32),
                   jax.ShapeDtypeStruct((m, LANES), F32)),
        grid=(m // tm,),
        in_specs=[rows(d), rows(d),
                  _mod_spec(gate, tm, rows_per_group),
                  _mod_spec(scale, tm, rows_per_group),
                  _mod_spec(shift, tm, rows_per_group),
                  const((1, d)), const((1, d)), const((d, d)),
                  const((d, LANES)), const((d, LANES)), const((1, LANES))],
        out_specs=(rows(d), rows(d), rows(LANES)),
        compiler_params=_params(("arbitrary",)),
        name="post_mix",
    )(mix_in, x, gate, scale, shift, g_post.reshape(1, d), g_pre.reshape(1, d), w_out, rw_hi, rw_lo, rb)


ROUTE_TOKENS = 128


def _route_kernel(lgp_ref, lgs_ref, idx_ref, p_ref, rank_ref, cnt_ref, carry, *, prompt_tiles):
    tm = lgp_ref.shape[0]

    @pl.when(pl.program_id(0) == 0)
    def _():
        carry[...] = jnp.zeros_like(carry)

    l = jnp.where(pl.program_id(0) < prompt_tiles, lgp_ref[...], lgs_ref[...])
    lane = lax.broadcasted_iota(I32, l.shape, 1)
    lane_f = lane.astype(F32)
    vals, hots = [], []
    idx_out = jnp.zeros(l.shape, F32)
    for k in range(TOP_K):
        m = jnp.max(l, axis=-1, keepdims=True)
        idx = jnp.min(jnp.where(l == m, lane_f, float(LANES)), axis=-1, keepdims=True)
        hot = lane_f == idx
        vals.append(m)
        hots.append(hot)
        idx_out = jnp.where(lane == k, idx, idx_out)
        l = jnp.where(hot, -jnp.inf, l)
    exps = [jnp.exp(v - vals[0]) for v in vals]
    denom = exps[0] + exps[1] + exps[2] + exps[3]
    p_out = jnp.zeros(l.shape, F32)
    for k in range(TOP_K):
        p_out = jnp.where(lane == k, exps[k] / denom, p_out)
    member = jnp.where(hots[0] | hots[1] | hots[2] | hots[3], 1.0, 0.0).astype(F32)
    row = lax.broadcasted_iota(I32, (tm, tm), 0)
    col = lax.broadcasted_iota(I32, (tm, tm), 1)
    before = jnp.where(row > col, 1.0, 0.0).astype(BF16)
    prefix = jnp.dot(before, member.astype(BF16), preferred_element_type=F32) + carry[...]
    rank_out = jnp.zeros(l.shape, F32)
    for k in range(TOP_K):
        r = jnp.sum(jnp.where(hots[k], prefix, 0.0), axis=-1, keepdims=True)
        rank_out = jnp.where(lane == k, r, rank_out)
    carry[...] = carry[...] + jnp.sum(member, axis=0, keepdims=True)
    idx_ref[...] = idx_out.astype(I32)
    p_ref[...] = p_out
    rank_ref[...] = rank_out.astype(I32)
    cnt_ref[...] = carry[...]


def _route(logits_p, logits_s):
    tm = ROUTE_TOKENS
    pt = logits_p.shape[0] // tm
    n = logits_p.shape[0] + logits_s.shape[0]
    tile = pl.BlockSpec((tm, LANES), lambda i: (i, 0))
    return pl.pallas_call(
        functools.partial(_route_kernel, prompt_tiles=pt),
        out_shape=(jax.ShapeDtypeStruct((n, LANES), I32), jax.ShapeDtypeStruct((n, LANES), F32),
                   jax.ShapeDtypeStruct((n, LANES), I32), jax.ShapeDtypeStruct((1, LANES), F32)),
        grid=(n // tm,),
        in_specs=[pl.BlockSpec((tm, LANES), lambda i: (jnp.minimum(i, pt - 1), 0)),
                  pl.BlockSpec((tm, LANES), lambda i: (jnp.maximum(i - pt, 0), 0))],
        out_specs=(tile, tile, tile, pl.BlockSpec((1, LANES), lambda i: (0, 0))),
        scratch_shapes=[pltpu.VMEM((1, LANES), F32)],
        compiler_params=_params(("arbitrary",)),
        name="route",
    )(logits_p, logits_s)


INVERT_TOKENS_MAX = 1024


def _largest_tile(n, unit, cap):
    return max(m for m in range(unit, cap + 1, unit) if n % m == 0)


def _invert_kernel(meta_ref, slot_ref, table_ref, *, pad_token, n_exp, tokens):
    i = pl.program_id(0)
    pairs = tokens * TOP_K

    @pl.when(i == 0)
    def _():
        def fill(r, carry):
            table_ref[r] = pad_token
            return carry

        def fill_group_tail(e, carry):
            lax.fori_loop(meta_ref[e] + meta_ref[2 * n_exp + e], meta_ref[e] + meta_ref[n_exp + e], fill, 0)
            return carry

        lax.fori_loop(0, n_exp, fill_group_tail, 0)
        lax.fori_loop(meta_ref[3 * n_exp] * EXPERT_ROW_TILE, table_ref.shape[0], fill, 0)

    def put(p, carry):
        table_ref[slot_ref[0, p]] = i * tokens + (p >> TOP_K_SHIFT)
        return carry

    lax.fori_loop(0, pairs, put, 0, unroll=16)


def _invert(slots, meta, n_rows, pad_token, n_exp):
    n = slots.shape[0]
    tokens = _largest_tile(n, LANES, INVERT_TOKENS_MAX)
    nt = n // tokens
    pairs = tokens * TOP_K
    return pl.pallas_call(
        functools.partial(_invert_kernel, pad_token=pad_token, n_exp=n_exp, tokens=tokens),
        out_shape=jax.ShapeDtypeStruct((n_rows,), I32),
        grid_spec=pltpu.PrefetchScalarGridSpec(
            num_scalar_prefetch=1,
            grid=(nt,),
            in_specs=[pl.BlockSpec((None, 1, pairs), lambda i, meta: (i, 0, 0), memory_space=pltpu.SMEM)],
            out_specs=pl.BlockSpec(memory_space=pltpu.SMEM)),
        compiler_params=_params(("arbitrary",)),
        name="invert",
    )(meta, slots.reshape(nt, 1, pairs))


ROW_UNROLL = 8
ROW_UNROLL_SHIFT = 3
DISPATCH_TILES = 6


def _dispatch_kernel(na_ref, split_ref, tok_ref, tok_next_ref, h2p_ref, h2s_ref, xs_ref, buf, sem, *,
                     n_prompt):
    i = pl.program_id(0)
    tile = EXPERT_ROW_TILE
    n_active = na_ref[0]

    def issue_tile(tok, t, sub, slot):
        split = split_ref[t]
        base = sub * tile

        def prompt_copy(r):
            pltpu.make_async_copy(h2p_ref.at[pl.ds(tok[0, base + r], 1)], buf.at[slot, pl.ds(base + r, 1)],
                                  sem.at[slot, sub]).start()

        def sample_copy(r):
            pltpu.make_async_copy(h2s_ref.at[pl.ds(tok[0, base + r] - n_prompt, 1)],
                                  buf.at[slot, pl.ds(base + r, 1)], sem.at[slot, sub]).start()

        def rows(lo, hi, fn):
            def body(r, carry):
                fn(r)
                return carry
            lax.fori_loop(lo, hi, body, 0)

        def groups(lo, hi, fn):
            def body(g, carry):
                for u in range(ROW_UNROLL):
                    fn(g * ROW_UNROLL + u)
                return carry
            lax.fori_loop(lo, hi, body, 0)

        whole = split >> ROW_UNROLL_SHIFT
        first = (split + ROW_UNROLL - 1) >> ROW_UNROLL_SHIFT
        groups(0, whole, prompt_copy)
        rows(whole * ROW_UNROLL, split, prompt_copy)
        rows(split, first * ROW_UNROLL, sample_copy)
        groups(first, tile // ROW_UNROLL, sample_copy)

    def issue(tok, step, slot):
        for sub in range(DISPATCH_TILES):
            t = step * DISPATCH_TILES + sub

            @pl.when(t < n_active)
            def _(t=t, sub=sub):
                issue_tile(tok, t, sub, slot)

    @pl.when(i == 0)
    def _():
        issue(tok_ref, 0, 0)

    @pl.when((i + 1) * DISPATCH_TILES < n_active)
    def _():
        issue(tok_next_ref, i + 1, (i + 1) & 1)

    slot = i & 1
    for sub in range(DISPATCH_TILES):
        t = i * DISPATCH_TILES + sub
        out_rows = slice(sub * tile, (sub + 1) * tile)

        @pl.when(t < n_active)
        def _(sub=sub, out_rows=out_rows):
            pltpu.make_async_copy(h2p_ref.at[pl.ds(0, tile)], buf.at[slot, out_rows], sem.at[slot, sub]).wait()
            xs_ref[out_rows] = buf[slot, out_rows].astype(BF16)

        @pl.when(t >= n_active)
        def _(out_rows=out_rows):
            xs_ref[out_rows] = jnp.zeros((tile, xs_ref.shape[1]), BF16)


def _dispatch(h2_p, h2_s, table, split, n_active):
    d = h2_p.shape[1]
    rows = EXPERT_ROW_TILE * DISPATCH_TILES
    n_rows = table.shape[0]
    assert n_rows % rows == 0, (n_rows, rows)
    nt = n_rows // rows
    return pl.pallas_call(
        functools.partial(_dispatch_kernel, n_prompt=h2_p.shape[0]),
        out_shape=jax.ShapeDtypeStruct((n_rows, d), BF16),
        grid_spec=pltpu.PrefetchScalarGridSpec(
            num_scalar_prefetch=2,
            grid=(nt,),
            in_specs=[pl.BlockSpec((None, 1, rows), lambda i, na, sp: (i, 0, 0), memory_space=pltpu.SMEM),
                      pl.BlockSpec((None, 1, rows), lambda i, na, sp: (jnp.minimum(i + 1, nt - 1), 0, 0),
                                   memory_space=pltpu.SMEM),
                      pl.BlockSpec(memory_space=pl.ANY), pl.BlockSpec(memory_space=pl.ANY)],
            out_specs=pl.BlockSpec((rows, d), lambda i, na, sp: (i, 0)),
            scratch_shapes=[pltpu.VMEM((2, rows, d), F32), pltpu.SemaphoreType.DMA((2, DISPATCH_TILES))]),
        compiler_params=_params(("arbitrary",)),
        name="dispatch",
    )(n_active, split, table.reshape(nt, 1, rows), table.reshape(nt, 1, rows), h2_p, h2_s)


def _stream_expert_weights(te_ref, first_ref, run_ref, next_ref, meta_ref, n_col_blocks, copies, consume):
    j = pl.program_id(0)
    i = pl.program_id(1)
    n_active = meta_ref[0]
    n_runs = meta_ref[1]

    @pl.when(jnp.logical_and(j == 0, i == 0))
    def _():
        for cp in copies(te_ref[0], 0, 0):
            cp.start()

    @pl.when(jnp.logical_and(i < n_active, first_ref[i] == 1))
    def _():
        slot = (j * n_runs + run_ref[i]) & 1
        for cp in copies(te_ref[i], j, slot):
            cp.wait()
        last = run_ref[i] == n_runs - 1
        e_next = jnp.where(last, te_ref[0], next_ref[i])
        j_next = jnp.where(last, j + 1, j)

        @pl.when(j_next < n_col_blocks)
        def _():
            for cp in copies(e_next, j_next, 1 - slot):
                cp.start()

        consume(slot)


def _gate_up_kernel(te_ref, first_ref, run_ref, next_ref, meta_ref, x_ref, bg_ref, bu_ref, w_hbm, o_ref,
                    wbuf, wg_b, wu_b, sem, *, nj, tn):
    i = pl.program_id(1)
    active = i < meta_ref[0]

    def copies(e, jj, slot):
        return [pltpu.make_async_copy(w_hbm.at[e, :, pl.ds(pl.multiple_of((c * nj + jj) * tn, tn), tn)],
                                      wbuf.at[slot, c], sem.at[slot, c]) for c in range(2)]

    def consume(slot):
        wg_b[...] = wbuf[slot, 0].astype(BF16)
        wu_b[...] = wbuf[slot, 1].astype(BF16)

    _stream_expert_weights(te_ref, first_ref, run_ref, next_ref, meta_ref, nj, copies, consume)

    @pl.when(active)
    def _():
        x = x_ref[...]
        gate = jnp.dot(x, wg_b[...], preferred_element_type=F32) + bg_ref[...]
        up = jnp.dot(x, wu_b[...], preferred_element_type=F32) + bu_ref[...]
        gate = jnp.minimum(gate, SWIGLU_LIMIT)
        up = jnp.clip(up, -SWIGLU_LIMIT, SWIGLU_LIMIT)
        o_ref[...] = ((up + 1.0) * gate * jax.nn.sigmoid(SWIGLU_ALPHA * gate)).astype(BF16)

    @pl.when(jnp.logical_not(active))
    def _():
        o_ref[...] = jnp.zeros_like(o_ref)


def _expert_gate_up(xs, w_gu, b_gu, sched):
    n_rows, d = xs.shape
    n_exp, _, f2 = w_gu.shape
    f = f2 // 2
    tm = EXPERT_ROW_TILE
    tn = MATMUL_COLS
    nj = f // tn
    row = lambda i, meta: jnp.minimum(i, meta[0] - 1)
    return pl.pallas_call(
        functools.partial(_gate_up_kernel, nj=nj, tn=tn),
        out_shape=jax.ShapeDtypeStruct((n_rows, f), BF16),
        grid_spec=pltpu.PrefetchScalarGridSpec(
            num_scalar_prefetch=5,
            grid=(nj, n_rows // tm),
            in_specs=[pl.BlockSpec((tm, d), lambda j, i, te, fi, ru, nx, meta: (row(i, meta), 0)),
                      pl.BlockSpec((None, 1, tn), lambda j, i, te, fi, ru, nx, meta: (te[row(i, meta)], 0, j)),
                      pl.BlockSpec((None, 1, tn),
                                   lambda j, i, te, fi, ru, nx, meta: (te[row(i, meta)], 0, nj + j)),
                      pl.BlockSpec(memory_space=pl.ANY)],
            out_specs=pl.BlockSpec((tm, tn), lambda j, i, te, fi, ru, nx, meta: (i, j)),
            scratch_shapes=[pltpu.VMEM((2, 2, d, tn), F32), pltpu.VMEM((d, tn), BF16),
                            pltpu.VMEM((d, tn), BF16), pltpu.SemaphoreType.DMA((2, 2))]),
        compiler_params=_params(("arbitrary", "arbitrary")),
        name="expert_gate_up",
    )(*sched, xs, b_gu.reshape(n_exp, 1, f2), b_gu.reshape(n_exp, 1, f2), w_gu)


def _down_kernel(te_ref, first_ref, run_ref, next_ref, meta_ref, a_ref, b_ref, w_hbm, o_ref, wbuf, w_b, sem):
    i = pl.program_id(1)
    active = i < meta_ref[0]

    def copies(e, jj, slot):
        return [pltpu.make_async_copy(w_hbm.at[e], wbuf.at[slot], sem.at[slot])]

    def consume(slot):
        w_b[...] = wbuf[slot].astype(BF16)

    _stream_expert_weights(te_ref, first_ref, run_ref, next_ref, meta_ref, 1, copies, consume)

    @pl.when(active)
    def _():
        o_ref[...] = jnp.dot(a_ref[...], w_b[...], preferred_element_type=F32) + b_ref[...]

    @pl.when(jnp.logical_not(active))
    def _():
        o_ref[...] = jnp.zeros_like(o_ref)


def _expert_down(act, w_d, b_d, sched):
    n_rows, f = act.shape
    n_exp, _, d = w_d.shape
    tm = EXPERT_ROW_TILE
    row = lambda i, meta: jnp.minimum(i, meta[0] - 1)
    return pl.pallas_call(
        _down_kernel,
        out_shape=jax.ShapeDtypeStruct((n_rows, d), F32),
        grid_spec=pltpu.PrefetchScalarGridSpec(
            num_scalar_prefetch=5,
            grid=(1, n_rows // tm),
            in_specs=[pl.BlockSpec((tm, f), lambda j, i, te, fi, ru, nx, meta: (row(i, meta), 0)),
                      pl.BlockSpec((None, 1, d), lambda j, i, te, fi, ru, nx, meta: (te[row(i, meta)], 0, 0)),
                      pl.BlockSpec(memory_space=pl.ANY)],
            out_specs=pl.BlockSpec((tm, d), lambda j, i, te, fi, ru, nx, meta: (i, 0)),
            scratch_shapes=[pltpu.VMEM((2, f, d), F32), pltpu.VMEM((f, d), BF16),
                            pltpu.SemaphoreType.DMA((2,))]),
        compiler_params=_params(("arbitrary", "arbitrary")),
        name="expert_down",
    )(*sched, act, b_d.reshape(n_exp, 1, d), w_d)


COMBINE_TOKENS = 512


def _combine_kernel(slot_ref, slot_next_ref, p_ref, x1_ref, gate_ref, g_ref, y_ref, o_ref, buf, sem):
    i = pl.program_id(0)
    tm = x1_ref.shape[0]
    pairs = tm * TOP_K

    def issue(slots, b):
        def body(g, carry):
            for u in range(ROW_UNROLL):
                p = g * ROW_UNROLL + u
                tok = g * (ROW_UNROLL // TOP_K) + u // TOP_K
                pltpu.make_async_copy(y_ref.at[pl.ds(slots[0, p], 1)], buf.at[b, u % TOP_K, pl.ds(tok, 1)],
                                      sem.at[b]).start()
            return carry
        lax.fori_loop(0, pairs // ROW_UNROLL, body, 0)

    @pl.when(i == 0)
    def _():
        issue(slot_ref, 0)

    @pl.when(i + 1 < pl.num_programs(0))
    def _():
        issue(slot_next_ref, (i + 1) & 1)

    b = i & 1
    for k in range(TOP_K):
        pltpu.make_async_copy(y_ref.at[pl.ds(0, tm)], buf.at[b, k], sem.at[b]).wait()
    probs = p_ref[...]
    f = probs[:, 0:1] * buf[b, 0]
    for k in range(1, TOP_K):
        f = f + probs[:, k:k + 1] * buf[b, k]
    o_ref[...] = x1_ref[...] + gate_ref[...] * (_rms(f) * g_ref[...])


def _combine(y, slots, probs, x1, gate, g_post, rows_per_group):
    m, d = x1.shape
    tm = min(m, COMBINE_TOKENS, rows_per_group if gate.ndim == 3 else m)
    nt = m // tm
    pairs = tm * TOP_K
    return pl.pallas_call(
        _combine_kernel,
        out_shape=jax.ShapeDtypeStruct((m, d), F32),
        grid=(nt,),
        in_specs=[pl.BlockSpec((None, 1, pairs), lambda i: (i, 0, 0), memory_space=pltpu.SMEM),
                  pl.BlockSpec((None, 1, pairs), lambda i: (jnp.minimum(i + 1, nt - 1), 0, 0),
                               memory_space=pltpu.SMEM),
                  pl.BlockSpec((tm, LANES), lambda i: (i, 0)),
                  pl.BlockSpec((tm, d), lambda i: (i, 0)),
                  _mod_spec(gate, tm, rows_per_group),
                  pl.BlockSpec((1, d), lambda i: (0, 0)),
                  pl.BlockSpec(memory_space=pl.ANY)],
        out_specs=pl.BlockSpec((tm, d), lambda i: (i, 0)),
        scratch_shapes=[pltpu.VMEM((2, TOP_K, tm, d), F32), pltpu.SemaphoreType.DMA((2,))],
        compiler_params=_params(("arbitrary",)),
        name="combine",
    )(slots.reshape(nt, 1, pairs), slots.reshape(nt, 1, pairs), probs, x1, gate, g_post.reshape(1, d), y)


def _moe(h2_p, h2_s, logits_p, logits_s, n_exp, w_gu, b_gu, w_d, b_d):
    n = h2_p.shape[0] + h2_s.shape[0]
    tile = EXPERT_ROW_TILE
    top_i, probs, rank, counts = _route(logits_p, logits_s)
    counts = counts[0, :n_exp].astype(I32)
    cap = (counts + tile - 1) // tile * tile
    ends = jnp.cumsum(cap)
    offs = ends - cap
    n_tiles = (n * TOP_K) // tile + n_exp
    n_active = (ends[-1] // tile).astype(I32).reshape(1)
    tile_id = jnp.arange(n_tiles, dtype=I32)
    tile_expert = jnp.minimum(jnp.sum(ends[None, :] <= tile_id[:, None] * tile, axis=1), n_exp - 1).astype(I32)
    first = ((tile_id == 0) | (tile_expert != jnp.roll(tile_expert, 1))) & (tile_id < n_active[0])
    run = jnp.cumsum(first.astype(I32)) - 1
    next_expert = tile_expert[jnp.minimum(ends[tile_expert] // tile, n_tiles - 1)]
    sched = (tile_expert, first.astype(I32), run.astype(I32), next_expert.astype(I32),
             jnp.stack([n_active[0], jnp.sum(first.astype(I32))]).astype(I32))
    expert_ids = jnp.arange(n_exp, dtype=I32)
    pair_offs = jnp.sum(jnp.where(top_i[:, :TOP_K, None] == expert_ids, offs, 0), axis=-1)
    slots = pair_offs + rank[:, :TOP_K]
    n_prompt = h2_p.shape[0]
    meta = jnp.concatenate([offs, cap, counts, n_active]).astype(I32)
    table = _invert(slots, meta, n_tiles * tile, n_prompt, n_exp)
    split = jnp.sum(table.reshape(n_tiles, tile) < n_prompt, axis=1).astype(I32)
    xs = _dispatch(h2_p, h2_s, table, split, n_active)
    act = _expert_gate_up(xs, w_gu, b_gu, sched)
    y = _expert_down(act, w_d, b_d, sched)
    return y, slots, probs


def _layer(xp, xs_, cp_mod, cs_mod, hist_a, hist_q, s0, g_pre_mix, g_post_mix, g_pre_ffn, g_post_ffn,
           w_in, conv_a_w, gdn_conv_w, a_log, dt_bias, g_conv_out, gdn_norm_g, w_out, router_w, router_b,
           w_gu, b_gu, w_d, b_d):
    bsz, seq, d = xp.shape
    ns = xs_.shape[0]
    n_exp = router_w.shape[1]
    dc = N_HEADS * HEAD_D
    d_main = 3 * dc + 3 * dc + dc
    xp2 = xp.reshape(bsz * seq, d)
    xs2 = xs_.reshape(ns, d)

    w_main = w_in.astype(BF16)
    w_ba = jnp.zeros((d, LANES), BF16).at[:, :2 * N_HEADS].set(w_main[:, d_main:])
    w_out_b = w_out.astype(BF16)
    rw = jnp.zeros((d, LANES), F32).at[:, :n_exp].set(router_w)
    rw_hi = rw.astype(BF16)
    rw_lo = (rw - rw_hi.astype(F32)).astype(BF16)
    rb = jnp.full((1, LANES), NEG_BIG, F32).at[0, :n_exp].set(router_b)

    mp = [cp_mod[:, i * d:(i + 1) * d].reshape(bsz, 1, d) for i in range(6)]
    ms = [cs_mod[:, i * d:(i + 1) * d] for i in range(6)]

    proj_p, ba_p = _in_proj(xp2, mp[1], mp[0], g_pre_mix, w_main, w_ba, seq, d_main)
    proj_s, ba_s = _in_proj(xs2, ms[1], ms[0], g_pre_mix, w_main, w_ba, 1, d_main)

    mix_p, ha_p, hq_p, s_p = _mixer_prompt(proj_p, ba_p, bsz, seq, conv_a_w, gdn_conv_w, a_log, dt_bias,
                                           g_conv_out, gdn_norm_g)
    mix_s, ha_s, hq_s, s_s = _mixer_sample(proj_s, ba_s, hist_a.reshape(ns, 2 * dc),
                                           hist_q.reshape(ns, 9 * dc), s0, conv_a_w, gdn_conv_w, a_log,
                                           dt_bias, g_conv_out, gdn_norm_g)

    x1_p, h2_p, lg_p = _post_mix(mix_p, xp2, mp[2], mp[4], mp[3], g_post_mix, g_pre_ffn, w_out_b, rw_hi,
                                 rw_lo, rb, seq)
    x1_s, h2_s, lg_s = _post_mix(mix_s, xs2, ms[2], ms[4], ms[3], g_post_mix, g_pre_ffn, w_out_b, rw_hi,
                                 rw_lo, rb, 1)

    y, slots, probs = _moe(h2_p, h2_s, lg_p, lg_s, n_exp, w_gu, b_gu, w_d, b_d)
    np_ = bsz * seq
    out_p = _combine(y, slots[:np_], probs[:np_], x1_p, mp[5], g_post_ffn, seq)
    out_s = _combine(y, slots[np_:], probs[np_:], x1_s, ms[5], g_post_ffn, 1)
    return (out_p.reshape(bsz, seq, d), out_s.reshape(ns, 1, d), ha_p, hq_p, s_p,
            ha_s.reshape(ns, 2, dc), hq_s.reshape(ns, 3, 3 * dc), s_s)


def kernel(x_prompt, x_sample, state_conv_a, state_gdn_conv, state_gdn_S, c_prompt, c_sample, w_mod, b_mod, g_pre_mix, g_post_mix, g_pre_ffn, g_post_ffn, w_in, conv_a_w, gdn_conv_w, gdn_a_log, gdn_dt_bias, g_conv_out, gdn_norm_g, w_out, router_w, router_b, exp_w_gate_up, exp_b_gate_up, exp_w_down, exp_b_down):
    depth = w_mod.shape[0]
    bp = x_prompt.shape[0]
    xp, xs_ = x_prompt, x_sample
    outs = [[] for _ in range(6)]
    for l in range(depth):
        mod = _modulation(jnp.concatenate([c_prompt, c_sample], axis=0), w_mod[l], b_mod[l])
        res = _layer(xp, xs_, mod[:bp], mod[bp:], state_conv_a[l], state_gdn_conv[l], state_gdn_S[l],
                     g_pre_mix[l], g_post_mix[l], g_pre_ffn[l], g_post_ffn[l], w_in[l], conv_a_w[l],
                     gdn_conv_w[l], gdn_a_log[l], gdn_dt_bias[l], g_conv_out[l], gdn_norm_g[l], w_out[l],
                     router_w[l], router_b[l], exp_w_gate_up[l], exp_b_gate_up[l], exp_w_down[l],
                     exp_b_down[l])
        xp, xs_ = res[0], res[1]
        for acc, r in zip(outs, res[2:]):
            acc.append(r)
    return (xp, xs_) + tuple(o[0][None] if depth == 1 else jnp.stack(o) for o in outs)
```

```python
import functools

import jax
import jax.numpy as jnp
from jax import lax
from jax.experimental import pallas as pl
from jax.experimental.pallas import tpu as pltpu

F32 = jnp.float32
BF16 = jnp.bfloat16
I32 = jnp.int32
HIGHEST = lax.Precision.HIGHEST

EPS = 1e-6
N_HEADS = 8
HEAD_D = 128
TOP_K = 4
TOP_K_SHIFT = 2
SWIGLU_LIMIT = 7.0
SWIGLU_ALPHA = 1.702
CHUNK = 64
LANES = 128
EXPERT_ROW_TILE = 256
EXPERT_STEP_TILES = 2
MATMUL_COLS = 1024
IN_PROJ_ROWS = 1024
POST_MIX_ROWS = 512
HALO = 8
NEG_BIG = -1e30
VMEM_LIMIT = 56 * 1024 * 1024


def _params(semantics, vmem=VMEM_LIMIT):
    return pltpu.CompilerParams(dimension_semantics=semantics, vmem_limit_bytes=vmem)


def _mm(a, b):
    return jnp.dot(a.astype(BF16), b.astype(BF16), preferred_element_type=F32)


def _mm_nt(a, b):
    return lax.dot_general(a.astype(BF16), b.astype(BF16), (((1,), (1,)), ((), ())),
                           preferred_element_type=F32)


def _rms(x):
    return x * lax.rsqrt(jnp.mean(x * x, axis=-1, keepdims=True) + EPS)


def _silu(x):
    return x * jax.nn.sigmoid(x)


def _softplus(x):
    return jnp.maximum(x, 0.0) + jnp.log1p(jnp.exp(-jnp.abs(x)))


def _mod_kernel(c_ref, w_ref, b_ref, o_ref):
    s = _silu(c_ref[...])
    o_ref[...] = _mm(s, w_ref[...]) + b_ref[...]


def _modulation(c_all, w_mod, b_mod):
    n, d = c_all.shape
    m = w_mod.shape[1]
    tn = MATMUL_COLS
    return pl.pallas_call(
        _mod_kernel,
        out_shape=jax.ShapeDtypeStruct((n, m), F32),
        grid=(m // tn,),
        in_specs=[pl.BlockSpec((n, d), lambda j: (0, 0)),
                  pl.BlockSpec((d, tn), lambda j: (0, j)),
                  pl.BlockSpec((1, tn), lambda j: (0, j))],
        out_specs=pl.BlockSpec((n, tn), lambda j: (0, j)),
        compiler_params=_params(("arbitrary",)),
        name="modulation",
    )(c_all, w_mod, b_mod.reshape(1, m))


def _mod_spec(arr, tm, rows_per_group):
    if arr.ndim == 3:
        tiles = rows_per_group // tm
        return pl.BlockSpec((None, 1, arr.shape[-1]), lambda i, *_: (i // tiles, 0, 0))
    return pl.BlockSpec((tm, arr.shape[-1]), lambda i, *_: (i, 0))


def _proj_kernel(x_ref, sc_ref, sh_ref, g_ref, w_ref, wba_ref, o_ref, ba_ref, h_scr):
    @pl.when(pl.program_id(1) == 0)
    def _():
        h = (_rms(x_ref[...]) * g_ref[...]) * (1.0 + sc_ref[...]) + sh_ref[...]
        hb = h.astype(BF16)
        h_scr[...] = hb
        ba_ref[...] = jnp.dot(hb, wba_ref[...], preferred_element_type=F32)

    o_ref[...] = jnp.dot(h_scr[...], w_ref[...], preferred_element_type=F32)


def _in_proj(x, scale, shift, g, w_main, w_ba, rows_per_group, n):
    m, d = x.shape
    tm = min(m, IN_PROJ_ROWS, rows_per_group if scale.ndim == 3 else m)
    tn = MATMUL_COLS
    return pl.pallas_call(
        _proj_kernel,
        out_shape=(jax.ShapeDtypeStruct((m, n), F32), jax.ShapeDtypeStruct((m, LANES), F32)),
        grid=(m // tm, n // tn),
        in_specs=[pl.BlockSpec((tm, d), lambda i, j: (i, 0)),
                  _mod_spec(scale, tm, rows_per_group),
                  _mod_spec(shift, tm, rows_per_group),
                  pl.BlockSpec((1, d), lambda i, j: (0, 0)),
                  pl.BlockSpec((d, tn), lambda i, j: (0, j)),
                  pl.BlockSpec((d, LANES), lambda i, j: (0, 0))],
        out_specs=(pl.BlockSpec((tm, tn), lambda i, j: (i, j)),
                   pl.BlockSpec((tm, LANES), lambda i, j: (i, 0))),
        scratch_shapes=[pltpu.VMEM((tm, d), BF16)],
        compiler_params=_params(("arbitrary", "arbitrary")),
        name="in_proj",
    )(x, scale, shift, g.reshape(1, d), w_main, w_ba)


PROMPT_SEQS_PER_STEP = 4


def _mixer_prompt_kernel(proj_ref, ba_ref, caw_ref, gcw_ref, alog_ref, dtb_ref, gco_ref, gng_ref,
                         mix_ref, ha_ref, hq_ref, s_ref, extu, extq, qc_scr, s_scr, *, nseq):
    c = CHUNK
    dc = N_HEADS * HEAD_D
    dq = 3 * dc
    t = pl.program_id(1)
    is_last = t == pl.num_programs(1) - 1

    @pl.when(t == 0)
    def _():
        extu[:, 0:HALO, :] = jnp.zeros((nseq, HALO, dc), F32)
        extq[:, 0:HALO, :] = jnp.zeros((nseq, HALO, dq), F32)
        s_scr[...] = jnp.zeros_like(s_scr)

    row = lax.broadcasted_iota(I32, (c, c), 0)
    col = lax.broadcasted_iota(I32, (c, c), 1)
    causal = row >= col
    strict = row > col
    lower = jnp.where(causal, 1.0, 0.0).astype(F32)
    upper = jnp.where(row <= col, 1.0, 0.0).astype(F32)
    caw = caw_ref[...]
    gcw = gcw_ref[...]

    chains = [(sq, h) for sq in range(nseq) for h in range(N_HEADS)]
    heads = range(len(chains))
    qn, kn, vb, kb, kbg, qg, kg, decay, s_decay = ([] for _ in range(9))
    for sq in range(nseq):
        u = proj_ref[sq, :, dc:2 * dc] * proj_ref[sq, :, 2 * dc:3 * dc]
        extu[sq, HALO:HALO + c, :] = u
        ya = (caw[0:1] * extu[sq, HALO - 2:HALO - 2 + c, :] + caw[1:2] * extu[sq, HALO - 1:HALO - 1 + c, :]
              + caw[2:3] * u)
        ya = proj_ref[sq, :, 0:dc] * ya
        mix_ref[sq, :, 0:dc] = (_rms(ya) * gco_ref[...]).astype(BF16)
        last_u = extu[sq, HALO - 2 + c:HALO + c, :]
        extu[sq, HALO - 2:HALO, :] = last_u

        qkv = proj_ref[sq, :, 3 * dc:3 * dc + dq]
        extq[sq, HALO:HALO + c, :] = qkv
        qc = (gcw[0:1] * extq[sq, HALO - 3:HALO - 3 + c, :] + gcw[1:2] * extq[sq, HALO - 2:HALO - 2 + c, :]
              + gcw[2:3] * extq[sq, HALO - 1:HALO - 1 + c, :] + gcw[3:4] * qkv)
        qc_scr[sq] = _silu(qc)
        last_q = extq[sq, HALO - 3 + c:HALO + c, :]
        extq[sq, HALO - 3:HALO, :] = last_q

        @pl.when(is_last)
        def _(sq=sq, last_u=last_u, last_q=last_q):
            ha_ref[sq] = last_u
            hq_ref[sq] = last_q

        ba = ba_ref[sq]
        beta_all = jax.nn.sigmoid(ba)
        g_all = -jnp.exp(alog_ref[...]) * _softplus(ba + dtb_ref[...])
        gc_all = jnp.dot(lower, g_all, precision=HIGHEST, preferred_element_type=F32)
        gc_t = lax.dot_general(g_all, upper, (((0,), (0,)), ((), ())), precision=HIGHEST,
                               preferred_element_type=F32)
        for h in range(N_HEADS):
            lo = h * HEAD_D
            q = qc_scr[sq, :, lo:lo + HEAD_D]
            k = qc_scr[sq, :, dc + lo:dc + lo + HEAD_D]
            v = qc_scr[sq, :, 2 * dc + lo:2 * dc + lo + HEAD_D]
            qn_h = q * lax.rsqrt(jnp.sum(q * q, axis=-1, keepdims=True) + EPS) * (HEAD_D ** -0.5)
            kn_h = k * lax.rsqrt(jnp.sum(k * k, axis=-1, keepdims=True) + EPS)
            beta = beta_all[:, h:h + 1]
            gcc = gc_all[:, N_HEADS + h:N_HEADS + h + 1]
            gcr = gc_t[N_HEADS + h:N_HEADS + h + 1, :]
            gl = gc_all[c - 1:c, N_HEADS + h:N_HEADS + h + 1]
            eg = jnp.exp(gcc)
            kb_h = kn_h * beta
            qn.append(qn_h)
            kn.append(kn_h)
            vb.append(v * beta)
            kb.append(kb_h)
            kbg.append(kb_h * eg)
            qg.append(qn_h * eg)
            kg.append(kn_h * jnp.exp(gl - gcc))
            decay.append(jnp.where(causal, jnp.exp(jnp.minimum(gcc - gcr, 0.0)), 0.0))
            s_decay.append(jnp.exp(gl))

    kq = [_mm_nt(jnp.concatenate([kb[h], qn[h]], axis=0), kn[h]) for h in heads]
    a_mat = [jnp.where(strict, kq[h][:c] * decay[h], 0.0) for h in heads]
    qk = [kq[h][c:] * decay[h] for h in heads]
    n_mat = [-a_mat[h] for h in heads]
    p = a_mat
    size = 2
    while size < c:
        p = [_mm(p[h], p[h]) for h in heads]
        n_p = [_mm(n_mat[h], p[h]) for h in heads]
        n_mat = [n_mat[h] + p[h] + n_p[h] for h in heads]
        size *= 2
    rhs = [jnp.concatenate([vb[h], kbg[h]], axis=-1) for h in heads]
    uw = [rhs[h] + _mm(n_mat[h], rhs[h]) for h in heads]
    s_old = [s_scr[sq, h] for sq, h in chains]
    ws = [_mm(jnp.concatenate([uw[h][:, HEAD_D:], qg[h]], axis=0), s_old[h]) for h in heads]
    v_new = [uw[h][:, :HEAD_D] - ws[h][:c] for h in heads]
    fin = [_mm(jnp.concatenate([qk[h], kg[h].T], axis=0), v_new[h]) for h in heads]
    for i, (sq, h) in enumerate(chains):
        lo = h * HEAD_D
        s_scr[sq, h] = s_old[i] * s_decay[i] + fin[i][c:]
        o = ws[i][c:] + fin[i][:c]
        z = proj_ref[sq, :, 3 * dc + dq + lo:3 * dc + dq + lo + HEAD_D]
        yb = _rms(o) * gng_ref[...] * _silu(z)
        mix_ref[sq, :, dc + lo:dc + lo + HEAD_D] = yb.astype(BF16)

    @pl.when(is_last)
    def _():
        s_ref[...] = s_scr[...]


def _lane_row(vec, offset):
    return jnp.zeros((1, LANES), F32).at[0, offset:offset + vec.shape[0]].set(vec.astype(F32))


def _mixer_prompt(proj, ba, bsz, seq, conv_a_w, gdn_conv_w, a_log, dt_bias, g_conv_out, gdn_norm_g):
    c = CHUNK
    dc = N_HEADS * HEAD_D
    dq = 3 * dc
    dproj = proj.shape[1]
    nt = seq // c
    const = lambda shape: pl.BlockSpec(shape, lambda b, t: (0,) * len(shape))
    nseq = PROMPT_SEQS_PER_STEP if bsz % PROMPT_SEQS_PER_STEP == 0 else 1
    seq_block = lambda *tail: pl.BlockSpec((nseq,) + tail, lambda b, t: (b, t) + (0,) * (len(tail) - 1))
    whole = lambda *tail: pl.BlockSpec((nseq,) + tail, lambda b, t: (b,) + (0,) * len(tail))
    mix, ha, hq, s_fin = pl.pallas_call(
        functools.partial(_mixer_prompt_kernel, nseq=nseq),
        out_shape=(jax.ShapeDtypeStruct((bsz, seq, 2 * dc), BF16),
                   jax.ShapeDtypeStruct((bsz, 2, dc), F32),
                   jax.ShapeDtypeStruct((bsz, 3, dq), F32),
                   jax.ShapeDtypeStruct((bsz, N_HEADS, HEAD_D, HEAD_D), F32)),
        grid=(bsz // nseq, nt),
        in_specs=[seq_block(c, dproj), seq_block(c, LANES),
                  const((3, dc)), const((4, dq)), const((1, LANES)), const((1, LANES)),
                  const((1, dc)), const((1, HEAD_D))],
        out_specs=(seq_block(c, 2 * dc), whole(2, dc), whole(3, dq), whole(N_HEADS, HEAD_D, HEAD_D)),
        scratch_shapes=[pltpu.VMEM((nseq, HALO + c, dc), F32), pltpu.VMEM((nseq, HALO + c, dq), F32),
                        pltpu.VMEM((nseq, c, dq), F32), pltpu.VMEM((nseq, N_HEADS, HEAD_D, HEAD_D), F32)],
        compiler_params=_params(("arbitrary", "arbitrary")),
        name="mixer_prompt",
    )(proj.reshape(bsz, seq, dproj), ba.reshape(bsz, seq, LANES), conv_a_w, gdn_conv_w,
      _lane_row(a_log, N_HEADS), _lane_row(dt_bias, N_HEADS), g_conv_out.reshape(1, dc),
      gdn_norm_g.reshape(1, HEAD_D))
    return mix.reshape(bsz * seq, 2 * dc), ha, hq, s_fin


SAMPLE_GROUP = 16


def _mixer_sample_kernel(proj_ref, ba_ref, hista_ref, histq_ref, s_in_ref, caw_ref, gcw_ref, alog_ref,
                         dtb_ref, gco_ref, gng_ref, mix_ref, ha_ref, hq_ref, s_out_ref, qc_scr, o_scr):
    tb = SAMPLE_GROUP
    dc = N_HEADS * HEAD_D
    dq = 3 * dc

    u = proj_ref[:, dc:2 * dc] * proj_ref[:, 2 * dc:3 * dc]
    caw = caw_ref[...]
    ya = caw[0:1] * hista_ref[:, 0:dc] + caw[1:2] * hista_ref[:, dc:2 * dc] + caw[2:3] * u
    ya = proj_ref[:, 0:dc] * ya
    mix_ref[:, 0:dc] = _rms(ya) * gco_ref[...]
    ha_ref[:, 0:dc] = hista_ref[:, dc:2 * dc]
    ha_ref[:, dc:2 * dc] = u

    qkv = proj_ref[:, 3 * dc:3 * dc + dq]
    gcw = gcw_ref[...]
    qc = (gcw[0:1] * histq_ref[:, 0:dq] + gcw[1:2] * histq_ref[:, dq:2 * dq]
          + gcw[2:3] * histq_ref[:, 2 * dq:3 * dq] + gcw[3:4] * qkv)
    qc_scr[...] = _silu(qc)
    hq_ref[:, 0:dq] = histq_ref[:, dq:2 * dq]
    hq_ref[:, dq:2 * dq] = histq_ref[:, 2 * dq:3 * dq]
    hq_ref[:, 2 * dq:3 * dq] = qkv

    ba = ba_ref[...]
    beta_all = jax.nn.sigmoid(ba)
    eg_all = jnp.exp(-jnp.exp(alog_ref[...]) * _softplus(ba + dtb_ref[...]))

    for h in range(N_HEADS):
        lo = h * HEAD_D
        q = qc_scr[:, lo:lo + HEAD_D]
        k = qc_scr[:, dc + lo:dc + lo + HEAD_D]
        v = qc_scr[:, 2 * dc + lo:2 * dc + lo + HEAD_D]
        qn = q * lax.rsqrt(jnp.sum(q * q, axis=-1, keepdims=True) + EPS) * (HEAD_D ** -0.5)
        kn = k * lax.rsqrt(jnp.sum(k * k, axis=-1, keepdims=True) + EPS)
        qk = jnp.sum(qn * kn, axis=-1, keepdims=True)
        kn_t = kn.T
        qn_t = qn.T
        for b in range(tb):
            s_old = s_in_ref[b, h]
            kc = jnp.broadcast_to(kn_t[:, b:b + 1], s_old.shape)
            e = eg_all[b:b + 1, N_HEADS + h:N_HEADS + h + 1]
            ks = jnp.sum(s_old * kc, axis=0, keepdims=True)
            qs = jnp.sum(s_old * qn_t[:, b:b + 1], axis=0, keepdims=True)
            v_new = beta_all[b:b + 1, h:h + 1] * (v[b:b + 1, :] - e * ks)
            o_scr[b:b + 1, lo:lo + HEAD_D] = e * qs + qk[b:b + 1, :] * v_new
            s_out_ref[b, h] = s_old * e + kc * v_new
        z = proj_ref[:, 3 * dc + dq + lo:3 * dc + dq + lo + HEAD_D]
        o = o_scr[:, lo:lo + HEAD_D]
        mix_ref[:, dc + lo:dc + lo + HEAD_D] = _rms(o) * gng_ref[...] * _silu(z)


def _mixer_sample(proj, ba, hist_a, hist_q, s_in, conv_a_w, gdn_conv_w, a_log, dt_bias, g_conv_out,
                  gdn_norm_g):
    n = proj.shape[0]
    tb = SAMPLE_GROUP
    dc = N_HEADS * HEAD_D
    dq = 3 * dc
    dproj = proj.shape[1]
    const = lambda shape: pl.BlockSpec(shape, lambda i: (0,) * len(shape))
    rows = lambda width: pl.BlockSpec((tb, width), lambda i: (i, 0))
    state = pl.BlockSpec((tb, N_HEADS, HEAD_D, HEAD_D), lambda i: (i, 0, 0, 0))
    return pl.pallas_call(
        _mixer_sample_kernel,
        out_shape=(jax.ShapeDtypeStruct((n, 2 * dc), F32),
                   jax.ShapeDtypeStruct((n, 2 * dc), F32),
                   jax.ShapeDtypeStruct((n, 3 * dq), F32),
                   jax.ShapeDtypeStruct((n, N_HEADS, HEAD_D, HEAD_D), F32)),
        grid=(n // tb,),
        in_specs=[rows(dproj), rows(LANES), rows(2 * dc), rows(3 * dq), state,
                  const((3, dc)), const((4, dq)), const((1, LANES)), const((1, LANES)),
                  const((1, dc)), const((1, HEAD_D))],
        out_specs=(rows(2 * dc), rows(2 * dc), rows(3 * dq), state),
        scratch_shapes=[pltpu.VMEM((tb, dq), F32), pltpu.VMEM((tb, dc), F32)],
        compiler_params=_params(("arbitrary",)),
        name="mixer_sample",
    )(proj, ba, hist_a, hist_q, s_in, conv_a_w, gdn_conv_w, _lane_row(a_log, N_HEADS),
      _lane_row(dt_bias, N_HEADS), g_conv_out.reshape(1, dc), gdn_norm_g.reshape(1, HEAD_D))


POST_MIX_SUB_ROWS = 256


def _post_mix_kernel(mix_ref, x_ref, gate_ref, sc_ref, sh_ref, gpost_ref, gpre_ref, wout_ref,
                     rwh_ref, rwl_ref, rb_ref, x1_ref, h2_ref, lg_ref):
    tm = x_ref.shape[0]
    sub = min(tm, POST_MIX_SUB_ROWS)
    tiles = [slice(r0, r0 + sub) for r0 in range(0, tm, sub)]
    mixes = [jnp.dot(mix_ref[rows].astype(BF16), wout_ref[...], preferred_element_type=F32)
             for rows in tiles]
    for rows, mix in zip(tiles, mixes):
        per_row = lambda ref: ref[rows] if ref.shape[0] == tm else ref[...]
        x1 = x_ref[rows] + per_row(gate_ref) * (_rms(mix) * gpost_ref[...])
        x1_ref[rows] = x1
        h2 = (_rms(x1) * gpre_ref[...]) * (1.0 + per_row(sc_ref)) + per_row(sh_ref)
        h2_ref[rows] = h2
        hi = h2.astype(BF16)
        lo = (h2 - hi.astype(F32)).astype(BF16)
        rwh = rwh_ref[...]
        lg_ref[rows] = (jnp.dot(hi, rwh, preferred_element_type=F32)
                        + jnp.dot(lo, rwh, preferred_element_type=F32)
                        + jnp.dot(hi, rwl_ref[...], preferred_element_type=F32) + rb_ref[...])


def _post_mix(mix_in, x, gate, scale, shift, g_post, g_pre, w_out, rw_hi, rw_lo, rb, rows_per_group):
    m, d = x.shape
    tm = min(m, POST_MIX_ROWS, rows_per_group if gate.ndim == 3 else m)
    const = lambda shape: pl.BlockSpec(shape, lambda i: (0,) * len(shape))
    rows = lambda width: pl.BlockSpec((tm, width), lambda i: (i, 0))
    return pl.pallas_call(
        _post_mix_kernel,
        out_shape=(jax.ShapeDtypeStruct((m, d), F32), jax.ShapeDtypeStruct((m, d), F32),
                   jax.ShapeDtypeStruct((m, LANES), F32)),
        grid=(m // tm,),
        in_specs=[rows(d), rows(d),
                  _mod_spec(gate, tm, rows_per_group),
                  _mod_spec(scale, tm, rows_per_group),
                  _mod_spec(shift, tm, rows_per_group),
                  const((1, d)), const((1, d)), const((d, d)),
                  const((d, LANES)), const((d, LANES)), const((1, LANES))],
        out_specs=(rows(d), rows(d), rows(LANES)),
        compiler_params=_params(("arbitrary",)),
        name="post_mix",
    )(mix_in, x, gate, scale, shift, g_post.reshape(1, d), g_pre.reshape(1, d), w_out, rw_hi, rw_lo, rb)


ROUTE_TOKENS = 128


def _route_kernel(lgp_ref, lgs_ref, idx_ref, p_ref, rank_ref, cnt_ref, carry, *, prompt_tiles):
    tm = lgp_ref.shape[0]

    @pl.when(pl.program_id(0) == 0)
    def _():
        carry[...] = jnp.zeros_like(carry)

    l = jnp.where(pl.program_id(0) < prompt_tiles, lgp_ref[...], lgs_ref[...])
    lane = lax.broadcasted_iota(I32, l.shape, 1)
    lane_f = lane.astype(F32)
    vals, hots = [], []
    idx_out = jnp.zeros(l.shape, F32)
    for k in range(TOP_K):
        m = jnp.max(l, axis=-1, keepdims=True)
        idx = jnp.min(jnp.where(l == m, lane_f, float(LANES)), axis=-1, keepdims=True)
        hot = lane_f == idx
        vals.append(m)
        hots.append(hot)
        idx_out = jnp.where(lane == k, idx, idx_out)
        l = jnp.where(hot, -jnp.inf, l)
    exps = [jnp.exp(v - vals[0]) for v in vals]
    denom = exps[0] + exps[1] + exps[2] + exps[3]
    p_out = jnp.zeros(l.shape, F32)
    for k in range(TOP_K):
        p_out = jnp.where(lane == k, exps[k] / denom, p_out)
    member = jnp.where(hots[0] | hots[1] | hots[2] | hots[3], 1.0, 0.0).astype(F32)
    row = lax.broadcasted_iota(I32, (tm, tm), 0)
    col = lax.broadcasted_iota(I32, (tm, tm), 1)
    before = jnp.where(row > col, 1.0, 0.0).astype(BF16)
    prefix = jnp.dot(before, member.astype(BF16), preferred_element_type=F32) + carry[...]
    rank_out = jnp.zeros(l.shape, F32)
    for k in range(TOP_K):
        r = jnp.sum(jnp.where(hots[k], prefix, 0.0), axis=-1, keepdims=True)
        rank_out = jnp.where(lane == k, r, rank_out)
    carry[...] = carry[...] + jnp.sum(member, axis=0, keepdims=True)
    idx_ref[...] = idx_out.astype(I32)
    p_ref[...] = p_out
    rank_ref[...] = rank_out.astype(I32)
    cnt_ref[...] = carry[...]


def _route(logits_p, logits_s):
    tm = ROUTE_TOKENS
    pt = logits_p.shape[0] // tm
    n = logits_p.shape[0] + logits_s.shape[0]
    tile = pl.BlockSpec((tm, LANES), lambda i: (i, 0))
    return pl.pallas_call(
        functools.partial(_route_kernel, prompt_tiles=pt),
        out_shape=(jax.ShapeDtypeStruct((n, LANES), I32), jax.ShapeDtypeStruct((n, LANES), F32),
                   jax.ShapeDtypeStruct((n, LANES), I32), jax.ShapeDtypeStruct((1, LANES), F32)),
        grid=(n // tm,),
        in_specs=[pl.BlockSpec((tm, LANES), lambda i: (jnp.minimum(i, pt - 1), 0)),
                  pl.BlockSpec((tm, LANES), lambda i: (jnp.maximum(i - pt, 0), 0))],
        out_specs=(tile, tile, tile, pl.BlockSpec((1, LANES), lambda i: (0, 0))),
        scratch_shapes=[pltpu.VMEM((1, LANES), F32)],
        compiler_params=_params(("arbitrary",)),
        name="route",
    )(logits_p, logits_s)


INVERT_TOKENS_MAX = 1024


def _largest_tile(n, unit, cap):
    return max(m for m in range(unit, cap + 1, unit) if n % m == 0)


def _invert_kernel(meta_ref, slot_ref, table_ref, *, pad_token, n_exp, tokens):
    i = pl.program_id(0)
    pairs = tokens * TOP_K

    @pl.when(i == 0)
    def _():
        def fill(r, carry):
            table_ref[r] = pad_token
            return carry

        def fill_group_tail(e, carry):
            lax.fori_loop(meta_ref[e] + meta_ref[2 * n_exp + e], meta_ref[e] + meta_ref[n_exp + e], fill, 0)
            return carry

        lax.fori_loop(0, n_exp, fill_group_tail, 0)
        lax.fori_loop(meta_ref[3 * n_exp] * EXPERT_ROW_TILE, table_ref.shape[0], fill, 0)

    def put(p, carry):
        table_ref[slot_ref[0, p]] = i * tokens + (p >> TOP_K_SHIFT)
        return carry

    lax.fori_loop(0, pairs, put, 0, unroll=16)


def _invert(slots, meta, n_rows, pad_token, n_exp):
    n = slots.shape[0]
    tokens = _largest_tile(n, LANES, INVERT_TOKENS_MAX)
    nt = n // tokens
    pairs = tokens * TOP_K
    return pl.pallas_call(
        functools.partial(_invert_kernel, pad_token=pad_token, n_exp=n_exp, tokens=tokens),
        out_shape=jax.ShapeDtypeStruct((n_rows,), I32),
        grid_spec=pltpu.PrefetchScalarGridSpec(
            num_scalar_prefetch=1,
            grid=(nt,),
            in_specs=[pl.BlockSpec((None, 1, pairs), lambda i, meta: (i, 0, 0), memory_space=pltpu.SMEM)],
            out_specs=pl.BlockSpec(memory_space=pltpu.SMEM)),
        compiler_params=_params(("arbitrary",)),
        name="invert",
    )(meta, slots.reshape(nt, 1, pairs))


ROW_UNROLL = 8
ROW_UNROLL_SHIFT = 3
DISPATCH_TILES = 6


def _dispatch_kernel(na_ref, split_ref, tok_ref, tok_next_ref, h2p_ref, h2s_ref, xs_ref, buf, sem, *,
                     n_prompt):
    i = pl.program_id(0)
    tile = EXPERT_ROW_TILE
    n_active = na_ref[0]

    def issue_tile(tok, t, sub, slot):
        split = split_ref[t]
        base = sub * tile

        def prompt_copy(r):
            pltpu.make_async_copy(h2p_ref.at[pl.ds(tok[0, base + r], 1)], buf.at[slot, pl.ds(base + r, 1)],
                                  sem.at[slot, sub]).start()

        def sample_copy(r):
            pltpu.make_async_copy(h2s_ref.at[pl.ds(tok[0, base + r] - n_prompt, 1)],
                                  buf.at[slot, pl.ds(base + r, 1)], sem.at[slot, sub]).start()

        def rows(lo, hi, fn):
            def body(r, carry):
                fn(r)
                return carry
            lax.fori_loop(lo, hi, body, 0)

        def groups(lo, hi, fn):
            def body(g, carry):
                for u in range(ROW_UNROLL):
                    fn(g * ROW_UNROLL + u)
                return carry
            lax.fori_loop(lo, hi, body, 0)

        whole = split >> ROW_UNROLL_SHIFT
        first = (split + ROW_UNROLL - 1) >> ROW_UNROLL_SHIFT
        groups(0, whole, prompt_copy)
        rows(whole * ROW_UNROLL, split, prompt_copy)
        rows(split, first * ROW_UNROLL, sample_copy)
        groups(first, tile // ROW_UNROLL, sample_copy)

    def issue(tok, step, slot):
        for sub in range(DISPATCH_TILES):
            t = step * DISPATCH_TILES + sub

            @pl.when(t < n_active)
            def _(t=t, sub=sub):
                issue_tile(tok, t, sub, slot)

    @pl.when(i == 0)
    def _():
        issue(tok_ref, 0, 0)

    @pl.when((i + 1) * DISPATCH_TILES < n_active)
    def _():
        issue(tok_next_ref, i + 1, (i + 1) & 1)

    slot = i & 1
    for sub in range(DISPATCH_TILES):
        t = i * DISPATCH_TILES + sub
        out_rows = slice(sub * tile, (sub + 1) * tile)

        @pl.when(t < n_active)
        def _(sub=sub, out_rows=out_rows):
            pltpu.make_async_copy(h2p_ref.at[pl.ds(0, tile)], buf.at[slot, out_rows], sem.at[slot, sub]).wait()
            xs_ref[out_rows] = buf[slot, out_rows].astype(BF16)

        @pl.when(t >= n_active)
        def _(out_rows=out_rows):
            xs_ref[out_rows] = jnp.zeros((tile, xs_ref.shape[1]), BF16)


def _dispatch(h2_p, h2_s, table, split, n_active):
    d = h2_p.shape[1]
    rows = EXPERT_ROW_TILE * DISPATCH_TILES
    n_rows = table.shape[0]
    assert n_rows % rows == 0, (n_rows, rows)
    nt = n_rows // rows
    return pl.pallas_call(
        functools.partial(_dispatch_kernel, n_prompt=h2_p.shape[0]),
        out_shape=jax.ShapeDtypeStruct((n_rows, d), BF16),
        grid_spec=pltpu.PrefetchScalarGridSpec(
            num_scalar_prefetch=2,
            grid=(nt,),
            in_specs=[pl.BlockSpec((None, 1, rows), lambda i, na, sp: (i, 0, 0), memory_space=pltpu.SMEM),
                      pl.BlockSpec((None, 1, rows), lambda i, na, sp: (jnp.minimum(i + 1, nt - 1), 0, 0),
                                   memory_space=pltpu.SMEM),
                      pl.BlockSpec(memory_space=pl.ANY), pl.BlockSpec(memory_space=pl.ANY)],
            out_specs=pl.BlockSpec((rows, d), lambda i, na, sp: (i, 0)),
            scratch_shapes=[pltpu.VMEM((2, rows, d), F32), pltpu.SemaphoreType.DMA((2, DISPATCH_TILES))]),
        compiler_params=_params(("arbitrary",)),
        name="dispatch",
    )(n_active, split, table.reshape(nt, 1, rows), table.reshape(nt, 1, rows), h2_p, h2_s)


def _stream_expert_weights(i, te_ref, first_ref, run_ref, next_ref, meta_ref, n_col_blocks, copies, consume):
    j = pl.program_id(0)
    n_active = meta_ref[0]
    n_runs = meta_ref[1]

    @pl.when(jnp.logical_and(j == 0, i == 0))
    def _():
        for cp in copies(te_ref[0], 0, 0):
            cp.start()

    @pl.when(jnp.logical_and(i < n_active, first_ref[i] == 1))
    def _():
        slot = (j * n_runs + run_ref[i]) & 1
        for cp in copies(te_ref[i], j, slot):
            cp.wait()
        last = run_ref[i] == n_runs - 1
        e_next = jnp.where(last, te_ref[0], next_ref[i])
        j_next = jnp.where(last, j + 1, j)

        @pl.when(j_next < n_col_blocks)
        def _():
            for cp in copies(e_next, j_next, 1 - slot):
                cp.start()

        consume(slot)


def _gate_up_kernel(te_ref, first_ref, run_ref, next_ref, meta_ref, x_ref, bg_ref, w_hbm, o_ref,
                    wbuf, wg_b, wu_b, sem, *, nj, tn):
    j = pl.program_id(0)
    tm = EXPERT_ROW_TILE
    gate_cols = pl.ds(pl.multiple_of(j * tn, tn), tn)
    up_cols = pl.ds(pl.multiple_of((nj + j) * tn, tn), tn)

    def copies(e, jj, slot):
        return [pltpu.make_async_copy(w_hbm.at[e, :, pl.ds(pl.multiple_of((c * nj + jj) * tn, tn), tn)],
                                      wbuf.at[slot, c], sem.at[slot, c]) for c in range(2)]

    def consume(slot):
        wg_b[...] = wbuf[slot, 0].astype(BF16)
        wu_b[...] = wbuf[slot, 1].astype(BF16)

    for sub in range(EXPERT_STEP_TILES):
        i = pl.program_id(1) * EXPERT_STEP_TILES + sub
        rows = slice(sub * tm, (sub + 1) * tm)
        active = i < meta_ref[0]
        _stream_expert_weights(i, te_ref, first_ref, run_ref, next_ref, meta_ref, nj, copies, consume)

        @pl.when(active)
        def _(i=i, rows=rows):
            e = te_ref[i]
            x = x_ref[rows]
            gate = jnp.dot(x, wg_b[...], preferred_element_type=F32) + bg_ref[e, :, gate_cols]
            up = jnp.dot(x, wu_b[...], preferred_element_type=F32) + bg_ref[e, :, up_cols]
            gate = jnp.minimum(gate, SWIGLU_LIMIT)
            up = jnp.clip(up, -SWIGLU_LIMIT, SWIGLU_LIMIT)
            o_ref[rows] = ((up + 1.0) * gate * jax.nn.sigmoid(SWIGLU_ALPHA * gate)).astype(BF16)

        @pl.when(jnp.logical_not(active))
        def _(rows=rows):
            o_ref[rows] = jnp.zeros((tm, o_ref.shape[1]), o_ref.dtype)


def _expert_gate_up(xs, w_gu, b_gu, sched):
    n_rows, d = xs.shape
    n_exp, _, f2 = w_gu.shape
    f = f2 // 2
    tm = EXPERT_ROW_TILE
    tn = MATMUL_COLS
    nj = f // tn
    rows = tm * EXPERT_STEP_TILES
    assert n_rows % rows == 0, (n_rows, rows)
    step = lambda i, meta: jnp.minimum(i, (meta[0] - 1) // EXPERT_STEP_TILES)
    return pl.pallas_call(
        functools.partial(_gate_up_kernel, nj=nj, tn=tn),
        out_shape=jax.ShapeDtypeStruct((n_rows, f), BF16),
        grid_spec=pltpu.PrefetchScalarGridSpec(
            num_scalar_prefetch=5,
            grid=(nj, n_rows // rows),
            in_specs=[pl.BlockSpec((rows, d), lambda j, i, te, fi, ru, nx, meta: (step(i, meta), 0)),
                      pl.BlockSpec((n_exp, 1, f2), lambda j, i, te, fi, ru, nx, meta: (0, 0, 0)),
                      pl.BlockSpec(memory_space=pl.ANY)],
            out_specs=pl.BlockSpec((rows, tn), lambda j, i, te, fi, ru, nx, meta: (i, j)),
            scratch_shapes=[pltpu.VMEM((2, 2, d, tn), F32), pltpu.VMEM((d, tn), BF16),
                            pltpu.VMEM((d, tn), BF16), pltpu.SemaphoreType.DMA((2, 2))]),
        compiler_params=_params(("arbitrary", "arbitrary")),
        name="expert_gate_up",
    )(*sched, xs, b_gu.reshape(n_exp, 1, f2), w_gu)


def _down_kernel(te_ref, first_ref, run_ref, next_ref, meta_ref, a_ref, b_ref, w_hbm, o_ref, wbuf, w_b, sem):
    i = pl.program_id(1)
    active = i < meta_ref[0]

    def copies(e, jj, slot):
        return [pltpu.make_async_copy(w_hbm.at[e], wbuf.at[slot], sem.at[slot])]

    def consume(slot):
        w_b[...] = wbuf[slot].astype(BF16)

    _stream_expert_weights(i, te_ref, first_ref, run_ref, next_ref, meta_ref, 1, copies, consume)

    @pl.when(active)
    def _():
        o_ref[...] = jnp.dot(a_ref[...], w_b[...], preferred_element_type=F32) + b_ref[...]

    @pl.when(jnp.logical_not(active))
    def _():
        o_ref[...] = jnp.zeros_like(o_ref)


def _expert_down(act, w_d, b_d, sched):
    n_rows, f = act.shape
    n_exp, _, d = w_d.shape
    tm = EXPERT_ROW_TILE
    row = lambda i, meta: jnp.minimum(i, meta[0] - 1)
    return pl.pallas_call(
        _down_kernel,
        out_shape=jax.ShapeDtypeStruct((n_rows, d), F32),
        grid_spec=pltpu.PrefetchScalarGridSpec(
            num_scalar_prefetch=5,
            grid=(1, n_rows // tm),
            in_specs=[pl.BlockSpec((tm, f), lambda j, i, te, fi, ru, nx, meta: (row(i, meta), 0)),
                      pl.BlockSpec((None, 1, d), lambda j, i, te, fi, ru, nx, meta: (te[row(i, meta)], 0, 0)),
                      pl.BlockSpec(memory_space=pl.ANY)],
            out_specs=pl.BlockSpec((tm, d), lambda j, i, te, fi, ru, nx, meta: (i, 0)),
            scratch_shapes=[pltpu.VMEM((2, f, d), F32), pltpu.VMEM((f, d), BF16),
                            pltpu.SemaphoreType.DMA((2,))]),
        compiler_params=_params(("arbitrary", "arbitrary")),
        name="expert_down",
    )(*sched, act, b_d.reshape(n_exp, 1, d), w_d)


COMBINE_TOKENS = 256


def _combine_kernel(slot_ref, slot_next_ref, p_ref, x1_ref, gate_ref, g_ref, y_ref, o_ref, buf, sem):
    i = pl.program_id(0)
    tm = x1_ref.shape[0]
    pairs = tm * TOP_K

    def issue(slots, b):
        def body(g, carry):
            for u in range(ROW_UNROLL):
                p = g * ROW_UNROLL + u
                tok = g * (ROW_UNROLL // TOP_K) + u // TOP_K
                pltpu.make_async_copy(y_ref.at[pl.ds(slots[0, p], 1)], buf.at[b, u % TOP_K, pl.ds(tok, 1)],
                                      sem.at[b]).start()
            return carry
        lax.fori_loop(0, pairs // ROW_UNROLL, body, 0)

    @pl.when(i == 0)
    def _():
        issue(slot_ref, 0)

    @pl.when(i + 1 < pl.num_programs(0))
    def _():
        issue(slot_next_ref, (i + 1) & 1)

    b = i & 1
    for k in range(TOP_K):
        pltpu.make_async_copy(y_ref.at[pl.ds(0, tm)], buf.at[b, k], sem.at[b]).wait()
    probs = p_ref[...]
    f = probs[:, 0:1] * buf[b, 0]
    for k in range(1, TOP_K):
        f = f + probs[:, k:k + 1] * buf[b, k]
    o_ref[...] = x1_ref[...] + gate_ref[...] * (_rms(f) * g_ref[...])


def _combine(y, slots, probs, x1, gate, g_post, rows_per_group):
    m, d = x1.shape
    tm = min(m, COMBINE_TOKENS)
    nt = m // tm
    pairs = tm * TOP_K
    return pl.pallas_call(
        _combine_kernel,
        out_shape=jax.ShapeDtypeStruct((m, d), F32),
        grid=(nt,),
        in_specs=[pl.BlockSpec((None, 1, pairs), lambda i: (i, 0, 0), memory_space=pltpu.SMEM),
                  pl.BlockSpec((None, 1, pairs), lambda i: (jnp.minimum(i + 1, nt - 1), 0, 0),
                               memory_space=pltpu.SMEM),
                  pl.BlockSpec((tm, LANES), lambda i: (i, 0)),
                  pl.BlockSpec((tm, d), lambda i: (i, 0)),
                  _mod_spec(gate, tm, rows_per_group),
                  pl.BlockSpec((1, d), lambda i: (0, 0)),
                  pl.BlockSpec(memory_space=pl.ANY)],
        out_specs=pl.BlockSpec((tm, d), lambda i: (i, 0)),
        scratch_shapes=[pltpu.VMEM((2, TOP_K, tm, d), F32), pltpu.SemaphoreType.DMA((2,))],
        compiler_params=_params(("arbitrary",)),
        name="combine",
    )(slots.reshape(nt, 1, pairs), slots.reshape(nt, 1, pairs), probs, x1, gate, g_post.reshape(1, d), y)


def _moe(h2_p, h2_s, logits_p, logits_s, n_exp, w_gu, b_gu, w_d, b_d):
    n = h2_p.shape[0] + h2_s.shape[0]
    tile = EXPERT_ROW_TILE
    top_i, probs, rank, counts = _route(logits_p, logits_s)
    counts = counts[0, :n_exp].astype(I32)
    cap = (counts + tile - 1) // tile * tile
    ends = jnp.cumsum(cap)
    offs = ends - cap
    n_tiles = (n * TOP_K) // tile + n_exp
    n_active = (ends[-1] // tile).astype(I32).reshape(1)
    tile_id = jnp.arange(n_tiles, dtype=I32)
    tile_expert = jnp.minimum(jnp.sum(ends[None, :] <= tile_id[:, None] * tile, axis=1), n_exp - 1).astype(I32)
    first = ((tile_id == 0) | (tile_expert != jnp.roll(tile_expert, 1))) & (tile_id < n_active[0])
    run = jnp.cumsum(first.astype(I32)) - 1
    next_expert = tile_expert[jnp.minimum(ends[tile_expert] // tile, n_tiles - 1)]
    sched = (tile_expert, first.astype(I32), run.astype(I32), next_expert.astype(I32),
             jnp.stack([n_active[0], jnp.sum(first.astype(I32))]).astype(I32))
    expert_ids = jnp.arange(n_exp, dtype=I32)
    pair_offs = jnp.sum(jnp.where(top_i[:, :TOP_K, None] == expert_ids, offs, 0), axis=-1)
    slots = pair_offs + rank[:, :TOP_K]
    n_prompt = h2_p.shape[0]
    meta = jnp.concatenate([offs, cap, counts, n_active]).astype(I32)
    table = _invert(slots, meta, n_tiles * tile, n_prompt, n_exp)
    split = jnp.sum(table.reshape(n_tiles, tile) < n_prompt, axis=1).astype(I32)
    xs = _dispatch(h2_p, h2_s, table, split, n_active)
    act = _expert_gate_up(xs, w_gu, b_gu, sched)
    y = _expert_down(act, w_d, b_d, sched)
    return y, slots, probs


def _layer(xp, xs_, cp_mod, cs_mod, hist_a, hist_q, s0, g_pre_mix, g_post_mix, g_pre_ffn, g_post_ffn,
           w_in, conv_a_w, gdn_conv_w, a_log, dt_bias, g_conv_out, gdn_norm_g, w_out, router_w, router_b,
           w_gu, b_gu, w_d, b_d):
    bsz, seq, d = xp.shape
    ns = xs_.shape[0]
    n_exp = router_w.shape[1]
    dc = N_HEADS * HEAD_D
    d_main = 3 * dc + 3 * dc + dc
    xp2 = xp.reshape(bsz * seq, d)
    xs2 = xs_.reshape(ns, d)

    w_main = w_in.astype(BF16)
    w_ba = jnp.zeros((d, LANES), BF16).at[:, :2 * N_HEADS].set(w_main[:, d_main:])
    w_out_b = w_out.astype(BF16)
    rw = jnp.zeros((d, LANES), F32).at[:, :n_exp].set(router_w)
    rw_hi = rw.astype(BF16)
    rw_lo = (rw - rw_hi.astype(F32)).astype(BF16)
    rb = jnp.full((1, LANES), NEG_BIG, F32).at[0, :n_exp].set(router_b)

    mp = [cp_mod[:, i * d:(i + 1) * d].reshape(bsz, 1, d) for i in range(6)]
    ms = [cs_mod[:, i * d:(i + 1) * d] for i in range(6)]

    proj_p, ba_p = _in_proj(xp2, mp[1], mp[0], g_pre_mix, w_main, w_ba, seq, d_main)
    proj_s, ba_s = _in_proj(xs2, ms[1], ms[0], g_pre_mix, w_main, w_ba, 1, d_main)

    mix_p, ha_p, hq_p, s_p = _mixer_prompt(proj_p, ba_p, bsz, seq, conv_a_w, gdn_conv_w, a_log, dt_bias,
                                           g_conv_out, gdn_norm_g)
    mix_s, ha_s, hq_s, s_s = _mixer_sample(proj_s, ba_s, hist_a.reshape(ns, 2 * dc),
                                           hist_q.reshape(ns, 9 * dc), s0, conv_a_w, gdn_conv_w, a_log,
                                           dt_bias, g_conv_out, gdn_norm_g)

    x1_p, h2_p, lg_p = _post_mix(mix_p, xp2, mp[2], mp[4], mp[3], g_post_mix, g_pre_ffn, w_out_b, rw_hi,
                                 rw_lo, rb, seq)
    x1_s, h2_s, lg_s = _post_mix(mix_s, xs2, ms[2], ms[4], ms[3], g_post_mix, g_pre_ffn, w_out_b, rw_hi,
                                 rw_lo, rb, 1)

    y, slots, probs = _moe(h2_p, h2_s, lg_p, lg_s, n_exp, w_gu, b_gu, w_d, b_d)
    np_ = bsz * seq
    out_p = _combine(y, slots[:np_], probs[:np_], x1_p, mp[5], g_post_ffn, seq)
    out_s = _combine(y, slots[np_:], probs[np_:], x1_s, ms[5], g_post_ffn, 1)
    return (out_p.reshape(bsz, seq, d), out_s.reshape(ns, 1, d), ha_p, hq_p, s_p,
            ha_s.reshape(ns, 2, dc), hq_s.reshape(ns, 3, 3 * dc), s_s)


def kernel(x_prompt, x_sample, state_conv_a, state_gdn_conv, state_gdn_S, c_prompt, c_sample, w_mod, b_mod, g_pre_mix, g_post_mix, g_pre_ffn, g_post_ffn, w_in, conv_a_w, gdn_conv_w, gdn_a_log, gdn_dt_bias, g_conv_out, gdn_norm_g, w_out, router_w, router_b, exp_w_gate_up, exp_b_gate_up, exp_w_down, exp_b_down):
    depth = w_mod.shape[0]
    bp = x_prompt.shape[0]
    xp, xs_ = x_prompt, x_sample
    outs = [[] for _ in range(6)]
    for l in range(depth):
        mod = _modulation(jnp.concatenate([c_prompt, c_sample], axis=0), w_mod[l], b_mod[l])
        res = _layer(xp, xs_, mod[:bp], mod[bp:], state_conv_a[l], state_gdn_conv[l], state_gdn_S[l],
                     g_pre_mix[l], g_post_mix[l], g_pre_ffn[l], g_post_ffn[l], w_in[l], conv_a_w[l],
                     gdn_conv_w[l], gdn_a_log[l], gdn_dt_bias[l], g_conv_out[l], gdn_norm_g[l], w_out[l],
                     router_w[l], router_b[l], exp_w_gate_up[l], exp_b_gate_up[l], exp_w_down[l],
                     exp_b_down[l])
        xp, xs_ = res[0], res[1]
        for acc, r in zip(outs, res[2:]):
            acc.append(r)
    return (xp, xs_) + tuple(o[0][None] if depth == 1 else jnp.stack(o) for o in outs)
```

```python
import functools

import jax
import jax.numpy as jnp
from jax import lax
from jax.experimental import pallas as pl
from jax.experimental.pallas import tpu as pltpu

F32 = jnp.float32
BF16 = jnp.bfloat16
I32 = jnp.int32
HIGHEST = lax.Precision.HIGHEST

EPS = 1e-6
N_HEADS = 8
HEAD_D = 128
TOP_K = 4
TOP_K_SHIFT = 2
SWIGLU_LIMIT = 7.0
SWIGLU_ALPHA = 1.702
CHUNK = 64
LANES = 128
EXPERT_ROW_TILE = 256
EXPERT_STEP_TILES = 2
MATMUL_COLS = 1024
IN_PROJ_ROWS = 1024
POST_MIX_ROWS = 512
HALO = 8
NEG_BIG = -1e30
VMEM_LIMIT = 56 * 1024 * 1024


def _params(semantics, vmem=VMEM_LIMIT):
    return pltpu.CompilerParams(dimension_semantics=semantics, vmem_limit_bytes=vmem)


def _mm(a, b):
    return jnp.dot(a.astype(BF16), b.astype(BF16), preferred_element_type=F32)


def _mm_nt(a, b):
    return lax.dot_general(a.astype(BF16), b.astype(BF16), (((1,), (1,)), ((), ())),
                           preferred_element_type=F32)


def _rms(x):
    return x * lax.rsqrt(jnp.mean(x * x, axis=-1, keepdims=True) + EPS)


def _silu(x):
    return x * jax.nn.sigmoid(x)


def _softplus(x):
    return jnp.maximum(x, 0.0) + jnp.log1p(jnp.exp(-jnp.abs(x)))


def _mod_kernel(c_ref, w_ref, b_ref, o_ref):
    s = _silu(c_ref[...])
    o_ref[...] = _mm(s, w_ref[...]) + b_ref[...]


def _modulation(c_all, w_mod, b_mod):
    n, d = c_all.shape
    m = w_mod.shape[1]
    tn = MATMUL_COLS
    return pl.pallas_call(
        _mod_kernel,
        out_shape=jax.ShapeDtypeStruct((n, m), F32),
        grid=(m // tn,),
        in_specs=[pl.BlockSpec((n, d), lambda j: (0, 0)),
                  pl.BlockSpec((d, tn), lambda j: (0, j)),
                  pl.BlockSpec((1, tn), lambda j: (0, j))],
        out_specs=pl.BlockSpec((n, tn), lambda j: (0, j)),
        compiler_params=_params(("arbitrary",)),
        name="modulation",
    )(c_all, w_mod, b_mod.reshape(1, m))


def _mod_spec(arr, tm, rows_per_group):
    if arr.ndim == 3:
        tiles = rows_per_group // tm
        return pl.BlockSpec((None, 1, arr.shape[-1]), lambda i, *_: (i // tiles, 0, 0))
    return pl.BlockSpec((tm, arr.shape[-1]), lambda i, *_: (i, 0))


def _proj_kernel(x_ref, sc_ref, sh_ref, g_ref, w_ref, wba_ref, o_ref, ba_ref, h_scr):
    @pl.when(pl.program_id(1) == 0)
    def _():
        h = (_rms(x_ref[...]) * g_ref[...]) * (1.0 + sc_ref[...]) + sh_ref[...]
        hb = h.astype(BF16)
        h_scr[...] = hb
        ba_ref[...] = jnp.dot(hb, wba_ref[...], preferred_element_type=F32)

    o_ref[...] = jnp.dot(h_scr[...], w_ref[...], preferred_element_type=F32)


def _in_proj(x, scale, shift, g, w_main, w_ba, rows_per_group, n):
    m, d = x.shape
    tm = min(m, IN_PROJ_ROWS, rows_per_group if scale.ndim == 3 else m)
    tn = MATMUL_COLS
    return pl.pallas_call(
        _proj_kernel,
        out_shape=(jax.ShapeDtypeStruct((m, n), F32), jax.ShapeDtypeStruct((m, LANES), F32)),
        grid=(m // tm, n // tn),
        in_specs=[pl.BlockSpec((tm, d), lambda i, j: (i, 0)),
                  _mod_spec(scale, tm, rows_per_group),
                  _mod_spec(shift, tm, rows_per_group),
                  pl.BlockSpec((1, d), lambda i, j: (0, 0)),
                  pl.BlockSpec((d, tn), lambda i, j: (0, j)),
                  pl.BlockSpec((d, LANES), lambda i, j: (0, 0))],
        out_specs=(pl.BlockSpec((tm, tn), lambda i, j: (i, j)),
                   pl.BlockSpec((tm, LANES), lambda i, j: (i, 0))),
        scratch_shapes=[pltpu.VMEM((tm, d), BF16)],
        compiler_params=_params(("arbitrary", "arbitrary")),
        name="in_proj",
    )(x, scale, shift, g.reshape(1, d), w_main, w_ba)


PROMPT_SEQS_PER_STEP = 4


def _mixer_prompt_kernel(proj_ref, ba_ref, caw_ref, gcw_ref, alog_ref, dtb_ref, gco_ref, gng_ref,
                         mix_ref, ha_ref, hq_ref, s_ref, extu, extq, qc_scr, s_scr, *, nseq):
    c = CHUNK
    dc = N_HEADS * HEAD_D
    dq = 3 * dc
    t = pl.program_id(1)
    is_last = t == pl.num_programs(1) - 1

    @pl.when(t == 0)
    def _():
        extu[:, 0:HALO, :] = jnp.zeros((nseq, HALO, dc), F32)
        extq[:, 0:HALO, :] = jnp.zeros((nseq, HALO, dq), F32)
        s_scr[...] = jnp.zeros_like(s_scr)

    row = lax.broadcasted_iota(I32, (c, c), 0)
    col = lax.broadcasted_iota(I32, (c, c), 1)
    causal = row >= col
    strict = row > col
    lower = jnp.where(causal, 1.0, 0.0).astype(F32)
    upper = jnp.where(row <= col, 1.0, 0.0).astype(F32)
    caw = caw_ref[...]
    gcw = gcw_ref[...]

    chains = [(sq, h) for sq in range(nseq) for h in range(N_HEADS)]
    heads = range(len(chains))
    qn, kn, vb, kb, kbg, qg, kg, decay, s_decay = ([] for _ in range(9))
    for sq in range(nseq):
        u = proj_ref[sq, :, dc:2 * dc] * proj_ref[sq, :, 2 * dc:3 * dc]
        extu[sq, HALO:HALO + c, :] = u
        ya = (caw[0:1] * extu[sq, HALO - 2:HALO - 2 + c, :] + caw[1:2] * extu[sq, HALO - 1:HALO - 1 + c, :]
              + caw[2:3] * u)
        ya = proj_ref[sq, :, 0:dc] * ya
        mix_ref[sq, :, 0:dc] = (_rms(ya) * gco_ref[...]).astype(BF16)
        last_u = extu[sq, HALO - 2 + c:HALO + c, :]
        extu[sq, HALO - 2:HALO, :] = last_u

        qkv = proj_ref[sq, :, 3 * dc:3 * dc + dq]
        extq[sq, HALO:HALO + c, :] = qkv
        qc = (gcw[0:1] * extq[sq, HALO - 3:HALO - 3 + c, :] + gcw[1:2] * extq[sq, HALO - 2:HALO - 2 + c, :]
              + gcw[2:3] * extq[sq, HALO - 1:HALO - 1 + c, :] + gcw[3:4] * qkv)
        qc_scr[sq] = _silu(qc)
        last_q = extq[sq, HALO - 3 + c:HALO + c, :]
        extq[sq, HALO - 3:HALO, :] = last_q

        @pl.when(is_last)
        def _(sq=sq, last_u=last_u, last_q=last_q):
            ha_ref[sq] = last_u
            hq_ref[sq] = last_q

        ba = ba_ref[sq]
        beta_all = jax.nn.sigmoid(ba)
        g_all = -jnp.exp(alog_ref[...]) * _softplus(ba + dtb_ref[...])
        gc_all = jnp.dot(lower, g_all, precision=HIGHEST, preferred_element_type=F32)
        gc_t = lax.dot_general(g_all, upper, (((0,), (0,)), ((), ())), precision=HIGHEST,
                               preferred_element_type=F32)
        for h in range(N_HEADS):
            lo = h * HEAD_D
            q = qc_scr[sq, :, lo:lo + HEAD_D]
            k = qc_scr[sq, :, dc + lo:dc + lo + HEAD_D]
            v = qc_scr[sq, :, 2 * dc + lo:2 * dc + lo + HEAD_D]
            qn_h = q * lax.rsqrt(jnp.sum(q * q, axis=-1, keepdims=True) + EPS) * (HEAD_D ** -0.5)
            kn_h = k * lax.rsqrt(jnp.sum(k * k, axis=-1, keepdims=True) + EPS)
            beta = beta_all[:, h:h + 1]
            gcc = gc_all[:, N_HEADS + h:N_HEADS + h + 1]
            gcr = gc_t[N_HEADS + h:N_HEADS + h + 1, :]
            gl = gc_all[c - 1:c, N_HEADS + h:N_HEADS + h + 1]
            eg = jnp.exp(gcc)
            kb_h = kn_h * beta
            qn.append(qn_h)
            kn.append(kn_h)
            vb.append(v * beta)
            kb.append(kb_h)
            kbg.append(kb_h * eg)
            qg.append(qn_h * eg)
            kg.append(kn_h * jnp.exp(gl - gcc))
            decay.append(jnp.where(causal, jnp.exp(jnp.minimum(gcc - gcr, 0.0)), 0.0))
            s_decay.append(jnp.exp(gl))

    kq = [_mm_nt(jnp.concatenate([kb[h], qn[h]], axis=0), kn[h]) for h in heads]
    a_mat = [jnp.where(strict, kq[h][:c] * decay[h], 0.0) for h in heads]
    qk = [kq[h][c:] * decay[h] for h in heads]
    n_mat = [-a_mat[h] for h in heads]
    p = a_mat
    size = 2
    while size < c:
        p = [_mm(p[h], p[h]) for h in heads]
        n_p = [_mm(n_mat[h], p[h]) for h in heads]
        n_mat = [n_mat[h] + p[h] + n_p[h] for h in heads]
        size *= 2
    rhs = [jnp.concatenate([vb[h], kbg[h]], axis=-1) for h in heads]
    uw = [rhs[h] + _mm(n_mat[h], rhs[h]) for h in heads]
    s_old = [s_scr[sq, h] for sq, h in chains]
    ws = [_mm(jnp.concatenate([uw[h][:, HEAD_D:], qg[h]], axis=0), s_old[h]) for h in heads]
    v_new = [uw[h][:, :HEAD_D] - ws[h][:c] for h in heads]
    fin = [_mm(jnp.concatenate([qk[h], kg[h].T], axis=0), v_new[h]) for h in heads]
    for i, (sq, h) in enumerate(chains):
        lo = h * HEAD_D
        s_scr[sq, h] = s_old[i] * s_decay[i] + fin[i][c:]
        o = ws[i][c:] + fin[i][:c]
        z = proj_ref[sq, :, 3 * dc + dq + lo:3 * dc + dq + lo + HEAD_D]
        yb = _rms(o) * gng_ref[...] * _silu(z)
        mix_ref[sq, :, dc + lo:dc + lo + HEAD_D] = yb.astype(BF16)

    @pl.when(is_last)
    def _():
        s_ref[...] = s_scr[...]


def _lane_row(vec, offset):
    return jnp.zeros((1, LANES), F32).at[0, offset:offset + vec.shape[0]].set(vec.astype(F32))


def _mixer_prompt(proj, ba, bsz, seq, conv_a_w, gdn_conv_w, a_log, dt_bias, g_conv_out, gdn_norm_g):
    c = CHUNK
    dc = N_HEADS * HEAD_D
    dq = 3 * dc
    dproj = proj.shape[1]
    nt = seq // c
    const = lambda shape: pl.BlockSpec(shape, lambda b, t: (0,) * len(shape))
    nseq = PROMPT_SEQS_PER_STEP if bsz % PROMPT_SEQS_PER_STEP == 0 else 1
    seq_block = lambda *tail: pl.BlockSpec((nseq,) + tail, lambda b, t: (b, t) + (0,) * (len(tail) - 1))
    whole = lambda *tail: pl.BlockSpec((nseq,) + tail, lambda b, t: (b,) + (0,) * len(tail))
    mix, ha, hq, s_fin = pl.pallas_call(
        functools.partial(_mixer_prompt_kernel, nseq=nseq),
        out_shape=(jax.ShapeDtypeStruct((bsz, seq, 2 * dc), BF16),
                   jax.ShapeDtypeStruct((bsz, 2, dc), F32),
                   jax.ShapeDtypeStruct((bsz, 3, dq), F32),
                   jax.ShapeDtypeStruct((bsz, N_HEADS, HEAD_D, HEAD_D), F32)),
        grid=(bsz // nseq, nt),
        in_specs=[seq_block(c, dproj), seq_block(c, LANES),
                  const((3, dc)), const((4, dq)), const((1, LANES)), const((1, LANES)),
                  const((1, dc)), const((1, HEAD_D))],
        out_specs=(seq_block(c, 2 * dc), whole(2, dc), whole(3, dq), whole(N_HEADS, HEAD_D, HEAD_D)),
        scratch_shapes=[pltpu.VMEM((nseq, HALO + c, dc), F32), pltpu.VMEM((nseq, HALO + c, dq), F32),
                        pltpu.VMEM((nseq, c, dq), F32), pltpu.VMEM((nseq, N_HEADS, HEAD_D, HEAD_D), F32)],
        compiler_params=_params(("arbitrary", "arbitrary")),
        name="mixer_prompt",
    )(proj.reshape(bsz, seq, dproj), ba.reshape(bsz, seq, LANES), conv_a_w, gdn_conv_w,
      _lane_row(a_log, N_HEADS), _lane_row(dt_bias, N_HEADS), g_conv_out.reshape(1, dc),
      gdn_norm_g.reshape(1, HEAD_D))
    return mix.reshape(bsz * seq, 2 * dc), ha, hq, s_fin


SAMPLE_GROUP = 16


def _mixer_sample_kernel(proj_ref, ba_ref, hista_ref, histq_ref, s_in_ref, caw_ref, gcw_ref, alog_ref,
                         dtb_ref, gco_ref, gng_ref, mix_ref, ha_ref, hq_ref, s_out_ref, qc_scr, o_scr):
    tb = SAMPLE_GROUP
    dc = N_HEADS * HEAD_D
    dq = 3 * dc

    u = proj_ref[:, dc:2 * dc] * proj_ref[:, 2 * dc:3 * dc]
    caw = caw_ref[...]
    ya = caw[0:1] * hista_ref[:, 0:dc] + caw[1:2] * hista_ref[:, dc:2 * dc] + caw[2:3] * u
    ya = proj_ref[:, 0:dc] * ya
    mix_ref[:, 0:dc] = _rms(ya) * gco_ref[...]
    ha_ref[:, 0:dc] = hista_ref[:, dc:2 * dc]
    ha_ref[:, dc:2 * dc] = u

    qkv = proj_ref[:, 3 * dc:3 * dc + dq]
    gcw = gcw_ref[...]
    qc = (gcw[0:1] * histq_ref[:, 0:dq] + gcw[1:2] * histq_ref[:, dq:2 * dq]
          + gcw[2:3] * histq_ref[:, 2 * dq:3 * dq] + gcw[3:4] * qkv)
    qc_scr[...] = _silu(qc)
    hq_ref[:, 0:dq] = histq_ref[:, dq:2 * dq]
    hq_ref[:, dq:2 * dq] = histq_ref[:, 2 * dq:3 * dq]
    hq_ref[:, 2 * dq:3 * dq] = qkv

    ba = ba_ref[...]
    beta_all = jax.nn.sigmoid(ba)
    eg_all = jnp.exp(-jnp.exp(alog_ref[...]) * _softplus(ba + dtb_ref[...]))

    for h in range(N_HEADS):
        lo = h * HEAD_D
        q = qc_scr[:, lo:lo + HEAD_D]
        k = qc_scr[:, dc + lo:dc + lo + HEAD_D]
        v = qc_scr[:, 2 * dc + lo:2 * dc + lo + HEAD_D]
        qn = q * lax.rsqrt(jnp.sum(q * q, axis=-1, keepdims=True) + EPS) * (HEAD_D ** -0.5)
        kn = k * lax.rsqrt(jnp.sum(k * k, axis=-1, keepdims=True) + EPS)
        qk = jnp.sum(qn * kn, axis=-1, keepdims=True)
        kn_t = kn.T
        qn_t = qn.T
        for b in range(tb):
            s_old = s_in_ref[b, h]
            kc = jnp.broadcast_to(kn_t[:, b:b + 1], s_old.shape)
            e = eg_all[b:b + 1, N_HEADS + h:N_HEADS + h + 1]
            ks = jnp.sum(s_old * kc, axis=0, keepdims=True)
            qs = jnp.sum(s_old * qn_t[:, b:b + 1], axis=0, keepdims=True)
            v_new = beta_all[b:b + 1, h:h + 1] * (v[b:b + 1, :] - e * ks)
            o_scr[b:b + 1, lo:lo + HEAD_D] = e * qs + qk[b:b + 1, :] * v_new
            s_out_ref[b, h] = s_old * e + kc * v_new
        z = proj_ref[:, 3 * dc + dq + lo:3 * dc + dq + lo + HEAD_D]
        o = o_scr[:, lo:lo + HEAD_D]
        mix_ref[:, dc + lo:dc + lo + HEAD_D] = _rms(o) * gng_ref[...] * _silu(z)


def _mixer_sample(proj, ba, hist_a, hist_q, s_in, conv_a_w, gdn_conv_w, a_log, dt_bias, g_conv_out,
                  gdn_norm_g):
    n = proj.shape[0]
    tb = SAMPLE_GROUP
    dc = N_HEADS * HEAD_D
    dq = 3 * dc
    dproj = proj.shape[1]
    const = lambda shape: pl.BlockSpec(shape, lambda i: (0,) * len(shape))
    rows = lambda width: pl.BlockSpec((tb, width), lambda i: (i, 0))
    state = pl.BlockSpec((tb, N_HEADS, HEAD_D, HEAD_D), lambda i: (i, 0, 0, 0))
    return pl.pallas_call(
        _mixer_sample_kernel,
        out_shape=(jax.ShapeDtypeStruct((n, 2 * dc), F32),
                   jax.ShapeDtypeStruct((n, 2 * dc), F32),
                   jax.ShapeDtypeStruct((n, 3 * dq), F32),
                   jax.ShapeDtypeStruct((n, N_HEADS, HEAD_D, HEAD_D), F32)),
        grid=(n // tb,),
        in_specs=[rows(dproj), rows(LANES), rows(2 * dc), rows(3 * dq), state,
                  const((3, dc)), const((4, dq)), const((1, LANES)), const((1, LANES)),
                  const((1, dc)), const((1, HEAD_D))],
        out_specs=(rows(2 * dc), rows(2 * dc), rows(3 * dq), state),
        scratch_shapes=[pltpu.VMEM((tb, dq), F32), pltpu.VMEM((tb, dc), F32)],
        compiler_params=_params(("arbitrary",)),
        name="mixer_sample",
    )(proj, ba, hist_a, hist_q, s_in, conv_a_w, gdn_conv_w, _lane_row(a_log, N_HEADS),
      _lane_row(dt_bias, N_HEADS), g_conv_out.reshape(1, dc), gdn_norm_g.reshape(1, HEAD_D))


POST_MIX_SUB_ROWS = 256


def _post_mix_kernel(mix_ref, x_ref, gate_ref, sc_ref, sh_ref, gpost_ref, gpre_ref, wout_ref,
                     rwh_ref, rwl_ref, rb_ref, x1_ref, h2_ref, lg_ref):
    tm = x_ref.shape[0]
    sub = min(tm, POST_MIX_SUB_ROWS)
    tiles = [slice(r0, r0 + sub) for r0 in range(0, tm, sub)]
    mixes = [jnp.dot(mix_ref[rows].astype(BF16), wout_ref[...], preferred_element_type=F32)
             for rows in tiles]
    for rows, mix in zip(tiles, mixes):
        per_row = lambda ref: ref[rows] if ref.shape[0] == tm else ref[...]
        x1 = x_ref[rows] + per_row(gate_ref) * (_rms(mix) * gpost_ref[...])
        x1_ref[rows] = x1
        h2 = (_rms(x1) * gpre_ref[...]) * (1.0 + per_row(sc_ref)) + per_row(sh_ref)
        h2_ref[rows] = h2
        hi = h2.astype(BF16)
        lo = (h2 - hi.astype(F32)).astype(BF16)
        rwh = rwh_ref[...]
        lg_ref[rows] = (jnp.dot(hi, rwh, preferred_element_type=F32)
                        + jnp.dot(lo, rwh, preferred_element_type=F32)
                        + jnp.dot(hi, rwl_ref[...], preferred_element_type=F32) + rb_ref[...])


def _post_mix(mix_in, x, gate, scale, shift, g_post, g_pre, w_out, rw_hi, rw_lo, rb, rows_per_group):
    m, d = x.shape
    tm = min(m, POST_MIX_ROWS, rows_per_group if gate.ndim == 3 else m)
    const = lambda shape: pl.BlockSpec(shape, lambda i: (0,) * len(shape))
    rows = lambda width: pl.BlockSpec((tm, width), lambda i: (i, 0))
    return pl.pallas_call(
        _post_mix_kernel,
        out_shape=(jax.ShapeDtypeStruct((m, d), F32), jax.ShapeDtypeStruct((m, d), F32),
                   jax.ShapeDtypeStruct((m, LANES), F32)),
        grid=(m // tm,),
        in_specs=[rows(d), rows(d),
                  _mod_spec(gate, tm, rows_per_group),
                  _mod_spec(scale, tm, rows_per_group),
                  _mod_spec(shift, tm, rows_per_group),
                  const((1, d)), const((1, d)), const((d, d)),
                  const((d, LANES)), const((d, LANES)), const((1, LANES))],
        out_specs=(rows(d), rows(d), rows(LANES)),
        compiler_params=_params(("arbitrary",)),
        name="post_mix",
    )(mix_in, x, gate, scale, shift, g_post.reshape(1, d), g_pre.reshape(1, d), w_out, rw_hi, rw_lo, rb)


ROUTE_TOKENS = 128


def _route_kernel(lgp_ref, lgs_ref, idx_ref, p_ref, rank_ref, cnt_ref, carry, *, prompt_tiles):
    tm = lgp_ref.shape[0]

    @pl.when(pl.program_id(0) == 0)
    def _():
        carry[...] = jnp.zeros_like(carry)

    l = jnp.where(pl.program_id(0) < prompt_tiles, lgp_ref[...], lgs_ref[...])
    lane = lax.broadcasted_iota(I32, l.shape, 1)
    lane_f = lane.astype(F32)
    vals, hots = [], []
    idx_out = jnp.zeros(l.shape, F32)
    for k in range(TOP_K):
        m = jnp.max(l, axis=-1, keepdims=True)
        idx = jnp.min(jnp.where(l == m, lane_f, float(LANES)), axis=-1, keepdims=True)
        hot = lane_f == idx
        vals.append(m)
        hots.append(hot)
        idx_out = jnp.where(lane == k, idx, idx_out)
        l = jnp.where(hot, -jnp.inf, l)
    exps = [jnp.exp(v - vals[0]) for v in vals]
    denom = exps[0] + exps[1] + exps[2] + exps[3]
    p_out = jnp.zeros(l.shape, F32)
    for k in range(TOP_K):
        p_out = jnp.where(lane == k, exps[k] / denom, p_out)
    member = jnp.where(hots[0] | hots[1] | hots[2] | hots[3], 1.0, 0.0).astype(F32)
    row = lax.broadcasted_iota(I32, (tm, tm), 0)
    col = lax.broadcasted_iota(I32, (tm, tm), 1)
    before = jnp.where(row > col, 1.0, 0.0).astype(BF16)
    prefix = jnp.dot(before, member.astype(BF16), preferred_element_type=F32) + carry[...]
    rank_out = jnp.zeros(l.shape, F32)
    for k in range(TOP_K):
        r = jnp.sum(jnp.where(hots[k], prefix, 0.0), axis=-1, keepdims=True)
        rank_out = jnp.where(lane == k, r, rank_out)
    carry[...] = carry[...] + jnp.sum(member, axis=0, keepdims=True)
    idx_ref[...] = idx_out.astype(I32)
    p_ref[...] = p_out
    rank_ref[...] = rank_out.astype(I32)
    cnt_ref[...] = carry[...]


def _route(logits_p, logits_s):
    tm = ROUTE_TOKENS
    pt = logits_p.shape[0] // tm
    n = logits_p.shape[0] + logits_s.shape[0]
    tile = pl.BlockSpec((tm, LANES), lambda i: (i, 0))
    return pl.pallas_call(
        functools.partial(_route_kernel, prompt_tiles=pt),
        out_shape=(jax.ShapeDtypeStruct((n, LANES), I32), jax.ShapeDtypeStruct((n, LANES), F32),
                   jax.ShapeDtypeStruct((n, LANES), I32), jax.ShapeDtypeStruct((1, LANES), F32)),
        grid=(n // tm,),
        in_specs=[pl.BlockSpec((tm, LANES), lambda i: (jnp.minimum(i, pt - 1), 0)),
                  pl.BlockSpec((tm, LANES), lambda i: (jnp.maximum(i - pt, 0), 0))],
        out_specs=(tile, tile, tile, pl.BlockSpec((1, LANES), lambda i: (0, 0))),
        scratch_shapes=[pltpu.VMEM((1, LANES), F32)],
        compiler_params=_params(("arbitrary",)),
        name="route",
    )(logits_p, logits_s)


INVERT_TOKENS_MAX = 1024


def _largest_tile(n, unit, cap):
    return max(m for m in range(unit, cap + 1, unit) if n % m == 0)


def _invert_kernel(meta_ref, slot_ref, table_ref, *, pad_token, n_exp, tokens):
    i = pl.program_id(0)
    pairs = tokens * TOP_K

    @pl.when(i == 0)
    def _():
        def fill(r, carry):
            table_ref[r] = pad_token
            return carry

        def fill_group_tail(e, carry):
            lax.fori_loop(meta_ref[e] + meta_ref[2 * n_exp + e], meta_ref[e] + meta_ref[n_exp + e], fill, 0)
            return carry

        lax.fori_loop(0, n_exp, fill_group_tail, 0)
        lax.fori_loop(meta_ref[3 * n_exp] * EXPERT_ROW_TILE, table_ref.shape[0], fill, 0)

    def put(p, carry):
        table_ref[slot_ref[0, p]] = i * tokens + (p >> TOP_K_SHIFT)
        return carry

    lax.fori_loop(0, pairs, put, 0, unroll=16)


def _invert(slots, meta, n_rows, pad_token, n_exp):
    n = slots.shape[0]
    tokens = _largest_tile(n, LANES, INVERT_TOKENS_MAX)
    nt = n // tokens
    pairs = tokens * TOP_K
    return pl.pallas_call(
        functools.partial(_invert_kernel, pad_token=pad_token, n_exp=n_exp, tokens=tokens),
        out_shape=jax.ShapeDtypeStruct((n_rows,), I32),
        grid_spec=pltpu.PrefetchScalarGridSpec(
            num_scalar_prefetch=1,
            grid=(nt,),
            in_specs=[pl.BlockSpec((None, 1, pairs), lambda i, meta: (i, 0, 0), memory_space=pltpu.SMEM)],
            out_specs=pl.BlockSpec(memory_space=pltpu.SMEM)),
        compiler_params=_params(("arbitrary",)),
        name="invert",
    )(meta, slots.reshape(nt, 1, pairs))


ROW_UNROLL = 8
ROW_UNROLL_SHIFT = 3
DISPATCH_TILES = 6


def _dispatch_kernel(na_ref, split_ref, tok_ref, tok_next_ref, h2p_ref, h2s_ref, xs_ref, buf, sem, *,
                     n_prompt):
    i = pl.program_id(0)
    tile = EXPERT_ROW_TILE
    n_active = na_ref[0]

    def issue_tile(tok, t, sub, slot):
        split = split_ref[t]
        base = sub * tile

        def prompt_copy(r):
            pltpu.make_async_copy(h2p_ref.at[pl.ds(tok[0, base + r], 1)], buf.at[slot, pl.ds(base + r, 1)],
                                  sem.at[slot, sub]).start()

        def sample_copy(r):
            pltpu.make_async_copy(h2s_ref.at[pl.ds(tok[0, base + r] - n_prompt, 1)],
                                  buf.at[slot, pl.ds(base + r, 1)], sem.at[slot, sub]).start()

        def rows(lo, hi, fn):
            def body(r, carry):
                fn(r)
                return carry
            lax.fori_loop(lo, hi, body, 0)

        def groups(lo, hi, fn):
            def body(g, carry):
                for u in range(ROW_UNROLL):
                    fn(g * ROW_UNROLL + u)
                return carry
            lax.fori_loop(lo, hi, body, 0)

        whole = split >> ROW_UNROLL_SHIFT
        first = (split + ROW_UNROLL - 1) >> ROW_UNROLL_SHIFT
        groups(0, whole, prompt_copy)
        rows(whole * ROW_UNROLL, split, prompt_copy)
        rows(split, first * ROW_UNROLL, sample_copy)
        groups(first, tile // ROW_UNROLL, sample_copy)

    def issue(tok, step, slot):
        for sub in range(DISPATCH_TILES):
            t = step * DISPATCH_TILES + sub

            @pl.when(t < n_active)
            def _(t=t, sub=sub):
                issue_tile(tok, t, sub, slot)

    @pl.when(i == 0)
    def _():
        issue(tok_ref, 0, 0)

    @pl.when((i + 1) * DISPATCH_TILES < n_active)
    def _():
        issue(tok_next_ref, i + 1, (i + 1) & 1)

    slot = i & 1
    for sub in range(DISPATCH_TILES):
        t = i * DISPATCH_TILES + sub
        out_rows = slice(sub * tile, (sub + 1) * tile)

        @pl.when(t < n_active)
        def _(sub=sub, out_rows=out_rows):
            pltpu.make_async_copy(h2p_ref.at[pl.ds(0, tile)], buf.at[slot, out_rows], sem.at[slot, sub]).wait()
            xs_ref[out_rows] = buf[slot, out_rows].astype(BF16)

        @pl.when(t >= n_active)
        def _(out_rows=out_rows):
            xs_ref[out_rows] = jnp.zeros((tile, xs_ref.shape[1]), BF16)


def _dispatch(h2_p, h2_s, table, split, n_active):
    d = h2_p.shape[1]
    rows = EXPERT_ROW_TILE * DISPATCH_TILES
    n_rows = table.shape[0]
    assert n_rows % rows == 0, (n_rows, rows)
    nt = n_rows // rows
    return pl.pallas_call(
        functools.partial(_dispatch_kernel, n_prompt=h2_p.shape[0]),
        out_shape=jax.ShapeDtypeStruct((n_rows, d), BF16),
        grid_spec=pltpu.PrefetchScalarGridSpec(
            num_scalar_prefetch=2,
            grid=(nt,),
            in_specs=[pl.BlockSpec((None, 1, rows), lambda i, na, sp: (i, 0, 0), memory_space=pltpu.SMEM),
                      pl.BlockSpec((None, 1, rows), lambda i, na, sp: (jnp.minimum(i + 1, nt - 1), 0, 0),
                                   memory_space=pltpu.SMEM),
                      pl.BlockSpec(memory_space=pl.ANY), pl.BlockSpec(memory_space=pl.ANY)],
            out_specs=pl.BlockSpec((rows, d), lambda i, na, sp: (i, 0)),
            scratch_shapes=[pltpu.VMEM((2, rows, d), F32), pltpu.SemaphoreType.DMA((2, DISPATCH_TILES))]),
        compiler_params=_params(("arbitrary",)),
        name="dispatch",
    )(n_active, split, table.reshape(nt, 1, rows), table.reshape(nt, 1, rows), h2_p, h2_s)


def _stream_expert_weights(i, te_ref, first_ref, run_ref, next_ref, meta_ref, n_col_blocks, copies, consume):
    j = pl.program_id(0)
    n_active = meta_ref[0]
    n_runs = meta_ref[1]

    @pl.when(jnp.logical_and(j == 0, i == 0))
    def _():
        for cp in copies(te_ref[0], 0, 0):
            cp.start()

    @pl.when(jnp.logical_and(i < n_active, first_ref[i] == 1))
    def _():
        slot = (j * n_runs + run_ref[i]) & 1
        for cp in copies(te_ref[i], j, slot):
            cp.wait()
        last = run_ref[i] == n_runs - 1
        e_next = jnp.where(last, te_ref[0], next_ref[i])
        j_next = jnp.where(last, j + 1, j)

        @pl.when(j_next < n_col_blocks)
        def _():
            for cp in copies(e_next, j_next, 1 - slot):
                cp.start()

        consume(slot)


def _gate_up_kernel(te_ref, first_ref, run_ref, next_ref, meta_ref, x_ref, bg_ref, w_hbm, o_ref,
                    wbuf, wg_b, wu_b, sem, *, nj, tn):
    j = pl.program_id(0)
    tm = EXPERT_ROW_TILE
    gate_cols = pl.ds(pl.multiple_of(j * tn, tn), tn)
    up_cols = pl.ds(pl.multiple_of((nj + j) * tn, tn), tn)

    def copies(e, jj, slot):
        return [pltpu.make_async_copy(w_hbm.at[e, :, pl.ds(pl.multiple_of((c * nj + jj) * tn, tn), tn)],
                                      wbuf.at[slot, c], sem.at[slot, c]) for c in range(2)]

    def consume(slot):
        wg_b[...] = wbuf[slot, 0].astype(BF16)
        wu_b[...] = wbuf[slot, 1].astype(BF16)

    for sub in range(EXPERT_STEP_TILES):
        i = pl.program_id(1) * EXPERT_STEP_TILES + sub
        rows = slice(sub * tm, (sub + 1) * tm)
        active = i < meta_ref[0]
        _stream_expert_weights(i, te_ref, first_ref, run_ref, next_ref, meta_ref, nj, copies, consume)

        @pl.when(active)
        def _(i=i, rows=rows):
            e = te_ref[i]
            x = x_ref[rows]
            gate = jnp.dot(x, wg_b[...], preferred_element_type=F32) + bg_ref[e, :, gate_cols]
            up = jnp.dot(x, wu_b[...], preferred_element_type=F32) + bg_ref[e, :, up_cols]
            gate = jnp.minimum(gate, SWIGLU_LIMIT)
            up = jnp.clip(up, -SWIGLU_LIMIT, SWIGLU_LIMIT)
            o_ref[rows] = ((up + 1.0) * gate * jax.nn.sigmoid(SWIGLU_ALPHA * gate)).astype(BF16)

        @pl.when(jnp.logical_not(active))
        def _(rows=rows):
            o_ref[rows] = jnp.zeros((tm, o_ref.shape[1]), o_ref.dtype)


def _expert_gate_up(xs, w_gu, b_gu, sched):
    n_rows, d = xs.shape
    n_exp, _, f2 = w_gu.shape
    f = f2 // 2
    tm = EXPERT_ROW_TILE
    tn = MATMUL_COLS
    nj = f // tn
    rows = tm * EXPERT_STEP_TILES
    assert n_rows % rows == 0, (n_rows, rows)
    step = lambda i, meta: jnp.minimum(i, (meta[0] - 1) // EXPERT_STEP_TILES)
    return pl.pallas_call(
        functools.partial(_gate_up_kernel, nj=nj, tn=tn),
        out_shape=jax.ShapeDtypeStruct((n_rows, f), BF16),
        grid_spec=pltpu.PrefetchScalarGridSpec(
            num_scalar_prefetch=5,
            grid=(nj, n_rows // rows),
            in_specs=[pl.BlockSpec((rows, d), lambda j, i, te, fi, ru, nx, meta: (step(i, meta), 0)),
                      pl.BlockSpec((n_exp, 1, f2), lambda j, i, te, fi, ru, nx, meta: (0, 0, 0)),
                      pl.BlockSpec(memory_space=pl.ANY)],
            out_specs=pl.BlockSpec((rows, tn), lambda j, i, te, fi, ru, nx, meta: (i, j)),
            scratch_shapes=[pltpu.VMEM((2, 2, d, tn), F32), pltpu.VMEM((d, tn), BF16),
                            pltpu.VMEM((d, tn), BF16), pltpu.SemaphoreType.DMA((2, 2))]),
        compiler_params=_params(("arbitrary", "arbitrary")),
        name="expert_gate_up",
    )(*sched, xs, b_gu.reshape(n_exp, 1, f2), w_gu)


def _down_kernel(te_ref, first_ref, run_ref, next_ref, meta_ref, a_ref, b_ref, w_hbm, o_ref, wbuf, w_b, sem):
    tm = EXPERT_ROW_TILE

    def copies(e, jj, slot):
        return [pltpu.make_async_copy(w_hbm.at[e], wbuf.at[slot], sem.at[slot])]

    def consume(slot):
        w_b[...] = wbuf[slot].astype(BF16)

    for sub in range(EXPERT_STEP_TILES):
        i = pl.program_id(1) * EXPERT_STEP_TILES + sub
        rows = slice(sub * tm, (sub + 1) * tm)
        active = i < meta_ref[0]
        _stream_expert_weights(i, te_ref, first_ref, run_ref, next_ref, meta_ref, 1, copies, consume)

        @pl.when(active)
        def _(i=i, rows=rows):
            o_ref[rows] = jnp.dot(a_ref[rows], w_b[...], preferred_element_type=F32) + b_ref[te_ref[i]]

        @pl.when(jnp.logical_not(active))
        def _(rows=rows):
            o_ref[rows] = jnp.zeros((tm, o_ref.shape[1]), o_ref.dtype)


def _expert_down(act, w_d, b_d, sched):
    n_rows, f = act.shape
    n_exp, _, d = w_d.shape
    rows = EXPERT_ROW_TILE * EXPERT_STEP_TILES
    assert n_rows % rows == 0, (n_rows, rows)
    step = lambda i, meta: jnp.minimum(i, (meta[0] - 1) // EXPERT_STEP_TILES)
    return pl.pallas_call(
        _down_kernel,
        out_shape=jax.ShapeDtypeStruct((n_rows, d), F32),
        grid_spec=pltpu.PrefetchScalarGridSpec(
            num_scalar_prefetch=5,
            grid=(1, n_rows // rows),
            in_specs=[pl.BlockSpec((rows, f), lambda j, i, te, fi, ru, nx, meta: (step(i, meta), 0)),
                      pl.BlockSpec((n_exp, 1, d), lambda j, i, te, fi, ru, nx, meta: (0, 0, 0)),
                      pl.BlockSpec(memory_space=pl.ANY)],
            out_specs=pl.BlockSpec((rows, d), lambda j, i, te, fi, ru, nx, meta: (i, 0)),
            scratch_shapes=[pltpu.VMEM((2, f, d), F32), pltpu.VMEM((f, d), BF16),
                            pltpu.SemaphoreType.DMA((2,))]),
        compiler_params=_params(("arbitrary", "arbitrary")),
        name="expert_down",
    )(*sched, act, b_d.reshape(n_exp, 1, d), w_d)


COMBINE_TOKENS = 256


def _combine_kernel(slot_ref, slot_next_ref, p_ref, x1_ref, gate_ref, g_ref, y_ref, o_ref, buf, sem):
    i = pl.program_id(0)
    tm = x1_ref.shape[0]
    pairs = tm * TOP_K

    def issue(slots, b):
        def body(g, carry):
            for u in range(ROW_UNROLL):
                p = g * ROW_UNROLL + u
                tok = g * (ROW_UNROLL // TOP_K) + u // TOP_K
                pltpu.make_async_copy(y_ref.at[pl.ds(slots[0, p], 1)], buf.at[b, u % TOP_K, pl.ds(tok, 1)],
                                      sem.at[b]).start()
            return carry
        lax.fori_loop(0, pairs // ROW_UNROLL, body, 0)

    @pl.when(i == 0)
    def _():
        issue(slot_ref, 0)

    @pl.when(i + 1 < pl.num_programs(0))
    def _():
        issue(slot_next_ref, (i + 1) & 1)

    b = i & 1
    for k in range(TOP_K):
        pltpu.make_async_copy(y_ref.at[pl.ds(0, tm)], buf.at[b, k], sem.at[b]).wait()
    probs = p_ref[...]
    f = probs[:, 0:1] * buf[b, 0]
    for k in range(1, TOP_K):
        f = f + probs[:, k:k + 1] * buf[b, k]
    o_ref[...] = x1_ref[...] + gate_ref[...] * (_rms(f) * g_ref[...])


def _combine(y, slots, probs, x1, gate, g_post, rows_per_group):
    m, d = x1.shape
    tm = min(m, COMBINE_TOKENS)
    nt = m // tm
    pairs = tm * TOP_K
    return pl.pallas_call(
        _combine_kernel,
        out_shape=jax.ShapeDtypeStruct((m, d), F32),
        grid=(nt,),
        in_specs=[pl.BlockSpec((None, 1, pairs), lambda i: (i, 0, 0), memory_space=pltpu.SMEM),
                  pl.BlockSpec((None, 1, pairs), lambda i: (jnp.minimum(i + 1, nt - 1), 0, 0),
                               memory_space=pltpu.SMEM),
                  pl.BlockSpec((tm, LANES), lambda i: (i, 0)),
                  pl.BlockSpec((tm, d), lambda i: (i, 0)),
                  _mod_spec(gate, tm, rows_per_group),
                  pl.BlockSpec((1, d), lambda i: (0, 0)),
                  pl.BlockSpec(memory_space=pl.ANY)],
        out_specs=pl.BlockSpec((tm, d), lambda i: (i, 0)),
        scratch_shapes=[pltpu.VMEM((2, TOP_K, tm, d), F32), pltpu.SemaphoreType.DMA((2,))],
        compiler_params=_params(("arbitrary",)),
        name="combine",
    )(slots.reshape(nt, 1, pairs), slots.reshape(nt, 1, pairs), probs, x1, gate, g_post.reshape(1, d), y)


def _moe(h2_p, h2_s, logits_p, logits_s, n_exp, w_gu, b_gu, w_d, b_d):
    n = h2_p.shape[0] + h2_s.shape[0]
    tile = EXPERT_ROW_TILE
    top_i, probs, rank, counts = _route(logits_p, logits_s)
    counts = counts[0, :n_exp].astype(I32)
    cap = (counts + tile - 1) // tile * tile
    ends = jnp.cumsum(cap)
    offs = ends - cap
    n_tiles = (n * TOP_K) // tile + n_exp
    n_active = (ends[-1] // tile).astype(I32).reshape(1)
    tile_id = jnp.arange(n_tiles, dtype=I32)
    tile_expert = jnp.minimum(jnp.sum(ends[None, :] <= tile_id[:, None] * tile, axis=1), n_exp - 1).astype(I32)
    first = ((tile_id == 0) | (tile_expert != jnp.roll(tile_expert, 1))) & (tile_id < n_active[0])
    run = jnp.cumsum(first.astype(I32)) - 1
    next_expert = tile_expert[jnp.minimum(ends[tile_expert] // tile, n_tiles - 1)]
    sched = (tile_expert, first.astype(I32), run.astype(I32), next_expert.astype(I32),
             jnp.stack([n_active[0], jnp.sum(first.astype(I32))]).astype(I32))
    expert_ids = jnp.arange(n_exp, dtype=I32)
    pair_offs = jnp.sum(jnp.where(top_i[:, :TOP_K, None] == expert_ids, offs, 0), axis=-1)
    slots = pair_offs + rank[:, :TOP_K]
    n_prompt = h2_p.shape[0]
    meta = jnp.concatenate([offs, cap, counts, n_active]).astype(I32)
    table = _invert(slots, meta, n_tiles * tile, n_prompt, n_exp)
    split = jnp.sum(table.reshape(n_tiles, tile) < n_prompt, axis=1).astype(I32)
    xs = _dispatch(h2_p, h2_s, table, split, n_active)
    act = _expert_gate_up(xs, w_gu, b_gu, sched)
    y = _expert_down(act, w_d, b_d, sched)
    return y, slots, probs


def _layer(xp, xs_, cp_mod, cs_mod, hist_a, hist_q, s0, g_pre_mix, g_post_mix, g_pre_ffn, g_post_ffn,
           w_in, conv_a_w, gdn_conv_w, a_log, dt_bias, g_conv_out, gdn_norm_g, w_out, router_w, router_b,
           w_gu, b_gu, w_d, b_d):
    bsz, seq, d = xp.shape
    ns = xs_.shape[0]
    n_exp = router_w.shape[1]
    dc = N_HEADS * HEAD_D
    d_main = 3 * dc + 3 * dc + dc
    xp2 = xp.reshape(bsz * seq, d)
    xs2 = xs_.reshape(ns, d)

    w_main = w_in.astype(BF16)
    w_ba = jnp.zeros((d, LANES), BF16).at[:, :2 * N_HEADS].set(w_main[:, d_main:])
    w_out_b = w_out.astype(BF16)
    rw = jnp.zeros((d, LANES), F32).at[:, :n_exp].set(router_w)
    rw_hi = rw.astype(BF16)
    rw_lo = (rw - rw_hi.astype(F32)).astype(BF16)
    rb = jnp.full((1, LANES), NEG_BIG, F32).at[0, :n_exp].set(router_b)

    mp = [cp_mod[:, i * d:(i + 1) * d].reshape(bsz, 1, d) for i in range(6)]
    ms = [cs_mod[:, i * d:(i + 1) * d] for i in range(6)]

    proj_p, ba_p = _in_proj(xp2, mp[1], mp[0], g_pre_mix, w_main, w_ba, seq, d_main)
    proj_s, ba_s = _in_proj(xs2, ms[1], ms[0], g_pre_mix, w_main, w_ba, 1, d_main)

    mix_p, ha_p, hq_p, s_p = _mixer_prompt(proj_p, ba_p, bsz, seq, conv_a_w, gdn_conv_w, a_log, dt_bias,
                                           g_conv_out, gdn_norm_g)
    mix_s, ha_s, hq_s, s_s = _mixer_sample(proj_s, ba_s, hist_a.reshape(ns, 2 * dc),
                                           hist_q.reshape(ns, 9 * dc), s0, conv_a_w, gdn_conv_w, a_log,
                                           dt_bias, g_conv_out, gdn_norm_g)

    x1_p, h2_p, lg_p = _post_mix(mix_p, xp2, mp[2], mp[4], mp[3], g_post_mix, g_pre_ffn, w_out_b, rw_hi,
                                 rw_lo, rb, seq)
    x1_s, h2_s, lg_s = _post_mix(mix_s, xs2, ms[2], ms[4], ms[3], g_post_mix, g_pre_ffn, w_out_b, rw_hi,
                                 rw_lo, rb, 1)

    y, slots, probs = _moe(h2_p, h2_s, lg_p, lg_s, n_exp, w_gu, b_gu, w_d, b_d)
    np_ = bsz * seq
    out_p = _combine(y, slots[:np_], probs[:np_], x1_p, mp[5], g_post_ffn, seq)
    out_s = _combine(y, slots[np_:], probs[np_:], x1_s, ms[5], g_post_ffn, 1)
    return (out_p.reshape(bsz, seq, d), out_s.reshape(ns, 1, d), ha_p, hq_p, s_p,
            ha_s.reshape(ns, 2, dc), hq_s.reshape(ns, 3, 3 * dc), s_s)


def kernel(x_prompt, x_sample, state_conv_a, state_gdn_conv, state_gdn_S, c_prompt, c_sample, w_mod, b_mod, g_pre_mix, g_post_mix, g_pre_ffn, g_post_ffn, w_in, conv_a_w, gdn_conv_w, gdn_a_log, gdn_dt_bias, g_conv_out, gdn_norm_g, w_out, router_w, router_b, exp_w_gate_up, exp_b_gate_up, exp_w_down, exp_b_down):
    depth = w_mod.shape[0]
    bp = x_prompt.shape[0]
    xp, xs_ = x_prompt, x_sample
    outs = [[] for _ in range(6)]
    for l in range(depth):
        mod = _modulation(jnp.concatenate([c_prompt, c_sample], axis=0), w_mod[l], b_mod[l])
        res = _layer(xp, xs_, mod[:bp], mod[bp:], state_conv_a[l], state_gdn_conv[l], state_gdn_S[l],
                     g_pre_mix[l], g_post_mix[l], g_pre_ffn[l], g_post_ffn[l], w_in[l], conv_a_w[l],
                     gdn_conv_w[l], gdn_a_log[l], gdn_dt_bias[l], g_conv_out[l], gdn_norm_g[l], w_out[l],
                     router_w[l], router_b[l], exp_w_gate_up[l], exp_b_gate_up[l], exp_w_down[l],
                     exp_b_down[l])
        xp, xs_ = res[0], res[1]
        for acc, r in zip(outs, res[2:]):
            acc.append(r)
    return (xp, xs_) + tuple(o[0][None] if depth == 1 else jnp.stack(o) for o in outs)
```

```python
import functools

import jax
import jax.numpy as jnp
from jax import lax
from jax.experimental import pallas as pl
from jax.experimental.pallas import tpu as pltpu

F32 = jnp.float32
BF16 = jnp.bfloat16
I32 = jnp.int32
HIGHEST = lax.Precision.HIGHEST

EPS = 1e-6
N_HEADS = 8
HEAD_D = 128
TOP_K = 4
TOP_K_SHIFT = 2
SWIGLU_LIMIT = 7.0
SWIGLU_ALPHA = 1.702
CHUNK = 64
LANES = 128
EXPERT_ROW_TILE = 256
EXPERT_STEP_TILES = 2
GATE_UP_STEP_TILES = 3
IN_PROJ_COLS = 1792
MATMUL_COLS = 1024
IN_PROJ_ROWS = 1024
POST_MIX_ROWS = 512
HALO = 8
NEG_BIG = -1e30
VMEM_LIMIT = 56 * 1024 * 1024


def _params(semantics, vmem=VMEM_LIMIT):
    return pltpu.CompilerParams(dimension_semantics=semantics, vmem_limit_bytes=vmem)


def _mm(a, b):
    return jnp.dot(a.astype(BF16), b.astype(BF16), preferred_element_type=F32)


def _mm_nt(a, b):
    return lax.dot_general(a.astype(BF16), b.astype(BF16), (((1,), (1,)), ((), ())),
                           preferred_element_type=F32)


def _rms(x):
    return x * lax.rsqrt(jnp.mean(x * x, axis=-1, keepdims=True) + EPS)


def _silu(x):
    return x * jax.nn.sigmoid(x)


def _softplus(x):
    return jnp.maximum(x, 0.0) + jnp.log1p(jnp.exp(-jnp.abs(x)))


def _mod_kernel(c_ref, w_ref, b_ref, o_ref):
    s = _silu(c_ref[...])
    o_ref[...] = _mm(s, w_ref[...]) + b_ref[...]


def _modulation(c_all, w_mod, b_mod):
    n, d = c_all.shape
    m = w_mod.shape[1]
    tn = MATMUL_COLS
    return pl.pallas_call(
        _mod_kernel,
        out_shape=jax.ShapeDtypeStruct((n, m), F32),
        grid=(m // tn,),
        in_specs=[pl.BlockSpec((n, d), lambda j: (0, 0)),
                  pl.BlockSpec((d, tn), lambda j: (0, j)),
                  pl.BlockSpec((1, tn), lambda j: (0, j))],
        out_specs=pl.BlockSpec((n, tn), lambda j: (0, j)),
        compiler_params=_params(("arbitrary",)),
        name="modulation",
    )(c_all, w_mod, b_mod.reshape(1, m))


def _mod_spec(arr, tm, rows_per_group):
    if arr.ndim == 3:
        tiles = rows_per_group // tm
        return pl.BlockSpec((None, 1, arr.shape[-1]), lambda i, *_: (i // tiles, 0, 0))
    return pl.BlockSpec((tm, arr.shape[-1]), lambda i, *_: (i, 0))


def _proj_kernel(x_ref, sc_ref, sh_ref, g_ref, w_ref, wba_ref, o_ref, ba_ref, h_scr):
    @pl.when(pl.program_id(1) == 0)
    def _():
        h = (_rms(x_ref[...]) * g_ref[...]) * (1.0 + sc_ref[...]) + sh_ref[...]
        hb = h.astype(BF16)
        h_scr[...] = hb
        ba_ref[...] = jnp.dot(hb, wba_ref[...], preferred_element_type=F32)

    o_ref[...] = jnp.dot(h_scr[...], w_ref[...], preferred_element_type=F32)


def _in_proj(x, scale, shift, g, w_main, w_ba, rows_per_group, n):
    m, d = x.shape
    tm = min(m, IN_PROJ_ROWS, rows_per_group if scale.ndim == 3 else m)
    tn = IN_PROJ_COLS
    return pl.pallas_call(
        _proj_kernel,
        out_shape=(jax.ShapeDtypeStruct((m, n), F32), jax.ShapeDtypeStruct((m, LANES), F32)),
        grid=(m // tm, n // tn),
        in_specs=[pl.BlockSpec((tm, d), lambda i, j: (i, 0)),
                  _mod_spec(scale, tm, rows_per_group),
                  _mod_spec(shift, tm, rows_per_group),
                  pl.BlockSpec((1, d), lambda i, j: (0, 0)),
                  pl.BlockSpec((d, tn), lambda i, j: (0, j)),
                  pl.BlockSpec((d, LANES), lambda i, j: (0, 0))],
        out_specs=(pl.BlockSpec((tm, tn), lambda i, j: (i, j)),
                   pl.BlockSpec((tm, LANES), lambda i, j: (i, 0))),
        scratch_shapes=[pltpu.VMEM((tm, d), BF16)],
        compiler_params=_params(("arbitrary", "arbitrary")),
        name="in_proj",
    )(x, scale, shift, g.reshape(1, d), w_main, w_ba)


PROMPT_SEQS_PER_STEP = 4


def _mixer_prompt_kernel(proj_ref, ba_ref, caw_ref, gcw_ref, alog_ref, dtb_ref, gco_ref, gng_ref,
                         mix_ref, ha_ref, hq_ref, s_ref, extu, extq, qc_scr, s_scr, *, nseq):
    c = CHUNK
    dc = N_HEADS * HEAD_D
    dq = 3 * dc
    t = pl.program_id(1)
    is_last = t == pl.num_programs(1) - 1

    @pl.when(t == 0)
    def _():
        extu[:, 0:HALO, :] = jnp.zeros((nseq, HALO, dc), F32)
        extq[:, 0:HALO, :] = jnp.zeros((nseq, HALO, dq), F32)
        s_scr[...] = jnp.zeros_like(s_scr)

    row = lax.broadcasted_iota(I32, (c, c), 0)
    col = lax.broadcasted_iota(I32, (c, c), 1)
    causal = row >= col
    strict = row > col
    lower = jnp.where(causal, 1.0, 0.0).astype(F32)
    upper = jnp.where(row <= col, 1.0, 0.0).astype(F32)
    caw = caw_ref[...]
    gcw = gcw_ref[...]

    chains = [(sq, h) for sq in range(nseq) for h in range(N_HEADS)]
    heads = range(len(chains))
    qn, kn, vb, kb, kbg, qg, kg, decay, s_decay = ([] for _ in range(9))
    for sq in range(nseq):
        u = proj_ref[sq, :, dc:2 * dc] * proj_ref[sq, :, 2 * dc:3 * dc]
        extu[sq, HALO:HALO + c, :] = u
        ya = (caw[0:1] * extu[sq, HALO - 2:HALO - 2 + c, :] + caw[1:2] * extu[sq, HALO - 1:HALO - 1 + c, :]
              + caw[2:3] * u)
        ya = proj_ref[sq, :, 0:dc] * ya
        mix_ref[sq, :, 0:dc] = (_rms(ya) * gco_ref[...]).astype(BF16)
        last_u = extu[sq, HALO - 2 + c:HALO + c, :]
        extu[sq, HALO - 2:HALO, :] = last_u

        qkv = proj_ref[sq, :, 3 * dc:3 * dc + dq]
        extq[sq, HALO:HALO + c, :] = qkv
        qc = (gcw[0:1] * extq[sq, HALO - 3:HALO - 3 + c, :] + gcw[1:2] * extq[sq, HALO - 2:HALO - 2 + c, :]
              + gcw[2:3] * extq[sq, HALO - 1:HALO - 1 + c, :] + gcw[3:4] * qkv)
        qc_scr[sq] = _silu(qc)
        last_q = extq[sq, HALO - 3 + c:HALO + c, :]
        extq[sq, HALO - 3:HALO, :] = last_q

        @pl.when(is_last)
        def _(sq=sq, last_u=last_u, last_q=last_q):
            ha_ref[sq] = last_u
            hq_ref[sq] = last_q

        ba = ba_ref[sq]
        beta_all = jax.nn.sigmoid(ba)
        g_all = -jnp.exp(alog_ref[...]) * _softplus(ba + dtb_ref[...])
        gc_all = jnp.dot(lower, g_all, precision=HIGHEST, preferred_element_type=F32)
        gc_t = lax.dot_general(g_all, upper, (((0,), (0,)), ((), ())), precision=HIGHEST,
                               preferred_element_type=F32)
        for h in range(N_HEADS):
            lo = h * HEAD_D
            q = qc_scr[sq, :, lo:lo + HEAD_D]
            k = qc_scr[sq, :, dc + lo:dc + lo + HEAD_D]
            v = qc_scr[sq, :, 2 * dc + lo:2 * dc + lo + HEAD_D]
            qn_h = q * lax.rsqrt(jnp.sum(q * q, axis=-1, keepdims=True) + EPS) * (HEAD_D ** -0.5)
            kn_h = k * lax.rsqrt(jnp.sum(k * k, axis=-1, keepdims=True) + EPS)
            beta = beta_all[:, h:h + 1]
            gcc = gc_all[:, N_HEADS + h:N_HEADS + h + 1]
            gcr = gc_t[N_HEADS + h:N_HEADS + h + 1, :]
            gl = gc_all[c - 1:c, N_HEADS + h:N_HEADS + h + 1]
            eg = jnp.exp(gcc)
            kb_h = kn_h * beta
            qn.append(qn_h)
            kn.append(kn_h)
            vb.append(v * beta)
            kb.append(kb_h)
            kbg.append(kb_h * eg)
            qg.append(qn_h * eg)
            kg.append(kn_h * jnp.exp(gl - gcc))
            decay.append(jnp.where(causal, jnp.exp(jnp.minimum(gcc - gcr, 0.0)), 0.0))
            s_decay.append(jnp.exp(gl))

    kq = [_mm_nt(jnp.concatenate([kb[h], qn[h]], axis=0), kn[h]) for h in heads]
    a_mat = [jnp.where(strict, kq[h][:c] * decay[h], 0.0) for h in heads]
    qk = [kq[h][c:] * decay[h] for h in heads]
    n_mat = [-a_mat[h] for h in heads]
    p = a_mat
    size = 2
    while size < c:
        p = [_mm(p[h], p[h]) for h in heads]
        n_p = [_mm(n_mat[h], p[h]) for h in heads]
        n_mat = [n_mat[h] + p[h] + n_p[h] for h in heads]
        size *= 2
    rhs = [jnp.concatenate([vb[h], kbg[h]], axis=-1) for h in heads]
    uw = [rhs[h] + _mm(n_mat[h], rhs[h]) for h in heads]
    s_old = [s_scr[sq, h] for sq, h in chains]
    ws = [_mm(jnp.concatenate([uw[h][:, HEAD_D:], qg[h]], axis=0), s_old[h]) for h in heads]
    v_new = [uw[h][:, :HEAD_D] - ws[h][:c] for h in heads]
    fin = [_mm(jnp.concatenate([qk[h], kg[h].T], axis=0), v_new[h]) for h in heads]
    for i, (sq, h) in enumerate(chains):
        lo = h * HEAD_D
        s_scr[sq, h] = s_old[i] * s_decay[i] + fin[i][c:]
        o = ws[i][c:] + fin[i][:c]
        z = proj_ref[sq, :, 3 * dc + dq + lo:3 * dc + dq + lo + HEAD_D]
        yb = _rms(o) * gng_ref[...] * _silu(z)
        mix_ref[sq, :, dc + lo:dc + lo + HEAD_D] = yb.astype(BF16)

    @pl.when(is_last)
    def _():
        s_ref[...] = s_scr[...]


def _lane_row(vec, offset):
    return jnp.zeros((1, LANES), F32).at[0, offset:offset + vec.shape[0]].set(vec.astype(F32))


def _mixer_prompt(proj, ba, bsz, seq, conv_a_w, gdn_conv_w, a_log, dt_bias, g_conv_out, gdn_norm_g):
    c = CHUNK
    dc = N_HEADS * HEAD_D
    dq = 3 * dc
    dproj = proj.shape[1]
    nt = seq // c
    const = lambda shape: pl.BlockSpec(shape, lambda b, t: (0,) * len(shape))
    nseq = PROMPT_SEQS_PER_STEP if bsz % PROMPT_SEQS_PER_STEP == 0 else 1
    seq_block = lambda *tail: pl.BlockSpec((nseq,) + tail, lambda b, t: (b, t) + (0,) * (len(tail) - 1))
    whole = lambda *tail: pl.BlockSpec((nseq,) + tail, lambda b, t: (b,) + (0,) * len(tail))
    mix, ha, hq, s_fin = pl.pallas_call(
        functools.partial(_mixer_prompt_kernel, nseq=nseq),
        out_shape=(jax.ShapeDtypeStruct((bsz, seq, 2 * dc), BF16),
                   jax.ShapeDtypeStruct((bsz, 2, dc), F32),
                   jax.ShapeDtypeStruct((bsz, 3, dq), F32),
                   jax.ShapeDtypeStruct((bsz, N_HEADS, HEAD_D, HEAD_D), F32)),
        grid=(bsz // nseq, nt),
        in_specs=[seq_block(c, dproj), seq_block(c, LANES),
                  const((3, dc)), const((4, dq)), const((1, LANES)), const((1, LANES)),
                  const((1, dc)), const((1, HEAD_D))],
        out_specs=(seq_block(c, 2 * dc), whole(2, dc), whole(3, dq), whole(N_HEADS, HEAD_D, HEAD_D)),
        scratch_shapes=[pltpu.VMEM((nseq, HALO + c, dc), F32), pltpu.VMEM((nseq, HALO + c, dq), F32),
                        pltpu.VMEM((nseq, c, dq), F32), pltpu.VMEM((nseq, N_HEADS, HEAD_D, HEAD_D), F32)],
        compiler_params=_params(("arbitrary", "arbitrary")),
        name="mixer_prompt",
    )(proj.reshape(bsz, seq, dproj), ba.reshape(bsz, seq, LANES), conv_a_w, gdn_conv_w,
      _lane_row(a_log, N_HEADS), _lane_row(dt_bias, N_HEADS), g_conv_out.reshape(1, dc),
      gdn_norm_g.reshape(1, HEAD_D))
    return mix.reshape(bsz * seq, 2 * dc), ha, hq, s_fin


SAMPLE_GROUP = 16


def _mixer_sample_kernel(proj_ref, ba_ref, hista_ref, histq_ref, s_in_ref, caw_ref, gcw_ref, alog_ref,
                         dtb_ref, gco_ref, gng_ref, mix_ref, ha_ref, hq_ref, s_out_ref, qc_scr, o_scr):
    tb = SAMPLE_GROUP
    dc = N_HEADS * HEAD_D
    dq = 3 * dc

    u = proj_ref[:, dc:2 * dc] * proj_ref[:, 2 * dc:3 * dc]
    caw = caw_ref[...]
    ya = caw[0:1] * hista_ref[:, 0:dc] + caw[1:2] * hista_ref[:, dc:2 * dc] + caw[2:3] * u
    ya = proj_ref[:, 0:dc] * ya
    mix_ref[:, 0:dc] = _rms(ya) * gco_ref[...]
    ha_ref[:, 0:dc] = hista_ref[:, dc:2 * dc]
    ha_ref[:, dc:2 * dc] = u

    qkv = proj_ref[:, 3 * dc:3 * dc + dq]
    gcw = gcw_ref[...]
    qc = (gcw[0:1] * histq_ref[:, 0:dq] + gcw[1:2] * histq_ref[:, dq:2 * dq]
          + gcw[2:3] * histq_ref[:, 2 * dq:3 * dq] + gcw[3:4] * qkv)
    qc_scr[...] = _silu(qc)
    hq_ref[:, 0:dq] = histq_ref[:, dq:2 * dq]
    hq_ref[:, dq:2 * dq] = histq_ref[:, 2 * dq:3 * dq]
    hq_ref[:, 2 * dq:3 * dq] = qkv

    ba = ba_ref[...]
    beta_all = jax.nn.sigmoid(ba)
    eg_all = jnp.exp(-jnp.exp(alog_ref[...]) * _softplus(ba + dtb_ref[...]))

    for h in range(N_HEADS):
        lo = h * HEAD_D
        q = qc_scr[:, lo:lo + HEAD_D]
        k = qc_scr[:, dc + lo:dc + lo + HEAD_D]
        v = qc_scr[:, 2 * dc + lo:2 * dc + lo + HEAD_D]
        qn = q * lax.rsqrt(jnp.sum(q * q, axis=-1, keepdims=True) + EPS) * (HEAD_D ** -0.5)
        kn = k * lax.rsqrt(jnp.sum(k * k, axis=-1, keepdims=True) + EPS)
        qk = jnp.sum(qn * kn, axis=-1, keepdims=True)
        kn_t = kn.T
        qn_t = qn.T
        for b in range(tb):
            s_old = s_in_ref[b, h]
            kc = jnp.broadcast_to(kn_t[:, b:b + 1], s_old.shape)
            e = eg_all[b:b + 1, N_HEADS + h:N_HEADS + h + 1]
            ks = jnp.sum(s_old * kc, axis=0, keepdims=True)
            qs = jnp.sum(s_old * qn_t[:, b:b + 1], axis=0, keepdims=True)
            v_new = beta_all[b:b + 1, h:h + 1] * (v[b:b + 1, :] - e * ks)
            o_scr[b:b + 1, lo:lo + HEAD_D] = e * qs + qk[b:b + 1, :] * v_new
            s_out_ref[b, h] = s_old * e + kc * v_new
        z = proj_ref[:, 3 * dc + dq + lo:3 * dc + dq + lo + HEAD_D]
        o = o_scr[:, lo:lo + HEAD_D]
        mix_ref[:, dc + lo:dc + lo + HEAD_D] = _rms(o) * gng_ref[...] * _silu(z)


def _mixer_sample(proj, ba, hist_a, hist_q, s_in, conv_a_w, gdn_conv_w, a_log, dt_bias, g_conv_out,
                  gdn_norm_g):
    n = proj.shape[0]
    tb = SAMPLE_GROUP
    dc = N_HEADS * HEAD_D
    dq = 3 * dc
    dproj = proj.shape[1]
    const = lambda shape: pl.BlockSpec(shape, lambda i: (0,) * len(shape))
    rows = lambda width: pl.BlockSpec((tb, width), lambda i: (i, 0))
    state = pl.BlockSpec((tb, N_HEADS, HEAD_D, HEAD_D), lambda i: (i, 0, 0, 0))
    return pl.pallas_call(
        _mixer_sample_kernel,
        out_shape=(jax.ShapeDtypeStruct((n, 2 * dc), F32),
                   jax.ShapeDtypeStruct((n, 2 * dc), F32),
                   jax.ShapeDtypeStruct((n, 3 * dq), F32),
                   jax.ShapeDtypeStruct((n, N_HEADS, HEAD_D, HEAD_D), F32)),
        grid=(n // tb,),
        in_specs=[rows(dproj), rows(LANES), rows(2 * dc), rows(3 * dq), state,
                  const((3, dc)), const((4, dq)), const((1, LANES)), const((1, LANES)),
                  const((1, dc)), const((1, HEAD_D))],
        out_specs=(rows(2 * dc), rows(2 * dc), rows(3 * dq), state),
        scratch_shapes=[pltpu.VMEM((tb, dq), F32), pltpu.VMEM((tb, dc), F32)],
        compiler_params=_params(("arbitrary",)),
        name="mixer_sample",
    )(proj, ba, hist_a, hist_q, s_in, conv_a_w, gdn_conv_w, _lane_row(a_log, N_HEADS),
      _lane_row(dt_bias, N_HEADS), g_conv_out.reshape(1, dc), gdn_norm_g.reshape(1, HEAD_D))


POST_MIX_SUB_ROWS = 256


def _post_mix_kernel(mix_ref, x_ref, gate_ref, sc_ref, sh_ref, gpost_ref, gpre_ref, wout_ref,
                     rwh_ref, rwl_ref, rb_ref, x1_ref, h2_ref, lg_ref):
    tm = x_ref.shape[0]
    sub = min(tm, POST_MIX_SUB_ROWS)
    tiles = [slice(r0, r0 + sub) for r0 in range(0, tm, sub)]
    mixes = [jnp.dot(mix_ref[rows].astype(BF16), wout_ref[...], preferred_element_type=F32)
             for rows in tiles]
    for rows, mix in zip(tiles, mixes):
        per_row = lambda ref: ref[rows] if ref.shape[0] == tm else ref[...]
        x1 = x_ref[rows] + per_row(gate_ref) * (_rms(mix) * gpost_ref[...])
        x1_ref[rows] = x1
        h2 = (_rms(x1) * gpre_ref[...]) * (1.0 + per_row(sc_ref)) + per_row(sh_ref)
        h2_ref[rows] = h2
        hi = h2.astype(BF16)
        lo = (h2 - hi.astype(F32)).astype(BF16)
        rwh = rwh_ref[...]
        lg_ref[rows] = (jnp.dot(hi, rwh, preferred_element_type=F32)
                        + jnp.dot(lo, rwh, preferred_element_type=F32)
                        + jnp.dot(hi, rwl_ref[...], preferred_element_type=F32) + rb_ref[...])


def _post_mix(mix_in, x, gate, scale, shift, g_post, g_pre, w_out, rw_hi, rw_lo, rb, rows_per_group):
    m, d = x.shape
    tm = min(m, POST_MIX_ROWS, rows_per_group if gate.ndim == 3 else m)
    const = lambda shape: pl.BlockSpec(shape, lambda i: (0,) * len(shape))
    rows = lambda width: pl.BlockSpec((tm, width), lambda i: (i, 0))
    return pl.pallas_call(
        _post_mix_kernel,
        out_shape=(jax.ShapeDtypeStruct((m, d), F32), jax.ShapeDtypeStruct((m, d), F32),
                   jax.ShapeDtypeStruct((m, LANES), F32)),
        grid=(m // tm,),
        in_specs=[rows(d), rows(d),
                  _mod_spec(gate, tm, rows_per_group),
                  _mod_spec(scale, tm, rows_per_group),
                  _mod_spec(shift, tm, rows_per_group),
                  const((1, d)), const((1, d)), const((d, d)),
                  const((d, LANES)), const((d, LANES)), const((1, LANES))],
        out_specs=(rows(d), rows(d), rows(LANES)),
        compiler_params=_params(("arbitrary",)),
        name="post_mix",
    )(mix_in, x, gate, scale, shift, g_post.reshape(1, d), g_pre.reshape(1, d), w_out, rw_hi, rw_lo, rb)


ROUTE_TOKENS = 128


def _route_kernel(lgp_ref, lgs_ref, idx_ref, p_ref, rank_ref, cnt_ref, carry, *, prompt_tiles):
    tm = lgp_ref.shape[0]

    @pl.when(pl.program_id(0) == 0)
    def _():
        carry[...] = jnp.zeros_like(carry)

    l = jnp.where(pl.program_id(0) < prompt_tiles, lgp_ref[...], lgs_ref[...])
    lane = lax.broadcasted_iota(I32, l.shape, 1)
    lane_f = lane.astype(F32)
    vals, hots = [], []
    idx_out = jnp.zeros(l.shape, F32)
    for k in range(TOP_K):
        m = jnp.max(l, axis=-1, keepdims=True)
        idx = jnp.min(jnp.where(l == m, lane_f, float(LANES)), axis=-1, keepdims=True)
        hot = lane_f == idx
        vals.append(m)
        hots.append(hot)
        idx_out = jnp.where(lane == k, idx, idx_out)
        l = jnp.where(hot, -jnp.inf, l)
    exps = [jnp.exp(v - vals[0]) for v in vals]
    denom = exps[0] + exps[1] + exps[2] + exps[3]
    p_out = jnp.zeros(l.shape, F32)
    for k in range(TOP_K):
        p_out = jnp.where(lane == k, exps[k] / denom, p_out)
    member = jnp.where(hots[0] | hots[1] | hots[2] | hots[3], 1.0, 0.0).astype(F32)
    row = lax.broadcasted_iota(I32, (tm, tm), 0)
    col = lax.broadcasted_iota(I32, (tm, tm), 1)
    before = jnp.where(row > col, 1.0, 0.0).astype(BF16)
    prefix = jnp.dot(before, member.astype(BF16), preferred_element_type=F32) + carry[...]
    rank_out = jnp.zeros(l.shape, F32)
    for k in range(TOP_K):
        r = jnp.sum(jnp.where(hots[k], prefix, 0.0), axis=-1, keepdims=True)
        rank_out = jnp.where(lane == k, r, rank_out)
    carry[...] = carry[...] + jnp.sum(member, axis=0, keepdims=True)
    idx_ref[...] = idx_out.astype(I32)
    p_ref[...] = p_out
    rank_ref[...] = rank_out.astype(I32)
    cnt_ref[...] = carry[...]


def _route(logits_p, logits_s):
    tm = ROUTE_TOKENS
    pt = logits_p.shape[0] // tm
    n = logits_p.shape[0] + logits_s.shape[0]
    tile = pl.BlockSpec((tm, LANES), lambda i: (i, 0))
    return pl.pallas_call(
        functools.partial(_route_kernel, prompt_tiles=pt),
        out_shape=(jax.ShapeDtypeStruct((n, LANES), I32), jax.ShapeDtypeStruct((n, LANES), F32),
                   jax.ShapeDtypeStruct((n, LANES), I32), jax.ShapeDtypeStruct((1, LANES), F32)),
        grid=(n // tm,),
        in_specs=[pl.BlockSpec((tm, LANES), lambda i: (jnp.minimum(i, pt - 1), 0)),
                  pl.BlockSpec((tm, LANES), lambda i: (jnp.maximum(i - pt, 0), 0))],
        out_specs=(tile, tile, tile, pl.BlockSpec((1, LANES), lambda i: (0, 0))),
        scratch_shapes=[pltpu.VMEM((1, LANES), F32)],
        compiler_params=_params(("arbitrary",)),
        name="route",
    )(logits_p, logits_s)


INVERT_TOKENS_MAX = 1024


def _largest_tile(n, unit, cap):
    return max(m for m in range(unit, cap + 1, unit) if n % m == 0)


def _invert_kernel(meta_ref, slot_ref, table_ref, *, pad_token, n_exp, tokens):
    i = pl.program_id(0)
    pairs = tokens * TOP_K

    @pl.when(i == 0)
    def _():
        def fill(r, carry):
            table_ref[r] = pad_token
            return carry

        def fill_group_tail(e, carry):
            lax.fori_loop(meta_ref[e] + meta_ref[2 * n_exp + e], meta_ref[e] + meta_ref[n_exp + e], fill, 0)
            return carry

        lax.fori_loop(0, n_exp, fill_group_tail, 0)
        lax.fori_loop(meta_ref[3 * n_exp] * EXPERT_ROW_TILE, table_ref.shape[0], fill, 0)

    def put(p, carry):
        table_ref[slot_ref[0, p]] = i * tokens + (p >> TOP_K_SHIFT)
        return carry

    lax.fori_loop(0, pairs, put, 0, unroll=16)


def _invert(slots, meta, n_rows, pad_token, n_exp):
    n = slots.shape[0]
    tokens = _largest_tile(n, LANES, INVERT_TOKENS_MAX)
    nt = n // tokens
    pairs = tokens * TOP_K
    return pl.pallas_call(
        functools.partial(_invert_kernel, pad_token=pad_token, n_exp=n_exp, tokens=tokens),
        out_shape=jax.ShapeDtypeStruct((n_rows,), I32),
        grid_spec=pltpu.PrefetchScalarGridSpec(
            num_scalar_prefetch=1,
            grid=(nt,),
            in_specs=[pl.BlockSpec((None, 1, pairs), lambda i, meta: (i, 0, 0), memory_space=pltpu.SMEM)],
            out_specs=pl.BlockSpec(memory_space=pltpu.SMEM)),
        compiler_params=_params(("arbitrary",)),
        name="invert",
    )(meta, slots.reshape(nt, 1, pairs))


ROW_UNROLL = 8
ROW_UNROLL_SHIFT = 3
DISPATCH_TILES = 6


def _dispatch_kernel(na_ref, split_ref, tok_ref, tok_next_ref, h2p_ref, h2s_ref, xs_ref, buf, sem, *,
                     n_prompt):
    i = pl.program_id(0)
    tile = EXPERT_ROW_TILE
    n_active = na_ref[0]

    def issue_tile(tok, t, sub, slot):
        split = split_ref[t]
        base = sub * tile

        def prompt_copy(r):
            pltpu.make_async_copy(h2p_ref.at[pl.ds(tok[0, base + r], 1)], buf.at[slot, pl.ds(base + r, 1)],
                                  sem.at[slot, sub]).start()

        def sample_copy(r):
            pltpu.make_async_copy(h2s_ref.at[pl.ds(tok[0, base + r] - n_prompt, 1)],
                                  buf.at[slot, pl.ds(base + r, 1)], sem.at[slot, sub]).start()

        def rows(lo, hi, fn):
            def body(r, carry):
                fn(r)
                return carry
            lax.fori_loop(lo, hi, body, 0)

        def groups(lo, hi, fn):
            def body(g, carry):
                for u in range(ROW_UNROLL):
                    fn(g * ROW_UNROLL + u)
                return carry
            lax.fori_loop(lo, hi, body, 0)

        whole = split >> ROW_UNROLL_SHIFT
        first = (split + ROW_UNROLL - 1) >> ROW_UNROLL_SHIFT
        groups(0, whole, prompt_copy)
        rows(whole * ROW_UNROLL, split, prompt_copy)
        rows(split, first * ROW_UNROLL, sample_copy)
        groups(first, tile // ROW_UNROLL, sample_copy)

    def issue(tok, step, slot):
        for sub in range(DISPATCH_TILES):
            t = step * DISPATCH_TILES + sub

            @pl.when(t < n_active)
            def _(t=t, sub=sub):
                issue_tile(tok, t, sub, slot)

    @pl.when(i == 0)
    def _():
        issue(tok_ref, 0, 0)

    @pl.when((i + 1) * DISPATCH_TILES < n_active)
    def _():
        issue(tok_next_ref, i + 1, (i + 1) & 1)

    slot = i & 1
    for sub in range(DISPATCH_TILES):
        t = i * DISPATCH_TILES + sub
        out_rows = slice(sub * tile, (sub + 1) * tile)

        @pl.when(t < n_active)
        def _(sub=sub, out_rows=out_rows):
            pltpu.make_async_copy(h2p_ref.at[pl.ds(0, tile)], buf.at[slot, out_rows], sem.at[slot, sub]).wait()
            xs_ref[out_rows] = buf[slot, out_rows].astype(BF16)

        @pl.when(t >= n_active)
        def _(out_rows=out_rows):
            xs_ref[out_rows] = jnp.zeros((tile, xs_ref.shape[1]), BF16)


def _dispatch(h2_p, h2_s, table, split, n_active):
    d = h2_p.shape[1]
    rows = EXPERT_ROW_TILE * DISPATCH_TILES
    n_rows = table.shape[0]
    assert n_rows % rows == 0, (n_rows, rows)
    nt = n_rows // rows
    return pl.pallas_call(
        functools.partial(_dispatch_kernel, n_prompt=h2_p.shape[0]),
        out_shape=jax.ShapeDtypeStruct((n_rows, d), BF16),
        grid_spec=pltpu.PrefetchScalarGridSpec(
            num_scalar_prefetch=2,
            grid=(nt,),
            in_specs=[pl.BlockSpec((None, 1, rows), lambda i, na, sp: (i, 0, 0), memory_space=pltpu.SMEM),
                      pl.BlockSpec((None, 1, rows), lambda i, na, sp: (jnp.minimum(i + 1, nt - 1), 0, 0),
                                   memory_space=pltpu.SMEM),
                      pl.BlockSpec(memory_space=pl.ANY), pl.BlockSpec(memory_space=pl.ANY)],
            out_specs=pl.BlockSpec((rows, d), lambda i, na, sp: (i, 0)),
            scratch_shapes=[pltpu.VMEM((2, rows, d), F32), pltpu.SemaphoreType.DMA((2, DISPATCH_TILES))]),
        compiler_params=_params(("arbitrary",)),
        name="dispatch",
    )(n_active, split, table.reshape(nt, 1, rows), table.reshape(nt, 1, rows), h2_p, h2_s)


def _stream_expert_weights(i, te_ref, first_ref, run_ref, next_ref, meta_ref, n_col_blocks, copies, consume):
    j = pl.program_id(0)
    n_active = meta_ref[0]
    n_runs = meta_ref[1]

    @pl.when(jnp.logical_and(j == 0, i == 0))
    def _():
        for cp in copies(te_ref[0], 0, 0):
            cp.start()

    @pl.when(jnp.logical_and(i < n_active, first_ref[i] == 1))
    def _():
        slot = (j * n_runs + run_ref[i]) & 1
        for cp in copies(te_ref[i], j, slot):
            cp.wait()
        last = run_ref[i] == n_runs - 1
        e_next = jnp.where(last, te_ref[0], next_ref[i])
        j_next = jnp.where(last, j + 1, j)

        @pl.when(j_next < n_col_blocks)
        def _():
            for cp in copies(e_next, j_next, 1 - slot):
                cp.start()

        consume(slot)


def _gate_up_kernel(te_ref, first_ref, run_ref, next_ref, meta_ref, x_ref, bg_ref, w_hbm, o_ref,
                    wbuf, wg_b, wu_b, sem, *, nj, tn):
    j = pl.program_id(0)
    tm = EXPERT_ROW_TILE
    gate_cols = pl.ds(pl.multiple_of(j * tn, tn), tn)
    up_cols = pl.ds(pl.multiple_of((nj + j) * tn, tn), tn)

    def copies(e, jj, slot):
        return [pltpu.make_async_copy(w_hbm.at[e, :, pl.ds(pl.multiple_of((c * nj + jj) * tn, tn), tn)],
                                      wbuf.at[slot, c], sem.at[slot, c]) for c in range(2)]

    def consume(slot):
        wg_b[...] = wbuf[slot, 0].astype(BF16)
        wu_b[...] = wbuf[slot, 1].astype(BF16)

    for sub in range(GATE_UP_STEP_TILES):
        i = pl.program_id(1) * GATE_UP_STEP_TILES + sub
        rows = slice(sub * tm, (sub + 1) * tm)
        active = i < meta_ref[0]
        _stream_expert_weights(i, te_ref, first_ref, run_ref, next_ref, meta_ref, nj, copies, consume)

        @pl.when(active)
        def _(i=i, rows=rows):
            e = te_ref[i]
            x = x_ref[rows]
            gate = jnp.dot(x, wg_b[...], preferred_element_type=F32) + bg_ref[e, :, gate_cols]
            up = jnp.dot(x, wu_b[...], preferred_element_type=F32) + bg_ref[e, :, up_cols]
            gate = jnp.minimum(gate, SWIGLU_LIMIT)
            up = jnp.clip(up, -SWIGLU_LIMIT, SWIGLU_LIMIT)
            o_ref[rows] = ((up + 1.0) * gate * jax.nn.sigmoid(SWIGLU_ALPHA * gate)).astype(BF16)

        @pl.when(jnp.logical_not(active))
        def _(rows=rows):
            o_ref[rows] = jnp.zeros((tm, o_ref.shape[1]), o_ref.dtype)


def _expert_gate_up(xs, w_gu, b_gu, sched):
    n_rows, d = xs.shape
    n_exp, _, f2 = w_gu.shape
    f = f2 // 2
    tm = EXPERT_ROW_TILE
    tn = MATMUL_COLS
    nj = f // tn
    rows = tm * GATE_UP_STEP_TILES
    assert n_rows % rows == 0, (n_rows, rows)
    step = lambda i, meta: jnp.minimum(i, (meta[0] - 1) // GATE_UP_STEP_TILES)
    return pl.pallas_call(
        functools.partial(_gate_up_kernel, nj=nj, tn=tn),
        out_shape=jax.ShapeDtypeStruct((n_rows, f), BF16),
        grid_spec=pltpu.PrefetchScalarGridSpec(
            num_scalar_prefetch=5,
            grid=(nj, n_rows // rows),
            in_specs=[pl.BlockSpec((rows, d), lambda j, i, te, fi, ru, nx, meta: (step(i, meta), 0)),
                      pl.BlockSpec((n_exp, 1, f2), lambda j, i, te, fi, ru, nx, meta: (0, 0, 0)),
                      pl.BlockSpec(memory_space=pl.ANY)],
            out_specs=pl.BlockSpec((rows, tn), lambda j, i, te, fi, ru, nx, meta: (i, j)),
            scratch_shapes=[pltpu.VMEM((2, 2, d, tn), F32), pltpu.VMEM((d, tn), BF16),
                            pltpu.VMEM((d, tn), BF16), pltpu.SemaphoreType.DMA((2, 2))]),
        compiler_params=_params(("arbitrary", "arbitrary")),
        name="expert_gate_up",
    )(*sched, xs, b_gu.reshape(n_exp, 1, f2), w_gu)


def _down_kernel(te_ref, first_ref, run_ref, next_ref, meta_ref, a_ref, b_ref, w_hbm, o_ref, wbuf, w_b, sem):
    tm = EXPERT_ROW_TILE

    def copies(e, jj, slot):
        return [pltpu.make_async_copy(w_hbm.at[e], wbuf.at[slot], sem.at[slot])]

    def consume(slot):
        w_b[...] = wbuf[slot].astype(BF16)

    for sub in range(EXPERT_STEP_TILES):
        i = pl.program_id(1) * EXPERT_STEP_TILES + sub
        rows = slice(sub * tm, (sub + 1) * tm)
        active = i < meta_ref[0]
        _stream_expert_weights(i, te_ref, first_ref, run_ref, next_ref, meta_ref, 1, copies, consume)

        @pl.when(active)
        def _(i=i, rows=rows):
            o_ref[rows] = jnp.dot(a_ref[rows], w_b[...], preferred_element_type=F32) + b_ref[te_ref[i]]

        @pl.when(jnp.logical_not(active))
        def _(rows=rows):
            o_ref[rows] = jnp.zeros((tm, o_ref.shape[1]), o_ref.dtype)


def _expert_down(act, w_d, b_d, sched):
    n_rows, f = act.shape
    n_exp, _, d = w_d.shape
    rows = EXPERT_ROW_TILE * EXPERT_STEP_TILES
    assert n_rows % rows == 0, (n_rows, rows)
    step = lambda i, meta: jnp.minimum(i, (meta[0] - 1) // EXPERT_STEP_TILES)
    return pl.pallas_call(
        _down_kernel,
        out_shape=jax.ShapeDtypeStruct((n_rows, d), F32),
        grid_spec=pltpu.PrefetchScalarGridSpec(
            num_scalar_prefetch=5,
            grid=(1, n_rows // rows),
            in_specs=[pl.BlockSpec((rows, f), lambda j, i, te, fi, ru, nx, meta: (step(i, meta), 0)),
                      pl.BlockSpec((n_exp, 1, d), lambda j, i, te, fi, ru, nx, meta: (0, 0, 0)),
                      pl.BlockSpec(memory_space=pl.ANY)],
            out_specs=pl.BlockSpec((rows, d), lambda j, i, te, fi, ru, nx, meta: (i, 0)),
            scratch_shapes=[pltpu.VMEM((2, f, d), F32), pltpu.VMEM((f, d), BF16),
                            pltpu.SemaphoreType.DMA((2,))]),
        compiler_params=_params(("arbitrary", "arbitrary")),
        name="expert_down",
    )(*sched, act, b_d.reshape(n_exp, 1, d), w_d)


COMBINE_TOKENS = 256


def _combine_kernel(slot_ref, slot_next_ref, p_ref, x1_ref, gate_ref, g_ref, y_ref, o_ref, buf, sem):
    i = pl.program_id(0)
    tm = x1_ref.shape[0]
    pairs = tm * TOP_K

    def issue(slots, b):
        def body(g, carry):
            for u in range(ROW_UNROLL):
                p = g * ROW_UNROLL + u
                tok = g * (ROW_UNROLL // TOP_K) + u // TOP_K
                pltpu.make_async_copy(y_ref.at[pl.ds(slots[0, p], 1)], buf.at[b, u % TOP_K, pl.ds(tok, 1)],
                                      sem.at[b]).start()
            return carry
        lax.fori_loop(0, pairs // ROW_UNROLL, body, 0)

    @pl.when(i == 0)
    def _():
        issue(slot_ref, 0)

    @pl.when(i + 1 < pl.num_programs(0))
    def _():
        issue(slot_next_ref, (i + 1) & 1)

    b = i & 1
    for k in range(TOP_K):
        pltpu.make_async_copy(y_ref.at[pl.ds(0, tm)], buf.at[b, k], sem.at[b]).wait()
    probs = p_ref[...]
    f = probs[:, 0:1] * buf[b, 0]
    for k in range(1, TOP_K):
        f = f + probs[:, k:k + 1] * buf[b, k]
    o_ref[...] = x1_ref[...] + gate_ref[...] * (_rms(f) * g_ref[...])


def _combine(y, slots, probs, x1, gate, g_post, rows_per_group):
    m, d = x1.shape
    tm = min(m, COMBINE_TOKENS)
    nt = m // tm
    pairs = tm * TOP_K
    return pl.pallas_call(
        _combine_kernel,
        out_shape=jax.ShapeDtypeStruct((m, d), F32),
        grid=(nt,),
        in_specs=[pl.BlockSpec((None, 1, pairs), lambda i: (i, 0, 0), memory_space=pltpu.SMEM),
                  pl.BlockSpec((None, 1, pairs), lambda i: (jnp.minimum(i + 1, nt - 1), 0, 0),
                               memory_space=pltpu.SMEM),
                  pl.BlockSpec((tm, LANES), lambda i: (i, 0)),
                  pl.BlockSpec((tm, d), lambda i: (i, 0)),
                  _mod_spec(gate, tm, rows_per_group),
                  pl.BlockSpec((1, d), lambda i: (0, 0)),
                  pl.BlockSpec(memory_space=pl.ANY)],
        out_specs=pl.BlockSpec((tm, d), lambda i: (i, 0)),
        scratch_shapes=[pltpu.VMEM((2, TOP_K, tm, d), F32), pltpu.SemaphoreType.DMA((2,))],
        compiler_params=_params(("arbitrary",)),
        name="combine",
    )(slots.reshape(nt, 1, pairs), slots.reshape(nt, 1, pairs), probs, x1, gate, g_post.reshape(1, d), y)


def _moe(h2_p, h2_s, logits_p, logits_s, n_exp, w_gu, b_gu, w_d, b_d):
    n = h2_p.shape[0] + h2_s.shape[0]
    tile = EXPERT_ROW_TILE
    top_i, probs, rank, counts = _route(logits_p, logits_s)
    counts = counts[0, :n_exp].astype(I32)
    cap = (counts + tile - 1) // tile * tile
    ends = jnp.cumsum(cap)
    offs = ends - cap
    n_tiles = (n * TOP_K) // tile + n_exp
    n_active = (ends[-1] // tile).astype(I32).reshape(1)
    tile_id = jnp.arange(n_tiles, dtype=I32)
    tile_expert = jnp.minimum(jnp.sum(ends[None, :] <= tile_id[:, None] * tile, axis=1), n_exp - 1).astype(I32)
    first = ((tile_id == 0) | (tile_expert != jnp.roll(tile_expert, 1))) & (tile_id < n_active[0])
    run = jnp.cumsum(first.astype(I32)) - 1
    next_expert = tile_expert[jnp.minimum(ends[tile_expert] // tile, n_tiles - 1)]
    sched = (tile_expert, first.astype(I32), run.astype(I32), next_expert.astype(I32),
             jnp.stack([n_active[0], jnp.sum(first.astype(I32))]).astype(I32))
    expert_ids = jnp.arange(n_exp, dtype=I32)
    pair_offs = jnp.sum(jnp.where(top_i[:, :TOP_K, None] == expert_ids, offs, 0), axis=-1)
    slots = pair_offs + rank[:, :TOP_K]
    n_prompt = h2_p.shape[0]
    meta = jnp.concatenate([offs, cap, counts, n_active]).astype(I32)
    table = _invert(slots, meta, n_tiles * tile, n_prompt, n_exp)
    split = jnp.sum(table.reshape(n_tiles, tile) < n_prompt, axis=1).astype(I32)
    xs = _dispatch(h2_p, h2_s, table, split, n_active)
    act = _expert_gate_up(xs, w_gu, b_gu, sched)
    y = _expert_down(act, w_d, b_d, sched)
    return y, slots, probs


def _layer(xp, xs_, cp_mod, cs_mod, hist_a, hist_q, s0, g_pre_mix, g_post_mix, g_pre_ffn, g_post_ffn,
           w_in, conv_a_w, gdn_conv_w, a_log, dt_bias, g_conv_out, gdn_norm_g, w_out, router_w, router_b,
           w_gu, b_gu, w_d, b_d):
    bsz, seq, d = xp.shape
    ns = xs_.shape[0]
    n_exp = router_w.shape[1]
    dc = N_HEADS * HEAD_D
    d_main = 3 * dc + 3 * dc + dc
    xp2 = xp.reshape(bsz * seq, d)
    xs2 = xs_.reshape(ns, d)

    w_main = w_in.astype(BF16)
    w_ba = jnp.zeros((d, LANES), BF16).at[:, :2 * N_HEADS].set(w_main[:, d_main:])
    w_out_b = w_out.astype(BF16)
    rw = jnp.zeros((d, LANES), F32).at[:, :n_exp].set(router_w)
    rw_hi = rw.astype(BF16)
    rw_lo = (rw - rw_hi.astype(F32)).astype(BF16)
    rb = jnp.full((1, LANES), NEG_BIG, F32).at[0, :n_exp].set(router_b)

    mp = [cp_mod[:, i * d:(i + 1) * d].reshape(bsz, 1, d) for i in range(6)]
    ms = [cs_mod[:, i * d:(i + 1) * d] for i in range(6)]

    proj_p, ba_p = _in_proj(xp2, mp[1], mp[0], g_pre_mix, w_main, w_ba, seq, d_main)
    proj_s, ba_s = _in_proj(xs2, ms[1], ms[0], g_pre_mix, w_main, w_ba, 1, d_main)

    mix_p, ha_p, hq_p, s_p = _mixer_prompt(proj_p, ba_p, bsz, seq, conv_a_w, gdn_conv_w, a_log, dt_bias,
                                           g_conv_out, gdn_norm_g)
    mix_s, ha_s, hq_s, s_s = _mixer_sample(proj_s, ba_s, hist_a.reshape(ns, 2 * dc),
                                           hist_q.reshape(ns, 9 * dc), s0, conv_a_w, gdn_conv_w, a_log,
                                           dt_bias, g_conv_out, gdn_norm_g)

    x1_p, h2_p, lg_p = _post_mix(mix_p, xp2, mp[2], mp[4], mp[3], g_post_mix, g_pre_ffn, w_out_b, rw_hi,
                                 rw_lo, rb, seq)
    x1_s, h2_s, lg_s = _post_mix(mix_s, xs2, ms[2], ms[4], ms[3], g_post_mix, g_pre_ffn, w_out_b, rw_hi,
                                 rw_lo, rb, 1)

    y, slots, probs = _moe(h2_p, h2_s, lg_p, lg_s, n_exp, w_gu, b_gu, w_d, b_d)
    np_ = bsz * seq
    out_p = _combine(y, slots[:np_], probs[:np_], x1_p, mp[5], g_post_ffn, seq)
    out_s = _combine(y, slots[np_:], probs[np_:], x1_s, ms[5], g_post_ffn, 1)
    return (out_p.reshape(bsz, seq, d), out_s.reshape(ns, 1, d), ha_p, hq_p, s_p,
            ha_s.reshape(ns, 2, dc), hq_s.reshape(ns, 3, 3 * dc), s_s)


def kernel(x_prompt, x_sample, state_conv_a, state_gdn_conv, state_gdn_S, c_prompt, c_sample, w_mod, b_mod, g_pre_mix, g_post_mix, g_pre_ffn, g_post_ffn, w_in, conv_a_w, gdn_conv_w, gdn_a_log, gdn_dt_bias, g_conv_out, gdn_norm_g, w_out, router_w, router_b, exp_w_gate_up, exp_b_gate_up, exp_w_down, exp_b_down):
    depth = w_mod.shape[0]
    bp = x_prompt.shape[0]
    xp, xs_ = x_prompt, x_sample
    outs = [[] for _ in range(6)]
    for l in range(depth):
        mod = _modulation(jnp.concatenate([c_prompt, c_sample], axis=0), w_mod[l], b_mod[l])
        res = _layer(xp, xs_, mod[:bp], mod[bp:], state_conv_a[l], state_gdn_conv[l], state_gdn_S[l],
                     g_pre_mix[l], g_post_mix[l], g_pre_ffn[l], g_post_ffn[l], w_in[l], conv_a_w[l],
                     gdn_conv_w[l], gdn_a_log[l], gdn_dt_bias[l], g_conv_out[l], gdn_norm_g[l], w_out[l],
                     router_w[l], router_b[l], exp_w_gate_up[l], exp_b_gate_up[l], exp_w_down[l],
                     exp_b_down[l])
        xp, xs_ = res[0], res[1]
        for acc, r in zip(outs, res[2:]):
            acc.append(r)
    return (xp, xs_) + tuple(o[0][None] if depth == 1 else jnp.stack(o) for o in outs)
```
